```python
import jax, jax.numpy as jnp
from jax import lax
import numpy as np

D_MODEL = 1024
BATCH = 8
SEQ = 8192
DEPTH = 2

CHUNK = 64
N_META = 16
MIX = D_MODEL
CONV_WIDTH_CH = MIX // 2
CONV_HEADS = 8
CONV_HEAD_DIM = CONV_WIDTH_CH // CONV_HEADS
CONV_K = 31
POOL_WIDTH_CH = MIX - CONV_WIDTH_CH
POOL_WINDOWS = (2, 4, 8, 16)
POOL_GROUPS = len(POOL_WINDOWS)
POOL_GROUP_DIM = POOL_WIDTH_CH // POOL_GROUPS
IN_COLS = 2 * CONV_WIDTH_CH + POOL_WIDTH_CH
D_FF = 2816
FFN_CONV_K = 3
EPS = 1e-6

kernel_name = "hybrid_conformer_conv_pool_encoder"


def rmsnorm(x, g):
    xf = x.astype(jnp.float32)
    y = xf * lax.rsqrt(jnp.mean(xf * xf, axis=-1, keepdims=True) + EPS)
    return (y * g.astype(jnp.float32)).astype(x.dtype)


def causal_dwconv(x, k):
    w, c = k.shape
    xp = jnp.pad(x, ((0, 0), (w - 1, 0), (0, 0)))
    return lax.conv_general_dilated(
        xp, k[:, None, :].astype(x.dtype), window_strides=(1,), padding="VALID",
        dimension_numbers=("NWC", "WIO", "NWC"), feature_group_count=c)


def conformer_conv_group(a, g, dw_k, dw_b, ln_g, ln_b):
    u = a * jax.nn.sigmoid(g)
    u = causal_dwconv(u, dw_k) + dw_b.astype(a.dtype)
    bsz, length, c = u.shape
    uh = u.reshape(bsz, length, CONV_HEADS, CONV_HEAD_DIM).astype(jnp.float32)
    mu = jnp.mean(uh, axis=-1, keepdims=True)
    var = jnp.mean(jnp.square(uh - mu), axis=-1, keepdims=True)
    uh = (uh - mu) * lax.rsqrt(var + EPS)
    u = uh.reshape(bsz, length, c) * ln_g.astype(jnp.float32) + ln_b.astype(jnp.float32)
    return jax.nn.silu(u).astype(a.dtype)


def multiscale_pool_group(p, pool_w, pool_scale):
    bsz, length, c = p.shape
    pf = p.astype(jnp.float32)
    cs = jnp.pad(jnp.cumsum(pf, axis=1), ((0, 0), (1, 0), (0, 0)))
    t = jnp.arange(length)
    outs = []
    for gi, w in enumerate(POOL_WINDOWS):
        sl = slice(gi * POOL_GROUP_DIM, (gi + 1) * POOL_GROUP_DIM)
        cg = cs[:, :, sl]
        upper = cg[:, 1:]
        lower = jnp.pad(cg[:, :length + 1 - w], ((0, 0), (w - 1, 0), (0, 0)))
        cnt = jnp.minimum(t + 1, w).astype(jnp.float32)[None, :, None]
        outs.append((upper - lower) / cnt - pf[:, :, sl])
    d = jnp.stack(outs, axis=2).astype(p.dtype)
    y = jnp.einsum("blgc,gcd->blgd", d, pool_w).reshape(bsz, length, c)
    return y * pool_scale


def _fwd_setup_inputs(seed: int = 0) -> dict:
    key = jax.random.key(seed)
    ks = jax.random.split(key, 16)
    f32 = jnp.float32
    nrm = lambda k, shape, s: (jax.random.normal(k, shape, f32) * s)
    return {
        "x": nrm(ks[0], (BATCH, SEQ, D_MODEL), 1.0),
        "meta_tokens": nrm(ks[1], (N_META, D_MODEL), 1.0),
        "norm1_g": 1.0 + nrm(ks[2], (DEPTH, D_MODEL), 0.02),
        "w_in": nrm(ks[3], (DEPTH, D_MODEL, IN_COLS), D_MODEL ** -0.5),
        "conv_dw_k": nrm(ks[4], (DEPTH, CONV_K, CONV_WIDTH_CH), CONV_K ** -0.5),
        "conv_dw_b": nrm(ks[5], (DEPTH, CONV_WIDTH_CH), 0.02),
        "conv_ln_g": 1.0 + nrm(ks[6], (DEPTH, CONV_WIDTH_CH), 0.02),
        "conv_ln_b": nrm(ks[7], (DEPTH, CONV_WIDTH_CH), 0.02),
        "pool_w": nrm(ks[8], (DEPTH, POOL_GROUPS, POOL_GROUP_DIM, POOL_GROUP_DIM), POOL_GROUP_DIM ** -0.5),
        "pool_scale": 1.0 + nrm(ks[9], (DEPTH, POOL_WIDTH_CH), 0.02),
        "w_out": nrm(ks[10], (DEPTH, MIX, D_MODEL), MIX ** -0.5),
        "norm2_g": 1.0 + nrm(ks[11], (DEPTH, D_MODEL), 0.02),
        "w_up": nrm(ks[12], (DEPTH, D_MODEL, 2 * D_FF), D_MODEL ** -0.5),
        "ffn_dw_k": nrm(ks[13], (DEPTH, FFN_CONV_K, 2 * D_FF), FFN_CONV_K ** -0.5),
        "w_down": nrm(ks[14], (DEPTH, D_FF, D_MODEL), D_FF ** -0.5),
        "final_g": 1.0 + nrm(ks[15], (D_MODEL,), 0.02),
    }


def _fwd_reference(x, meta_tokens, norm1_g, w_in, conv_dw_k, conv_dw_b, conv_ln_g, conv_ln_b,
              pool_w, pool_scale, w_out, norm2_g, w_up, ffn_dw_k, w_down, final_g):
    bsz = x.shape[0]
    meta = jnp.broadcast_to(meta_tokens[None].astype(x.dtype), (bsz, N_META, D_MODEL))
    h = jnp.concatenate([meta, x], axis=1)
    for i in range(DEPTH):
        hn = rmsnorm(h, norm1_g[i])
        z = hn @ w_in[i]
        a = z[..., :CONV_WIDTH_CH]
        g = z[..., CONV_WIDTH_CH:2 * CONV_WIDTH_CH]
        p = z[..., 2 * CONV_WIDTH_CH:]
        y_conv = conformer_conv_group(a, g, conv_dw_k[i], conv_dw_b[i], conv_ln_g[i], conv_ln_b[i])
        y_pool = multiscale_pool_group(p, pool_w[i], pool_scale[i])
        h = h + jnp.concatenate([y_conv, y_pool], axis=-1) @ w_out[i]
        hn = rmsnorm(h, norm2_g[i])
        ug = causal_dwconv(hn @ w_up[i], ffn_dw_k[i])
        gate, val = ug[..., :D_FF], ug[..., D_FF:]
        h = h + (jax.nn.silu(gate) * val) @ w_down[i]
    return rmsnorm(h, final_g)[:, N_META:]


import jax as _jax
import jax.numpy as _jnp

TWIN_FORMAT = 'train_step'
FWD_PARAMS = ['x', 'meta_tokens', 'norm1_g', 'w_in', 'conv_dw_k', 'conv_dw_b', 'conv_ln_g', 'conv_ln_b', 'pool_w', 'pool_scale', 'w_out', 'norm2_g', 'w_up', 'ffn_dw_k', 'w_down', 'final_g']
TWIN_WEIGHTS = ['meta_tokens', 'norm1_g', 'w_in', 'conv_dw_k', 'conv_dw_b', 'conv_ln_g', 'conv_ln_b', 'pool_w', 'pool_scale', 'w_out', 'norm2_g', 'w_up', 'ffn_dw_k', 'w_down', 'final_g']
TWIN_DIFF_INPUT = 'x'
TWIN_INPUTS = ['x', 'meta_tokens', 'norm1_g', 'w_in', 'conv_dw_k', 'conv_dw_b', 'conv_ln_g', 'conv_ln_b', 'pool_w', 'pool_scale', 'w_out', 'norm2_g', 'w_up', 'ffn_dw_k', 'w_down', 'final_g', 'loss_target', 'm_meta_tokens', 'm_norm1_g', 'm_w_in', 'm_conv_dw_k', 'm_conv_dw_b', 'm_conv_ln_g', 'm_conv_ln_b', 'm_pool_w', 'm_pool_scale', 'm_w_out', 'm_norm2_g', 'm_w_up', 'm_ffn_dw_k', 'm_w_down', 'm_final_g', 'v_meta_tokens', 'v_norm1_g', 'v_w_in', 'v_conv_dw_k', 'v_conv_dw_b', 'v_conv_ln_g', 'v_conv_ln_b', 'v_pool_w', 'v_pool_scale', 'v_w_out', 'v_norm2_g', 'v_w_up', 'v_ffn_dw_k', 'v_w_down', 'v_final_g']
TWIN_OUTPUTS = ['loss', 'grad_x', 'grad_meta_tokens', 'grad_norm1_g', 'grad_w_in', 'grad_conv_dw_k', 'grad_conv_dw_b', 'grad_conv_ln_g', 'grad_conv_ln_b', 'grad_pool_w', 'grad_pool_scale', 'grad_w_out', 'grad_norm2_g', 'grad_w_up', 'grad_ffn_dw_k', 'grad_w_down', 'grad_final_g', 'delta_meta_tokens', 'delta_norm1_g', 'delta_w_in', 'delta_conv_dw_k', 'delta_conv_dw_b', 'delta_conv_ln_g', 'delta_conv_ln_b', 'delta_pool_w', 'delta_pool_scale', 'delta_w_out', 'delta_norm2_g', 'delta_w_up', 'delta_ffn_dw_k', 'delta_w_down', 'delta_final_g', 'new_m_meta_tokens', 'new_m_norm1_g', 'new_m_w_in', 'new_m_conv_dw_k', 'new_m_conv_dw_b', 'new_m_conv_ln_g', 'new_m_conv_ln_b', 'new_m_pool_w', 'new_m_pool_scale', 'new_m_w_out', 'new_m_norm2_g', 'new_m_w_up', 'new_m_ffn_dw_k', 'new_m_w_down', 'new_m_final_g', 'new_v_meta_tokens', 'new_v_norm1_g', 'new_v_w_in', 'new_v_conv_dw_k', 'new_v_conv_dw_b', 'new_v_conv_ln_g', 'new_v_conv_ln_b', 'new_v_pool_w', 'new_v_pool_scale', 'new_v_w_out', 'new_v_norm2_g', 'new_v_w_up', 'new_v_ffn_dw_k', 'new_v_w_down', 'new_v_final_g']
TWIN_LEAF_KINDS = {'loss': 'loss', 'grad_x': 'grad_x', 'grad_meta_tokens': 'grad_w', 'grad_norm1_g': 'grad_w', 'grad_w_in': 'grad_w', 'grad_conv_dw_k': 'grad_w', 'grad_conv_dw_b': 'grad_w', 'grad_conv_ln_g': 'grad_w', 'grad_conv_ln_b': 'grad_w', 'grad_pool_w': 'grad_w', 'grad_pool_scale': 'grad_w', 'grad_w_out': 'grad_w', 'grad_norm2_g': 'grad_w', 'grad_w_up': 'grad_w', 'grad_ffn_dw_k': 'grad_w', 'grad_w_down': 'grad_w', 'grad_final_g': 'grad_w', 'delta_meta_tokens': 'delta_w', 'delta_norm1_g': 'delta_w', 'delta_w_in': 'delta_w', 'delta_conv_dw_k': 'delta_w', 'delta_conv_dw_b': 'delta_w', 'delta_conv_ln_g': 'delta_w', 'delta_conv_ln_b': 'delta_w', 'delta_pool_w': 'delta_w', 'delta_pool_scale': 'delta_w', 'delta_w_out': 'delta_w', 'delta_norm2_g': 'delta_w', 'delta_w_up': 'delta_w', 'delta_ffn_dw_k': 'delta_w', 'delta_w_down': 'delta_w', 'delta_final_g': 'delta_w', 'new_m_meta_tokens': 'new_m', 'new_m_norm1_g': 'new_m', 'new_m_w_in': 'new_m', 'new_m_conv_dw_k': 'new_m', 'new_m_conv_dw_b': 'new_m', 'new_m_conv_ln_g': 'new_m', 'new_m_conv_ln_b': 'new_m', 'new_m_pool_w': 'new_m', 'new_m_pool_scale': 'new_m', 'new_m_w_out': 'new_m', 'new_m_norm2_g': 'new_m', 'new_m_w_up': 'new_m', 'new_m_ffn_dw_k': 'new_m', 'new_m_w_down': 'new_m', 'new_m_final_g': 'new_m', 'new_v_meta_tokens': 'new_v', 'new_v_norm1_g': 'new_v', 'new_v_w_in': 'new_v', 'new_v_conv_dw_k': 'new_v', 'new_v_conv_dw_b': 'new_v', 'new_v_conv_ln_g': 'new_v', 'new_v_conv_ln_b': 'new_v', 'new_v_pool_w': 'new_v', 'new_v_pool_scale': 'new_v', 'new_v_w_out': 'new_v', 'new_v_norm2_g': 'new_v', 'new_v_w_up': 'new_v', 'new_v_ffn_dw_k': 'new_v', 'new_v_w_down': 'new_v', 'new_v_final_g': 'new_v'}


def _forward(args):
    return _fwd_reference(*[args[k] for k in FWD_PARAMS])


def _output_shape():
    def fwd():
        inp = _fwd_setup_inputs(0)
        return _fwd_reference(*[inp[k] for k in FWD_PARAMS])
    out = _jax.eval_shape(fwd)
    return out.shape, out.dtype

N_MICROBATCH = 1
ADAM_LR = 0.001
ADAM_B1 = 0.9
ADAM_B2 = 0.999
ADAM_EPS = 1e-08
ADAM_WD = 0.01
ADAM_STEP = 10
PER_EXAMPLE_BATCH_AXIS = {'x': 0, 'loss_target': 0}
SHARED_INPUTS = []
_WEIGHT_DTYPES = {'meta_tokens': _jnp.float32, 'norm1_g': _jnp.float32, 'w_in': _jnp.float32, 'conv_dw_k': _jnp.float32, 'conv_dw_b': _jnp.float32, 'conv_ln_g': _jnp.float32, 'conv_ln_b': _jnp.float32, 'pool_w': _jnp.float32, 'pool_scale': _jnp.float32, 'w_out': _jnp.float32, 'norm2_g': _jnp.float32, 'w_up': _jnp.float32, 'ffn_dw_k': _jnp.float32, 'w_down': _jnp.float32, 'final_g': _jnp.float32}
MOMENT_SCALE = {'meta_tokens': 9.248630e-03, 'norm1_g': 1.614528e-01, 'w_in': 1.322764e-01, 'conv_dw_k': 1.298677e-01, 'conv_dw_b': 2.645510e-01, 'conv_ln_g': 1.546947e-01, 'conv_ln_b': 1.433673e-01, 'pool_w': 1.830504e-01, 'pool_scale': 1.937109e-01, 'w_out': 1.570076e-01, 'norm2_g': 1.621050e-01, 'w_up': 6.547581e-02, 'ffn_dw_k': 6.629134e-02, 'w_down': 1.069834e-01, 'final_g': 6.399299e+01}


def _to_microbatches(a, axis):
    t = _jnp.moveaxis(a, axis, 0)
    t = t.reshape((N_MICROBATCH, t.shape[0] // N_MICROBATCH) + t.shape[1:])
    return _jnp.moveaxis(t, 1, axis + 1)


def setup_inputs(seed: int = 0) -> dict:
    inp = _fwd_setup_inputs(seed)
    key = _jax.random.fold_in(_jax.random.key(seed), 7919)
    shape, _ = _output_shape()
    out = dict(inp)
    out["loss_target"] = _jax.random.normal(_jax.random.fold_in(key, 0), shape, _jnp.float32)
    for i, name in enumerate(TWIN_WEIGHTS):
        w = inp[name].astype(_jnp.float32)
        if MOMENT_SCALE is None:
            s = _jnp.sqrt(_jnp.mean(_jnp.square(w)) + 1e-30)
        else:
            s = MOMENT_SCALE[name]
        km, kv = _jax.random.split(_jax.random.fold_in(key, i + 1))
        out[name] = w
        out["m_" + name] = s * _jax.random.normal(km, w.shape, _jnp.float32)
        out["v_" + name] = (s * s) * _jax.random.uniform(kv, w.shape, _jnp.float32, 0.5, 1.5)
    if N_MICROBATCH > 1:
        for name, axis in PER_EXAMPLE_BATCH_AXIS.items():
            out[name] = _to_microbatches(out[name], axis)
    return {'x': out['x'], 'meta_tokens': out['meta_tokens'], 'norm1_g': out['norm1_g'], 'w_in': out['w_in'], 'conv_dw_k': out['conv_dw_k'], 'conv_dw_b': out['conv_dw_b'], 'conv_ln_g': out['conv_ln_g'], 'conv_ln_b': out['conv_ln_b'], 'pool_w': out['pool_w'], 'pool_scale': out['pool_scale'], 'w_out': out['w_out'], 'norm2_g': out['norm2_g'], 'w_up': out['w_up'], 'ffn_dw_k': out['ffn_dw_k'], 'w_down': out['w_down'], 'final_g': out['final_g'], 'loss_target': out['loss_target'], 'm_meta_tokens': out['m_meta_tokens'], 'm_norm1_g': out['m_norm1_g'], 'm_w_in': out['m_w_in'], 'm_conv_dw_k': out['m_conv_dw_k'], 'm_conv_dw_b': out['m_conv_dw_b'], 'm_conv_ln_g': out['m_conv_ln_g'], 'm_conv_ln_b': out['m_conv_ln_b'], 'm_pool_w': out['m_pool_w'], 'm_pool_scale': out['m_pool_scale'], 'm_w_out': out['m_w_out'], 'm_norm2_g': out['m_norm2_g'], 'm_w_up': out['m_w_up'], 'm_ffn_dw_k': out['m_ffn_dw_k'], 'm_w_down': out['m_w_down'], 'm_final_g': out['m_final_g'], 'v_meta_tokens': out['v_meta_tokens'], 'v_norm1_g': out['v_norm1_g'], 'v_w_in': out['v_w_in'], 'v_conv_dw_k': out['v_conv_dw_k'], 'v_conv_dw_b': out['v_conv_dw_b'], 'v_conv_ln_g': out['v_conv_ln_g'], 'v_conv_ln_b': out['v_conv_ln_b'], 'v_pool_w': out['v_pool_w'], 'v_pool_scale': out['v_pool_scale'], 'v_w_out': out['v_w_out'], 'v_norm2_g': out['v_norm2_g'], 'v_w_up': out['v_w_up'], 'v_ffn_dw_k': out['v_ffn_dw_k'], 'v_w_down': out['v_w_down'], 'v_final_g': out['v_final_g']}


def _loss(weights, diff, rest, loss_target):
    with _jax.named_scope("forward"):
        args = {**rest, TWIN_DIFF_INPUT: diff, **{k: w.astype(_WEIGHT_DTYPES[k]) for k, w in weights.items()}}
        y = _forward(args)
    with _jax.named_scope("loss_head"):
        err = _jnp.square(y.astype(_jnp.float32) - loss_target)
        return 0.5 * _jnp.sum(_jnp.mean(err, axis=-1)) if err.ndim else 0.5 * err


def _adamw(w, g, m, v):
    m = ADAM_B1 * m + (1.0 - ADAM_B1) * g
    v = ADAM_B2 * v + (1.0 - ADAM_B2) * _jnp.square(g)
    m_hat = m / (1.0 - ADAM_B1 ** ADAM_STEP)
    v_hat = v / (1.0 - ADAM_B2 ** ADAM_STEP)
    delta = -ADAM_LR * (m_hat / (_jnp.sqrt(v_hat) + ADAM_EPS) + ADAM_WD * w)
    return delta, m, v


def reference(x, meta_tokens, norm1_g, w_in, conv_dw_k, conv_dw_b, conv_ln_g, conv_ln_b, pool_w, pool_scale, w_out, norm2_g, w_up, ffn_dw_k, w_down, final_g, loss_target, m_meta_tokens, m_norm1_g, m_w_in, m_conv_dw_k, m_conv_dw_b, m_conv_ln_g, m_conv_ln_b, m_pool_w, m_pool_scale, m_w_out, m_norm2_g, m_w_up, m_ffn_dw_k, m_w_down, m_final_g, v_meta_tokens, v_norm1_g, v_w_in, v_conv_dw_k, v_conv_dw_b, v_conv_ln_g, v_conv_ln_b, v_pool_w, v_pool_scale, v_w_out, v_norm2_g, v_w_up, v_ffn_dw_k, v_w_down, v_final_g):
    given = dict(x=x, meta_tokens=meta_tokens, norm1_g=norm1_g, w_in=w_in, conv_dw_k=conv_dw_k, conv_dw_b=conv_dw_b, conv_ln_g=conv_ln_g, conv_ln_b=conv_ln_b, pool_w=pool_w, pool_scale=pool_scale, w_out=w_out, norm2_g=norm2_g, w_up=w_up, ffn_dw_k=ffn_dw_k, w_down=w_down, final_g=final_g, loss_target=loss_target, m_meta_tokens=m_meta_tokens, m_norm1_g=m_norm1_g, m_w_in=m_w_in, m_conv_dw_k=m_conv_dw_k, m_conv_dw_b=m_conv_dw_b, m_conv_ln_g=m_conv_ln_g, m_conv_ln_b=m_conv_ln_b, m_pool_w=m_pool_w, m_pool_scale=m_pool_scale, m_w_out=m_w_out, m_norm2_g=m_norm2_g, m_w_up=m_w_up, m_ffn_dw_k=m_ffn_dw_k, m_w_down=m_w_down, m_final_g=m_final_g, v_meta_tokens=v_meta_tokens, v_norm1_g=v_norm1_g, v_w_in=v_w_in, v_conv_dw_k=v_conv_dw_k, v_conv_dw_b=v_conv_dw_b, v_conv_ln_g=v_conv_ln_g, v_conv_ln_b=v_conv_ln_b, v_pool_w=v_pool_w, v_pool_scale=v_pool_scale, v_w_out=v_w_out, v_norm2_g=v_norm2_g, v_w_up=v_w_up, v_ffn_dw_k=v_ffn_dw_k, v_w_down=v_w_down, v_final_g=v_final_g)
    weights = {n: given[n] for n in TWIN_WEIGHTS}
    shared = {n: given[n] for n in SHARED_INPUTS}
    per_example = {n: given[n] for n in ['x']}
    grad_fn = _jax.value_and_grad(_loss, argnums=(0, 1))

    def one_microbatch(ex, loss_target):
        ex = dict(ex)
        diff = ex.pop(TWIN_DIFF_INPUT)
        return grad_fn(weights, diff, {**shared, **ex}, loss_target)

    if N_MICROBATCH == 1:
        loss, (grad_w, grad_x) = one_microbatch(per_example, given["loss_target"])
    else:
        def body(carry, xs):
            loss_sum, grad_sum = carry
            l_k, (gw_k, gx_k) = one_microbatch(xs[0], xs[1])
            with _jax.named_scope("update"):
                return (loss_sum + l_k, _jax.tree.map(_jnp.add, grad_sum, gw_k)), gx_k

        init = (_jnp.zeros((), _jnp.float32), _jax.tree.map(_jnp.zeros_like, weights))
        (loss, grad_w), grad_x = _jax.lax.scan(body, init, (per_example, given["loss_target"]))
    with _jax.named_scope("update"):
        delta_w, new_m, new_v = {}, {}, {}
        for n in TWIN_WEIGHTS:
            delta_w[n], new_m[n], new_v[n] = _adamw(weights[n], grad_w[n], given["m_" + n], given["v_" + n])
    return (loss, grad_x, *[grad_w[n] for n in TWIN_WEIGHTS], *[delta_w[n] for n in TWIN_WEIGHTS],
            *[new_m[n] for n in TWIN_WEIGHTS], *[new_v[n] for n in TWIN_WEIGHTS])
```

```python
import functools

import jax
import jax.numpy as jnp
from jax import lax
from jax.experimental import pallas as pl
from jax.experimental.pallas import tpu as pltpu

F32 = jnp.float32
BF16 = jnp.bfloat16

D_MODEL = 1024
CONV_W = 512
POOL_W = 512
POOL_G = 128
POOL_WINDOWS = (2, 4, 8, 16)
IN_COLS = 1536
D_FF = 2816
FF_CHUNK = 1408
CONV_TAPS = 31
CONV_HIST = 32
POOL_HIST = 16
FFN_TAPS = 3
N_META = 16
EPS = 1e-6

ADAM_LR = 0.001
ADAM_B1 = 0.9
ADAM_B2 = 0.999
ADAM_EPS = 1e-08
ADAM_WD = 0.01
ADAM_STEP = 10

ROW_TILE = 256
CONV_ROW_BLOCK = 64
V7X_VMEM_LIMIT = 56 * 1024 * 1024

MESH = pl.DeviceIdType.MESH


def _cparams(n_axes, vmem=None):
    return pltpu.CompilerParams(dimension_semantics=("arbitrary",) * n_axes, vmem_limit_bytes=vmem)


def _whole(shape, single=False):
    zeros = (0,) * len(shape)
    if single:
        return pl.BlockSpec(shape, lambda *_: zeros, pipeline_mode=pl.Buffered(1))
    return pl.BlockSpec(shape, lambda *_: zeros)


def _sigmoid(x):
    return 1.0 / (1.0 + jnp.exp(-x))


def _dot(a, b):
    return jnp.dot(a, b, preferred_element_type=F32)


def _dot_tn(a, b):
    return lax.dot_general(a, b, (((0,), (0,)), ((), ())), preferred_element_type=F32)


def _split_dot(v, a_ref):
    hi = v.astype(BF16)
    lo = (v - hi.astype(F32)).astype(BF16)
    return _dot(hi, a_ref[...]) + _dot(lo, a_ref[...])


def _rms(x):
    r = lax.rsqrt(jnp.mean(x * x, axis=-1, keepdims=True) + EPS)
    return x * r, r


def _rms_bwd(dy, xhat, r, g):
    gd = dy * g
    return r * (gd - xhat * jnp.mean(gd * xhat, axis=-1, keepdims=True)), dy * xhat


def _colsum(v):
    return jnp.sum(v, axis=0, keepdims=True)


def _shifted(window, s):
    return window if s == 0 else pltpu.roll(window, window.shape[0] - s, 0)


def _inv_count(tile, tm, w):
    t = tile * tm + lax.broadcasted_iota(jnp.int32, (tm, POOL_G), 0)
    return 1.0 / jnp.minimum(t + 1, w).astype(F32)


def _norm_first(h, g, tm):
    rows = h.shape[0]

    def body(h_ref, g_ref, o_ref):
        xhat, _ = _rms(h_ref[...])
        o_ref[...] = (xhat * g_ref[...]).astype(BF16)

    return pl.pallas_call(
        body, name="norm_first", grid=(rows // tm,),
        in_specs=[pl.BlockSpec((tm, D_MODEL), lambda i: (i, 0)), _whole((1, D_MODEL))],
        out_specs=pl.BlockSpec((tm, D_MODEL), lambda i: (i, 0)),
        out_shape=jax.ShapeDtypeStruct((rows, D_MODEL), BF16),
        compiler_params=_cparams(1),
    )(h, g)


def _row_block(r):
    for cand in (256, 176, 128, 64, 32, 16):
        if r % cand == 0:
            return cand
    return r


def _cast_bf16(w, name):
    n, r, c = w.shape
    rb = _row_block(r)

    def body(w_ref, o_ref):
        o_ref[...] = w_ref[...].astype(BF16)

    spec = pl.BlockSpec((1, rb, c), lambda i, j: (i, j, 0))
    return pl.pallas_call(
        body, name=name, grid=(n, r // rb), in_specs=[spec], out_specs=spec,
        out_shape=jax.ShapeDtypeStruct(w.shape, BF16), compiler_params=_cparams(2),
    )(w)


def _pair_sum_bf16(core, g0, g1, other, name):
    n, r, c = g0.shape
    rb = _row_block(r)

    def body(core_ref, g0_ref, g1_ref, o_ref, out_ref):
        mine = jnp.where(core_ref[0] == 0, g0_ref[...], g1_ref[...])
        out_ref[...] = (mine + o_ref[...]).astype(BF16)

    spec = pl.BlockSpec((1, rb, c), lambda i, j: (i, j, 0))
    return pl.pallas_call(
        body, name=name, grid=(n, r // rb),
        in_specs=[pl.BlockSpec(memory_space=pltpu.SMEM), spec, spec, spec], out_specs=spec,
        out_shape=jax.ShapeDtypeStruct(g0.shape, BF16), compiler_params=_cparams(2),
    )(core, g0, g1, other)


def _chip_sum(parts, name):
    _, r, c = parts.shape
    rb = _row_block(r)

    def body(p_ref, out_ref):
        p = p_ref[...].astype(F32)
        out_ref[...] = (p[0] + p[1]) + (p[2] + p[3])

    return pl.pallas_call(
        body, name=name, grid=(r // rb,),
        in_specs=[pl.BlockSpec((4, rb, c), lambda j: (0, j, 0))],
        out_specs=pl.BlockSpec((rb, c), lambda j: (j, 0)),
        out_shape=jax.ShapeDtypeStruct((r, c), F32), compiler_params=_cparams(1),
    )(parts)


def _adamw(w, g, m, v, name):
    n, r, c = w.shape
    rb = _row_block(r)
    c1 = 1.0 - ADAM_B1 ** ADAM_STEP
    c2 = 1.0 - ADAM_B2 ** ADAM_STEP

    def body(w_ref, g_ref, m_ref, v_ref, d_ref, nm_ref, nv_ref):
        gg = g_ref[...]
        nm = ADAM_B1 * m_ref[...] + (1.0 - ADAM_B1) * gg
        nv = ADAM_B2 * v_ref[...] + (1.0 - ADAM_B2) * (gg * gg)
        d_ref[...] = -ADAM_LR * ((nm / c1) / (jnp.sqrt(nv / c2) + ADAM_EPS) + ADAM_WD * w_ref[...])
        nm_ref[...] = nm
        nv_ref[...] = nv

    spec = pl.BlockSpec((1, rb, c), lambda i, j: (i, j, 0))
    shp = jax.ShapeDtypeStruct(w.shape, F32)
    return pl.pallas_call(
        body, name=name, grid=(n, r // rb), in_specs=[spec] * 4, out_specs=[spec] * 3,
        out_shape=[shp] * 3, compiler_params=_cparams(2),
    )(w, g, m, v)


def _mixer_fwd(h0, hn, w_in, conv_k, conv_v, avg, pool_w, w_out, g2, tm):
    rows = h0.shape[0]
    rb = CONV_ROW_BLOCK

    def body(h0_ref, hn_ref, win_ref, ck_ref, cv_ref, avg_ref, pw_ref, wout_ref, g2_ref,
             z_ref, u1_ref, m_ref, h1_ref, hn2_ref, ubuf, pbuf):
        i = pl.program_id(0)

        @pl.when(i == 0)
        def _():
            ubuf[pl.ds(0, CONV_HIST), :] = jnp.zeros((CONV_HIST, CONV_W), F32)
            pbuf[pl.ds(0, POOL_HIST), :] = jnp.zeros((POOL_HIST, POOL_W), F32)

        z = _dot(hn_ref[...], win_ref[...])
        z_ref[...] = z
        ubuf[pl.ds(CONV_HIST, tm), :] = z[:, :CONV_W] * _sigmoid(z[:, CONV_W:2 * CONV_W])
        p = z[:, 2 * CONV_W:]
        pbuf[pl.ds(POOL_HIST, tm), :] = p

        def conv_block(r, carry):
            r0 = pl.multiple_of(r * rb, rb)
            for l in range(CONV_W // 128):
                ls = pl.ds(128 * l, 128)
                window = ubuf[pl.ds(r0, rb + CONV_HIST), ls]
                acc = jnp.broadcast_to(cv_ref[0:1, ls], (rb, 128))
                for s in range(8):
                    ws = _shifted(window, s)
                    for q in range(CONV_HIST // 8 + 1):
                        j = 8 * q + s - 2
                        if 0 <= j < CONV_TAPS:
                            acc = acc + ck_ref[j:j + 1, ls] * ws[8 * q:8 * q + rb]
                u1_ref[pl.ds(r0, rb), ls] = acc
            return carry

        lax.fori_loop(0, tm // rb, conv_block, 0)
        ubuf[pl.ds(0, CONV_HIST), :] = ubuf[pl.ds(tm, CONV_HIST), :]

        u1 = u1_ref[...]
        cen = u1 - _split_dot(u1, avg_ref)
        xhat = cen * lax.rsqrt(_split_dot(cen * cen, avg_ref) + EPS)
        u2 = xhat * cv_ref[1:2, :] + cv_ref[2:3, :]
        m_ref[:, 0:CONV_W] = (u2 * _sigmoid(u2)).astype(BF16)

        for gi, w in enumerate(POOL_WINDOWS):
            ls = pl.ds(POOL_G * gi, POOL_G)
            s = pbuf[pl.ds(POOL_HIST, tm), ls]
            for j in range(1, w):
                s = s + pbuf[pl.ds(POOL_HIST - j, tm), ls]
            d = s * _inv_count(i, tm, w) - p[:, POOL_G * gi:POOL_G * (gi + 1)]
            y = _dot(d.astype(BF16), pw_ref[gi]) * cv_ref[3:4, ls]
            m_ref[:, pl.ds(CONV_W + POOL_G * gi, POOL_G)] = y.astype(BF16)
        pbuf[pl.ds(0, POOL_HIST), :] = pbuf[pl.ds(tm, POOL_HIST), :]

        h1 = h0_ref[...] + _dot(m_ref[...], wout_ref[...])
        h1_ref[...] = h1
        xh, _ = _rms(h1)
        hn2_ref[...] = (xh * g2_ref[...]).astype(BF16)

    def tile(c):
        return pl.BlockSpec((tm, c), lambda i: (i, 0))

    return pl.pallas_call(
        body, name="mixer_fwd", grid=(rows // tm,),
        in_specs=[tile(D_MODEL), tile(D_MODEL), _whole((D_MODEL, IN_COLS)), _whole((CONV_HIST, CONV_W)),
                  _whole((8, CONV_W)), _whole((CONV_W, CONV_W)), _whole((4, POOL_G, POOL_G)),
                  _whole((D_MODEL, D_MODEL)), _whole((1, D_MODEL))],
        out_specs=[tile(IN_COLS), tile(CONV_W), tile(D_MODEL), tile(D_MODEL), tile(D_MODEL)],
        out_shape=[jax.ShapeDtypeStruct((rows, IN_COLS), F32), jax.ShapeDtypeStruct((rows, CONV_W), F32),
                   jax.ShapeDtypeStruct((rows, D_MODEL), BF16), jax.ShapeDtypeStruct((rows, D_MODEL), F32),
                   jax.ShapeDtypeStruct((rows, D_MODEL), BF16)],
        scratch_shapes=[pltpu.VMEM((CONV_HIST + tm, CONV_W), F32), pltpu.VMEM((POOL_HIST + tm, POOL_W), F32)],
        compiler_params=_cparams(1, V7X_VMEM_LIMIT),
    )(h0, hn, w_in, conv_k, conv_v, avg, pool_w, w_out, g2)


def _ffn_fwd(h1, hn2, w_up, kf, w_down, g_next, tm):
    rows = h1.shape[0]
    hist = 8

    def body(h1_ref, hn2_ref, wup_ref, kf_ref, wdn_ref, gn_ref, ug_ref, h2_ref, hnn_ref, wg, wv, carry, acc):
        i = pl.program_id(0)

        @pl.when(i == 0)
        def _():
            carry[...] = jnp.zeros(carry.shape, F32)

        acc[...] = h1_ref[...]
        for c in range(2):
            for buf, cc in ((wg, c), (wv, c + 2)):
                ug = _dot(hn2_ref[...], wup_ref[cc])
                ug_ref[cc] = ug.astype(BF16)
                buf[pl.ds(0, hist), :] = carry[cc]
                buf[pl.ds(hist, tm), :] = ug
                carry[cc] = buf[pl.ds(tm, hist), :]
            gate = kf_ref[c, 0:1, :] * wg[pl.ds(hist - 2, tm), :]
            val = kf_ref[c + 2, 0:1, :] * wv[pl.ds(hist - 2, tm), :]
            for j in range(1, FFN_TAPS):
                gate = gate + kf_ref[c, j:j + 1, :] * wg[pl.ds(hist - 2 + j, tm), :]
                val = val + kf_ref[c + 2, j:j + 1, :] * wv[pl.ds(hist - 2 + j, tm), :]
            act = gate * _sigmoid(gate) * val
            acc[...] += _dot(act.astype(BF16), wdn_ref[c])
        h2 = acc[...]
        h2_ref[...] = h2
        xh, _ = _rms(h2)
        hnn_ref[...] = (xh * gn_ref[...]).astype(BF16)

    def tile(c):
        return pl.BlockSpec((tm, c), lambda i: (i, 0))

    return pl.pallas_call(
        body, name="ffn_fwd", grid=(rows // tm,),
        in_specs=[tile(D_MODEL), tile(D_MODEL), _whole((4, D_MODEL, FF_CHUNK), single=True), _whole((4, 8, FF_CHUNK)),
                  _whole((2, FF_CHUNK, D_MODEL), single=True), _whole((1, D_MODEL))],
        out_specs=[pl.BlockSpec((4, tm, FF_CHUNK), lambda i: (0, i, 0)), tile(D_MODEL), tile(D_MODEL)],
        out_shape=[jax.ShapeDtypeStruct((4, rows, FF_CHUNK), BF16), jax.ShapeDtypeStruct((rows, D_MODEL), F32),
                   jax.ShapeDtypeStruct((rows, D_MODEL), BF16)],
        scratch_shapes=[pltpu.VMEM((hist + tm, FF_CHUNK), F32), pltpu.VMEM((hist + tm, FF_CHUNK), F32),
                        pltpu.VMEM((4, hist, FF_CHUNK), F32), pltpu.VMEM((tm, D_MODEL), F32)],
        compiler_params=_cparams(1, V7X_VMEM_LIMIT),
    )(h1, hn2, w_up, kf, w_down, g_next)


def _loss_head(h, tgt, g, seq, tm):
    rows = h.shape[0]

    def body(h_ref, t_ref, g_ref, dh_ref, loss_ref, dg_ref):
        i = pl.program_id(0)

        @pl.when(i == 0)
        def _():
            loss_ref[...] = jnp.zeros(loss_ref.shape, F32)
            dg_ref[...] = jnp.zeros(dg_ref.shape, F32)

        t = i * tm + lax.broadcasted_iota(jnp.int32, (tm, 1), 0)
        mask = jnp.logical_and(t >= N_META, t < N_META + seq).astype(F32)
        xhat, r = _rms(h_ref[...])
        err = (xhat * g_ref[...] - t_ref[...]) * mask
        loss_ref[...] += _colsum(err * err)
        dh, dg_rows = _rms_bwd(err * (1.0 / D_MODEL), xhat, r, g_ref[...])
        dh_ref[...] = dh
        dg_ref[...] += _colsum(dg_rows)

    tile = pl.BlockSpec((tm, D_MODEL), lambda i: (i, 0))
    vec = jax.ShapeDtypeStruct((1, D_MODEL), F32)
    return pl.pallas_call(
        body, name="loss_head", grid=(rows // tm,),
        in_specs=[tile, tile, _whole((1, D_MODEL))],
        out_specs=[tile, _whole((1, D_MODEL)), _whole((1, D_MODEL))],
        out_shape=[jax.ShapeDtypeStruct((rows, D_MODEL), F32), vec, vec],
        compiler_params=_cparams(1),
    )(h, tgt, g)


def _ffn_bwd(dh2, h1, ug0, w_down_t, w_up_t, kf, g2, tm):
    rows = dh2.shape[0]
    nt = rows // tm
    hist = 16
    fut = 8

    def body(dh2_ref, h1_ref, ug_ref, halo_ref, wdt_ref, wut_ref, kf_ref, g2_ref,
             dh1_ref, dug_ref, act_ref, dkf_ref, dg2_ref, wg, wv, dgb, dvb, carry, acc):
        i = pl.program_id(0)
        first_tile = jnp.where(i == nt - 1, 1.0, 0.0)

        @pl.when(i == 0)
        def _():
            carry[...] = jnp.zeros(carry.shape, F32)
            dkf_ref[...] = jnp.zeros(dkf_ref.shape, F32)
            dg2_ref[...] = jnp.zeros(dg2_ref.shape, F32)

        dh2b = dh2_ref[...].astype(BF16)
        acc[...] = jnp.zeros(acc.shape, F32)
        for c in range(2):
            for buf, cc in ((wg, c), (wv, c + 2)):
                buf[pl.ds(0, hist), :] = halo_ref[cc].astype(F32) * (1.0 - first_tile)
                buf[pl.ds(hist, tm), :] = ug_ref[cc].astype(F32)
            gate = kf_ref[c, 0:1, :] * wg[pl.ds(hist - 2, tm), :]
            val = kf_ref[c + 2, 0:1, :] * wv[pl.ds(hist - 2, tm), :]
            for j in range(1, FFN_TAPS):
                gate = gate + kf_ref[c, j:j + 1, :] * wg[pl.ds(hist - 2 + j, tm), :]
                val = val + kf_ref[c + 2, j:j + 1, :] * wv[pl.ds(hist - 2 + j, tm), :]
            sg = _sigmoid(gate)
            silu = gate * sg
            act_ref[c] = (silu * val).astype(BF16)
            dact = _dot(dh2b, wdt_ref[c])
            dgate = dact * val * (sg * (1.0 + gate * (1.0 - sg)))
            dval = dact * silu
            for buf, cc, dv, src in ((dgb, c, dgate, wg), (dvb, c + 2, dval, wv)):
                buf[pl.ds(0, tm), :] = dv
                buf[pl.ds(tm, fut), :] = carry[cc]
                carry[cc] = buf[pl.ds(0, fut), :]
                dug0 = kf_ref[cc, 0:1, :] * buf[pl.ds(2, tm), :]
                for j in range(1, FFN_TAPS):
                    dug0 = dug0 + kf_ref[cc, j:j + 1, :] * buf[pl.ds(2 - j, tm), :]
                for j in range(FFN_TAPS):
                    dkf_ref[cc, j:j + 1, :] += _colsum(dv * src[pl.ds(hist - 2 + j, tm), :])
                dug0b = dug0.astype(BF16)
                dug_ref[cc] = dug0b
                acc[...] += _dot(dug0b, wut_ref[cc])
        xhat, r = _rms(h1_ref[...])
        dx, dg_rows = _rms_bwd(acc[...], xhat, r, g2_ref[...])
        dh1_ref[...] = dh2_ref[...] + dx
        dg2_ref[...] += _colsum(dg_rows)

    def tile(c):
        return pl.BlockSpec((tm, c), lambda i: (nt - 1 - i, 0))

    def chunks(n):
        return pl.BlockSpec((n, tm, FF_CHUNK), lambda i: (0, nt - 1 - i, 0))

    halo = pl.BlockSpec((4, hist, FF_CHUNK), lambda i: (0, jnp.maximum((nt - 1 - i) * (tm // hist) - 1, 0), 0))
    return pl.pallas_call(
        body, name="ffn_bwd", grid=(nt,),
        in_specs=[tile(D_MODEL), tile(D_MODEL), chunks(4), halo, _whole((2, D_MODEL, FF_CHUNK), single=True),
                  _whole((4, FF_CHUNK, D_MODEL), single=True), _whole((4, 8, FF_CHUNK)), _whole((1, D_MODEL))],
        out_specs=[tile(D_MODEL), chunks(4), chunks(2), _whole((4, 8, FF_CHUNK)), _whole((1, D_MODEL))],
        out_shape=[jax.ShapeDtypeStruct((rows, D_MODEL), F32), jax.ShapeDtypeStruct((4, rows, FF_CHUNK), BF16),
                   jax.ShapeDtypeStruct((2, rows, FF_CHUNK), BF16), jax.ShapeDtypeStruct((4, 8, FF_CHUNK), F32),
                   jax.ShapeDtypeStruct((1, D_MODEL), F32)],
        scratch_shapes=[pltpu.VMEM((hist + tm, FF_CHUNK), F32), pltpu.VMEM((hist + tm, FF_CHUNK), F32),
                        pltpu.VMEM((tm + fut, FF_CHUNK), F32), pltpu.VMEM((tm + fut, FF_CHUNK), F32),
                        pltpu.VMEM((4, fut, FF_CHUNK), F32), pltpu.VMEM((tm, D_MODEL), F32)],
        compiler_params=_cparams(1, V7X_VMEM_LIMIT),
    )(dh2, h1, ug0, ug0, w_down_t, w_up_t, kf, g2)


def _mixer_bwd(dh1, h0, z, u1, w_out_t, w_in_t, conv_k, conv_v, avg, pool_w, pool_w_t, g1, tm):
    rows = dh1.shape[0]
    nt = rows // tm
    rb = CONV_ROW_BLOCK

    def body(dh1_ref, h0_ref, z_ref, zh_ref, u1_ref, wot_ref, wit_ref, ck_ref, cv_ref, avg_ref, pw_ref, pwt_ref, g1_ref,
             dh0_ref, dz_ref, dk_ref, ds_ref, dpw_ref, dg1_ref, ubuf, dbuf, pbuf, ebuf, dcarry, ecarry, dkacc, dzs):
        i = pl.program_id(0)
        ti = nt - 1 - i
        has_past = jnp.where(ti > 0, 1.0, 0.0)

        @pl.when(i == 0)
        def _():
            dcarry[...] = jnp.zeros(dcarry.shape, F32)
            ecarry[...] = jnp.zeros(ecarry.shape, F32)
            dkacc[...] = jnp.zeros(dkacc.shape, F32)
            ds_ref[...] = jnp.zeros(ds_ref.shape, F32)
            dpw_ref[...] = jnp.zeros(dpw_ref.shape, F32)
            dg1_ref[...] = jnp.zeros(dg1_ref.shape, F32)

        dm = _dot(dh1_ref[...].astype(BF16), wot_ref[...])
        z = z_ref[...]
        a = z[:, :CONV_W]
        sg = _sigmoid(z[:, CONV_W:2 * CONV_W])
        p = z[:, 2 * CONV_W:]
        zh = zh_ref[...] * has_past
        ubuf[pl.ds(0, CONV_HIST), :] = zh[:, :CONV_W] * _sigmoid(zh[:, CONV_W:2 * CONV_W])
        ubuf[pl.ds(CONV_HIST, tm), :] = a * sg
        pbuf[pl.ds(0, POOL_HIST), :] = zh[CONV_HIST - POOL_HIST:, 2 * CONV_W:]
        pbuf[pl.ds(POOL_HIST, tm), :] = p

        u1 = u1_ref[...]
        cen = u1 - _split_dot(u1, avg_ref)
        rstd = lax.rsqrt(_split_dot(cen * cen, avg_ref) + EPS)
        xhat = cen * rstd
        u2 = xhat * cv_ref[1:2, :] + cv_ref[2:3, :]
        s2 = _sigmoid(u2)
        du2 = dm[:, :CONV_W] * (s2 * (1.0 + u2 * (1.0 - s2)))
        ds_ref[1:2, :] += _colsum(du2 * xhat)
        ds_ref[2:3, :] += _colsum(du2)
        dxh = du2 * cv_ref[1:2, :]
        du1 = rstd * (dxh - _split_dot(dxh, avg_ref) - xhat * _split_dot(dxh * xhat, avg_ref))
        ds_ref[0:1, :] += _colsum(du1)
        dbuf[pl.ds(0, tm), :] = du1
        dbuf[pl.ds(tm, CONV_HIST), :] = dcarry[...]
        dcarry[...] = dbuf[pl.ds(0, CONV_HIST), :]

        def conv_block(r, carry):
            r0 = pl.multiple_of(r * rb, rb)
            for l in range(CONV_W // 128):
                ls = pl.ds(128 * l, 128)
                dwin = dbuf[pl.ds(r0, rb + CONV_HIST), ls]
                uwin = ubuf[pl.ds(r0, rb + CONV_HIST), ls]
                dblk = dwin[0:rb]
                du0 = jnp.zeros((rb, 128), F32)
                for s in range(8):
                    ds_ = _shifted(dwin, s)
                    us_ = _shifted(uwin, s)
                    for q in range(CONV_HIST // 8 + 1):
                        o = 8 * q + s
                        if 0 <= CONV_TAPS - 1 - o < CONV_TAPS:
                            j = CONV_TAPS - 1 - o
                            du0 = du0 + ck_ref[j:j + 1, ls] * ds_[8 * q:8 * q + rb]
                        j = o - 2
                        if 0 <= j < CONV_TAPS:
                            prod = dblk * us_[8 * q:8 * q + rb]
                            part = prod[0:8]
                            for v in range(1, rb // 8):
                                part = part + prod[8 * v:8 * v + 8]
                            dkacc[pl.ds(8 * j, 8), ls] += part
                dzs[pl.ds(r0, rb), ls] = du0
            return carry

        lax.fori_loop(0, tm // rb, conv_block, 0)
        du0 = dzs[:, 0:CONV_W]
        dz_ref[:, 0:CONV_W] = (du0 * sg).astype(BF16)
        dz_ref[:, CONV_W:2 * CONV_W] = (du0 * a * sg * (1.0 - sg)).astype(BF16)

        for gi, w in enumerate(POOL_WINDOWS):
            ls = pl.ds(POOL_G * gi, POOL_G)
            cols = slice(CONV_W + POOL_G * gi, CONV_W + POOL_G * (gi + 1))
            inv = _inv_count(ti, tm, w)
            s = pbuf[pl.ds(POOL_HIST, tm), ls]
            for j in range(1, w):
                s = s + pbuf[pl.ds(POOL_HIST - j, tm), ls]
            d = (s * inv - p[:, POOL_G * gi:POOL_G * (gi + 1)]).astype(BF16)
            dyp = dm[:, cols]
            ds_ref[3:4, ls] += _colsum(dyp * _dot(d, pw_ref[gi]))
            dyb = (dyp * cv_ref[3:4, ls]).astype(BF16)
            dpw_ref[gi] += _dot_tn(d, dyb)
            dd = _dot(dyb, pwt_ref[gi])
            ebuf[pl.ds(0, tm), ls] = dd * inv
            ebuf[pl.ds(tm, POOL_HIST), ls] = ecarry[:, ls]
            dp = ebuf[pl.ds(0, tm), ls] - dd
            for j in range(1, w):
                dp = dp + ebuf[pl.ds(j, tm), ls]
            dz_ref[:, pl.ds(2 * CONV_W + POOL_G * gi, POOL_G)] = dp.astype(BF16)
        ecarry[...] = ebuf[pl.ds(0, POOL_HIST), :]

        dhn = _dot(dz_ref[...], wit_ref[...])
        xh, r = _rms(h0_ref[...])
        dx, dg_rows = _rms_bwd(dhn, xh, r, g1_ref[...])
        dh0_ref[...] = dh1_ref[...] + dx
        dg1_ref[...] += _colsum(dg_rows)

        @pl.when(i == nt - 1)
        def _():
            for j in range(CONV_TAPS):
                dk_ref[j:j + 1, :] = _colsum(dkacc[pl.ds(8 * j, 8), :])
            dk_ref[CONV_TAPS:CONV_HIST, :] = jnp.zeros((CONV_HIST - CONV_TAPS, CONV_W), F32)

    def tile(c):
        return pl.BlockSpec((tm, c), lambda i: (nt - 1 - i, 0))

    halo = pl.BlockSpec((CONV_HIST, IN_COLS), lambda i: (jnp.maximum((nt - 1 - i) * (tm // CONV_HIST) - 1, 0), 0))
    return pl.pallas_call(
        body, name="mixer_bwd", grid=(nt,),
        in_specs=[tile(D_MODEL), tile(D_MODEL), tile(IN_COLS), halo, tile(CONV_W), _whole((D_MODEL, D_MODEL)),
                  _whole((IN_COLS, D_MODEL)), _whole((CONV_HIST, CONV_W)), _whole((8, CONV_W)), _whole((CONV_W, CONV_W)),
                  _whole((4, POOL_G, POOL_G)), _whole((4, POOL_G, POOL_G)), _whole((1, D_MODEL))],
        out_specs=[tile(D_MODEL), tile(IN_COLS), _whole((CONV_HIST, CONV_W)), _whole((8, CONV_W)),
                   _whole((4, POOL_G, POOL_G)), _whole((1, D_MODEL))],
        out_shape=[jax.ShapeDtypeStruct((rows, D_MODEL), F32), jax.ShapeDtypeStruct((rows, IN_COLS), BF16),
                   jax.ShapeDtypeStruct((CONV_HIST, CONV_W), F32), jax.ShapeDtypeStruct((8, CONV_W), F32),
                   jax.ShapeDtypeStruct((4, POOL_G, POOL_G), F32), jax.ShapeDtypeStruct((1, D_MODEL), F32)],
        scratch_shapes=[pltpu.VMEM((CONV_HIST + tm, CONV_W), F32), pltpu.VMEM((tm + CONV_HIST, CONV_W), F32),
                        pltpu.VMEM((POOL_HIST + tm, POOL_W), F32), pltpu.VMEM((tm + POOL_HIST, POOL_W), F32),
                        pltpu.VMEM((CONV_HIST, CONV_W), F32), pltpu.VMEM((POOL_HIST, POOL_W), F32),
                        pltpu.VMEM((8 * CONV_HIST, CONV_W), F32), pltpu.VMEM((tm, CONV_W), F32)],
        compiler_params=_cparams(1, V7X_VMEM_LIMIT),
    )(dh1, h0, z, z, u1, w_out_t, w_in_t, conv_k, conv_v, avg, pool_w, pool_w_t, g1)


def _wgrad(lhs, rhs, lhs_spec, rhs_spec, n_chunks, out_rows, out_cols, nt, name):
    def body(l_ref, r_ref, o_ref):
        @pl.when(pl.program_id(1) == 0)
        def _():
            o_ref[...] = jnp.zeros(o_ref.shape, F32)

        l = (l_ref[0] if len(l_ref.shape) == 3 else l_ref[...]).astype(BF16)
        r = (r_ref[0] if len(r_ref.shape) == 3 else r_ref[...]).astype(BF16)
        o_ref[0] += _dot_tn(l, r)

    return pl.pallas_call(
        body, name=name, grid=(n_chunks, nt), in_specs=[lhs_spec, rhs_spec],
        out_specs=pl.BlockSpec((1, out_rows, out_cols), lambda c, t: (c, 0, 0)),
        out_shape=jax.ShapeDtypeStruct((n_chunks, out_rows, out_cols), F32),
        compiler_params=_cparams(2, V7X_VMEM_LIMIT),
    )(lhs, rhs)


def _head_average():
    head = lax.broadcasted_iota(jnp.int32, (CONV_W, CONV_W), 0) // 64
    return jnp.where(head == head.T, 1.0 / 64, 0.0).astype(BF16)


def _wgrad_tile(rows, tm):
    return 768 if rows % 768 == 0 else tm


def _local_step(h_pad, tgt_pad, seq, layers, final_g, tm):
    rows = h_pad.shape[0]
    tmw = _wgrad_tile(rows, tm)
    ntw = rows // tmw
    avg = _head_average()
    depth = len(layers)
    saved = []
    h = h_pad
    hn = _norm_first(h, layers[0]["g1"], tm)
    for l, w in enumerate(layers):
        g_next = layers[l + 1]["g1"] if l + 1 < depth else final_g
        z, u1, m, h1, hn2 = _mixer_fwd(h, hn, w["w_in"], w["conv_k"], w["conv_v"], avg, w["pool_w"], w["w_out"], w["g2"], tm)
        ug0, h2, hn_next = _ffn_fwd(h1, hn2, w["w_up"], w["kf"], w["w_down"], g_next, tm)
        saved.append((h, hn, z, u1, m, h1, hn2, ug0))
        h, hn = h2, hn_next
    dh, loss_cols, dfinal_g = _loss_head(h, tgt_pad, final_g, seq, tm)

    row_tile = pl.BlockSpec((tmw, D_MODEL), lambda c, t: (t, 0))
    grads = [None] * depth
    for l in reversed(range(depth)):
        w = layers[l]
        h0, hn1, z, u1, m, h1, hn2, ug0 = saved[l]
        dh1, dug0, act, dkf, dg2 = _ffn_bwd(dh, h1, ug0, w["w_down_t"], w["w_up_t"], w["kf"], w["g2"], tm)
        chunk_tile = pl.BlockSpec((1, tmw, FF_CHUNK), lambda c, t: (c, t, 0))
        dw_up = _wgrad(hn2, dug0, row_tile, chunk_tile, 4, D_MODEL, FF_CHUNK, ntw, "wgrad_up")
        dw_down = _wgrad(act, dh, chunk_tile, row_tile, 2, FF_CHUNK, D_MODEL, ntw, "wgrad_down")
        dh0, dz, dk, dsmall, dpw, dg1 = _mixer_bwd(dh1, h0, z, u1, w["w_out_t"], w["w_in_t"], w["conv_k"], w["conv_v"], avg,
                                                   w["pool_w"], w["pool_w_t"], w["g1"], tm)
        dw_out = _wgrad(m, dh1, row_tile, row_tile, 1, D_MODEL, D_MODEL, ntw, "wgrad_out")
        in_tile = pl.BlockSpec((tmw, IN_COLS // 4), lambda c, t: (t, c))
        dw_in = _wgrad(hn1, dz, row_tile, in_tile, 4, D_MODEL, IN_COLS // 4, ntw, "wgrad_in")
        grads[l] = dict(dw_in=dw_in, dw_out=dw_out.reshape(4, D_MODEL // 4, D_MODEL), dw_up=dw_up,
                        dw_down=dw_down.reshape(4, D_FF // 4, D_MODEL), dk=dk, dsmall=dsmall, dpw=dpw, dg1=dg1, dg2=dg2, dkf=dkf)
        dh = dh0
    return loss_cols, dh, grads, dfinal_g


_ANY = pl.BlockSpec(memory_space=pl.ANY)


def _place():
    x, y, c = lax.axis_index("x"), lax.axis_index("y"), lax.axis_index("c")
    chips = [(1 - x, y), (x, 1 - y), (1 - x, 1 - y)]
    return x, y, c, chips


def _remote(src, dst, send_sems, recv_sems, idx, to):
    return pltpu.make_async_remote_copy(src_ref=src, dst_ref=dst, send_sem=send_sems.at[idx], recv_sem=recv_sems.at[idx],
                                        device_id=to, device_id_type=MESH)


def _gather_chips(xs):
    n = len(xs)

    def body(*refs):
        x_refs, o_refs = refs[:n], refs[n:2 * n]
        send_sems, recv_sems, local_sems = refs[2 * n:]
        x, y, c, chips = _place()
        k = 2 * x + y
        sibling = (x, y, 1 - c)
        own = [pltpu.make_async_copy(x_refs[a], o_refs[a].at[k], local_sems.at[a]) for a in range(n)]
        for cp in own:
            cp.start()
        sends = []
        for j, chip in enumerate(chips):
            for a in range(n):
                sends.append(_remote(x_refs[a].at[c], o_refs[a].at[k, c], send_sems, recv_sems, 3 * a + j, (*chip, c)))
                sends[-1].start()
        for j, chip in enumerate(chips):
            kj = 2 * chip[0] + chip[1]
            for a in range(n):
                landed = o_refs[a].at[kj, c]
                _remote(landed, landed, send_sems, recv_sems, 3 * a + j, sibling).wait_recv()
                sends.append(_remote(landed, landed, send_sems, recv_sems, 3 * n + 3 * a + j, sibling))
                sends[-1].start()
        for j, chip in enumerate(chips):
            kj = 2 * chip[0] + chip[1]
            for a in range(n):
                passed = o_refs[a].at[kj, 1 - c]
                _remote(passed, passed, send_sems, recv_sems, 3 * n + 3 * a + j, sibling).wait_recv()
        for cp in sends:
            cp.wait_send()
        for cp in own:
            cp.wait()

    return pl.pallas_call(
        body, name="gather_chips", in_specs=[_ANY] * n, out_specs=[_ANY] * n,
        out_shape=[jax.ShapeDtypeStruct((4,) + v.shape, v.dtype) for v in xs],
        scratch_shapes=[pltpu.SemaphoreType.DMA((6 * n,)), pltpu.SemaphoreType.DMA((6 * n,)), pltpu.SemaphoreType.DMA((n,))],
    )(*xs)


def _pair_exchange(g0s, g1s):
    n = len(g0s)

    def body(*refs):
        g0, g1, out = refs[:n], refs[n:2 * n], refs[2 * n:3 * n]
        send_sems, recv_sems = refs[3 * n:]
        x, y, c, _ = _place()
        sibling = (x, y, 1 - c)
        for a in range(n):
            @pl.when(c == 0)
            def _():
                _remote(g1[a], out[a], send_sems, recv_sems, a, sibling).start()

            @pl.when(c == 1)
            def _():
                _remote(g0[a], out[a], send_sems, recv_sems, a, sibling).start()
        for a in range(n):
            cp = _remote(g0[a], out[a], send_sems, recv_sems, a, sibling)
            cp.wait_recv()
            cp.wait_send()

    return pl.pallas_call(
        body, name="pair_exchange", in_specs=[_ANY] * (2 * n), out_specs=[_ANY] * n,
        out_shape=[jax.ShapeDtypeStruct(v.shape, v.dtype) for v in g0s],
        scratch_shapes=[pltpu.SemaphoreType.DMA((n,)), pltpu.SemaphoreType.DMA((n,))],
    )(*g0s, *g1s)


def _chip_exchange(parts):
    n = len(parts)

    def body(*refs):
        p_refs, o_refs = refs[:n], refs[n:2 * n]
        send_sems, recv_sems, local_sems = refs[2 * n:]
        x, y, c, chips = _place()
        k = 2 * x + y
        own = [pltpu.make_async_copy(p_refs[a].at[k], o_refs[a].at[k], local_sems.at[a]) for a in range(n)]
        for cp in own:
            cp.start()
        sends = []
        for j, chip in enumerate(chips):
            kj = 2 * chip[0] + chip[1]
            for a in range(n):
                sends.append(_remote(p_refs[a].at[kj], o_refs[a].at[k], send_sems, recv_sems, 3 * a + j, (*chip, c)))
                sends[-1].start()
        for j, chip in enumerate(chips):
            kj = 2 * chip[0] + chip[1]
            for a in range(n):
                landed = o_refs[a].at[kj]
                _remote(landed, landed, send_sems, recv_sems, 3 * a + j, (*chip, c)).wait_recv()
        for cp in sends:
            cp.wait_send()
        for cp in own:
            cp.wait()

    return pl.pallas_call(
        body, name="chip_exchange", in_specs=[_ANY] * n, out_specs=[_ANY] * n,
        out_shape=[jax.ShapeDtypeStruct(v.shape, v.dtype) for v in parts],
        scratch_shapes=[pltpu.SemaphoreType.DMA((3 * n,)), pltpu.SemaphoreType.DMA((3 * n,)), pltpu.SemaphoreType.DMA((n,))],
    )(*parts)


def _pair_share(reds):
    n = len(reds)

    def body(*refs):
        r_refs, o_refs = refs[:n], refs[n:2 * n]
        send_sems, recv_sems, local_sems = refs[2 * n:]
        x, y, c, _ = _place()
        sibling = (x, y, 1 - c)
        own = [pltpu.make_async_copy(r_refs[a], o_refs[a].at[c], local_sems.at[a]) for a in range(n)]
        sends = [_remote(r_refs[a], o_refs[a].at[c], send_sems, recv_sems, a, sibling) for a in range(n)]
        for cp in own + sends:
            cp.start()
        for a in range(n):
            theirs = o_refs[a].at[1 - c]
            _remote(theirs, theirs, send_sems, recv_sems, a, sibling).wait_recv()
        for cp in sends:
            cp.wait_send()
        for cp in own:
            cp.wait()

    return pl.pallas_call(
        body, name="pair_share", in_specs=[_ANY] * n, out_specs=[_ANY] * n,
        out_shape=[jax.ShapeDtypeStruct((2,) + v.shape, v.dtype) for v in reds],
        scratch_shapes=[pltpu.SemaphoreType.DMA((n,)), pltpu.SemaphoreType.DMA((n,)), pltpu.SemaphoreType.DMA((n,))],
    )(*reds)


def _all_reduce_small(pack):
    p, cols = pack.shape

    def body(x_ref, o_ref, sib, chipbuf, send_sems, recv_sems):
        x, y, c, chips = _place()
        k = 2 * x + y
        sibling = (x, y, 1 - c)
        pair = _remote(x_ref, sib, send_sems, recv_sems, 0, sibling)
        pair.start()
        pair.wait_recv()
        chipbuf[k] = x_ref[...] + sib[...]
        sends = [_remote(chipbuf.at[k], chipbuf.at[k], send_sems, recv_sems, 1 + j, (*chip, c)) for j, chip in enumerate(chips)]
        for cp in sends:
            cp.start()
        for j, chip in enumerate(chips):
            landed = chipbuf.at[2 * chip[0] + chip[1]]
            _remote(landed, landed, send_sems, recv_sems, 1 + j, (*chip, c)).wait_recv()
        pair.wait_send()
        for cp in sends:
            cp.wait_send()
        o_ref[...] = (chipbuf[0] + chipbuf[1]) + (chipbuf[2] + chipbuf[3])

    vm = pl.BlockSpec(memory_space=pltpu.VMEM)
    return pl.pallas_call(
        body, name="all_reduce_small", in_specs=[vm], out_specs=vm,
        out_shape=jax.ShapeDtypeStruct(pack.shape, F32),
        scratch_shapes=[pltpu.VMEM((p, cols), F32), pltpu.VMEM((4, p, cols), F32),
                        pltpu.SemaphoreType.DMA((4,)), pltpu.SemaphoreType.DMA((4,))],
    )(pack)


_BIG = ("w_in", "w_out", "w_up", "w_down")
_SMALL = ("norm1_g", "conv_dw_b", "conv_ln_g", "conv_ln_b", "pool_w", "pool_scale", "norm2_g", "final_g",
          "meta_tokens", "conv_dw_k", "ffn_dw_k")


def _rows8(v):
    return jnp.pad(v, ((0, -v.shape[0] % 8), (0, 0)))


def _pack_flat(arrs, rows):
    flat = jnp.concatenate([a.reshape(-1) for a in arrs])
    return jnp.pad(flat, (0, rows * D_MODEL - flat.shape[0])).reshape(1, rows, D_MODEL)


def _unpack_flat(packed, like):
    flat = packed.reshape(-1)
    out, off = [], 0
    for a in like:
        out.append(flat[off:off + a.size].reshape(a.shape))
        off += a.size
    return out


def kernel(x, meta_tokens, norm1_g, w_in, conv_dw_k, conv_dw_b, conv_ln_g, conv_ln_b, pool_w, pool_scale, w_out, norm2_g, w_up, ffn_dw_k, w_down, final_g, loss_target, m_meta_tokens, m_norm1_g, m_w_in, m_conv_dw_k, m_conv_dw_b, m_conv_ln_g, m_conv_ln_b, m_pool_w, m_pool_scale, m_w_out, m_norm2_g, m_w_up, m_ffn_dw_k, m_w_down, m_final_g, v_meta_tokens, v_norm1_g, v_w_in, v_conv_dw_k, v_conv_dw_b, v_conv_ln_g, v_conv_ln_b, v_pool_w, v_pool_scale, v_w_out, v_norm2_g, v_w_up, v_ffn_dw_k, v_w_down, v_final_g):
    weights = dict(meta_tokens=meta_tokens, norm1_g=norm1_g, w_in=w_in, conv_dw_k=conv_dw_k, conv_dw_b=conv_dw_b,
                   conv_ln_g=conv_ln_g, conv_ln_b=conv_ln_b, pool_w=pool_w, pool_scale=pool_scale, w_out=w_out,
                   norm2_g=norm2_g, w_up=w_up, ffn_dw_k=ffn_dw_k, w_down=w_down, final_g=final_g)
    mom1 = dict(meta_tokens=m_meta_tokens, norm1_g=m_norm1_g, w_in=m_w_in, conv_dw_k=m_conv_dw_k, conv_dw_b=m_conv_dw_b,
                conv_ln_g=m_conv_ln_g, conv_ln_b=m_conv_ln_b, pool_w=m_pool_w, pool_scale=m_pool_scale, w_out=m_w_out,
                norm2_g=m_norm2_g, w_up=m_w_up, ffn_dw_k=m_ffn_dw_k, w_down=m_w_down, final_g=m_final_g)
    mom2 = dict(meta_tokens=v_meta_tokens, norm1_g=v_norm1_g, w_in=v_w_in, conv_dw_k=v_conv_dw_k, conv_dw_b=v_conv_dw_b,
                conv_ln_g=v_conv_ln_g, conv_ln_b=v_conv_ln_b, pool_w=v_pool_w, pool_scale=v_pool_scale, w_out=v_w_out,
                norm2_g=v_norm2_g, w_up=v_w_up, ffn_dw_k=v_ffn_dw_k, w_down=v_w_down, final_g=v_final_g)
    order = list(weights)
    depth = w_in.shape[0]
    seq = x.shape[1]
    tm = ROW_TILE
    rows = -(-(N_META + seq) // tm) * tm
    chip = 2 * lax.axis_index("x") + lax.axis_index("y")
    core = lax.axis_index("c")

    g_in, g_out, g_up, g_down, g_cdk, g_fdk, g_meta = _gather_chips(
        [_cast_bf16(weights[nm], "cast_" + nm) for nm in _BIG]
        + [jnp.pad(conv_dw_k, ((0, 0), (0, CONV_HIST - CONV_TAPS), (0, 0))),
           jnp.pad(ffn_dw_k, ((0, 0), (0, 8 - FFN_TAPS), (0, 0))),
           meta_tokens.reshape(2, N_META // 2, D_MODEL // 4)])
    meta_full = g_meta.transpose(1, 2, 0, 3).reshape(N_META, D_MODEL)
    layers = []
    for l in range(depth):
        w_in_l = g_in[:, l].transpose(1, 0, 2).reshape(D_MODEL, IN_COLS)
        w_out_l = g_out[:, l].reshape(D_MODEL, D_MODEL)
        w_down_l = g_down[:, l].reshape(2, FF_CHUNK, D_MODEL)
        pw = pool_w[l].astype(BF16)
        conv_v = jnp.pad(jnp.stack([conv_dw_b[l], conv_ln_g[l], conv_ln_b[l], pool_scale[l]]), ((0, 4), (0, 0)))
        layers.append(dict(
            w_in=w_in_l, w_in_t=w_in_l.T, conv_k=g_cdk[:, l].transpose(1, 0, 2).reshape(CONV_HIST, CONV_W), conv_v=conv_v,
            pool_w=pw, pool_w_t=pw.transpose(0, 2, 1), w_out=w_out_l, w_out_t=w_out_l.T,
            w_up=g_up[:, l], w_up_t=g_up[:, l].transpose(0, 2, 1), kf=g_fdk[:, l],
            w_down=w_down_l, w_down_t=w_down_l.transpose(0, 2, 1), g1=norm1_g[l][None], g2=norm2_g[l][None]))

    h_pad = jnp.concatenate([meta_full, x[0], jnp.zeros((rows - N_META - seq, D_MODEL), F32)])
    tgt_pad = jnp.pad(loss_target[0], ((N_META, rows - N_META - seq), (0, 0)))
    loss_cols, dh, grads, dfinal_g = _local_step(h_pad, tgt_pad, seq, layers, final_g[None], tm)
    loss = lax.psum(jnp.sum(loss_cols) * (0.5 / D_MODEL), ("x", "y", "c"))
    grad_x = dh[N_META:N_META + seq][None]

    keys = ("dw_in", "dw_out", "dw_up", "dw_down")
    g0s = [grads[0][kk] for kk in keys]
    g1s = [grads[1][kk] for kk in keys]
    theirs = _pair_exchange(g0s, g1s)
    core_arr = core.astype(jnp.int32).reshape(1)
    parts = [_pair_sum_bf16(core_arr, g0s[a], g1s[a], theirs[a], "pair_sum_" + _BIG[a]) for a in range(4)]
    chip_parts = _chip_exchange(parts)
    reds = [_chip_sum(chip_parts[a], "chip_sum_" + _BIG[a]) for a in range(4)]
    grad, delta, new_m, new_v = {}, {}, {}, {}
    for a, full in enumerate(_pair_share(reds)):
        nm = _BIG[a]
        grad[nm] = full
        delta[nm], new_m[nm], new_v[nm] = _adamw(weights[nm], full, mom1[nm], mom2[nm], "adamw_" + nm)

    pack = jnp.concatenate([
        _rows8(jnp.concatenate([grads[l]["dg1"] for l in range(depth)])),
        _rows8(jnp.concatenate([grads[l]["dg2"] for l in range(depth)])),
        _rows8(dfinal_g),
        jnp.concatenate([grads[l]["dsmall"] for l in range(depth)], axis=1),
        jnp.stack([grads[l]["dpw"] for l in range(depth)]).reshape(-1, D_MODEL),
        dh[:N_META],
        jnp.concatenate([grads[l]["dk"] for l in range(depth)], axis=1),
        jnp.stack([grads[l]["dkf"] for l in range(depth)]).reshape(-1, D_MODEL),
    ])
    red = _all_reduce_small(pack)
    o = 0
    grad["norm1_g"] = red[o:o + depth]
    o += 8
    grad["norm2_g"] = red[o:o + depth]
    o += 8
    grad["final_g"] = red[o]
    o += 8
    sm = red[o:o + 8].reshape(8, depth, CONV_W)
    grad["conv_dw_b"], grad["conv_ln_g"], grad["conv_ln_b"], grad["pool_scale"] = sm[0], sm[1], sm[2], sm[3]
    o += 8
    n_pw = depth * 4 * POOL_G * POOL_G // D_MODEL
    grad["pool_w"] = red[o:o + n_pw].reshape(pool_w.shape)
    o += n_pw
    grad["meta_tokens"] = lax.dynamic_slice_in_dim(red[o:o + N_META], chip * (D_MODEL // 4), D_MODEL // 4, axis=1)
    o += N_META
    dk_all = red[o:o + CONV_HIST].reshape(CONV_HIST, depth, 4, CONV_W // 4)
    grad["conv_dw_k"] = lax.dynamic_index_in_dim(dk_all, chip, axis=2, keepdims=False)[:CONV_TAPS].transpose(1, 0, 2)
    o += CONV_HIST
    dkf_all = red[o:].reshape(depth, 4, 8, FF_CHUNK)
    grad["ffn_dw_k"] = lax.dynamic_index_in_dim(dkf_all, chip, axis=1, keepdims=False)[:, :FFN_TAPS]

    small_rows = -(-sum(weights[nm].size for nm in _SMALL) // (8 * D_MODEL)) * 8
    packed = [_pack_flat([d[nm] for nm in _SMALL], small_rows) for d in (weights, grad, mom1, mom2)]
    for res, packed_out in zip((delta, new_m, new_v), _adamw(*packed, "adamw_small")):
        for nm, val in zip(_SMALL, _unpack_flat(packed_out, [weights[nm] for nm in _SMALL])):
            res[nm] = val

    return (loss, grad_x, *[grad[nm] for nm in order], *[delta[nm] for nm in order],
            *[new_m[nm] for nm in order], *[new_v[nm] for nm in order])
```

```python
import functools

import jax
import jax.numpy as jnp
from jax import lax
from jax.experimental import pallas as pl
from jax.experimental.pallas import tpu as pltpu

F32 = jnp.float32
BF16 = jnp.bfloat16

D_MODEL = 1024
CONV_W = 512
POOL_W = 512
POOL_G = 128
POOL_WINDOWS = (2, 4, 8, 16)
IN_COLS = 1536
D_FF = 2816
FF_CHUNK = 1408
CONV_TAPS = 31
CONV_HIST = 32
POOL_HIST = 16
FFN_TAPS = 3
N_META = 16
EPS = 1e-6

ADAM_LR = 0.001
ADAM_B1 = 0.9
ADAM_B2 = 0.999
ADAM_EPS = 1e-08
ADAM_WD = 0.01
ADAM_STEP = 10

ROW_TILE = 256
CONV_ROW_BLOCK = 64
FFN_ROW_BLOCK = 32
MIXER_UNROLL = True
V7X_VMEM_LIMIT = 56 * 1024 * 1024

MESH = pl.DeviceIdType.MESH


def _cparams(n_axes, vmem=None):
    return pltpu.CompilerParams(dimension_semantics=("arbitrary",) * n_axes, vmem_limit_bytes=vmem)


def _whole(shape, single=False):
    zeros = (0,) * len(shape)
    if single:
        return pl.BlockSpec(shape, lambda *_: zeros, pipeline_mode=pl.Buffered(1))
    return pl.BlockSpec(shape, lambda *_: zeros)


def _sigmoid(x):
    return 1.0 / (1.0 + jnp.exp(-x))


def _dot(a, b):
    return jnp.dot(a, b, preferred_element_type=F32)


def _dot_tn(a, b):
    return lax.dot_general(a, b, (((0,), (0,)), ((), ())), preferred_element_type=F32)


def _split_dot(v, a_ref):
    hi = v.astype(BF16)
    lo = (v - hi.astype(F32)).astype(BF16)
    return _dot(hi, a_ref[...]) + _dot(lo, a_ref[...])


def _rms(x):
    r = lax.rsqrt(jnp.mean(x * x, axis=-1, keepdims=True) + EPS)
    return x * r, r


def _rms_bwd(dy, xhat, r, g):
    gd = dy * g
    return r * (gd - xhat * jnp.mean(gd * xhat, axis=-1, keepdims=True)), dy * xhat


def _colsum(v):
    return jnp.sum(v, axis=0, keepdims=True)


def _shifted(window, s):
    return window if s == 0 else pltpu.roll(window, window.shape[0] - s, 0)


def _conv3(window, kf_ref, cc, ls, hist):
    x2 = window[hist:]
    x1 = pltpu.roll(window, 1, 0)[hist:]
    x0 = pltpu.roll(window, 2, 0)[hist:]
    return x0, x1, x2, kf_ref[cc, 0:1, ls] * x0 + kf_ref[cc, 1:2, ls] * x1 + kf_ref[cc, 2:3, ls] * x2


def _for_row_blocks(n, rb, fn, unroll):
    if unroll:
        for r in range(n):
            fn(r * rb)
    else:
        def step(r, keep):
            fn(pl.multiple_of(r * rb, rb))
            return keep

        lax.fori_loop(0, n, step, 0)


def _fold8(v):
    part = v[0:8]
    for k in range(1, v.shape[0] // 8):
        part = part + v[8 * k:8 * k + 8]
    return part


def _inv_count(tile, tm, w):
    t = tile * tm + lax.broadcasted_iota(jnp.int32, (tm, POOL_G), 0)
    return 1.0 / jnp.minimum(t + 1, w).astype(F32)


def _norm_first(h, g, tm):
    rows = h.shape[0]

    def body(h_ref, g_ref, o_ref):
        xhat, _ = _rms(h_ref[...])
        o_ref[...] = (xhat * g_ref[...]).astype(BF16)

    return pl.pallas_call(
        body, name="norm_first", grid=(rows // tm,),
        in_specs=[pl.BlockSpec((tm, D_MODEL), lambda i: (i, 0)), _whole((1, D_MODEL))],
        out_specs=pl.BlockSpec((tm, D_MODEL), lambda i: (i, 0)),
        out_shape=jax.ShapeDtypeStruct((rows, D_MODEL), BF16),
        compiler_params=_cparams(1),
    )(h, g)


def _row_block(r):
    for cand in (256, 176, 128, 64, 32, 16):
        if r % cand == 0:
            return cand
    return r


def _place_own(chip, w, dtype, name):
    n, r, c = w.shape
    rb = _row_block(r)

    def body(chip_ref, w_ref, o_ref):
        o_ref[0] = w_ref[...].astype(dtype)

    return pl.pallas_call(
        body, name=name,
        grid_spec=pltpu.PrefetchScalarGridSpec(
            num_scalar_prefetch=1, grid=(n, r // rb),
            in_specs=[pl.BlockSpec((1, rb, c), lambda i, j, chip_ref: (i, j, 0))],
            out_specs=pl.BlockSpec((1, 1, rb, c), lambda i, j, chip_ref: (chip_ref[0], i, j, 0))),
        out_shape=jax.ShapeDtypeStruct((4,) + w.shape, dtype), compiler_params=_cparams(2),
    )(chip, w)


def _pair_sum_bf16(core, g0, g1, other, name):
    n, r, c = g0.shape
    rb = _row_block(r)

    def body(core_ref, g0_ref, g1_ref, o_ref, out_ref):
        mine = jnp.where(core_ref[0] == 0, g0_ref[...], g1_ref[...])
        out_ref[...] = (mine + o_ref[...]).astype(BF16)

    spec = pl.BlockSpec((1, rb, c), lambda i, j: (i, j, 0))
    return pl.pallas_call(
        body, name=name, grid=(n, r // rb),
        in_specs=[pl.BlockSpec(memory_space=pltpu.SMEM), spec, spec, spec], out_specs=spec,
        out_shape=jax.ShapeDtypeStruct(g0.shape, BF16), compiler_params=_cparams(2),
    )(core, g0, g1, other)


def _chip_sum(chip, parts, recv, name):
    _, r, c = parts.shape
    rb = _row_block(r)

    def body(chip_ref, p_ref, r_ref, out_ref):
        got = r_ref[...].astype(F32)
        out_ref[...] = (p_ref[0].astype(F32) + got[0]) + (got[1] + got[2])

    return pl.pallas_call(
        body, name=name,
        grid_spec=pltpu.PrefetchScalarGridSpec(
            num_scalar_prefetch=1, grid=(r // rb,),
            in_specs=[pl.BlockSpec((1, rb, c), lambda j, chip_ref: (chip_ref[0], j, 0)),
                      pl.BlockSpec((3, rb, c), lambda j, chip_ref: (0, j, 0))],
            out_specs=pl.BlockSpec((rb, c), lambda j, chip_ref: (j, 0))),
        out_shape=jax.ShapeDtypeStruct((r, c), F32), compiler_params=_cparams(1),
    )(chip, parts, recv)


def _adamw_update(w, g, m, v):
    nm = ADAM_B1 * m + (1.0 - ADAM_B1) * g
    nv = ADAM_B2 * v + (1.0 - ADAM_B2) * (g * g)
    m_hat = nm / (1.0 - ADAM_B1 ** ADAM_STEP)
    v_hat = nv / (1.0 - ADAM_B2 ** ADAM_STEP)
    return -ADAM_LR * (m_hat / (jnp.sqrt(v_hat) + ADAM_EPS) + ADAM_WD * w), nm, nv


def _adamw(w, g, m, v, name):
    n, r, c = w.shape
    rb = _row_block(r)

    def body(w_ref, g_ref, m_ref, v_ref, d_ref, nm_ref, nv_ref):
        d_ref[...], nm_ref[...], nv_ref[...] = _adamw_update(w_ref[...], g_ref[...], m_ref[...], v_ref[...])

    spec = pl.BlockSpec((1, rb, c), lambda i, j: (i, j, 0))
    shp = jax.ShapeDtypeStruct(w.shape, F32)
    return pl.pallas_call(
        body, name=name, grid=(n, r // rb), in_specs=[spec] * 4, out_specs=[spec] * 3,
        out_shape=[shp] * 3, compiler_params=_cparams(2),
    )(w, g, m, v)


def _adamw_pair(core, w, mine, theirs, m, v, name):
    n, r, c = w.shape
    rb = _row_block(r)

    def body(core_ref, w_ref, a_ref, b_ref, m_ref, v_ref, g_ref, d_ref, nm_ref, nv_ref):
        g = jnp.where(pl.program_id(0) == core_ref[0], a_ref[...], b_ref[...])
        g_ref[0] = g
        d_ref[0], nm_ref[0], nv_ref[0] = _adamw_update(w_ref[0], g, m_ref[0], v_ref[0])

    spec = pl.BlockSpec((1, rb, c), lambda i, j, core_ref: (i, j, 0))
    flat = pl.BlockSpec((rb, c), lambda i, j, core_ref: (j, 0))
    shp = jax.ShapeDtypeStruct(w.shape, F32)
    return pl.pallas_call(
        body, name=name,
        grid_spec=pltpu.PrefetchScalarGridSpec(num_scalar_prefetch=1, grid=(n, r // rb),
                                               in_specs=[spec, flat, flat, spec, spec], out_specs=[spec] * 4),
        out_shape=[shp] * 4, compiler_params=_cparams(2),
    )(core, w, mine, theirs, m, v)


def _mixer_fwd(h0, hn, w_in, conv_k, conv_v, avg, pool_w, w_out, g2, tm):
    rows = h0.shape[0]
    rb = CONV_ROW_BLOCK

    def body(h0_ref, hn_ref, win_ref, ck_ref, cv_ref, avg_ref, pw_ref, wout_ref, g2_ref,
             z_ref, u1_ref, m_ref, h1_ref, hn2_ref, ubuf, pbuf):
        i = pl.program_id(0)

        @pl.when(i == 0)
        def _():
            ubuf[pl.ds(0, CONV_HIST), :] = jnp.zeros((CONV_HIST, CONV_W), F32)
            pbuf[pl.ds(0, POOL_HIST), :] = jnp.zeros((POOL_HIST, POOL_W), F32)

        z = _dot(hn_ref[...], win_ref[...])
        z_ref[...] = z
        ubuf[pl.ds(CONV_HIST, tm), :] = z[:, :CONV_W] * _sigmoid(z[:, CONV_W:2 * CONV_W])
        p = z[:, 2 * CONV_W:]
        pbuf[pl.ds(POOL_HIST, tm), :] = p

        def conv_block(r0):
            for l in range(CONV_W // 128):
                ls = pl.ds(128 * l, 128)
                window = ubuf[pl.ds(r0, rb + CONV_HIST), ls]
                acc = jnp.broadcast_to(cv_ref[0:1, ls], (rb, 128))
                for s in range(8):
                    ws = _shifted(window, s)
                    for q in range(CONV_HIST // 8 + 1):
                        j = 8 * q + s - 2
                        if 0 <= j < CONV_TAPS:
                            acc = acc + ck_ref[j:j + 1, ls] * ws[8 * q:8 * q + rb]
                u1_ref[pl.ds(r0, rb), ls] = acc

        _for_row_blocks(tm // rb, rb, conv_block, unroll=MIXER_UNROLL)
        ubuf[pl.ds(0, CONV_HIST), :] = ubuf[pl.ds(tm, CONV_HIST), :]

        u1 = u1_ref[...]
        cen = u1 - _split_dot(u1, avg_ref)
        xhat = cen * lax.rsqrt(_split_dot(cen * cen, avg_ref) + EPS)
        u2 = xhat * cv_ref[1:2, :] + cv_ref[2:3, :]
        m_ref[:, 0:CONV_W] = (u2 * _sigmoid(u2)).astype(BF16)

        for gi, w in enumerate(POOL_WINDOWS):
            ls = pl.ds(POOL_G * gi, POOL_G)
            s = pbuf[pl.ds(POOL_HIST, tm), ls]
            for j in range(1, w):
                s = s + pbuf[pl.ds(POOL_HIST - j, tm), ls]
            d = s * _inv_count(i, tm, w) - p[:, POOL_G * gi:POOL_G * (gi + 1)]
            y = _dot(d.astype(BF16), pw_ref[gi]) * cv_ref[3:4, ls]
            m_ref[:, pl.ds(CONV_W + POOL_G * gi, POOL_G)] = y.astype(BF16)
        pbuf[pl.ds(0, POOL_HIST), :] = pbuf[pl.ds(tm, POOL_HIST), :]

        h1 = h0_ref[...] + _dot(m_ref[...], wout_ref[...])
        h1_ref[...] = h1
        xh, _ = _rms(h1)
        hn2_ref[...] = (xh * g2_ref[...]).astype(BF16)

    def tile(c):
        return pl.BlockSpec((tm, c), lambda i: (i, 0))

    return pl.pallas_call(
        body, name="mixer_fwd", grid=(rows // tm,),
        in_specs=[tile(D_MODEL), tile(D_MODEL), _whole((D_MODEL, IN_COLS)), _whole((CONV_HIST, CONV_W)),
                  _whole((8, CONV_W)), _whole((CONV_W, CONV_W)), _whole((4, POOL_G, POOL_G)),
                  _whole((D_MODEL, D_MODEL)), _whole((1, D_MODEL))],
        out_specs=[tile(IN_COLS), tile(CONV_W), tile(D_MODEL), tile(D_MODEL), tile(D_MODEL)],
        out_shape=[jax.ShapeDtypeStruct((rows, IN_COLS), F32), jax.ShapeDtypeStruct((rows, CONV_W), F32),
                   jax.ShapeDtypeStruct((rows, D_MODEL), BF16), jax.ShapeDtypeStruct((rows, D_MODEL), F32),
                   jax.ShapeDtypeStruct((rows, D_MODEL), BF16)],
        scratch_shapes=[pltpu.VMEM((CONV_HIST + tm, CONV_W), F32), pltpu.VMEM((POOL_HIST + tm, POOL_W), F32)],
        compiler_params=_cparams(1, V7X_VMEM_LIMIT),
    )(h0, hn, w_in, conv_k, conv_v, avg, pool_w, w_out, g2)


def _ffn_fwd(h1, hn2, w_up, kf, w_down, g_next, tm):
    rows = h1.shape[0]
    hist = 8

    rb = FFN_ROW_BLOCK

    def body(h1_ref, hn2_ref, wup_ref, kf_ref, wdn_ref, gn_ref, ug_ref, h2_ref, hnn_ref, wg, wv, carry, act_s, acc):
        i = pl.program_id(0)

        @pl.when(i == 0)
        def _():
            carry[...] = jnp.zeros(carry.shape, F32)

        acc[...] = h1_ref[...]
        for c in range(2):
            for buf, cc in ((wg, c), (wv, c + 2)):
                ug = _dot(hn2_ref[...], wup_ref[cc])
                ug_ref[cc] = ug.astype(BF16)
                buf[pl.ds(0, hist), :] = carry[cc]
                buf[pl.ds(hist, tm), :] = ug
                carry[cc] = buf[pl.ds(tm, hist), :]

            def act_block(r0):
                for l in range(FF_CHUNK // 128):
                    ls = pl.ds(128 * l, 128)
                    gate = _conv3(wg[pl.ds(r0, rb + hist), ls], kf_ref, c, ls, hist)[3]
                    val = _conv3(wv[pl.ds(r0, rb + hist), ls], kf_ref, c + 2, ls, hist)[3]
                    act_s[pl.ds(r0, rb), ls] = (gate * _sigmoid(gate) * val).astype(BF16)

            _for_row_blocks(tm // rb, rb, act_block, unroll=True)
            acc[...] += _dot(act_s[...], wdn_ref[c])
        h2 = acc[...]
        h2_ref[...] = h2
        xh, _ = _rms(h2)
        hnn_ref[...] = (xh * gn_ref[...]).astype(BF16)

    def tile(c):
        return pl.BlockSpec((tm, c), lambda i: (i, 0))

    return pl.pallas_call(
        body, name="ffn_fwd", grid=(rows // tm,),
        in_specs=[tile(D_MODEL), tile(D_MODEL), _whole((4, D_MODEL, FF_CHUNK), single=True), _whole((4, 8, FF_CHUNK)),
                  _whole((2, FF_CHUNK, D_MODEL), single=True), _whole((1, D_MODEL))],
        out_specs=[pl.BlockSpec((4, tm, FF_CHUNK), lambda i: (0, i, 0)), tile(D_MODEL), tile(D_MODEL)],
        out_shape=[jax.ShapeDtypeStruct((4, rows, FF_CHUNK), BF16), jax.ShapeDtypeStruct((rows, D_MODEL), F32),
                   jax.ShapeDtypeStruct((rows, D_MODEL), BF16)],
        scratch_shapes=[pltpu.VMEM((hist + tm, FF_CHUNK), F32), pltpu.VMEM((hist + tm, FF_CHUNK), F32),
                        pltpu.VMEM((4, hist, FF_CHUNK), F32), pltpu.VMEM((tm, FF_CHUNK), BF16),
                        pltpu.VMEM((tm, D_MODEL), F32)],
        compiler_params=_cparams(1, V7X_VMEM_LIMIT),
    )(h1, hn2, w_up, kf, w_down, g_next)


def _loss_head(h, tgt, g, seq, tm):
    rows = h.shape[0]

    def body(h_ref, t_ref, g_ref, dh_ref, loss_ref, dg_ref):
        i = pl.program_id(0)

        @pl.when(i == 0)
        def _():
            loss_ref[...] = jnp.zeros(loss_ref.shape, F32)
            dg_ref[...] = jnp.zeros(dg_ref.shape, F32)

        t = i * tm + lax.broadcasted_iota(jnp.int32, (tm, 1), 0)
        mask = jnp.logical_and(t >= N_META, t < N_META + seq).astype(F32)
        xhat, r = _rms(h_ref[...])
        err = (xhat * g_ref[...] - t_ref[...]) * mask
        loss_ref[...] += _colsum(err * err)
        dh, dg_rows = _rms_bwd(err * (1.0 / D_MODEL), xhat, r, g_ref[...])
        dh_ref[...] = dh
        dg_ref[...] += _colsum(dg_rows)

    tile = pl.BlockSpec((tm, D_MODEL), lambda i: (i, 0))
    vec = jax.ShapeDtypeStruct((1, D_MODEL), F32)
    return pl.pallas_call(
        body, name="loss_head", grid=(rows // tm,),
        in_specs=[tile, tile, _whole((1, D_MODEL))],
        out_specs=[tile, _whole((1, D_MODEL)), _whole((1, D_MODEL))],
        out_shape=[jax.ShapeDtypeStruct((rows, D_MODEL), F32), vec, vec],
        compiler_params=_cparams(1),
    )(h, tgt, g)


def _ffn_bwd(dh2, h1, ug0, w_down_t, w_up_t, kf, g2, tm):
    rows = dh2.shape[0]
    nt = rows // tm
    hist = 16
    fut = 8

    rb = FFN_ROW_BLOCK
    near = 8

    def body(dh2_ref, h1_ref, ug_ref, halo_ref, wdt_ref, wut_ref, kf_ref, g2_ref,
             dh1_ref, dug_ref, act_ref, dkf_ref, dg2_ref, wg, wv, dgb, dvb, carry, dkacc, acc):
        i = pl.program_id(0)
        first_tile = jnp.where(i == nt - 1, 1.0, 0.0)

        @pl.when(i == 0)
        def _():
            carry[...] = jnp.zeros(carry.shape, F32)
            dkacc[...] = jnp.zeros(dkacc.shape, F32)
            dg2_ref[...] = jnp.zeros(dg2_ref.shape, F32)

        dh2b = dh2_ref[...].astype(BF16)
        acc[...] = jnp.zeros(acc.shape, F32)
        for c in range(2):
            pairs = ((wg, dgb, c), (wv, dvb, c + 2))
            for buf, dbuf, cc in pairs:
                buf[pl.ds(0, hist), :] = halo_ref[cc].astype(F32) * (1.0 - first_tile)
                buf[pl.ds(hist, tm), :] = ug_ref[cc].astype(F32)
                dbuf[pl.ds(tm, fut), :] = carry[cc]
            dgb[pl.ds(0, tm), :] = _dot(dh2b, wdt_ref[c])

            def grad_block(r0):
                for l in range(FF_CHUNK // 128):
                    ls = pl.ds(128 * l, 128)
                    g0, g1, g2, gate = _conv3(wg[pl.ds(r0 + hist - near, rb + near), ls], kf_ref, c, ls, near)
                    v0, v1, v2, val = _conv3(wv[pl.ds(r0 + hist - near, rb + near), ls], kf_ref, c + 2, ls, near)
                    sg = _sigmoid(gate)
                    silu = gate * sg
                    act_ref[c, pl.ds(r0, rb), ls] = (silu * val).astype(BF16)
                    dact = dgb[pl.ds(r0, rb), ls]
                    dgate = dact * val * (sg * (1.0 + gate * (1.0 - sg)))
                    dval = dact * silu
                    dgb[pl.ds(r0, rb), ls] = dgate
                    dvb[pl.ds(r0, rb), ls] = dval
                    for cc, dv, taps in ((c, dgate, (g0, g1, g2)), (c + 2, dval, (v0, v1, v2))):
                        for j in range(FFN_TAPS):
                            dkacc[cc, pl.ds(8 * j, 8), ls] += _fold8(dv * taps[j])

            _for_row_blocks(tm // rb, rb, grad_block, unroll=True)
            for buf, dbuf, cc in pairs:
                carry[cc] = dbuf[pl.ds(0, fut), :]

            def conv_block(r0):
                for l in range(FF_CHUNK // 128):
                    ls = pl.ds(128 * l, 128)
                    for buf, dbuf, cc in pairs:
                        window = dbuf[pl.ds(r0, rb + fut), ls]
                        dug0 = (kf_ref[cc, 0:1, ls] * _shifted(window, 2)[0:rb] + kf_ref[cc, 1:2, ls] * _shifted(window, 1)[0:rb]
                                + kf_ref[cc, 2:3, ls] * window[0:rb])
                        dug_ref[cc, pl.ds(r0, rb), ls] = dug0.astype(BF16)

            _for_row_blocks(tm // rb, rb, conv_block, unroll=True)
            acc[...] += _dot(dug_ref[c], wut_ref[c]) + _dot(dug_ref[c + 2], wut_ref[c + 2])
        xhat, r = _rms(h1_ref[...])
        dx, dg_rows = _rms_bwd(acc[...], xhat, r, g2_ref[...])
        dh1_ref[...] = dh2_ref[...] + dx
        dg2_ref[...] += _colsum(dg_rows)

        @pl.when(i == nt - 1)
        def _():
            for cc in range(4):
                for j in range(FFN_TAPS):
                    dkf_ref[cc, j:j + 1, :] = _colsum(dkacc[cc, pl.ds(8 * j, 8), :])
                dkf_ref[cc, FFN_TAPS:8, :] = jnp.zeros((8 - FFN_TAPS, FF_CHUNK), F32)

    def tile(c):
        return pl.BlockSpec((tm, c), lambda i: (nt - 1 - i, 0))

    def chunks(n):
        return pl.BlockSpec((n, tm, FF_CHUNK), lambda i: (0, nt - 1 - i, 0))

    halo = pl.BlockSpec((4, hist, FF_CHUNK), lambda i: (0, jnp.maximum((nt - 1 - i) * (tm // hist) - 1, 0), 0))
    return pl.pallas_call(
        body, name="ffn_bwd", grid=(nt,),
        in_specs=[tile(D_MODEL), tile(D_MODEL), chunks(4), halo, _whole((2, D_MODEL, FF_CHUNK), single=True),
                  _whole((4, FF_CHUNK, D_MODEL), single=True), _whole((4, 8, FF_CHUNK)), _whole((1, D_MODEL))],
        out_specs=[tile(D_MODEL), chunks(4), chunks(2), _whole((4, 8, FF_CHUNK)), _whole((1, D_MODEL))],
        out_shape=[jax.ShapeDtypeStruct((rows, D_MODEL), F32), jax.ShapeDtypeStruct((4, rows, FF_CHUNK), BF16),
                   jax.ShapeDtypeStruct((2, rows, FF_CHUNK), BF16), jax.ShapeDtypeStruct((4, 8, FF_CHUNK), F32),
                   jax.ShapeDtypeStruct((1, D_MODEL), F32)],
        scratch_shapes=[pltpu.VMEM((hist + tm, FF_CHUNK), F32), pltpu.VMEM((hist + tm, FF_CHUNK), F32),
                        pltpu.VMEM((tm + fut, FF_CHUNK), F32), pltpu.VMEM((tm + fut, FF_CHUNK), F32),
                        pltpu.VMEM((4, fut, FF_CHUNK), F32), pltpu.VMEM((4, 8 * FFN_TAPS, FF_CHUNK), F32),
                        pltpu.VMEM((tm, D_MODEL), F32)],
        compiler_params=_cparams(1, V7X_VMEM_LIMIT),
    )(dh2, h1, ug0, ug0, w_down_t, w_up_t, kf, g2)


def _mixer_bwd(dh1, h0, z, u1, w_out_t, w_in_t, conv_k, conv_v, avg, pool_w, pool_w_t, g1, tm):
    rows = dh1.shape[0]
    nt = rows // tm
    rb = CONV_ROW_BLOCK

    def body(dh1_ref, h0_ref, z_ref, zh_ref, u1_ref, wot_ref, wit_ref, ck_ref, cv_ref, avg_ref, pw_ref, pwt_ref, g1_ref,
             dh0_ref, dz_ref, dk_ref, ds_ref, dpw_ref, dg1_ref, ubuf, dbuf, pbuf, ebuf, dcarry, ecarry, dkacc, dzs):
        i = pl.program_id(0)
        ti = nt - 1 - i
        has_past = jnp.where(ti > 0, 1.0, 0.0)

        @pl.when(i == 0)
        def _():
            dcarry[...] = jnp.zeros(dcarry.shape, F32)
            ecarry[...] = jnp.zeros(ecarry.shape, F32)
            dkacc[...] = jnp.zeros(dkacc.shape, F32)
            ds_ref[...] = jnp.zeros(ds_ref.shape, F32)
            dpw_ref[...] = jnp.zeros(dpw_ref.shape, F32)
            dg1_ref[...] = jnp.zeros(dg1_ref.shape, F32)

        dm = _dot(dh1_ref[...].astype(BF16), wot_ref[...])
        z = z_ref[...]
        a = z[:, :CONV_W]
        sg = _sigmoid(z[:, CONV_W:2 * CONV_W])
        p = z[:, 2 * CONV_W:]
        zh = zh_ref[...] * has_past
        ubuf[pl.ds(0, CONV_HIST), :] = zh[:, :CONV_W] * _sigmoid(zh[:, CONV_W:2 * CONV_W])
        ubuf[pl.ds(CONV_HIST, tm), :] = a * sg
        pbuf[pl.ds(0, POOL_HIST), :] = zh[CONV_HIST - POOL_HIST:, 2 * CONV_W:]
        pbuf[pl.ds(POOL_HIST, tm), :] = p

        u1 = u1_ref[...]
        cen = u1 - _split_dot(u1, avg_ref)
        rstd = lax.rsqrt(_split_dot(cen * cen, avg_ref) + EPS)
        xhat = cen * rstd
        u2 = xhat * cv_ref[1:2, :] + cv_ref[2:3, :]
        s2 = _sigmoid(u2)
        du2 = dm[:, :CONV_W] * (s2 * (1.0 + u2 * (1.0 - s2)))
        ds_ref[1:2, :] += _colsum(du2 * xhat)
        ds_ref[2:3, :] += _colsum(du2)
        dxh = du2 * cv_ref[1:2, :]
        du1 = rstd * (dxh - _split_dot(dxh, avg_ref) - xhat * _split_dot(dxh * xhat, avg_ref))
        ds_ref[0:1, :] += _colsum(du1)
        dbuf[pl.ds(0, tm), :] = du1
        dbuf[pl.ds(tm, CONV_HIST), :] = dcarry[...]
        dcarry[...] = dbuf[pl.ds(0, CONV_HIST), :]

        def conv_block(r0):
            for l in range(CONV_W // 128):
                ls = pl.ds(128 * l, 128)
                dwin = dbuf[pl.ds(r0, rb + CONV_HIST), ls]
                uwin = ubuf[pl.ds(r0, rb + CONV_HIST), ls]
                dblk = dwin[0:rb]
                du0 = jnp.zeros((rb, 128), F32)
                for s in range(8):
                    ds_ = _shifted(dwin, s)
                    us_ = _shifted(uwin, s)
                    for q in range(CONV_HIST // 8 + 1):
                        o = 8 * q + s
                        if 0 <= CONV_TAPS - 1 - o < CONV_TAPS:
                            j = CONV_TAPS - 1 - o
                            du0 = du0 + ck_ref[j:j + 1, ls] * ds_[8 * q:8 * q + rb]
                        j = o - 2
                        if 0 <= j < CONV_TAPS:
                            prod = dblk * us_[8 * q:8 * q + rb]
                            part = prod[0:8]
                            for v in range(1, rb // 8):
                                part = part + prod[8 * v:8 * v + 8]
                            dkacc[pl.ds(8 * j, 8), ls] += part
                dzs[pl.ds(r0, rb), ls] = du0

        _for_row_blocks(tm // rb, rb, conv_block, unroll=MIXER_UNROLL)
        du0 = dzs[:, 0:CONV_W]
        dz_ref[:, 0:CONV_W] = (du0 * sg).astype(BF16)
        dz_ref[:, CONV_W:2 * CONV_W] = (du0 * a * sg * (1.0 - sg)).astype(BF16)

        for gi, w in enumerate(POOL_WINDOWS):
            ls = pl.ds(POOL_G * gi, POOL_G)
            cols = slice(CONV_W + POOL_G * gi, CONV_W + POOL_G * (gi + 1))
            inv = _inv_count(ti, tm, w)
            s = pbuf[pl.ds(POOL_HIST, tm), ls]
            for j in range(1, w):
                s = s + pbuf[pl.ds(POOL_HIST - j, tm), ls]
            d = (s * inv - p[:, POOL_G * gi:POOL_G * (gi + 1)]).astype(BF16)
            dyp = dm[:, cols]
            ds_ref[3:4, ls] += _colsum(dyp * _dot(d, pw_ref[gi]))
            dyb = (dyp * cv_ref[3:4, ls]).astype(BF16)
            dpw_ref[gi] += _dot_tn(d, dyb)
            dd = _dot(dyb, pwt_ref[gi])
            ebuf[pl.ds(0, tm), ls] = dd * inv
            ebuf[pl.ds(tm, POOL_HIST), ls] = ecarry[:, ls]
            dp = ebuf[pl.ds(0, tm), ls] - dd
            for j in range(1, w):
                dp = dp + ebuf[pl.ds(j, tm), ls]
            dz_ref[:, pl.ds(2 * CONV_W + POOL_G * gi, POOL_G)] = dp.astype(BF16)
        ecarry[...] = ebuf[pl.ds(0, POOL_HIST), :]

        dhn = _dot(dz_ref[...], wit_ref[...])
        xh, r = _rms(h0_ref[...])
        dx, dg_rows = _rms_bwd(dhn, xh, r, g1_ref[...])
        dh0_ref[...] = dh1_ref[...] + dx
        dg1_ref[...] += _colsum(dg_rows)

        @pl.when(i == nt - 1)
        def _():
            for j in range(CONV_TAPS):
                dk_ref[j:j + 1, :] = _colsum(dkacc[pl.ds(8 * j, 8), :])
            dk_ref[CONV_TAPS:CONV_HIST, :] = jnp.zeros((CONV_HIST - CONV_TAPS, CONV_W), F32)

    def tile(c):
        return pl.BlockSpec((tm, c), lambda i: (nt - 1 - i, 0))

    halo = pl.BlockSpec((CONV_HIST, IN_COLS), lambda i: (jnp.maximum((nt - 1 - i) * (tm // CONV_HIST) - 1, 0), 0))
    return pl.pallas_call(
        body, name="mixer_bwd", grid=(nt,),
        in_specs=[tile(D_MODEL), tile(D_MODEL), tile(IN_COLS), halo, tile(CONV_W), _whole((D_MODEL, D_MODEL)),
                  _whole((IN_COLS, D_MODEL)), _whole((CONV_HIST, CONV_W)), _whole((8, CONV_W)), _whole((CONV_W, CONV_W)),
                  _whole((4, POOL_G, POOL_G)), _whole((4, POOL_G, POOL_G)), _whole((1, D_MODEL))],
        out_specs=[tile(D_MODEL), tile(IN_COLS), _whole((CONV_HIST, CONV_W)), _whole((8, CONV_W)),
                   _whole((4, POOL_G, POOL_G)), _whole((1, D_MODEL))],
        out_shape=[jax.ShapeDtypeStruct((rows, D_MODEL), F32), jax.ShapeDtypeStruct((rows, IN_COLS), BF16),
                   jax.ShapeDtypeStruct((CONV_HIST, CONV_W), F32), jax.ShapeDtypeStruct((8, CONV_W), F32),
                   jax.ShapeDtypeStruct((4, POOL_G, POOL_G), F32), jax.ShapeDtypeStruct((1, D_MODEL), F32)],
        scratch_shapes=[pltpu.VMEM((CONV_HIST + tm, CONV_W), F32), pltpu.VMEM((tm + CONV_HIST, CONV_W), F32),
                        pltpu.VMEM((POOL_HIST + tm, POOL_W), F32), pltpu.VMEM((tm + POOL_HIST, POOL_W), F32),
                        pltpu.VMEM((CONV_HIST, CONV_W), F32), pltpu.VMEM((POOL_HIST, POOL_W), F32),
                        pltpu.VMEM((8 * CONV_HIST, CONV_W), F32), pltpu.VMEM((tm, CONV_W), F32)],
        compiler_params=_cparams(1, V7X_VMEM_LIMIT),
    )(dh1, h0, z, z, u1, w_out_t, w_in_t, conv_k, conv_v, avg, pool_w, pool_w_t, g1)


def _wgrad(lhs, rhs, lhs_spec, rhs_spec, n_chunks, out_rows, out_cols, nt, name):
    def body(l_ref, r_ref, o_ref):
        @pl.when(pl.program_id(1) == 0)
        def _():
            o_ref[...] = jnp.zeros(o_ref.shape, F32)

        l = (l_ref[0] if len(l_ref.shape) == 3 else l_ref[...]).astype(BF16)
        r = (r_ref[0] if len(r_ref.shape) == 3 else r_ref[...]).astype(BF16)
        o_ref[0] += _dot_tn(l, r)

    return pl.pallas_call(
        body, name=name, grid=(n_chunks, nt), in_specs=[lhs_spec, rhs_spec],
        out_specs=pl.BlockSpec((1, out_rows, out_cols), lambda c, t: (c, 0, 0)),
        out_shape=jax.ShapeDtypeStruct((n_chunks, out_rows, out_cols), F32),
        compiler_params=_cparams(2, V7X_VMEM_LIMIT),
    )(lhs, rhs)


def _head_average():
    head = lax.broadcasted_iota(jnp.int32, (CONV_W, CONV_W), 0) // 64
    return jnp.where(head == head.T, 1.0 / 64, 0.0).astype(BF16)


def _wgrad_tile(rows, tm):
    return 768 if rows % 768 == 0 else tm


def _local_step(h_pad, tgt_pad, seq, layers, final_g, tm):
    rows = h_pad.shape[0]
    tmw = _wgrad_tile(rows, tm)
    ntw = rows // tmw
    avg = _head_average()
    depth = len(layers)
    saved = []
    h = h_pad
    hn = _norm_first(h, layers[0]["g1"], tm)
    for l, w in enumerate(layers):
        g_next = layers[l + 1]["g1"] if l + 1 < depth else final_g
        z, u1, m, h1, hn2 = _mixer_fwd(h, hn, w["w_in"], w["conv_k"], w["conv_v"], avg, w["pool_w"], w["w_out"], w["g2"], tm)
        ug0, h2, hn_next = _ffn_fwd(h1, hn2, w["w_up"], w["kf"], w["w_down"], g_next, tm)
        saved.append((h, hn, z, u1, m, h1, hn2, ug0))
        h, hn = h2, hn_next
    dh, loss_cols, dfinal_g = _loss_head(h, tgt_pad, final_g, seq, tm)

    row_tile = pl.BlockSpec((tmw, D_MODEL), lambda c, t: (t, 0))
    grads = [None] * depth
    for l in reversed(range(depth)):
        w = layers[l]
        h0, hn1, z, u1, m, h1, hn2, ug0 = saved[l]
        dh1, dug0, act, dkf, dg2 = _ffn_bwd(dh, h1, ug0, w["w_down_t"], w["w_up_t"], w["kf"], w["g2"], tm)
        chunk_tile = pl.BlockSpec((1, tmw, FF_CHUNK), lambda c, t: (c, t, 0))
        dw_up = _wgrad(hn2, dug0, row_tile, chunk_tile, 4, D_MODEL, FF_CHUNK, ntw, "wgrad_up")
        dw_down = _wgrad(act, dh, chunk_tile, row_tile, 2, FF_CHUNK, D_MODEL, ntw, "wgrad_down")
        dh0, dz, dk, dsmall, dpw, dg1 = _mixer_bwd(dh1, h0, z, u1, w["w_out_t"], w["w_in_t"], w["conv_k"], w["conv_v"], avg,
                                                   w["pool_w"], w["pool_w_t"], w["g1"], tm)
        dw_out = _wgrad(m, dh1, row_tile, row_tile, 1, D_MODEL, D_MODEL, ntw, "wgrad_out")
        in_tile = pl.BlockSpec((tmw, IN_COLS // 4), lambda c, t: (t, c))
        dw_in = _wgrad(hn1, dz, row_tile, in_tile, 4, D_MODEL, IN_COLS // 4, ntw, "wgrad_in")
        grads[l] = dict(dw_in=dw_in, dw_out=dw_out.reshape(4, D_MODEL // 4, D_MODEL), dw_up=dw_up,
                        dw_down=dw_down.reshape(4, D_FF // 4, D_MODEL), dk=dk, dsmall=dsmall, dpw=dpw, dg1=dg1, dg2=dg2, dkf=dkf)
        dh = dh0
    return loss_cols, dh, grads, dfinal_g


_ANY = pl.BlockSpec(memory_space=pl.ANY)


def _place():
    x, y, c = lax.axis_index("x"), lax.axis_index("y"), lax.axis_index("c")
    chips = [(1 - x, y), (x, 1 - y), (1 - x, 1 - y)]
    return x, y, c, chips


def _remote(src, dst, send_sems, recv_sems, idx, to):
    return pltpu.make_async_remote_copy(src_ref=src, dst_ref=dst, send_sem=send_sems.at[idx], recv_sem=recv_sems.at[idx],
                                        device_id=to, device_id_type=MESH)


def _gather_chips(xs):
    n = len(xs)

    def body(*refs):
        x_refs, o_refs = refs[:n], refs[n:2 * n]
        send_sems, recv_sems = refs[2 * n:]
        x, y, c, chips = _place()
        k = 2 * x + y
        sibling = (x, y, 1 - c)
        sends = []
        for j, chip in enumerate(chips):
            for a in range(n):
                sends.append(_remote(x_refs[a].at[k, c], o_refs[a].at[k, c], send_sems, recv_sems, 3 * a + j, (*chip, c)))
                sends[-1].start()
        for j, chip in enumerate(chips):
            kj = 2 * chip[0] + chip[1]
            for a in range(n):
                landed = o_refs[a].at[kj, c]
                _remote(landed, landed, send_sems, recv_sems, 3 * a + j, sibling).wait_recv()
                sends.append(_remote(landed, landed, send_sems, recv_sems, 3 * n + 3 * a + j, sibling))
                sends[-1].start()
        for j, chip in enumerate(chips):
            kj = 2 * chip[0] + chip[1]
            for a in range(n):
                passed = o_refs[a].at[kj, 1 - c]
                _remote(passed, passed, send_sems, recv_sems, 3 * n + 3 * a + j, sibling).wait_recv()
        for cp in sends:
            cp.wait_send()

    return pl.pallas_call(
        body, name="gather_chips", in_specs=[_ANY] * n, out_specs=[_ANY] * n,
        out_shape=[jax.ShapeDtypeStruct(v.shape, v.dtype) for v in xs],
        input_output_aliases={a: a for a in range(n)},
        scratch_shapes=[pltpu.SemaphoreType.DMA((6 * n,)), pltpu.SemaphoreType.DMA((6 * n,))],
    )(*xs)


def _pair_exchange(g0s, g1s):
    n = len(g0s)

    def body(*refs):
        g0, g1, out = refs[:n], refs[n:2 * n], refs[2 * n:3 * n]
        send_sems, recv_sems = refs[3 * n:]
        x, y, c, _ = _place()
        sibling = (x, y, 1 - c)
        for a in range(n):
            @pl.when(c == 0)
            def _():
                _remote(g1[a], out[a], send_sems, recv_sems, a, sibling).start()

            @pl.when(c == 1)
            def _():
                _remote(g0[a], out[a], send_sems, recv_sems, a, sibling).start()
        for a in range(n):
            cp = _remote(g0[a], out[a], send_sems, recv_sems, a, sibling)
            cp.wait_recv()
            cp.wait_send()

    return pl.pallas_call(
        body, name="pair_exchange", in_specs=[_ANY] * (2 * n), out_specs=[_ANY] * n,
        out_shape=[jax.ShapeDtypeStruct(v.shape, v.dtype) for v in g0s],
        scratch_shapes=[pltpu.SemaphoreType.DMA((n,)), pltpu.SemaphoreType.DMA((n,))],
    )(*g0s, *g1s)


def _chip_exchange(parts):
    n = len(parts)

    def body(*refs):
        p_refs, o_refs = refs[:n], refs[n:2 * n]
        send_sems, recv_sems = refs[2 * n:]
        x, y, c, chips = _place()
        sends = []
        for j, chip in enumerate(chips):
            kj = 2 * chip[0] + chip[1]
            for a in range(n):
                sends.append(_remote(p_refs[a].at[kj], o_refs[a].at[j], send_sems, recv_sems, 3 * a + j, (*chip, c)))
                sends[-1].start()
        for j, chip in enumerate(chips):
            for a in range(n):
                landed = o_refs[a].at[j]
                _remote(landed, landed, send_sems, recv_sems, 3 * a + j, (*chip, c)).wait_recv()
        for cp in sends:
            cp.wait_send()

    return pl.pallas_call(
        body, name="chip_exchange", in_specs=[_ANY] * n, out_specs=[_ANY] * n,
        out_shape=[jax.ShapeDtypeStruct((3,) + v.shape[1:], v.dtype) for v in parts],
        scratch_shapes=[pltpu.SemaphoreType.DMA((3 * n,)), pltpu.SemaphoreType.DMA((3 * n,))],
    )(*parts)


def _pair_share(reds):
    n = len(reds)

    def body(*refs):
        r_refs, o_refs = refs[:n], refs[n:2 * n]
        send_sems, recv_sems = refs[2 * n:]
        x, y, c, _ = _place()
        sends = [_remote(r_refs[a], o_refs[a], send_sems, recv_sems, a, (x, y, 1 - c)) for a in range(n)]
        for cp in sends:
            cp.start()
        for cp in sends:
            cp.wait_recv()
        for cp in sends:
            cp.wait_send()

    return pl.pallas_call(
        body, name="pair_share", in_specs=[_ANY] * n, out_specs=[_ANY] * n,
        out_shape=[jax.ShapeDtypeStruct(v.shape, v.dtype) for v in reds],
        scratch_shapes=[pltpu.SemaphoreType.DMA((n,)), pltpu.SemaphoreType.DMA((n,))],
    )(*reds)


def _all_reduce_small(pack):
    p, cols = pack.shape

    def body(x_ref, o_ref, sib, chipbuf, send_sems, recv_sems):
        x, y, c, chips = _place()
        k = 2 * x + y
        sibling = (x, y, 1 - c)
        pair = _remote(x_ref, sib, send_sems, recv_sems, 0, sibling)
        pair.start()
        pair.wait_recv()
        chipbuf[k] = x_ref[...] + sib[...]
        sends = [_remote(chipbuf.at[k], chipbuf.at[k], send_sems, recv_sems, 1 + j, (*chip, c)) for j, chip in enumerate(chips)]
        for cp in sends:
            cp.start()
        for j, chip in enumerate(chips):
            landed = chipbuf.at[2 * chip[0] + chip[1]]
            _remote(landed, landed, send_sems, recv_sems, 1 + j, (*chip, c)).wait_recv()
        pair.wait_send()
        for cp in sends:
            cp.wait_send()
        o_ref[...] = (chipbuf[0] + chipbuf[1]) + (chipbuf[2] + chipbuf[3])

    vm = pl.BlockSpec(memory_space=pltpu.VMEM)
    return pl.pallas_call(
        body, name="all_reduce_small", in_specs=[vm], out_specs=vm,
        out_shape=jax.ShapeDtypeStruct(pack.shape, F32),
        scratch_shapes=[pltpu.VMEM((p, cols), F32), pltpu.VMEM((4, p, cols), F32),
                        pltpu.SemaphoreType.DMA((4,)), pltpu.SemaphoreType.DMA((4,))],
    )(pack)


_BIG = ("w_in", "w_out", "w_up", "w_down")
_SMALL = ("norm1_g", "conv_dw_b", "conv_ln_g", "conv_ln_b", "pool_w", "pool_scale", "norm2_g", "final_g",
          "meta_tokens", "conv_dw_k", "ffn_dw_k")


def _rows8(v):
    return jnp.pad(v, ((0, -v.shape[0] % 8), (0, 0)))


def _pack_flat(arrs, rows):
    flat = jnp.concatenate([a.reshape(-1) for a in arrs])
    return jnp.pad(flat, (0, rows * D_MODEL - flat.shape[0])).reshape(1, rows, D_MODEL)


def _unpack_flat(packed, like):
    flat = packed.reshape(-1)
    out, off = [], 0
    for a in like:
        out.append(flat[off:off + a.size].reshape(a.shape))
        off += a.size
    return out


def kernel(x, meta_tokens, norm1_g, w_in, conv_dw_k, conv_dw_b, conv_ln_g, conv_ln_b, pool_w, pool_scale, w_out, norm2_g, w_up, ffn_dw_k, w_down, final_g, loss_target, m_meta_tokens, m_norm1_g, m_w_in, m_conv_dw_k, m_conv_dw_b, m_conv_ln_g, m_conv_ln_b, m_pool_w, m_pool_scale, m_w_out, m_norm2_g, m_w_up, m_ffn_dw_k, m_w_down, m_final_g, v_meta_tokens, v_norm1_g, v_w_in, v_conv_dw_k, v_conv_dw_b, v_conv_ln_g, v_conv_ln_b, v_pool_w, v_pool_scale, v_w_out, v_norm2_g, v_w_up, v_ffn_dw_k, v_w_down, v_final_g):
    weights = dict(meta_tokens=meta_tokens, norm1_g=norm1_g, w_in=w_in, conv_dw_k=conv_dw_k, conv_dw_b=conv_dw_b,
                   conv_ln_g=conv_ln_g, conv_ln_b=conv_ln_b, pool_w=pool_w, pool_scale=pool_scale, w_out=w_out,
                   norm2_g=norm2_g, w_up=w_up, ffn_dw_k=ffn_dw_k, w_down=w_down, final_g=final_g)
    mom1 = dict(meta_tokens=m_meta_tokens, norm1_g=m_norm1_g, w_in=m_w_in, conv_dw_k=m_conv_dw_k, conv_dw_b=m_conv_dw_b,
                conv_ln_g=m_conv_ln_g, conv_ln_b=m_conv_ln_b, pool_w=m_pool_w, pool_scale=m_pool_scale, w_out=m_w_out,
                norm2_g=m_norm2_g, w_up=m_w_up, ffn_dw_k=m_ffn_dw_k, w_down=m_w_down, final_g=m_final_g)
    mom2 = dict(meta_tokens=v_meta_tokens, norm1_g=v_norm1_g, w_in=v_w_in, conv_dw_k=v_conv_dw_k, conv_dw_b=v_conv_dw_b,
                conv_ln_g=v_conv_ln_g, conv_ln_b=v_conv_ln_b, pool_w=v_pool_w, pool_scale=v_pool_scale, w_out=v_w_out,
                norm2_g=v_norm2_g, w_up=v_w_up, ffn_dw_k=v_ffn_dw_k, w_down=v_w_down, final_g=v_final_g)
    order = list(weights)
    depth = w_in.shape[0]
    seq = x.shape[1]
    tm = ROW_TILE
    rows = -(-(N_META + seq) // tm) * tm
    chip = 2 * lax.axis_index("x") + lax.axis_index("y")
    core = lax.axis_index("c")
    chip_arr = chip.astype(jnp.int32).reshape(1)
    core_arr = core.astype(jnp.int32).reshape(1)

    small_sharded = dict(conv_dw_k=jnp.pad(conv_dw_k, ((0, 0), (0, CONV_HIST - CONV_TAPS), (0, 0))),
                         ffn_dw_k=jnp.pad(ffn_dw_k, ((0, 0), (0, 8 - FFN_TAPS), (0, 0))),
                         meta_tokens=meta_tokens.reshape(2, N_META // 2, D_MODEL // 4))
    g_in, g_out, g_up, g_down, g_cdk, g_fdk, g_meta = _gather_chips(
        [_place_own(chip_arr, weights[nm], BF16, "place_" + nm) for nm in _BIG]
        + [_place_own(chip_arr, v, F32, "place_" + nm) for nm, v in small_sharded.items()])
    meta_full = g_meta.transpose(1, 2, 0, 3).reshape(N_META, D_MODEL)
    layers = []
    for l in range(depth):
        w_in_l = g_in[:, l].transpose(1, 0, 2).reshape(D_MODEL, IN_COLS)
        w_out_l = g_out[:, l].reshape(D_MODEL, D_MODEL)
        w_down_l = g_down[:, l].reshape(2, FF_CHUNK, D_MODEL)
        pw = pool_w[l].astype(BF16)
        conv_v = jnp.pad(jnp.stack([conv_dw_b[l], conv_ln_g[l], conv_ln_b[l], pool_scale[l]]), ((0, 4), (0, 0)))
        layers.append(dict(
            w_in=w_in_l, w_in_t=w_in_l.T, conv_k=g_cdk[:, l].transpose(1, 0, 2).reshape(CONV_HIST, CONV_W), conv_v=conv_v,
            pool_w=pw, pool_w_t=pw.transpose(0, 2, 1), w_out=w_out_l, w_out_t=w_out_l.T,
            w_up=g_up[:, l], w_up_t=g_up[:, l].transpose(0, 2, 1), kf=g_fdk[:, l],
            w_down=w_down_l, w_down_t=w_down_l.transpose(0, 2, 1), g1=norm1_g[l][None], g2=norm2_g[l][None]))

    h_pad = jnp.concatenate([meta_full, x[0], jnp.zeros((rows - N_META - seq, D_MODEL), F32)])
    tgt_pad = jnp.pad(loss_target[0], ((N_META, rows - N_META - seq), (0, 0)))
    loss_cols, dh, grads, dfinal_g = _local_step(h_pad, tgt_pad, seq, layers, final_g[None], tm)
    loss = lax.psum(jnp.sum(loss_cols) * (0.5 / D_MODEL), ("x", "y", "c"))
    grad_x = dh[N_META:N_META + seq][None]

    keys = ("dw_in", "dw_out", "dw_up", "dw_down")
    g0s = [grads[0][kk] for kk in keys]
    g1s = [grads[1][kk] for kk in keys]
    theirs = _pair_exchange(g0s, g1s)
    parts = [_pair_sum_bf16(core_arr, g0s[a], g1s[a], theirs[a], "pair_sum_" + _BIG[a]) for a in range(4)]
    received = _chip_exchange(parts)
    reds = [_chip_sum(chip_arr, parts[a], received[a], "chip_sum_" + _BIG[a]) for a in range(4)]
    grad, delta, new_m, new_v = {}, {}, {}, {}
    for a, other in enumerate(_pair_share(reds)):
        nm = _BIG[a]
        grad[nm], delta[nm], new_m[nm], new_v[nm] = _adamw_pair(core_arr, weights[nm], reds[a], other, mom1[nm], mom2[nm],
                                                                "adamw_" + nm)

    pack = jnp.concatenate([
        _rows8(jnp.concatenate([grads[l]["dg1"] for l in range(depth)])),
        _rows8(jnp.concatenate([grads[l]["dg2"] for l in range(depth)])),
        _rows8(dfinal_g),
        jnp.concatenate([grads[l]["dsmall"] for l in range(depth)], axis=1),
        jnp.stack([grads[l]["dpw"] for l in range(depth)]).reshape(-1, D_MODEL),
        dh[:N_META],
        jnp.concatenate([grads[l]["dk"] for l in range(depth)], axis=1),
        jnp.stack([grads[l]["dkf"] for l in range(depth)]).reshape(-1, D_MODEL),
    ])
    red = _all_reduce_small(pack)
    o = 0
    grad["norm1_g"] = red[o:o + depth]
    o += 8
    grad["norm2_g"] = red[o:o + depth]
    o += 8
    grad["final_g"] = red[o]
    o += 8
    sm = red[o:o + 8].reshape(8, depth, CONV_W)
    grad["conv_dw_b"], grad["conv_ln_g"], grad["conv_ln_b"], grad["pool_scale"] = sm[0], sm[1], sm[2], sm[3]
    o += 8
    n_pw = depth * 4 * POOL_G * POOL_G // D_MODEL
    grad["pool_w"] = red[o:o + n_pw].reshape(pool_w.shape)
    o += n_pw
    grad["meta_tokens"] = lax.dynamic_slice_in_dim(red[o:o + N_META], chip * (D_MODEL // 4), D_MODEL // 4, axis=1)
    o += N_META
    dk_all = red[o:o + CONV_HIST].reshape(CONV_HIST, depth, 4, CONV_W // 4)
    grad["conv_dw_k"] = lax.dynamic_index_in_dim(dk_all, chip, axis=2, keepdims=False)[:CONV_TAPS].transpose(1, 0, 2)
    o += CONV_HIST
    dkf_all = red[o:].reshape(depth, 4, 8, FF_CHUNK)
    grad["ffn_dw_k"] = lax.dynamic_index_in_dim(dkf_all, chip, axis=1, keepdims=False)[:, :FFN_TAPS]

    small_rows = -(-sum(weights[nm].size for nm in _SMALL) // (8 * D_MODEL)) * 8
    packed = [_pack_flat([d[nm] for nm in _SMALL], small_rows) for d in (weights, grad, mom1, mom2)]
    for res, packed_out in zip((delta, new_m, new_v), _adamw(*packed, "adamw_small")):
        for nm, val in zip(_SMALL, _unpack_flat(packed_out, [weights[nm] for nm in _SMALL])):
            res[nm] = val

    return (loss, grad_x, *[grad[nm] for nm in order], *[delta[nm] for nm in order],
            *[new_m[nm] for nm in order], *[new_v[nm] for nm in order])
```

```python
import functools

import jax
import jax.numpy as jnp
from jax import lax
from jax.experimental import pallas as pl
from jax.experimental.pallas import tpu as pltpu

F32 = jnp.float32
BF16 = jnp.bfloat16

D_MODEL = 1024
CONV_W = 512
POOL_W = 512
POOL_G = 128
POOL_WINDOWS = (2, 4, 8, 16)
IN_COLS = 1536
D_FF = 2816
FF_CHUNK = 1408
CONV_TAPS = 31
CONV_HIST = 32
POOL_HIST = 16
FFN_TAPS = 3
N_META = 16
EPS = 1e-6

ADAM_LR = 0.001
ADAM_B1 = 0.9
ADAM_B2 = 0.999
ADAM_EPS = 1e-08
ADAM_WD = 0.01
ADAM_STEP = 10

ROW_TILE = 256
CONV_ROW_BLOCK = 64
FFN_ROW_BLOCK = 32
V7X_VMEM_LIMIT = 56 * 1024 * 1024

MESH = pl.DeviceIdType.MESH


def _cparams(n_axes, vmem=None):
    return pltpu.CompilerParams(dimension_semantics=("arbitrary",) * n_axes, vmem_limit_bytes=vmem)


def _whole(shape, single=False):
    zeros = (0,) * len(shape)
    if single:
        return pl.BlockSpec(shape, lambda *_: zeros, pipeline_mode=pl.Buffered(1))
    return pl.BlockSpec(shape, lambda *_: zeros)


def _sigmoid(x):
    return 1.0 / (1.0 + jnp.exp(-x))


def _dot(a, b):
    return jnp.dot(a, b, preferred_element_type=F32)


def _dot_tn(a, b):
    return lax.dot_general(a, b, (((0,), (0,)), ((), ())), preferred_element_type=F32)


def _split_dot(v, a_ref):
    hi = v.astype(BF16)
    lo = (v - hi.astype(F32)).astype(BF16)
    return _dot(hi, a_ref[...]) + _dot(lo, a_ref[...])


def _rms(x):
    r = lax.rsqrt(jnp.mean(x * x, axis=-1, keepdims=True) + EPS)
    return x * r, r


def _rms_bwd(dy, xhat, r, g):
    gd = dy * g
    return r * (gd - xhat * jnp.mean(gd * xhat, axis=-1, keepdims=True)), dy * xhat


def _colsum(v):
    return jnp.sum(v, axis=0, keepdims=True)


def _shifted(window, s):
    return window if s == 0 else pltpu.roll(window, window.shape[0] - s, 0)


def _conv3(window, kf_ref, cc, ls, hist):
    x2 = window[hist:]
    x1 = pltpu.roll(window, 1, 0)[hist:]
    x0 = pltpu.roll(window, 2, 0)[hist:]
    return x0, x1, x2, kf_ref[cc, 0:1, ls] * x0 + kf_ref[cc, 1:2, ls] * x1 + kf_ref[cc, 2:3, ls] * x2


def _for_row_blocks(n, rb, fn, unroll):
    if unroll:
        for r in range(n):
            fn(r * rb)
    else:
        def step(r, keep):
            fn(pl.multiple_of(r * rb, rb))
            return keep

        lax.fori_loop(0, n, step, 0)


def _fold8(v):
    part = v[0:8]
    for k in range(1, v.shape[0] // 8):
        part = part + v[8 * k:8 * k + 8]
    return part


def _inv_count(tile, tm, w):
    t = tile * tm + lax.broadcasted_iota(jnp.int32, (tm, POOL_G), 0)
    return 1.0 / jnp.minimum(t + 1, w).astype(F32)


def _norm_first(h, g, tm):
    rows = h.shape[0]

    def body(h_ref, g_ref, o_ref):
        xhat, _ = _rms(h_ref[...])
        o_ref[...] = (xhat * g_ref[...]).astype(BF16)

    return pl.pallas_call(
        body, name="norm_first", grid=(rows // tm,),
        in_specs=[pl.BlockSpec((tm, D_MODEL), lambda i: (i, 0)), _whole((1, D_MODEL))],
        out_specs=pl.BlockSpec((tm, D_MODEL), lambda i: (i, 0)),
        out_shape=jax.ShapeDtypeStruct((rows, D_MODEL), BF16),
        compiler_params=_cparams(1),
    )(h, g)


def _row_block(r):
    for cand in (256, 176, 128, 64, 32, 16):
        if r % cand == 0:
            return cand
    return r


def _place_own(chip, w, dtype, name):
    n, r, c = w.shape
    rb = _row_block(r)

    def body(chip_ref, w_ref, o_ref):
        o_ref[0] = w_ref[...].astype(dtype)

    return pl.pallas_call(
        body, name=name,
        grid_spec=pltpu.PrefetchScalarGridSpec(
            num_scalar_prefetch=1, grid=(n, r // rb),
            in_specs=[pl.BlockSpec((1, rb, c), lambda i, j, chip_ref: (i, j, 0))],
            out_specs=pl.BlockSpec((1, 1, rb, c), lambda i, j, chip_ref: (chip_ref[0], i, j, 0))),
        out_shape=jax.ShapeDtypeStruct((4,) + w.shape, dtype), compiler_params=_cparams(2),
    )(chip, w)


def _pair_sum_bf16(core, g0, g1, other, name):
    n, r, c = g0.shape
    rb = _row_block(r)

    def body(core_ref, g0_ref, g1_ref, o_ref, out_ref):
        mine = jnp.where(core_ref[0] == 0, g0_ref[...], g1_ref[...])
        out_ref[...] = (mine + o_ref[...]).astype(BF16)

    spec = pl.BlockSpec((1, rb, c), lambda i, j: (i, j, 0))
    return pl.pallas_call(
        body, name=name, grid=(n, r // rb),
        in_specs=[pl.BlockSpec(memory_space=pltpu.SMEM), spec, spec, spec], out_specs=spec,
        out_shape=jax.ShapeDtypeStruct(g0.shape, BF16), compiler_params=_cparams(2),
    )(core, g0, g1, other)


def _chip_sum(chip, parts, recv, name):
    _, r, c = parts.shape
    rb = _row_block(r)

    def body(chip_ref, p_ref, r_ref, out_ref):
        got = r_ref[...].astype(F32)
        out_ref[...] = (p_ref[0].astype(F32) + got[0]) + (got[1] + got[2])

    return pl.pallas_call(
        body, name=name,
        grid_spec=pltpu.PrefetchScalarGridSpec(
            num_scalar_prefetch=1, grid=(r // rb,),
            in_specs=[pl.BlockSpec((1, rb, c), lambda j, chip_ref: (chip_ref[0], j, 0)),
                      pl.BlockSpec((3, rb, c), lambda j, chip_ref: (0, j, 0))],
            out_specs=pl.BlockSpec((rb, c), lambda j, chip_ref: (j, 0))),
        out_shape=jax.ShapeDtypeStruct((r, c), F32), compiler_params=_cparams(1),
    )(chip, parts, recv)


def _adamw_update(w, g, m, v):
    nm = ADAM_B1 * m + (1.0 - ADAM_B1) * g
    nv = ADAM_B2 * v + (1.0 - ADAM_B2) * (g * g)
    m_hat = nm / (1.0 - ADAM_B1 ** ADAM_STEP)
    v_hat = nv / (1.0 - ADAM_B2 ** ADAM_STEP)
    return -ADAM_LR * (m_hat / (jnp.sqrt(v_hat) + ADAM_EPS) + ADAM_WD * w), nm, nv


def _adamw(w, g, m, v, name):
    n, r, c = w.shape
    rb = _row_block(r)

    def body(w_ref, g_ref, m_ref, v_ref, d_ref, nm_ref, nv_ref):
        d_ref[...], nm_ref[...], nv_ref[...] = _adamw_update(w_ref[...], g_ref[...], m_ref[...], v_ref[...])

    spec = pl.BlockSpec((1, rb, c), lambda i, j: (i, j, 0))
    shp = jax.ShapeDtypeStruct(w.shape, F32)
    return pl.pallas_call(
        body, name=name, grid=(n, r // rb), in_specs=[spec] * 4, out_specs=[spec] * 3,
        out_shape=[shp] * 3, compiler_params=_cparams(2),
    )(w, g, m, v)


def _adamw_pair(core, w, mine, theirs, m, v, name):
    n, r, c = w.shape
    rb = _row_block(r)

    def body(core_ref, w_ref, a_ref, b_ref, m_ref, v_ref, g_ref, d_ref, nm_ref, nv_ref):
        g = jnp.where(pl.program_id(0) == core_ref[0], a_ref[...], b_ref[...])
        g_ref[0] = g
        d_ref[0], nm_ref[0], nv_ref[0] = _adamw_update(w_ref[0], g, m_ref[0], v_ref[0])

    spec = pl.BlockSpec((1, rb, c), lambda i, j, core_ref: (i, j, 0))
    flat = pl.BlockSpec((rb, c), lambda i, j, core_ref: (j, 0))
    shp = jax.ShapeDtypeStruct(w.shape, F32)
    return pl.pallas_call(
        body, name=name,
        grid_spec=pltpu.PrefetchScalarGridSpec(num_scalar_prefetch=1, grid=(n, r // rb),
                                               in_specs=[spec, flat, flat, spec, spec], out_specs=[spec] * 4),
        out_shape=[shp] * 4, compiler_params=_cparams(2),
    )(core, w, mine, theirs, m, v)


def _mixer_fwd(h0, hn, w_in, conv_k, conv_v, avg, pool_w, w_out, g2, tm):
    rows = h0.shape[0]
    rb = CONV_ROW_BLOCK

    def body(h0_ref, hn_ref, win_ref, ck_ref, cv_ref, avg_ref, pw_ref, wout_ref, g2_ref,
             z_ref, u1_ref, m_ref, h1_ref, hn2_ref, ubuf, pbuf):
        i = pl.program_id(0)

        @pl.when(i == 0)
        def _():
            ubuf[pl.ds(0, CONV_HIST), :] = jnp.zeros((CONV_HIST, CONV_W), F32)
            pbuf[pl.ds(0, POOL_HIST), :] = jnp.zeros((POOL_HIST, POOL_W), F32)

        z = _dot(hn_ref[...], win_ref[...])
        z_ref[...] = z
        ubuf[pl.ds(CONV_HIST, tm), :] = z[:, :CONV_W] * _sigmoid(z[:, CONV_W:2 * CONV_W])
        p = z[:, 2 * CONV_W:]
        pbuf[pl.ds(POOL_HIST, tm), :] = p

        def conv_block(r0):
            for l in range(CONV_W // 128):
                ls = pl.ds(128 * l, 128)
                window = ubuf[pl.ds(r0, rb + CONV_HIST), ls]
                acc = jnp.broadcast_to(cv_ref[0:1, ls], (rb, 128))
                for s in range(8):
                    ws = _shifted(window, s)
                    for q in range(CONV_HIST // 8 + 1):
                        j = 8 * q + s - 2
                        if 0 <= j < CONV_TAPS:
                            acc = acc + ck_ref[j:j + 1, ls] * ws[8 * q:8 * q + rb]
                u1_ref[pl.ds(r0, rb), ls] = acc

        _for_row_blocks(tm // rb, rb, conv_block, unroll=True)
        ubuf[pl.ds(0, CONV_HIST), :] = ubuf[pl.ds(tm, CONV_HIST), :]

        u1 = u1_ref[...]
        cen = u1 - _split_dot(u1, avg_ref)
        xhat = cen * lax.rsqrt(_split_dot(cen * cen, avg_ref) + EPS)
        u2 = xhat * cv_ref[1:2, :] + cv_ref[2:3, :]
        m_ref[:, 0:CONV_W] = (u2 * _sigmoid(u2)).astype(BF16)

        for gi, w in enumerate(POOL_WINDOWS):
            ls = pl.ds(POOL_G * gi, POOL_G)
            s = pbuf[pl.ds(POOL_HIST, tm), ls]
            for j in range(1, w):
                s = s + pbuf[pl.ds(POOL_HIST - j, tm), ls]
            d = s * _inv_count(i, tm, w) - p[:, POOL_G * gi:POOL_G * (gi + 1)]
            y = _dot(d.astype(BF16), pw_ref[gi]) * cv_ref[3:4, ls]
            m_ref[:, pl.ds(CONV_W + POOL_G * gi, POOL_G)] = y.astype(BF16)
        pbuf[pl.ds(0, POOL_HIST), :] = pbuf[pl.ds(tm, POOL_HIST), :]

        h1 = h0_ref[...] + _dot(m_ref[...], wout_ref[...])
        h1_ref[...] = h1
        xh, _ = _rms(h1)
        hn2_ref[...] = (xh * g2_ref[...]).astype(BF16)

    def tile(c):
        return pl.BlockSpec((tm, c), lambda i: (i, 0))

    return pl.pallas_call(
        body, name="mixer_fwd", grid=(rows // tm,),
        in_specs=[tile(D_MODEL), tile(D_MODEL), _whole((D_MODEL, IN_COLS)), _whole((CONV_HIST, CONV_W)),
                  _whole((8, CONV_W)), _whole((CONV_W, CONV_W)), _whole((4, POOL_G, POOL_G)),
                  _whole((D_MODEL, D_MODEL)), _whole((1, D_MODEL))],
        out_specs=[tile(IN_COLS), tile(CONV_W), tile(D_MODEL), tile(D_MODEL), tile(D_MODEL)],
        out_shape=[jax.ShapeDtypeStruct((rows, IN_COLS), F32), jax.ShapeDtypeStruct((rows, CONV_W), F32),
                   jax.ShapeDtypeStruct((rows, D_MODEL), BF16), jax.ShapeDtypeStruct((rows, D_MODEL), F32),
                   jax.ShapeDtypeStruct((rows, D_MODEL), BF16)],
        scratch_shapes=[pltpu.VMEM((CONV_HIST + tm, CONV_W), F32), pltpu.VMEM((POOL_HIST + tm, POOL_W), F32)],
        compiler_params=_cparams(1, V7X_VMEM_LIMIT),
    )(h0, hn, w_in, conv_k, conv_v, avg, pool_w, w_out, g2)


def _ffn_fwd(h1, hn2, w_up, layer, kf, w_down, g_next, tm):
    rows = h1.shape[0]
    hist = 8
    w_up_spec = pl.BlockSpec((4, None, D_MODEL, FF_CHUNK), lambda i: (0, layer, 0, 0), pipeline_mode=pl.Buffered(1))

    rb = FFN_ROW_BLOCK

    def body(h1_ref, hn2_ref, wup_ref, kf_ref, wdn_ref, gn_ref, ug_ref, h2_ref, hnn_ref, wg, wv, carry, act_s, acc):
        i = pl.program_id(0)

        @pl.when(i == 0)
        def _():
            carry[...] = jnp.zeros(carry.shape, F32)

        acc[...] = h1_ref[...]
        for c in range(2):
            for buf, cc in ((wg, c), (wv, c + 2)):
                ug = _dot(hn2_ref[...], wup_ref[cc])
                ug_ref[cc] = ug.astype(BF16)
                buf[pl.ds(0, hist), :] = carry[cc]
                buf[pl.ds(hist, tm), :] = ug
                carry[cc] = buf[pl.ds(tm, hist), :]

            def act_block(r0):
                for l in range(FF_CHUNK // 128):
                    ls = pl.ds(128 * l, 128)
                    gate = _conv3(wg[pl.ds(r0, rb + hist), ls], kf_ref, c, ls, hist)[3]
                    val = _conv3(wv[pl.ds(r0, rb + hist), ls], kf_ref, c + 2, ls, hist)[3]
                    act_s[pl.ds(r0, rb), ls] = (gate * _sigmoid(gate) * val).astype(BF16)

            _for_row_blocks(tm // rb, rb, act_block, unroll=True)
            acc[...] += _dot(act_s[...], wdn_ref[c])
        h2 = acc[...]
        h2_ref[...] = h2
        xh, _ = _rms(h2)
        hnn_ref[...] = (xh * gn_ref[...]).astype(BF16)

    def tile(c):
        return pl.BlockSpec((tm, c), lambda i: (i, 0))

    return pl.pallas_call(
        body, name="ffn_fwd", grid=(rows // tm,),
        in_specs=[tile(D_MODEL), tile(D_MODEL), w_up_spec, _whole((4, 8, FF_CHUNK)),
                  _whole((2, FF_CHUNK, D_MODEL), single=True), _whole((1, D_MODEL))],
        out_specs=[pl.BlockSpec((4, tm, FF_CHUNK), lambda i: (0, i, 0)), tile(D_MODEL), tile(D_MODEL)],
        out_shape=[jax.ShapeDtypeStruct((4, rows, FF_CHUNK), BF16), jax.ShapeDtypeStruct((rows, D_MODEL), F32),
                   jax.ShapeDtypeStruct((rows, D_MODEL), BF16)],
        scratch_shapes=[pltpu.VMEM((hist + tm, FF_CHUNK), F32), pltpu.VMEM((hist + tm, FF_CHUNK), F32),
                        pltpu.VMEM((4, hist, FF_CHUNK), F32), pltpu.VMEM((tm, FF_CHUNK), BF16),
                        pltpu.VMEM((tm, D_MODEL), F32)],
        compiler_params=_cparams(1, V7X_VMEM_LIMIT),
    )(h1, hn2, w_up, kf, w_down, g_next)


def _loss_head(h, tgt, g, seq, tm):
    rows = h.shape[0]
    tgt_tiles = -(-seq // tm)

    def body(h_ref, tprev_ref, t_ref, g_ref, dh_ref, loss_ref, dg_ref):
        i = pl.program_id(0)

        @pl.when(i == 0)
        def _():
            loss_ref[...] = jnp.zeros(loss_ref.shape, F32)
            dg_ref[...] = jnp.zeros(dg_ref.shape, F32)

        t = i * tm + lax.broadcasted_iota(jnp.int32, (tm, 1), 0)
        inside = jnp.logical_and(t >= N_META, t < N_META + seq)
        tgt = jnp.concatenate([tprev_ref[...], t_ref[0:tm - N_META, :]], axis=0)
        xhat, r = _rms(h_ref[...])
        err = jnp.where(inside, xhat * g_ref[...] - tgt, 0.0)
        loss_ref[...] += _colsum(err * err)
        dh, dg_rows = _rms_bwd(err * (1.0 / D_MODEL), xhat, r, g_ref[...])
        dh_ref[...] = dh
        dg_ref[...] += _colsum(dg_rows)

        @pl.when(i == rows // tm - 1)
        def _():
            total = jnp.sum(loss_ref[...], axis=1, keepdims=True) * (0.5 / D_MODEL)
            loss_ref[...] = jnp.broadcast_to(total, loss_ref.shape)

    tile = pl.BlockSpec((tm, D_MODEL), lambda i: (i, 0))
    t_prev = pl.BlockSpec((N_META, D_MODEL), lambda i: (jnp.maximum(i * (tm // N_META) - 1, 0), 0))
    t_own = pl.BlockSpec((tm, D_MODEL), lambda i: (jnp.minimum(i, tgt_tiles - 1), 0))
    vec = jax.ShapeDtypeStruct((1, D_MODEL), F32)
    return pl.pallas_call(
        body, name="loss_head", grid=(rows // tm,),
        in_specs=[tile, t_prev, t_own, _whole((1, D_MODEL))],
        out_specs=[tile, _whole((1, D_MODEL)), _whole((1, D_MODEL))],
        out_shape=[jax.ShapeDtypeStruct((rows, D_MODEL), F32), vec, vec],
        compiler_params=_cparams(1),
    )(h, tgt, tgt, g)


def _ffn_bwd(dh2, hn2, ug0, w_down_t, w_up_t, kf, tm):
    rows = dh2.shape[0]
    nt = rows // tm
    hist = 16
    fut = 8
    rb = FFN_ROW_BLOCK
    near = 8

    def body(dh2_ref, hn2_ref, ugg_ref, ugv_ref, hg_ref, hv_ref, wdt_ref, wutg_ref, wutv_ref, kf_ref,
             dhn_ref, dkf_ref, dwup_ref, dwdn_ref, wg, wv, dgb, dvb, carry, dkacc, act_s, dug_s, acc_up, acc_dn):
        c = pl.program_id(0)
        i = pl.program_id(1)
        first_tile = jnp.where(i == nt - 1, 1.0, 0.0)

        @pl.when(i == 0)
        def _():
            carry[...] = jnp.zeros(carry.shape, F32)
            dkacc[...] = jnp.zeros(dkacc.shape, F32)
            acc_up[...] = jnp.zeros(acc_up.shape, F32)
            acc_dn[...] = jnp.zeros(acc_dn.shape, F32)

        def run():
            n_blocks = tm // rb
            dh2b = dh2_ref[...].astype(BF16)
            sides = ((wg, dgb, ugg_ref, hg_ref, 0, c), (wv, dvb, ugv_ref, hv_ref, 1, c + 2))
            for buf, dbuf, u_ref, h_ref, s, cc in sides:
                buf[pl.ds(0, hist), :] = h_ref[0].astype(F32) * (1.0 - first_tile)
                buf[pl.ds(hist, tm), :] = u_ref[0].astype(F32)
                dbuf[pl.ds(tm, fut), :] = carry[s]
            dgb[pl.ds(0, tm), :] = _dot(dh2b, wdt_ref[0])

            def grad_block(r0):
                for l in range(FF_CHUNK // 128):
                    ls = pl.ds(128 * l, 128)
                    g0, g1, g2, gate = _conv3(wg[pl.ds(r0 + hist - near, rb + near), ls], kf_ref, c, ls, near)
                    v0, v1, v2, val = _conv3(wv[pl.ds(r0 + hist - near, rb + near), ls], kf_ref, c + 2, ls, near)
                    sg = _sigmoid(gate)
                    silu = gate * sg
                    act_s[pl.ds(r0, rb), ls] = (silu * val).astype(BF16)
                    dact = dgb[pl.ds(r0, rb), ls]
                    dgate = dact * val * (sg * (1.0 + gate * (1.0 - sg)))
                    dval = dact * silu
                    dgb[pl.ds(r0, rb), ls] = dgate
                    dvb[pl.ds(r0, rb), ls] = dval
                    for s, dv, taps in ((0, dgate, (g0, g1, g2)), (1, dval, (v0, v1, v2))):
                        for j in range(FFN_TAPS):
                            dkacc[s, pl.ds(8 * j, 8), ls] += _fold8(dv * taps[j])

            _for_row_blocks(n_blocks, rb, grad_block, unroll=True)
            for buf, dbuf, u_ref, h_ref, s, cc in sides:
                carry[s] = dbuf[pl.ds(0, fut), :]

            def conv_block(r0):
                for l in range(FF_CHUNK // 128):
                    ls = pl.ds(128 * l, 128)
                    for buf, dbuf, u_ref, h_ref, s, cc in sides:
                        window = dbuf[pl.ds(r0, rb + fut), ls]
                        dug0 = (kf_ref[cc, 0:1, ls] * _shifted(window, 2)[0:rb] + kf_ref[cc, 1:2, ls] * _shifted(window, 1)[0:rb]
                                + kf_ref[cc, 2:3, ls] * window[0:rb])
                        dug_s[s, pl.ds(r0, rb), ls] = dug0.astype(BF16)

            _for_row_blocks(n_blocks, rb, conv_block, unroll=True)
            dhn_ref[0] = _dot(dug_s[0], wutg_ref[0]) + _dot(dug_s[1], wutv_ref[0])
            acc_up[0] += _dot_tn(hn2_ref[...], dug_s[0])
            acc_up[1] += _dot_tn(hn2_ref[...], dug_s[1])
            acc_dn[...] += _dot_tn(act_s[...], dh2b)

        run()

        @pl.when(i == nt - 1)
        def _():
            for s in range(2):
                for j in range(FFN_TAPS):
                    dkf_ref[c + 2 * s, j:j + 1, :] = _colsum(dkacc[s, pl.ds(8 * j, 8), :])
                dkf_ref[c + 2 * s, FFN_TAPS:8, :] = jnp.zeros((8 - FFN_TAPS, FF_CHUNK), F32)
            pltpu.sync_copy(acc_up, dwup_ref.at[c])
            pltpu.sync_copy(acc_dn, dwdn_ref.at[c])

    def tile(cols):
        return pl.BlockSpec((tm, cols), lambda c, i: (nt - 1 - i, 0))

    def chunk(off, r, halo_rows=None):
        if halo_rows is None:
            return pl.BlockSpec((1, r, FF_CHUNK), lambda c, i: (c + off, nt - 1 - i, 0))
        return pl.BlockSpec((1, r, FF_CHUNK), lambda c, i: (c + off, jnp.maximum((nt - 1 - i) * (tm // r) - 1, 0), 0))

    def weight(off, r, cols):
        return pl.BlockSpec((1, r, cols), lambda c, i: (c + off, 0, 0), pipeline_mode=pl.Buffered(1))

    return pl.pallas_call(
        body, name="ffn_bwd", grid=(2, nt),
        in_specs=[tile(D_MODEL), tile(D_MODEL), chunk(0, tm), chunk(2, tm), chunk(0, hist, True), chunk(2, hist, True),
                  weight(0, D_MODEL, FF_CHUNK), weight(0, FF_CHUNK, D_MODEL), weight(2, FF_CHUNK, D_MODEL),
                  _whole((4, 8, FF_CHUNK))],
        out_specs=[pl.BlockSpec((1, tm, D_MODEL), lambda c, i: (c, nt - 1 - i, 0)), _whole((4, 8, FF_CHUNK)), _ANY, _ANY],
        out_shape=[jax.ShapeDtypeStruct((2, rows, D_MODEL), F32), jax.ShapeDtypeStruct((4, 8, FF_CHUNK), F32),
                   jax.ShapeDtypeStruct((2, 2, D_MODEL, FF_CHUNK), F32), jax.ShapeDtypeStruct((2, FF_CHUNK, D_MODEL), F32)],
        scratch_shapes=[pltpu.VMEM((hist + tm, FF_CHUNK), F32), pltpu.VMEM((hist + tm, FF_CHUNK), F32),
                        pltpu.VMEM((tm + fut, FF_CHUNK), F32), pltpu.VMEM((tm + fut, FF_CHUNK), F32),
                        pltpu.VMEM((2, fut, FF_CHUNK), F32), pltpu.VMEM((2, 8 * FFN_TAPS, FF_CHUNK), F32),
                        pltpu.VMEM((tm, FF_CHUNK), BF16), pltpu.VMEM((2, tm, FF_CHUNK), BF16),
                        pltpu.VMEM((2, D_MODEL, FF_CHUNK), F32), pltpu.VMEM((FF_CHUNK, D_MODEL), F32)],
        compiler_params=_cparams(2, V7X_VMEM_LIMIT),
    )(dh2, hn2, ug0, ug0, ug0, ug0, w_down_t, w_up_t, w_up_t, kf)


def _mixer_bwd(dh2, dhn2, h1, g2, h0, z, u1, hn1, m, w_out_t, w_in_t, conv_k, conv_v, avg, pool_w, pool_w_t, g1, tm):
    rows = dh2.shape[0]
    nt = rows // tm
    rb = CONV_ROW_BLOCK
    in_shard = IN_COLS // 4

    def body(dh2_ref, dhn2_ref, h1_ref, g2_ref, h0_ref, z_ref, zh_ref, u1_ref, hn1_ref, m_ref, wot_ref, wit_ref, ck_ref,
             cv_ref, avg_ref, pw_ref, pwt_ref, g1_ref,
             dh0_ref, dk_ref, ds_ref, dpw_ref, dg1_ref, dg2_ref, dwo_ref, dwi_ref,
             ubuf, dbuf, pbuf, ebuf, dcarry, ecarry, dkacc, dzs, dz_s, dh1_s, acc_out, acc_in):
        i = pl.program_id(0)
        ti = nt - 1 - i
        has_past = jnp.where(ti > 0, 1.0, 0.0)

        @pl.when(i == 0)
        def _():
            dcarry[...] = jnp.zeros(dcarry.shape, F32)
            ecarry[...] = jnp.zeros(ecarry.shape, F32)
            dkacc[...] = jnp.zeros(dkacc.shape, F32)
            ds_ref[...] = jnp.zeros(ds_ref.shape, F32)
            dpw_ref[...] = jnp.zeros(dpw_ref.shape, F32)
            dg1_ref[...] = jnp.zeros(dg1_ref.shape, F32)
            dg2_ref[...] = jnp.zeros(dg2_ref.shape, F32)
            acc_out[...] = jnp.zeros(acc_out.shape, F32)
            acc_in[...] = jnp.zeros(acc_in.shape, F32)

        xh1, r1 = _rms(h1_ref[...])
        dx1, dg2_rows = _rms_bwd(dhn2_ref[0] + dhn2_ref[1], xh1, r1, g2_ref[...])
        dh1_s[...] = dh2_ref[...] + dx1
        dg2_ref[...] += _colsum(dg2_rows)
        dh1b = dh1_s[...].astype(BF16)
        acc_out[...] += _dot_tn(m_ref[...], dh1b)
        dm = _dot(dh1b, wot_ref[...])
        z = z_ref[...]
        a = z[:, :CONV_W]
        sg = _sigmoid(z[:, CONV_W:2 * CONV_W])
        p = z[:, 2 * CONV_W:]
        zh = zh_ref[...] * has_past
        ubuf[pl.ds(0, CONV_HIST), :] = zh[:, :CONV_W] * _sigmoid(zh[:, CONV_W:2 * CONV_W])
        ubuf[pl.ds(CONV_HIST, tm), :] = a * sg
        pbuf[pl.ds(0, POOL_HIST), :] = zh[CONV_HIST - POOL_HIST:, 2 * CONV_W:]
        pbuf[pl.ds(POOL_HIST, tm), :] = p

        u1 = u1_ref[...]
        cen = u1 - _split_dot(u1, avg_ref)
        rstd = lax.rsqrt(_split_dot(cen * cen, avg_ref) + EPS)
        xhat = cen * rstd
        u2 = xhat * cv_ref[1:2, :] + cv_ref[2:3, :]
        s2 = _sigmoid(u2)
        du2 = dm[:, :CONV_W] * (s2 * (1.0 + u2 * (1.0 - s2)))
        ds_ref[1:2, :] += _colsum(du2 * xhat)
        ds_ref[2:3, :] += _colsum(du2)
        dxh = du2 * cv_ref[1:2, :]
        du1 = rstd * (dxh - _split_dot(dxh, avg_ref) - xhat * _split_dot(dxh * xhat, avg_ref))
        ds_ref[0:1, :] += _colsum(du1)
        dbuf[pl.ds(0, tm), :] = du1
        dbuf[pl.ds(tm, CONV_HIST), :] = dcarry[...]
        dcarry[...] = dbuf[pl.ds(0, CONV_HIST), :]

        def conv_block(r0):
            for l in range(CONV_W // 128):
                ls = pl.ds(128 * l, 128)
                dwin = dbuf[pl.ds(r0, rb + CONV_HIST), ls]
                uwin = ubuf[pl.ds(r0, rb + CONV_HIST), ls]
                dblk = dwin[0:rb]
                du0 = jnp.zeros((rb, 128), F32)
                for s in range(8):
                    ds_ = _shifted(dwin, s)
                    us_ = _shifted(uwin, s)
                    for q in range(CONV_HIST // 8 + 1):
                        o = 8 * q + s
                        if 0 <= CONV_TAPS - 1 - o < CONV_TAPS:
                            j = CONV_TAPS - 1 - o
                            du0 = du0 + ck_ref[j:j + 1, ls] * ds_[8 * q:8 * q + rb]
                        j = o - 2
                        if 0 <= j < CONV_TAPS:
                            prod = dblk * us_[8 * q:8 * q + rb]
                            part = prod[0:8]
                            for v in range(1, rb // 8):
                                part = part + prod[8 * v:8 * v + 8]
                            dkacc[pl.ds(8 * j, 8), ls] += part
                dzs[pl.ds(r0, rb), ls] = du0

        _for_row_blocks(tm // rb, rb, conv_block, unroll=True)
        du0 = dzs[:, 0:CONV_W]
        dz_s[:, 0:CONV_W] = (du0 * sg).astype(BF16)
        dz_s[:, CONV_W:2 * CONV_W] = (du0 * a * sg * (1.0 - sg)).astype(BF16)

        for gi, w in enumerate(POOL_WINDOWS):
            ls = pl.ds(POOL_G * gi, POOL_G)
            cols = slice(CONV_W + POOL_G * gi, CONV_W + POOL_G * (gi + 1))
            inv = _inv_count(ti, tm, w)
            s = pbuf[pl.ds(POOL_HIST, tm), ls]
            for j in range(1, w):
                s = s + pbuf[pl.ds(POOL_HIST - j, tm), ls]
            d = (s * inv - p[:, POOL_G * gi:POOL_G * (gi + 1)]).astype(BF16)
            dyp = dm[:, cols]
            ds_ref[3:4, ls] += _colsum(dyp * _dot(d, pw_ref[gi]))
            dyb = (dyp * cv_ref[3:4, ls]).astype(BF16)
            dpw_ref[gi] += _dot_tn(d, dyb)
            dd = _dot(dyb, pwt_ref[gi])
            ebuf[pl.ds(0, tm), ls] = dd * inv
            ebuf[pl.ds(tm, POOL_HIST), ls] = ecarry[:, ls]
            dp = ebuf[pl.ds(0, tm), ls] - dd
            for j in range(1, w):
                dp = dp + ebuf[pl.ds(j, tm), ls]
            dz_s[:, pl.ds(2 * CONV_W + POOL_G * gi, POOL_G)] = dp.astype(BF16)
        ecarry[...] = ebuf[pl.ds(0, POOL_HIST), :]

        acc_in[...] += _dot_tn(hn1_ref[...], dz_s[...])
        dhn = _dot(dz_s[...], wit_ref[...])
        xh, r = _rms(h0_ref[...])
        dx, dg_rows = _rms_bwd(dhn, xh, r, g1_ref[...])
        dh0_ref[...] = dh1_s[...] + dx
        dg1_ref[...] += _colsum(dg_rows)

        @pl.when(i == nt - 1)
        def _():
            for j in range(CONV_TAPS):
                dk_ref[j:j + 1, :] = _colsum(dkacc[pl.ds(8 * j, 8), :])
            dk_ref[CONV_TAPS:CONV_HIST, :] = jnp.zeros((CONV_HIST - CONV_TAPS, CONV_W), F32)
            pltpu.sync_copy(acc_out, dwo_ref)
            for k in range(4):
                pltpu.sync_copy(acc_in.at[:, pl.ds(in_shard * k, in_shard)], dwi_ref.at[k])

    def tile(c):
        return pl.BlockSpec((tm, c), lambda i: (nt - 1 - i, 0))

    halo = pl.BlockSpec((CONV_HIST, IN_COLS), lambda i: (jnp.maximum((nt - 1 - i) * (tm // CONV_HIST) - 1, 0), 0))
    vec = jax.ShapeDtypeStruct((1, D_MODEL), F32)
    return pl.pallas_call(
        body, name="mixer_bwd", grid=(nt,),
        in_specs=[tile(D_MODEL), pl.BlockSpec((2, tm, D_MODEL), lambda i: (0, nt - 1 - i, 0)), tile(D_MODEL),
                  _whole((1, D_MODEL)), tile(D_MODEL), tile(IN_COLS), halo, tile(CONV_W), tile(D_MODEL), tile(D_MODEL),
                  _whole((D_MODEL, D_MODEL)), _whole((IN_COLS, D_MODEL)), _whole((CONV_HIST, CONV_W)), _whole((8, CONV_W)),
                  _whole((CONV_W, CONV_W)), _whole((4, POOL_G, POOL_G)), _whole((4, POOL_G, POOL_G)), _whole((1, D_MODEL))],
        out_specs=[tile(D_MODEL), _whole((CONV_HIST, CONV_W)), _whole((8, CONV_W)), _whole((4, POOL_G, POOL_G)),
                   _whole((1, D_MODEL)), _whole((1, D_MODEL)), _ANY, _ANY],
        out_shape=[jax.ShapeDtypeStruct((rows, D_MODEL), F32), jax.ShapeDtypeStruct((CONV_HIST, CONV_W), F32),
                   jax.ShapeDtypeStruct((8, CONV_W), F32), jax.ShapeDtypeStruct((4, POOL_G, POOL_G), F32), vec, vec,
                   jax.ShapeDtypeStruct((D_MODEL, D_MODEL), F32), jax.ShapeDtypeStruct((4, D_MODEL, in_shard), F32)],
        scratch_shapes=[pltpu.VMEM((CONV_HIST + tm, CONV_W), F32), pltpu.VMEM((tm + CONV_HIST, CONV_W), F32),
                        pltpu.VMEM((POOL_HIST + tm, POOL_W), F32), pltpu.VMEM((tm + POOL_HIST, POOL_W), F32),
                        pltpu.VMEM((CONV_HIST, CONV_W), F32), pltpu.VMEM((POOL_HIST, POOL_W), F32),
                        pltpu.VMEM((8 * CONV_HIST, CONV_W), F32), pltpu.VMEM((tm, CONV_W), F32),
                        pltpu.VMEM((tm, IN_COLS), BF16), pltpu.VMEM((tm, D_MODEL), F32),
                        pltpu.VMEM((D_MODEL, D_MODEL), F32), pltpu.VMEM((D_MODEL, IN_COLS), F32)],
        compiler_params=_cparams(1, V7X_VMEM_LIMIT),
    )(dh2, dhn2, h1, g2, h0, z, z, u1, hn1, m, w_out_t, w_in_t, conv_k, conv_v, avg, pool_w, pool_w_t, g1)


def _head_average():
    head = lax.broadcasted_iota(jnp.int32, (CONV_W, CONV_W), 0) // 64
    return jnp.where(head == head.T, 1.0 / 64, 0.0).astype(BF16)


def _local_step(h_pad, tgt, seq, layers, w_up_all, final_g, tm):
    avg = _head_average()
    depth = len(layers)
    saved = []
    h = h_pad
    hn = _norm_first(h, layers[0]["g1"], tm)
    for l, w in enumerate(layers):
        g_next = layers[l + 1]["g1"] if l + 1 < depth else final_g
        z, u1, m, h1, hn2 = _mixer_fwd(h, hn, w["w_in"], w["conv_k"], w["conv_v"], avg, w["pool_w"], w["w_out"], w["g2"], tm)
        ug0, h2, hn_next = _ffn_fwd(h1, hn2, w_up_all, l, w["kf"], w["w_down"], g_next, tm)
        saved.append((h, hn, z, u1, m, h1, hn2, ug0))
        h, hn = h2, hn_next
    dh, loss_cols, dfinal_g = _loss_head(h, tgt, final_g, seq, tm)

    grads = [None] * depth
    for l in reversed(range(depth)):
        w = layers[l]
        h0, hn1, z, u1, m, h1, hn2, ug0 = saved[l]
        dhn2, dkf, dw_up, dw_down = _ffn_bwd(dh, hn2, ug0, w["w_down_t"], w["w_up_t"], w["kf"], tm)
        dh0, dk, dsmall, dpw, dg1, dg2, dw_out, dw_in = _mixer_bwd(
            dh, dhn2, h1, w["g2"], h0, z, u1, hn1, m, w["w_out_t"], w["w_in_t"], w["conv_k"], w["conv_v"], avg,
            w["pool_w"], w["pool_w_t"], w["g1"], tm)
        grads[l] = dict(dw_in=dw_in, dw_out=dw_out.reshape(4, D_MODEL // 4, D_MODEL),
                        dw_up=dw_up.transpose(1, 0, 2, 3).reshape(4, D_MODEL, FF_CHUNK),
                        dw_down=dw_down.reshape(4, D_FF // 4, D_MODEL), dk=dk, dsmall=dsmall, dpw=dpw, dg1=dg1, dg2=dg2, dkf=dkf)
        dh = dh0
    return loss_cols, dh, grads, dfinal_g


_ANY = pl.BlockSpec(memory_space=pl.ANY)


def _place():
    x, y, c = lax.axis_index("x"), lax.axis_index("y"), lax.axis_index("c")
    chips = [(1 - x, y), (x, 1 - y), (1 - x, 1 - y)]
    return x, y, c, chips


def _remote(src, dst, send_sems, recv_sems, idx, to):
    return pltpu.make_async_remote_copy(src_ref=src, dst_ref=dst, send_sem=send_sems.at[idx], recv_sem=recv_sems.at[idx],
                                        device_id=to, device_id_type=MESH)


def _gather_chips(xs):
    n = len(xs)

    def body(*refs):
        x_refs, o_refs = refs[:n], refs[n:2 * n]
        send_sems, recv_sems = refs[2 * n:]
        x, y, c, chips = _place()
        k = 2 * x + y
        sibling = (x, y, 1 - c)
        sends = []
        for j, chip in enumerate(chips):
            for a in range(n):
                sends.append(_remote(x_refs[a].at[k, c], o_refs[a].at[k, c], send_sems, recv_sems, 3 * a + j, (*chip, c)))
                sends[-1].start()
        for j, chip in enumerate(chips):
            kj = 2 * chip[0] + chip[1]
            for a in range(n):
                landed = o_refs[a].at[kj, c]
                _remote(landed, landed, send_sems, recv_sems, 3 * a + j, sibling).wait_recv()
                sends.append(_remote(landed, landed, send_sems, recv_sems, 3 * n + 3 * a + j, sibling))
                sends[-1].start()
        for j, chip in enumerate(chips):
            kj = 2 * chip[0] + chip[1]
            for a in range(n):
                passed = o_refs[a].at[kj, 1 - c]
                _remote(passed, passed, send_sems, recv_sems, 3 * n + 3 * a + j, sibling).wait_recv()
        for cp in sends:
            cp.wait_send()

    return pl.pallas_call(
        body, name="gather_chips", in_specs=[_ANY] * n, out_specs=[_ANY] * n,
        out_shape=[jax.ShapeDtypeStruct(v.shape, v.dtype) for v in xs],
        input_output_aliases={a: a for a in range(n)},
        scratch_shapes=[pltpu.SemaphoreType.DMA((6 * n,)), pltpu.SemaphoreType.DMA((6 * n,))],
    )(*xs)


def _pair_exchange(g0s, g1s):
    n = len(g0s)

    def body(*refs):
        g0, g1, out = refs[:n], refs[n:2 * n], refs[2 * n:3 * n]
        send_sems, recv_sems = refs[3 * n:]
        x, y, c, _ = _place()
        sibling = (x, y, 1 - c)
        for a in range(n):
            @pl.when(c == 0)
            def _():
                _remote(g1[a], out[a], send_sems, recv_sems, a, sibling).start()

            @pl.when(c == 1)
            def _():
                _remote(g0[a], out[a], send_sems, recv_sems, a, sibling).start()
        for a in range(n):
            cp = _remote(g0[a], out[a], send_sems, recv_sems, a, sibling)
            cp.wait_recv()
            cp.wait_send()

    return pl.pallas_call(
        body, name="pair_exchange", in_specs=[_ANY] * (2 * n), out_specs=[_ANY] * n,
        out_shape=[jax.ShapeDtypeStruct(v.shape, v.dtype) for v in g0s],
        scratch_shapes=[pltpu.SemaphoreType.DMA((n,)), pltpu.SemaphoreType.DMA((n,))],
    )(*g0s, *g1s)


def _chip_exchange(parts):
    n = len(parts)

    def body(*refs):
        p_refs, o_refs = refs[:n], refs[n:2 * n]
        send_sems, recv_sems = refs[2 * n:]
        x, y, c, chips = _place()
        sends = []
        for j, chip in enumerate(chips):
            kj = 2 * chip[0] + chip[1]
            for a in range(n):
                sends.append(_remote(p_refs[a].at[kj], o_refs[a].at[j], send_sems, recv_sems, 3 * a + j, (*chip, c)))
                sends[-1].start()
        for j, chip in enumerate(chips):
            for a in range(n):
                landed = o_refs[a].at[j]
                _remote(landed, landed, send_sems, recv_sems, 3 * a + j, (*chip, c)).wait_recv()
        for cp in sends:
            cp.wait_send()

    return pl.pallas_call(
        body, name="chip_exchange", in_specs=[_ANY] * n, out_specs=[_ANY] * n,
        out_shape=[jax.ShapeDtypeStruct((3,) + v.shape[1:], v.dtype) for v in parts],
        scratch_shapes=[pltpu.SemaphoreType.DMA((3 * n,)), pltpu.SemaphoreType.DMA((3 * n,))],
    )(*parts)


def _pair_share(reds):
    n = len(reds)

    def body(*refs):
        r_refs, o_refs = refs[:n], refs[n:2 * n]
        send_sems, recv_sems = refs[2 * n:]
        x, y, c, _ = _place()
        sends = [_remote(r_refs[a], o_refs[a], send_sems, recv_sems, a, (x, y, 1 - c)) for a in range(n)]
        for cp in sends:
            cp.start()
        for cp in sends:
            cp.wait_recv()
        for cp in sends:
            cp.wait_send()

    return pl.pallas_call(
        body, name="pair_share", in_specs=[_ANY] * n, out_specs=[_ANY] * n,
        out_shape=[jax.ShapeDtypeStruct(v.shape, v.dtype) for v in reds],
        scratch_shapes=[pltpu.SemaphoreType.DMA((n,)), pltpu.SemaphoreType.DMA((n,))],
    )(*reds)


def _all_reduce_small(pack):
    p, cols = pack.shape

    def body(x_ref, o_ref, sib, chipbuf, send_sems, recv_sems):
        x, y, c, chips = _place()
        k = 2 * x + y
        sibling = (x, y, 1 - c)
        pair = _remote(x_ref, sib, send_sems, recv_sems, 0, sibling)
        pair.start()
        pair.wait_recv()
        chipbuf[k] = x_ref[...] + sib[...]
        sends = [_remote(chipbuf.at[k], chipbuf.at[k], send_sems, recv_sems, 1 + j, (*chip, c)) for j, chip in enumerate(chips)]
        for cp in sends:
            cp.start()
        for j, chip in enumerate(chips):
            landed = chipbuf.at[2 * chip[0] + chip[1]]
            _remote(landed, landed, send_sems, recv_sems, 1 + j, (*chip, c)).wait_recv()
        pair.wait_send()
        for cp in sends:
            cp.wait_send()
        o_ref[...] = (chipbuf[0] + chipbuf[1]) + (chipbuf[2] + chipbuf[3])

    vm = pl.BlockSpec(memory_space=pltpu.VMEM)
    return pl.pallas_call(
        body, name="all_reduce_small", in_specs=[vm], out_specs=vm,
        out_shape=jax.ShapeDtypeStruct(pack.shape, F32),
        scratch_shapes=[pltpu.VMEM((p, cols), F32), pltpu.VMEM((4, p, cols), F32),
                        pltpu.SemaphoreType.DMA((4,)), pltpu.SemaphoreType.DMA((4,))],
    )(pack)


_BIG = ("w_in", "w_out", "w_up", "w_down")
_SMALL = ("norm1_g", "conv_dw_b", "conv_ln_g", "conv_ln_b", "pool_w", "pool_scale", "norm2_g", "final_g",
          "meta_tokens", "conv_dw_k", "ffn_dw_k")


def _rows8(v):
    return jnp.pad(v, ((0, -v.shape[0] % 8), (0, 0)))


def _pack_flat(arrs, rows):
    flat = jnp.concatenate([a.reshape(-1) for a in arrs])
    return jnp.pad(flat, (0, rows * D_MODEL - flat.shape[0])).reshape(1, rows, D_MODEL)


def _unpack_flat(packed, like):
    flat = packed.reshape(-1)
    out, off = [], 0
    for a in like:
        out.append(flat[off:off + a.size].reshape(a.shape))
        off += a.size
    return out


def kernel(x, meta_tokens, norm1_g, w_in, conv_dw_k, conv_dw_b, conv_ln_g, conv_ln_b, pool_w, pool_scale, w_out, norm2_g, w_up, ffn_dw_k, w_down, final_g, loss_target, m_meta_tokens, m_norm1_g, m_w_in, m_conv_dw_k, m_conv_dw_b, m_conv_ln_g, m_conv_ln_b, m_pool_w, m_pool_scale, m_w_out, m_norm2_g, m_w_up, m_ffn_dw_k, m_w_down, m_final_g, v_meta_tokens, v_norm1_g, v_w_in, v_conv_dw_k, v_conv_dw_b, v_conv_ln_g, v_conv_ln_b, v_pool_w, v_pool_scale, v_w_out, v_norm2_g, v_w_up, v_ffn_dw_k, v_w_down, v_final_g):
    weights = dict(meta_tokens=meta_tokens, norm1_g=norm1_g, w_in=w_in, conv_dw_k=conv_dw_k, conv_dw_b=conv_dw_b,
                   conv_ln_g=conv_ln_g, conv_ln_b=conv_ln_b, pool_w=pool_w, pool_scale=pool_scale, w_out=w_out,
                   norm2_g=norm2_g, w_up=w_up, ffn_dw_k=ffn_dw_k, w_down=w_down, final_g=final_g)
    mom1 = dict(meta_tokens=m_meta_tokens, norm1_g=m_norm1_g, w_in=m_w_in, conv_dw_k=m_conv_dw_k, conv_dw_b=m_conv_dw_b,
                conv_ln_g=m_conv_ln_g, conv_ln_b=m_conv_ln_b, pool_w=m_pool_w, pool_scale=m_pool_scale, w_out=m_w_out,
                norm2_g=m_norm2_g, w_up=m_w_up, ffn_dw_k=m_ffn_dw_k, w_down=m_w_down, final_g=m_final_g)
    mom2 = dict(meta_tokens=v_meta_tokens, norm1_g=v_norm1_g, w_in=v_w_in, conv_dw_k=v_conv_dw_k, conv_dw_b=v_conv_dw_b,
                conv_ln_g=v_conv_ln_g, conv_ln_b=v_conv_ln_b, pool_w=v_pool_w, pool_scale=v_pool_scale, w_out=v_w_out,
                norm2_g=v_norm2_g, w_up=v_w_up, ffn_dw_k=v_ffn_dw_k, w_down=v_w_down, final_g=v_final_g)
    order = list(weights)
    depth = w_in.shape[0]
    seq = x.shape[1]
    tm = ROW_TILE
    rows = -(-(N_META + seq) // tm) * tm
    chip = 2 * lax.axis_index("x") + lax.axis_index("y")
    core = lax.axis_index("c")
    chip_arr = chip.astype(jnp.int32).reshape(1)
    core_arr = core.astype(jnp.int32).reshape(1)

    small_sharded = dict(conv_dw_k=jnp.pad(conv_dw_k, ((0, 0), (0, CONV_HIST - CONV_TAPS), (0, 0))),
                         ffn_dw_k=jnp.pad(ffn_dw_k, ((0, 0), (0, 8 - FFN_TAPS), (0, 0))),
                         meta_tokens=meta_tokens.reshape(2, N_META // 2, D_MODEL // 4))
    g_in, g_out, g_up, g_down, g_cdk, g_fdk, g_meta = _gather_chips(
        [_place_own(chip_arr, weights[nm], BF16, "place_" + nm) for nm in _BIG]
        + [_place_own(chip_arr, v, F32, "place_" + nm) for nm, v in small_sharded.items()])
    meta_full = g_meta.transpose(1, 2, 0, 3).reshape(N_META, D_MODEL)
    layers = []
    for l in range(depth):
        w_in_l = g_in[:, l].transpose(1, 0, 2).reshape(D_MODEL, IN_COLS)
        w_out_l = g_out[:, l].reshape(D_MODEL, D_MODEL)
        w_down_l = g_down[:, l].reshape(2, FF_CHUNK, D_MODEL)
        pw = pool_w[l].astype(BF16)
        conv_v = jnp.pad(jnp.stack([conv_dw_b[l], conv_ln_g[l], conv_ln_b[l], pool_scale[l]]), ((0, 4), (0, 0)))
        layers.append(dict(
            w_in=w_in_l, w_in_t=w_in_l.T, conv_k=g_cdk[:, l].transpose(1, 0, 2).reshape(CONV_HIST, CONV_W), conv_v=conv_v,
            pool_w=pw, pool_w_t=pw.transpose(0, 2, 1), w_out=w_out_l, w_out_t=w_out_l.T,
            w_up_t=g_up[:, l].transpose(0, 2, 1), kf=g_fdk[:, l],
            w_down=w_down_l, w_down_t=w_down_l.transpose(0, 2, 1), g1=norm1_g[l][None], g2=norm2_g[l][None]))

    h_pad = jnp.concatenate([meta_full, x[0], jnp.zeros((rows - N_META - seq, D_MODEL), F32)])
    loss_cols, dh, grads, dfinal_g = _local_step(h_pad, loss_target[0], seq, layers, g_up, final_g[None], tm)
    grad_x = dh[N_META:N_META + seq][None]

    keys = ("dw_in", "dw_out", "dw_up", "dw_down")
    g0s = [grads[0][kk] for kk in keys]
    g1s = [grads[1][kk] for kk in keys]
    theirs = _pair_exchange(g0s, g1s)
    parts = [_pair_sum_bf16(core_arr, g0s[a], g1s[a], theirs[a], "pair_sum_" + _BIG[a]) for a in range(4)]
    received = _chip_exchange(parts)
    reds = [_chip_sum(chip_arr, parts[a], received[a], "chip_sum_" + _BIG[a]) for a in range(4)]
    grad, delta, new_m, new_v = {}, {}, {}, {}
    for a, other in enumerate(_pair_share(reds)):
        nm = _BIG[a]
        grad[nm], delta[nm], new_m[nm], new_v[nm] = _adamw_pair(core_arr, weights[nm], reds[a], other, mom1[nm], mom2[nm],
                                                                "adamw_" + nm)

    pack = jnp.concatenate([
        _rows8(jnp.concatenate([grads[l]["dg1"] for l in range(depth)])),
        _rows8(jnp.concatenate([grads[l]["dg2"] for l in range(depth)])),
        _rows8(jnp.concatenate([dfinal_g, loss_cols])),
        jnp.concatenate([grads[l]["dsmall"] for l in range(depth)], axis=1),
        jnp.stack([grads[l]["dpw"] for l in range(depth)]).reshape(-1, D_MODEL),
        dh[:N_META],
        jnp.concatenate([grads[l]["dk"] for l in range(depth)], axis=1),
        jnp.stack([grads[l]["dkf"] for l in range(depth)]).reshape(-1, D_MODEL),
    ])
    red = _all_reduce_small(pack)
    o = 0
    grad["norm1_g"] = red[o:o + depth]
    o += 8
    grad["norm2_g"] = red[o:o + depth]
    o += 8
    grad["final_g"] = red[o]
    loss = red[o + 1, 0]
    o += 8
    sm = red[o:o + 8].reshape(8, depth, CONV_W)
    grad["conv_dw_b"], grad["conv_ln_g"], grad["conv_ln_b"], grad["pool_scale"] = sm[0], sm[1], sm[2], sm[3]
    o += 8
    n_pw = depth * 4 * POOL_G * POOL_G // D_MODEL
    grad["pool_w"] = red[o:o + n_pw].reshape(pool_w.shape)
    o += n_pw
    grad["meta_tokens"] = lax.dynamic_slice_in_dim(red[o:o + N_META], chip * (D_MODEL // 4), D_MODEL // 4, axis=1)
    o += N_META
    dk_all = red[o:o + CONV_HIST].reshape(CONV_HIST, depth, 4, CONV_W // 4)
    grad["conv_dw_k"] = lax.dynamic_index_in_dim(dk_all, chip, axis=2, keepdims=False)[:CONV_TAPS].transpose(1, 0, 2)
    o += CONV_HIST
    dkf_all = red[o:].reshape(depth, 4, 8, FF_CHUNK)
    grad["ffn_dw_k"] = lax.dynamic_index_in_dim(dkf_all, chip, axis=1, keepdims=False)[:, :FFN_TAPS]

    small_rows = -(-sum(weights[nm].size for nm in _SMALL) // (8 * D_MODEL)) * 8
    packed = [_pack_flat([d[nm] for nm in _SMALL], small_rows) for d in (weights, grad, mom1, mom2)]
    for res, packed_out in zip((delta, new_m, new_v), _adamw(*packed, "adamw_small")):
        for nm, val in zip(_SMALL, _unpack_flat(packed_out, [weights[nm] for nm in _SMALL])):
            res[nm] = val

    return (loss, grad_x, *[grad[nm] for nm in order], *[delta[nm] for nm in order],
            *[new_m[nm] for nm in order], *[new_v[nm] for nm in order])
```

```python
import functools

import jax
import jax.numpy as jnp
from jax import lax
from jax.experimental import pallas as pl
from jax.experimental.pallas import tpu as pltpu

F32 = jnp.float32
BF16 = jnp.bfloat16

D_MODEL = 1024
CONV_W = 512
POOL_W = 512
POOL_G = 128
POOL_WINDOWS = (2, 4, 8, 16)
IN_COLS = 1536
D_FF = 2816
FF_CHUNK = 1408
CONV_TAPS = 31
CONV_HIST = 32
POOL_HIST = 16
FFN_TAPS = 3
N_META = 16
EPS = 1e-6

ADAM_LR = 0.001
ADAM_B1 = 0.9
ADAM_B2 = 0.999
ADAM_EPS = 1e-08
ADAM_WD = 0.01
ADAM_STEP = 10

ROW_TILE = 256
CONV_ROW_BLOCK = 64
FFN_ROW_BLOCK = 32
V7X_VMEM_LIMIT = 56 * 1024 * 1024

MESH = pl.DeviceIdType.MESH


def _cparams(n_axes, vmem=None):
    return pltpu.CompilerParams(dimension_semantics=("arbitrary",) * n_axes, vmem_limit_bytes=vmem)


def _whole(shape, single=False):
    zeros = (0,) * len(shape)
    if single:
        return pl.BlockSpec(shape, lambda *_: zeros, pipeline_mode=pl.Buffered(1))
    return pl.BlockSpec(shape, lambda *_: zeros)


def _sigmoid(x):
    return 0.5 * jnp.tanh(0.5 * x) + 0.5


def _dot(a, b):
    return jnp.dot(a, b, preferred_element_type=F32)


def _dot_tn(a, b):
    return lax.dot_general(a, b, (((0,), (0,)), ((), ())), preferred_element_type=F32)


def _split_dot(v, a_ref):
    hi = v.astype(BF16)
    lo = (v - hi.astype(F32)).astype(BF16)
    return _dot(hi, a_ref[...]) + _dot(lo, a_ref[...])


def _rms(x):
    r = lax.rsqrt(jnp.mean(x * x, axis=-1, keepdims=True) + EPS)
    return x * r, r


def _rms_bwd(dy, xhat, r, g):
    gd = dy * g
    return r * (gd - xhat * jnp.mean(gd * xhat, axis=-1, keepdims=True)), dy * xhat


def _colsum(v):
    return jnp.sum(v, axis=0, keepdims=True)


def _shifted(window, s):
    return window if s == 0 else pltpu.roll(window, window.shape[0] - s, 0)


def _conv3(window, kf_ref, cc, ls, hist):
    x2 = window[hist:]
    x1 = pltpu.roll(window, 1, 0)[hist:]
    x0 = pltpu.roll(window, 2, 0)[hist:]
    return x0, x1, x2, kf_ref[cc, 0:1, ls] * x0 + kf_ref[cc, 1:2, ls] * x1 + kf_ref[cc, 2:3, ls] * x2


def _for_row_blocks(n, rb, fn, unroll):
    if unroll:
        for r in range(n):
            fn(r * rb)
    else:
        def step(r, keep):
            fn(pl.multiple_of(r * rb, rb))
            return keep

        lax.fori_loop(0, n, step, 0)


def _fold8(v):
    part = v[0:8]
    for k in range(1, v.shape[0] // 8):
        part = part + v[8 * k:8 * k + 8]
    return part


def _inv_count(tile, tm, w):
    t = tile * tm + lax.broadcasted_iota(jnp.int32, (tm, POOL_G), 0)
    return 1.0 / jnp.minimum(t + 1, w).astype(F32)


def _embed(x, meta, g, rows, tm):
    seq = x.shape[0]
    x_tiles = -(-seq // tm)

    def body(xprev_ref, x_ref, meta_ref, g_ref, h_ref, hn_ref):
        i = pl.program_id(0)
        t = i * tm + lax.broadcasted_iota(jnp.int32, (tm, 1), 0)
        head = jnp.where(i == 0, meta_ref[...], xprev_ref[...])
        h = jnp.concatenate([head, x_ref[0:tm - N_META, :]], axis=0)
        h = jnp.where(t < N_META + seq, h, 0.0)
        h_ref[...] = h
        xhat, _ = _rms(h)
        hn_ref[...] = (xhat * g_ref[...]).astype(BF16)

    tile = pl.BlockSpec((tm, D_MODEL), lambda i: (i, 0))
    x_prev = pl.BlockSpec((N_META, D_MODEL), lambda i: (jnp.maximum(i * (tm // N_META) - 1, 0), 0))
    x_own = pl.BlockSpec((tm, D_MODEL), lambda i: (jnp.minimum(i, x_tiles - 1), 0))
    return pl.pallas_call(
        body, name="embed", grid=(rows // tm,),
        in_specs=[x_prev, x_own, _whole((N_META, D_MODEL)), _whole((1, D_MODEL))],
        out_specs=[tile, tile],
        out_shape=[jax.ShapeDtypeStruct((rows, D_MODEL), F32), jax.ShapeDtypeStruct((rows, D_MODEL), BF16)],
        compiler_params=_cparams(1),
    )(x, x, meta, g)


def _row_block(r):
    for cand in (256, 176, 128, 64, 32, 16):
        if r % cand == 0:
            return cand
    return r


def _place_own(chip, w, dtype, name):
    n, r, c = w.shape
    rb = _row_block(r)

    def body(chip_ref, w_ref, o_ref):
        o_ref[0] = w_ref[...].astype(dtype)

    return pl.pallas_call(
        body, name=name,
        grid_spec=pltpu.PrefetchScalarGridSpec(
            num_scalar_prefetch=1, grid=(n, r // rb),
            in_specs=[pl.BlockSpec((1, rb, c), lambda i, j, chip_ref: (i, j, 0))],
            out_specs=pl.BlockSpec((1, 1, rb, c), lambda i, j, chip_ref: (chip_ref[0], i, j, 0))),
        out_shape=jax.ShapeDtypeStruct((4,) + w.shape, dtype), compiler_params=_cparams(2),
    )(chip, w)


def _pair_sum_bf16(core, g0, g1, other, name):
    n, r, c = g0.shape
    rb = _row_block(r)

    def body(core_ref, g0_ref, g1_ref, o_ref, out_ref):
        mine = jnp.where(core_ref[0] == 0, g0_ref[...], g1_ref[...])
        out_ref[...] = (mine + o_ref[...]).astype(BF16)

    spec = pl.BlockSpec((1, rb, c), lambda i, j: (i, j, 0))
    return pl.pallas_call(
        body, name=name, grid=(n, r // rb),
        in_specs=[pl.BlockSpec(memory_space=pltpu.SMEM), spec, spec, spec], out_specs=spec,
        out_shape=jax.ShapeDtypeStruct(g0.shape, BF16), compiler_params=_cparams(2),
    )(core, g0, g1, other)


def _chip_sum(chip, parts, recv, name):
    _, r, c = parts.shape
    rb = _row_block(r)

    def body(chip_ref, p_ref, r_ref, out_ref):
        got = r_ref[...].astype(F32)
        out_ref[...] = (p_ref[0].astype(F32) + got[0]) + (got[1] + got[2])

    return pl.pallas_call(
        body, name=name,
        grid_spec=pltpu.PrefetchScalarGridSpec(
            num_scalar_prefetch=1, grid=(r // rb,),
            in_specs=[pl.BlockSpec((1, rb, c), lambda j, chip_ref: (chip_ref[0], j, 0)),
                      pl.BlockSpec((3, rb, c), lambda j, chip_ref: (0, j, 0))],
            out_specs=pl.BlockSpec((rb, c), lambda j, chip_ref: (j, 0))),
        out_shape=jax.ShapeDtypeStruct((r, c), F32), compiler_params=_cparams(1),
    )(chip, parts, recv)


def _adamw_update(w, g, m, v):
    nm = ADAM_B1 * m + (1.0 - ADAM_B1) * g
    nv = ADAM_B2 * v + (1.0 - ADAM_B2) * (g * g)
    m_hat = nm / (1.0 - ADAM_B1 ** ADAM_STEP)
    v_hat = nv / (1.0 - ADAM_B2 ** ADAM_STEP)
    return -ADAM_LR * (m_hat / (jnp.sqrt(v_hat) + ADAM_EPS) + ADAM_WD * w), nm, nv


def _adamw(w, g, m, v, name):
    n, r, c = w.shape
    rb = _row_block(r)

    def body(w_ref, g_ref, m_ref, v_ref, d_ref, nm_ref, nv_ref):
        d_ref[...], nm_ref[...], nv_ref[...] = _adamw_update(w_ref[...], g_ref[...], m_ref[...], v_ref[...])

    spec = pl.BlockSpec((1, rb, c), lambda i, j: (i, j, 0))
    shp = jax.ShapeDtypeStruct(w.shape, F32)
    return pl.pallas_call(
        body, name=name, grid=(n, r // rb), in_specs=[spec] * 4, out_specs=[spec] * 3,
        out_shape=[shp] * 3, compiler_params=_cparams(2),
    )(w, g, m, v)


def _adamw_pair(core, w, mine, theirs, m, v, name):
    n, r, c = w.shape
    rb = _row_block(r)

    def body(core_ref, w_ref, a_ref, b_ref, m_ref, v_ref, g_ref, d_ref, nm_ref, nv_ref):
        g = jnp.where(pl.program_id(0) == core_ref[0], a_ref[...], b_ref[...])
        g_ref[0] = g
        d_ref[0], nm_ref[0], nv_ref[0] = _adamw_update(w_ref[0], g, m_ref[0], v_ref[0])

    spec = pl.BlockSpec((1, rb, c), lambda i, j, core_ref: (i, j, 0))
    flat = pl.BlockSpec((rb, c), lambda i, j, core_ref: (j, 0))
    shp = jax.ShapeDtypeStruct(w.shape, F32)
    return pl.pallas_call(
        body, name=name,
        grid_spec=pltpu.PrefetchScalarGridSpec(num_scalar_prefetch=1, grid=(n, r // rb),
                                               in_specs=[spec, flat, flat, spec, spec], out_specs=[spec] * 4),
        out_shape=[shp] * 4, compiler_params=_cparams(2),
    )(core, w, mine, theirs, m, v)


def _mixer_fwd(h0, hn, w_in, conv_k, conv_v, avg, pool_w, w_out, g2, tm):
    rows = h0.shape[0]
    rb = CONV_ROW_BLOCK

    def body(h0_ref, hn_ref, win_ref, ck_ref, cv_ref, avg_ref, pw_ref, wout_ref, g2_ref,
             z_ref, u1_ref, m_ref, h1_ref, hn2_ref, ubuf, pbuf):
        i = pl.program_id(0)

        @pl.when(i == 0)
        def _():
            ubuf[pl.ds(0, CONV_HIST), :] = jnp.zeros((CONV_HIST, CONV_W), F32)
            pbuf[pl.ds(0, POOL_HIST), :] = jnp.zeros((POOL_HIST, POOL_W), F32)

        z = _dot(hn_ref[...], win_ref[...])
        z_ref[...] = z
        ubuf[pl.ds(CONV_HIST, tm), :] = z[:, :CONV_W] * _sigmoid(z[:, CONV_W:2 * CONV_W])
        p = z[:, 2 * CONV_W:]
        pbuf[pl.ds(POOL_HIST, tm), :] = p

        def conv_block(r0):
            for l in range(CONV_W // 128):
                ls = pl.ds(128 * l, 128)
                window = ubuf[pl.ds(r0, rb + CONV_HIST), ls]
                acc = jnp.broadcast_to(cv_ref[0:1, ls], (rb, 128))
                for s in range(8):
                    ws = _shifted(window, s)
                    for q in range(CONV_HIST // 8 + 1):
                        j = 8 * q + s - 2
                        if 0 <= j < CONV_TAPS:
                            acc = acc + ck_ref[j:j + 1, ls] * ws[8 * q:8 * q + rb]
                u1_ref[pl.ds(r0, rb), ls] = acc

        _for_row_blocks(tm // rb, rb, conv_block, unroll=True)
        ubuf[pl.ds(0, CONV_HIST), :] = ubuf[pl.ds(tm, CONV_HIST), :]

        u1 = u1_ref[...]
        cen = u1 - _split_dot(u1, avg_ref)
        xhat = cen * lax.rsqrt(_split_dot(cen * cen, avg_ref) + EPS)
        u2 = xhat * cv_ref[1:2, :] + cv_ref[2:3, :]
        m_ref[:, 0:CONV_W] = (u2 * _sigmoid(u2)).astype(BF16)

        for gi, w in enumerate(POOL_WINDOWS):
            ls = pl.ds(POOL_G * gi, POOL_G)
            s = pbuf[pl.ds(POOL_HIST, tm), ls]
            for j in range(1, w):
                s = s + pbuf[pl.ds(POOL_HIST - j, tm), ls]
            d = s * _inv_count(i, tm, w) - p[:, POOL_G * gi:POOL_G * (gi + 1)]
            y = _dot(d.astype(BF16), pw_ref[gi]) * cv_ref[3:4, ls]
            m_ref[:, pl.ds(CONV_W + POOL_G * gi, POOL_G)] = y.astype(BF16)
        pbuf[pl.ds(0, POOL_HIST), :] = pbuf[pl.ds(tm, POOL_HIST), :]

        h1 = h0_ref[...] + _dot(m_ref[...], wout_ref[...])
        h1_ref[...] = h1
        xh, _ = _rms(h1)
        hn2_ref[...] = (xh * g2_ref[...]).astype(BF16)

    def tile(c):
        return pl.BlockSpec((tm, c), lambda i: (i, 0))

    return pl.pallas_call(
        body, name="mixer_fwd", grid=(rows // tm,),
        in_specs=[tile(D_MODEL), tile(D_MODEL), _whole((D_MODEL, IN_COLS)), _whole((CONV_HIST, CONV_W)),
                  _whole((8, CONV_W)), _whole((CONV_W, CONV_W)), _whole((4, POOL_G, POOL_G)),
                  _whole((D_MODEL, D_MODEL)), _whole((1, D_MODEL))],
        out_specs=[tile(IN_COLS), tile(CONV_W), tile(D_MODEL), tile(D_MODEL), tile(D_MODEL)],
        out_shape=[jax.ShapeDtypeStruct((rows, IN_COLS), F32), jax.ShapeDtypeStruct((rows, CONV_W), F32),
                   jax.ShapeDtypeStruct((rows, D_MODEL), BF16), jax.ShapeDtypeStruct((rows, D_MODEL), F32),
                   jax.ShapeDtypeStruct((rows, D_MODEL), BF16)],
        scratch_shapes=[pltpu.VMEM((CONV_HIST + tm, CONV_W), F32), pltpu.VMEM((POOL_HIST + tm, POOL_W), F32)],
        compiler_params=_cparams(1, V7X_VMEM_LIMIT),
    )(h0, hn, w_in, conv_k, conv_v, avg, pool_w, w_out, g2)


def _ffn_fwd(h1, hn2, w_up, layer, kf, w_down, g_next, tm):
    rows = h1.shape[0]
    hist = 8
    w_up_spec = pl.BlockSpec((4, None, D_MODEL, FF_CHUNK), lambda i: (0, layer, 0, 0), pipeline_mode=pl.Buffered(1))

    rb = FFN_ROW_BLOCK

    def body(h1_ref, hn2_ref, wup_ref, kf_ref, wdn_ref, gn_ref, ug_ref, h2_ref, hnn_ref, wg, wv, carry, act_s, acc):
        i = pl.program_id(0)

        @pl.when(i == 0)
        def _():
            carry[...] = jnp.zeros(carry.shape, F32)

        acc[...] = h1_ref[...]
        for c in range(2):
            for buf, cc in ((wg, c), (wv, c + 2)):
                ug = _dot(hn2_ref[...], wup_ref[cc])
                ug_ref[cc] = ug.astype(BF16)
                buf[pl.ds(0, hist), :] = carry[cc]
                buf[pl.ds(hist, tm), :] = ug
                carry[cc] = buf[pl.ds(tm, hist), :]

            def act_block(r0):
                for l in range(FF_CHUNK // 128):
                    ls = pl.ds(128 * l, 128)
                    gate = _conv3(wg[pl.ds(r0, rb + hist), ls], kf_ref, c, ls, hist)[3]
                    val = _conv3(wv[pl.ds(r0, rb + hist), ls], kf_ref, c + 2, ls, hist)[3]
                    act_s[pl.ds(r0, rb), ls] = (gate * _sigmoid(gate) * val).astype(BF16)

            _for_row_blocks(tm // rb, rb, act_block, unroll=True)
            acc[...] += _dot(act_s[...], wdn_ref[c])
        h2 = acc[...]
        h2_ref[...] = h2
        xh, _ = _rms(h2)
        hnn_ref[...] = (xh * gn_ref[...]).astype(BF16)

    def tile(c):
        return pl.BlockSpec((tm, c), lambda i: (i, 0))

    return pl.pallas_call(
        body, name="ffn_fwd", grid=(rows // tm,),
        in_specs=[tile(D_MODEL), tile(D_MODEL), w_up_spec, _whole((4, 8, FF_CHUNK)),
                  _whole((2, FF_CHUNK, D_MODEL), single=True), _whole((1, D_MODEL))],
        out_specs=[pl.BlockSpec((4, tm, FF_CHUNK), lambda i: (0, i, 0)), tile(D_MODEL), tile(D_MODEL)],
        out_shape=[jax.ShapeDtypeStruct((4, rows, FF_CHUNK), BF16), jax.ShapeDtypeStruct((rows, D_MODEL), F32),
                   jax.ShapeDtypeStruct((rows, D_MODEL), BF16)],
        scratch_shapes=[pltpu.VMEM((hist + tm, FF_CHUNK), F32), pltpu.VMEM((hist + tm, FF_CHUNK), F32),
                        pltpu.VMEM((4, hist, FF_CHUNK), F32), pltpu.VMEM((tm, FF_CHUNK), BF16),
                        pltpu.VMEM((tm, D_MODEL), F32)],
        compiler_params=_cparams(1, V7X_VMEM_LIMIT),
    )(h1, hn2, w_up, kf, w_down, g_next)


def _loss_head(h, tgt, g, seq, tm):
    rows = h.shape[0]
    tgt_tiles = -(-seq // tm)

    def body(h_ref, tprev_ref, t_ref, g_ref, dh_ref, loss_ref, dg_ref):
        i = pl.program_id(0)

        @pl.when(i == 0)
        def _():
            loss_ref[...] = jnp.zeros(loss_ref.shape, F32)
            dg_ref[...] = jnp.zeros(dg_ref.shape, F32)

        t = i * tm + lax.broadcasted_iota(jnp.int32, (tm, 1), 0)
        inside = jnp.logical_and(t >= N_META, t < N_META + seq)
        tgt = jnp.concatenate([tprev_ref[...], t_ref[0:tm - N_META, :]], axis=0)
        xhat, r = _rms(h_ref[...])
        err = jnp.where(inside, xhat * g_ref[...] - tgt, 0.0)
        loss_ref[...] += _colsum(err * err)
        dh, dg_rows = _rms_bwd(err * (1.0 / D_MODEL), xhat, r, g_ref[...])
        dh_ref[...] = dh
        dg_ref[...] += _colsum(dg_rows)

        @pl.when(i == rows // tm - 1)
        def _():
            total = jnp.sum(loss_ref[...], axis=1, keepdims=True) * (0.5 / D_MODEL)
            loss_ref[...] = jnp.broadcast_to(total, loss_ref.shape)

    tile = pl.BlockSpec((tm, D_MODEL), lambda i: (i, 0))
    t_prev = pl.BlockSpec((N_META, D_MODEL), lambda i: (jnp.maximum(i * (tm // N_META) - 1, 0), 0))
    t_own = pl.BlockSpec((tm, D_MODEL), lambda i: (jnp.minimum(i, tgt_tiles - 1), 0))
    vec = jax.ShapeDtypeStruct((1, D_MODEL), F32)
    return pl.pallas_call(
        body, name="loss_head", grid=(rows // tm,),
        in_specs=[tile, t_prev, t_own, _whole((1, D_MODEL))],
        out_specs=[tile, _whole((1, D_MODEL)), _whole((1, D_MODEL))],
        out_shape=[jax.ShapeDtypeStruct((rows, D_MODEL), F32), vec, vec],
        compiler_params=_cparams(1),
    )(h, tgt, tgt, g)


def _ffn_bwd(dh2, hn2, ug0, w_down_t, w_up_t, kf, tm):
    rows = dh2.shape[0]
    nt = rows // tm
    hist = 16
    fut = 8
    rb = FFN_ROW_BLOCK
    near = 8

    def body(dh2_ref, hn2_ref, ugg_ref, ugv_ref, hg_ref, hv_ref, wdt_ref, wutg_ref, wutv_ref, kf_ref,
             dhn_ref, dkf_ref, dwup_ref, dwdn_ref, wg, wv, dgb, dvb, carry, dkacc, act_s, dug_s, acc_up, acc_dn):
        c = pl.program_id(0)
        i = pl.program_id(1)
        first_tile = jnp.where(i == nt - 1, 1.0, 0.0)

        @pl.when(i == 0)
        def _():
            carry[...] = jnp.zeros(carry.shape, F32)
            dkacc[...] = jnp.zeros(dkacc.shape, F32)
            acc_up[...] = jnp.zeros(acc_up.shape, F32)
            acc_dn[...] = jnp.zeros(acc_dn.shape, F32)

        def run():
            n_blocks = tm // rb
            dh2b = dh2_ref[...].astype(BF16)
            sides = ((wg, dgb, ugg_ref, hg_ref, 0, c), (wv, dvb, ugv_ref, hv_ref, 1, c + 2))
            for buf, dbuf, u_ref, h_ref, s, cc in sides:
                buf[pl.ds(0, hist), :] = h_ref[0].astype(F32) * (1.0 - first_tile)
                buf[pl.ds(hist, tm), :] = u_ref[0].astype(F32)
                dbuf[pl.ds(tm, fut), :] = carry[s]
            dgb[pl.ds(0, tm), :] = _dot(dh2b, wdt_ref[0])

            def grad_block(r0):
                for l in range(FF_CHUNK // 128):
                    ls = pl.ds(128 * l, 128)
                    g0, g1, g2, gate = _conv3(wg[pl.ds(r0 + hist - near, rb + near), ls], kf_ref, c, ls, near)
                    v0, v1, v2, val = _conv3(wv[pl.ds(r0 + hist - near, rb + near), ls], kf_ref, c + 2, ls, near)
                    sg = _sigmoid(gate)
                    silu = gate * sg
                    act_s[pl.ds(r0, rb), ls] = (silu * val).astype(BF16)
                    dact = dgb[pl.ds(r0, rb), ls]
                    dgate = dact * val * (sg * (1.0 + gate * (1.0 - sg)))
                    dval = dact * silu
                    dgb[pl.ds(r0, rb), ls] = dgate
                    dvb[pl.ds(r0, rb), ls] = dval
                    for s, dv, taps in ((0, dgate, (g0, g1, g2)), (1, dval, (v0, v1, v2))):
                        for j in range(FFN_TAPS):
                            dkacc[s, pl.ds(8 * j, 8), ls] += _fold8(dv * taps[j])

            _for_row_blocks(n_blocks, rb, grad_block, unroll=True)
            for buf, dbuf, u_ref, h_ref, s, cc in sides:
                carry[s] = dbuf[pl.ds(0, fut), :]

            def conv_block(r0):
                for l in range(FF_CHUNK // 128):
                    ls = pl.ds(128 * l, 128)
                    for buf, dbuf, u_ref, h_ref, s, cc in sides:
                        window = dbuf[pl.ds(r0, rb + fut), ls]
                        dug0 = (kf_ref[cc, 0:1, ls] * _shifted(window, 2)[0:rb] + kf_ref[cc, 1:2, ls] * _shifted(window, 1)[0:rb]
                                + kf_ref[cc, 2:3, ls] * window[0:rb])
                        dug_s[s, pl.ds(r0, rb), ls] = dug0.astype(BF16)

            _for_row_blocks(n_blocks, rb, conv_block, unroll=True)
            dhn_ref[0] = _dot(dug_s[0], wutg_ref[0]) + _dot(dug_s[1], wutv_ref[0])
            acc_up[0] += _dot_tn(hn2_ref[...], dug_s[0])
            acc_up[1] += _dot_tn(hn2_ref[...], dug_s[1])
            acc_dn[...] += _dot_tn(act_s[...], dh2b)

        run()

        @pl.when(i == nt - 1)
        def _():
            for s in range(2):
                for j in range(FFN_TAPS):
                    dkf_ref[c + 2 * s, j:j + 1, :] = _colsum(dkacc[s, pl.ds(8 * j, 8), :])
                dkf_ref[c + 2 * s, FFN_TAPS:8, :] = jnp.zeros((8 - FFN_TAPS, FF_CHUNK), F32)
            pltpu.sync_copy(acc_up, dwup_ref.at[c])
            pltpu.sync_copy(acc_dn, dwdn_ref.at[c])

    def tile(cols):
        return pl.BlockSpec((tm, cols), lambda c, i: (nt - 1 - i, 0))

    def chunk(off, r, halo_rows=None):
        if halo_rows is None:
            return pl.BlockSpec((1, r, FF_CHUNK), lambda c, i: (c + off, nt - 1 - i, 0))
        return pl.BlockSpec((1, r, FF_CHUNK), lambda c, i: (c + off, jnp.maximum((nt - 1 - i) * (tm // r) - 1, 0), 0))

    def weight(off, r, cols):
        return pl.BlockSpec((1, r, cols), lambda c, i: (c + off, 0, 0), pipeline_mode=pl.Buffered(1))

    return pl.pallas_call(
        body, name="ffn_bwd", grid=(2, nt),
        in_specs=[tile(D_MODEL), tile(D_MODEL), chunk(0, tm), chunk(2, tm), chunk(0, hist, True), chunk(2, hist, True),
                  weight(0, D_MODEL, FF_CHUNK), weight(0, FF_CHUNK, D_MODEL), weight(2, FF_CHUNK, D_MODEL),
                  _whole((4, 8, FF_CHUNK))],
        out_specs=[pl.BlockSpec((1, tm, D_MODEL), lambda c, i: (c, nt - 1 - i, 0)), _whole((4, 8, FF_CHUNK)), _ANY, _ANY],
        out_shape=[jax.ShapeDtypeStruct((2, rows, D_MODEL), F32), jax.ShapeDtypeStruct((4, 8, FF_CHUNK), F32),
                   jax.ShapeDtypeStruct((2, 2, D_MODEL, FF_CHUNK), F32), jax.ShapeDtypeStruct((2, FF_CHUNK, D_MODEL), F32)],
        scratch_shapes=[pltpu.VMEM((hist + tm, FF_CHUNK), F32), pltpu.VMEM((hist + tm, FF_CHUNK), F32),
                        pltpu.VMEM((tm + fut, FF_CHUNK), F32), pltpu.VMEM((tm + fut, FF_CHUNK), F32),
                        pltpu.VMEM((2, fut, FF_CHUNK), F32), pltpu.VMEM((2, 8 * FFN_TAPS, FF_CHUNK), F32),
                        pltpu.VMEM((tm, FF_CHUNK), BF16), pltpu.VMEM((2, tm, FF_CHUNK), BF16),
                        pltpu.VMEM((2, D_MODEL, FF_CHUNK), F32), pltpu.VMEM((FF_CHUNK, D_MODEL), F32)],
        compiler_params=_cparams(2, V7X_VMEM_LIMIT),
    )(dh2, hn2, ug0, ug0, ug0, ug0, w_down_t, w_up_t, w_up_t, kf)


def _mixer_bwd(dh2, dhn2, h1, g2, h0, z, u1, hn1, m, w_out_t, w_in_t, conv_k, conv_v, avg, pool_w, pool_w_t, g1, tm,
               x_rows=None):
    rows = dh2.shape[0]
    nt = rows // tm
    rb = CONV_ROW_BLOCK
    in_shard = IN_COLS // 4

    def body(dh2_ref, dhn2_ref, h1_ref, g2_ref, h0_ref, z_ref, zh_ref, u1_ref, hn1_ref, m_ref, wot_ref, wit_ref, ck_ref,
             cv_ref, avg_ref, pw_ref, pwt_ref, g1_ref,
             dh0_ref, dtop_ref, dk_ref, ds_ref, dpw_ref, dg1_ref, dg2_ref, dwo_ref, dwi_ref,
             ubuf, dbuf, pbuf, ebuf, dcarry, ecarry, dkacc, dzs, dz_s, dh1_s, acc_out, acc_in, xcarry):
        i = pl.program_id(0)
        ti = nt - 1 - i
        has_past = jnp.where(ti > 0, 1.0, 0.0)

        @pl.when(i == 0)
        def _():
            dcarry[...] = jnp.zeros(dcarry.shape, F32)
            ecarry[...] = jnp.zeros(ecarry.shape, F32)
            dkacc[...] = jnp.zeros(dkacc.shape, F32)
            ds_ref[...] = jnp.zeros(ds_ref.shape, F32)
            dpw_ref[...] = jnp.zeros(dpw_ref.shape, F32)
            dg1_ref[...] = jnp.zeros(dg1_ref.shape, F32)
            dg2_ref[...] = jnp.zeros(dg2_ref.shape, F32)
            acc_out[...] = jnp.zeros(acc_out.shape, F32)
            acc_in[...] = jnp.zeros(acc_in.shape, F32)
            xcarry[...] = jnp.zeros(xcarry.shape, F32)

        xh1, r1 = _rms(h1_ref[...])
        dx1, dg2_rows = _rms_bwd(dhn2_ref[0] + dhn2_ref[1], xh1, r1, g2_ref[...])
        dh1_s[...] = dh2_ref[...] + dx1
        dg2_ref[...] += _colsum(dg2_rows)
        dh1b = dh1_s[...].astype(BF16)
        acc_out[...] += _dot_tn(m_ref[...], dh1b)
        dm = _dot(dh1b, wot_ref[...])
        z = z_ref[...]
        a = z[:, :CONV_W]
        sg = _sigmoid(z[:, CONV_W:2 * CONV_W])
        p = z[:, 2 * CONV_W:]
        zh = zh_ref[...] * has_past
        ubuf[pl.ds(0, CONV_HIST), :] = zh[:, :CONV_W] * _sigmoid(zh[:, CONV_W:2 * CONV_W])
        ubuf[pl.ds(CONV_HIST, tm), :] = a * sg
        pbuf[pl.ds(0, POOL_HIST), :] = zh[CONV_HIST - POOL_HIST:, 2 * CONV_W:]
        pbuf[pl.ds(POOL_HIST, tm), :] = p

        u1 = u1_ref[...]
        cen = u1 - _split_dot(u1, avg_ref)
        rstd = lax.rsqrt(_split_dot(cen * cen, avg_ref) + EPS)
        xhat = cen * rstd
        u2 = xhat * cv_ref[1:2, :] + cv_ref[2:3, :]
        s2 = _sigmoid(u2)
        du2 = dm[:, :CONV_W] * (s2 * (1.0 + u2 * (1.0 - s2)))
        ds_ref[1:2, :] += _colsum(du2 * xhat)
        ds_ref[2:3, :] += _colsum(du2)
        dxh = du2 * cv_ref[1:2, :]
        du1 = rstd * (dxh - _split_dot(dxh, avg_ref) - xhat * _split_dot(dxh * xhat, avg_ref))
        ds_ref[0:1, :] += _colsum(du1)
        dbuf[pl.ds(0, tm), :] = du1
        dbuf[pl.ds(tm, CONV_HIST), :] = dcarry[...]
        dcarry[...] = dbuf[pl.ds(0, CONV_HIST), :]

        def conv_block(r0):
            for l in range(CONV_W // 128):
                ls = pl.ds(128 * l, 128)
                dwin = dbuf[pl.ds(r0, rb + CONV_HIST), ls]
                uwin = ubuf[pl.ds(r0, rb + CONV_HIST), ls]
                dblk = dwin[0:rb]
                du0 = jnp.zeros((rb, 128), F32)
                for s in range(8):
                    ds_ = _shifted(dwin, s)
                    us_ = _shifted(uwin, s)
                    for q in range(CONV_HIST // 8 + 1):
                        o = 8 * q + s
                        if 0 <= CONV_TAPS - 1 - o < CONV_TAPS:
                            j = CONV_TAPS - 1 - o
                            du0 = du0 + ck_ref[j:j + 1, ls] * ds_[8 * q:8 * q + rb]
                        j = o - 2
                        if 0 <= j < CONV_TAPS:
                            prod = dblk * us_[8 * q:8 * q + rb]
                            part = prod[0:8]
                            for v in range(1, rb // 8):
                                part = part + prod[8 * v:8 * v + 8]
                            dkacc[pl.ds(8 * j, 8), ls] += part
                dzs[pl.ds(r0, rb), ls] = du0

        _for_row_blocks(tm // rb, rb, conv_block, unroll=True)
        du0 = dzs[:, 0:CONV_W]
        dz_s[:, 0:CONV_W] = (du0 * sg).astype(BF16)
        dz_s[:, CONV_W:2 * CONV_W] = (du0 * a * sg * (1.0 - sg)).astype(BF16)

        for gi, w in enumerate(POOL_WINDOWS):
            ls = pl.ds(POOL_G * gi, POOL_G)
            cols = slice(CONV_W + POOL_G * gi, CONV_W + POOL_G * (gi + 1))
            inv = _inv_count(ti, tm, w)
            s = pbuf[pl.ds(POOL_HIST, tm), ls]
            for j in range(1, w):
                s = s + pbuf[pl.ds(POOL_HIST - j, tm), ls]
            d = (s * inv - p[:, POOL_G * gi:POOL_G * (gi + 1)]).astype(BF16)
            dyp = dm[:, cols]
            ds_ref[3:4, ls] += _colsum(dyp * _dot(d, pw_ref[gi]))
            dyb = (dyp * cv_ref[3:4, ls]).astype(BF16)
            dpw_ref[gi] += _dot_tn(d, dyb)
            dd = _dot(dyb, pwt_ref[gi])
            ebuf[pl.ds(0, tm), ls] = dd * inv
            ebuf[pl.ds(tm, POOL_HIST), ls] = ecarry[:, ls]
            dp = ebuf[pl.ds(0, tm), ls] - dd
            for j in range(1, w):
                dp = dp + ebuf[pl.ds(j, tm), ls]
            dz_s[:, pl.ds(2 * CONV_W + POOL_G * gi, POOL_G)] = dp.astype(BF16)
        ecarry[...] = ebuf[pl.ds(0, POOL_HIST), :]

        acc_in[...] += _dot_tn(hn1_ref[...], dz_s[...])
        dhn = _dot(dz_s[...], wit_ref[...])
        xh, r = _rms(h0_ref[...])
        dx, dg_rows = _rms_bwd(dhn, xh, r, g1_ref[...])
        dh0 = dh1_s[...] + dx
        if x_rows is None:
            dh0_ref[...] = dh0
        else:
            dh0_ref[0:tm - N_META, :] = dh0[N_META:]
            dh0_ref[tm - N_META:tm, :] = xcarry[...]
            xcarry[...] = dh0[0:N_META]
        dg1_ref[...] += _colsum(dg_rows)

        @pl.when(i == nt - 1)
        def _():
            dtop_ref[...] = dh0[0:N_META]
            for j in range(CONV_TAPS):
                dk_ref[j:j + 1, :] = _colsum(dkacc[pl.ds(8 * j, 8), :])
            dk_ref[CONV_TAPS:CONV_HIST, :] = jnp.zeros((CONV_HIST - CONV_TAPS, CONV_W), F32)
            pltpu.sync_copy(acc_out, dwo_ref)
            for k in range(4):
                pltpu.sync_copy(acc_in.at[:, pl.ds(in_shard * k, in_shard)], dwi_ref.at[k])

    def tile(c):
        return pl.BlockSpec((tm, c), lambda i: (nt - 1 - i, 0))

    halo = pl.BlockSpec((CONV_HIST, IN_COLS), lambda i: (jnp.maximum((nt - 1 - i) * (tm // CONV_HIST) - 1, 0), 0))
    vec = jax.ShapeDtypeStruct((1, D_MODEL), F32)
    if x_rows is None:
        first_spec, first_shape = tile(D_MODEL), jax.ShapeDtypeStruct((rows, D_MODEL), F32)
    else:
        x_tiles = -(-x_rows // tm)
        first_spec = pl.BlockSpec((tm, D_MODEL), lambda i: (jnp.minimum(nt - 1 - i, x_tiles - 1), 0))
        first_shape = jax.ShapeDtypeStruct((x_rows, D_MODEL), F32)
    return pl.pallas_call(
        body, name="mixer_bwd", grid=(nt,),
        in_specs=[tile(D_MODEL), pl.BlockSpec((2, tm, D_MODEL), lambda i: (0, nt - 1 - i, 0)), tile(D_MODEL),
                  _whole((1, D_MODEL)), tile(D_MODEL), tile(IN_COLS), halo, tile(CONV_W), tile(D_MODEL), tile(D_MODEL),
                  _whole((D_MODEL, D_MODEL)), _whole((IN_COLS, D_MODEL)), _whole((CONV_HIST, CONV_W)), _whole((8, CONV_W)),
                  _whole((CONV_W, CONV_W)), _whole((4, POOL_G, POOL_G)), _whole((4, POOL_G, POOL_G)), _whole((1, D_MODEL))],
        out_specs=[first_spec, _whole((N_META, D_MODEL)), _whole((CONV_HIST, CONV_W)), _whole((8, CONV_W)),
                   _whole((4, POOL_G, POOL_G)), _whole((1, D_MODEL)), _whole((1, D_MODEL)), _ANY, _ANY],
        out_shape=[first_shape, jax.ShapeDtypeStruct((N_META, D_MODEL), F32), jax.ShapeDtypeStruct((CONV_HIST, CONV_W), F32),
                   jax.ShapeDtypeStruct((8, CONV_W), F32), jax.ShapeDtypeStruct((4, POOL_G, POOL_G), F32), vec, vec,
                   jax.ShapeDtypeStruct((D_MODEL, D_MODEL), F32), jax.ShapeDtypeStruct((4, D_MODEL, in_shard), F32)],
        scratch_shapes=[pltpu.VMEM((CONV_HIST + tm, CONV_W), F32), pltpu.VMEM((tm + CONV_HIST, CONV_W), F32),
                        pltpu.VMEM((POOL_HIST + tm, POOL_W), F32), pltpu.VMEM((tm + POOL_HIST, POOL_W), F32),
                        pltpu.VMEM((CONV_HIST, CONV_W), F32), pltpu.VMEM((POOL_HIST, POOL_W), F32),
                        pltpu.VMEM((8 * CONV_HIST, CONV_W), F32), pltpu.VMEM((tm, CONV_W), F32),
                        pltpu.VMEM((tm, IN_COLS), BF16), pltpu.VMEM((tm, D_MODEL), F32),
                        pltpu.VMEM((D_MODEL, D_MODEL), F32), pltpu.VMEM((D_MODEL, IN_COLS), F32),
                        pltpu.VMEM((N_META, D_MODEL), F32)],
        compiler_params=_cparams(1, V7X_VMEM_LIMIT),
    )(dh2, dhn2, h1, g2, h0, z, z, u1, hn1, m, w_out_t, w_in_t, conv_k, conv_v, avg, pool_w, pool_w_t, g1)


def _head_average():
    head = lax.broadcasted_iota(jnp.int32, (CONV_W, CONV_W), 0) // 64
    return jnp.where(head == head.T, 1.0 / 64, 0.0).astype(BF16)


def _local_step(x, meta, tgt, layers, w_up_all, final_g, tm):
    avg = _head_average()
    depth = len(layers)
    saved = []
    seq = x.shape[0]
    rows = -(-(N_META + seq) // tm) * tm
    h, hn = _embed(x, meta, layers[0]["g1"], rows, tm)
    for l, w in enumerate(layers):
        g_next = layers[l + 1]["g1"] if l + 1 < depth else final_g
        z, u1, m, h1, hn2 = _mixer_fwd(h, hn, w["w_in"], w["conv_k"], w["conv_v"], avg, w["pool_w"], w["w_out"], w["g2"], tm)
        ug0, h2, hn_next = _ffn_fwd(h1, hn2, w_up_all, l, w["kf"], w["w_down"], g_next, tm)
        saved.append((h, hn, z, u1, m, h1, hn2, ug0))
        h, hn = h2, hn_next
    dh, loss_cols, dfinal_g = _loss_head(h, tgt, final_g, seq, tm)

    grads = [None] * depth
    for l in reversed(range(depth)):
        w = layers[l]
        h0, hn1, z, u1, m, h1, hn2, ug0 = saved[l]
        dhn2, dkf, dw_up, dw_down = _ffn_bwd(dh, hn2, ug0, w["w_down_t"], w["w_up_t"], w["kf"], tm)
        dh0, dtop, dk, dsmall, dpw, dg1, dg2, dw_out, dw_in = _mixer_bwd(
            dh, dhn2, h1, w["g2"], h0, z, u1, hn1, m, w["w_out_t"], w["w_in_t"], w["conv_k"], w["conv_v"], avg,
            w["pool_w"], w["pool_w_t"], w["g1"], tm, x_rows=seq if l == 0 else None)
        grads[l] = dict(dw_in=dw_in, dw_out=dw_out.reshape(4, D_MODEL // 4, D_MODEL),
                        dw_up=dw_up.transpose(1, 0, 2, 3).reshape(4, D_MODEL, FF_CHUNK),
                        dw_down=dw_down.reshape(4, D_FF // 4, D_MODEL), dk=dk, dsmall=dsmall, dpw=dpw, dg1=dg1, dg2=dg2, dkf=dkf)
        dh = dh0
    return loss_cols, dh, dtop, grads, dfinal_g


_ANY = pl.BlockSpec(memory_space=pl.ANY)


def _place():
    x, y, c = lax.axis_index("x"), lax.axis_index("y"), lax.axis_index("c")
    chips = [(1 - x, y), (x, 1 - y), (1 - x, 1 - y)]
    return x, y, c, chips


def _remote(src, dst, send_sems, recv_sems, idx, to):
    return pltpu.make_async_remote_copy(src_ref=src, dst_ref=dst, send_sem=send_sems.at[idx], recv_sem=recv_sems.at[idx],
                                        device_id=to, device_id_type=MESH)


def _gather_chips(xs):
    n = len(xs)

    def body(*refs):
        x_refs, o_refs = refs[:n], refs[n:2 * n]
        send_sems, recv_sems = refs[2 * n:]
        x, y, c, chips = _place()
        k = 2 * x + y
        sibling = (x, y, 1 - c)
        sends = []
        for j, chip in enumerate(chips):
            for a in range(n):
                sends.append(_remote(x_refs[a].at[k, c], o_refs[a].at[k, c], send_sems, recv_sems, 3 * a + j, (*chip, c)))
                sends[-1].start()
        for j, chip in enumerate(chips):
            kj = 2 * chip[0] + chip[1]
            for a in range(n):
                landed = o_refs[a].at[kj, c]
                _remote(landed, landed, send_sems, recv_sems, 3 * a + j, sibling).wait_recv()
                sends.append(_remote(landed, landed, send_sems, recv_sems, 3 * n + 3 * a + j, sibling))
                sends[-1].start()
        for j, chip in enumerate(chips):
            kj = 2 * chip[0] + chip[1]
            for a in range(n):
                passed = o_refs[a].at[kj, 1 - c]
                _remote(passed, passed, send_sems, recv_sems, 3 * n + 3 * a + j, sibling).wait_recv()
        for cp in sends:
            cp.wait_send()

    return pl.pallas_call(
        body, name="gather_chips", in_specs=[_ANY] * n, out_specs=[_ANY] * n,
        out_shape=[jax.ShapeDtypeStruct(v.shape, v.dtype) for v in xs],
        input_output_aliases={a: a for a in range(n)},
        scratch_shapes=[pltpu.SemaphoreType.DMA((6 * n,)), pltpu.SemaphoreType.DMA((6 * n,))],
    )(*xs)


def _pair_exchange(g0s, g1s):
    n = len(g0s)

    def body(*refs):
        g0, g1, out = refs[:n], refs[n:2 * n], refs[2 * n:3 * n]
        send_sems, recv_sems = refs[3 * n:]
        x, y, c, _ = _place()
        sibling = (x, y, 1 - c)
        for a in range(n):
            @pl.when(c == 0)
            def _():
                _remote(g1[a], out[a], send_sems, recv_sems, a, sibling).start()

            @pl.when(c == 1)
            def _():
                _remote(g0[a], out[a], send_sems, recv_sems, a, sibling).start()
        for a in range(n):
            cp = _remote(g0[a], out[a], send_sems, recv_sems, a, sibling)
            cp.wait_recv()
            cp.wait_send()

    return pl.pallas_call(
        body, name="pair_exchange", in_specs=[_ANY] * (2 * n), out_specs=[_ANY] * n,
        out_shape=[jax.ShapeDtypeStruct(v.shape, v.dtype) for v in g0s],
        scratch_shapes=[pltpu.SemaphoreType.DMA((n,)), pltpu.SemaphoreType.DMA((n,))],
    )(*g0s, *g1s)


def _chip_exchange(parts):
    n = len(parts)

    def body(*refs):
        p_refs, o_refs = refs[:n], refs[n:2 * n]
        send_sems, recv_sems = refs[2 * n:]
        x, y, c, chips = _place()
        sends = []
        for j, chip in enumerate(chips):
            kj = 2 * chip[0] + chip[1]
            for a in range(n):
                sends.append(_remote(p_refs[a].at[kj], o_refs[a].at[j], send_sems, recv_sems, 3 * a + j, (*chip, c)))
                sends[-1].start()
        for j, chip in enumerate(chips):
            for a in range(n):
                landed = o_refs[a].at[j]
                _remote(landed, landed, send_sems, recv_sems, 3 * a + j, (*chip, c)).wait_recv()
        for cp in sends:
            cp.wait_send()

    return pl.pallas_call(
        body, name="chip_exchange", in_specs=[_ANY] * n, out_specs=[_ANY] * n,
        out_shape=[jax.ShapeDtypeStruct((3,) + v.shape[1:], v.dtype) for v in parts],
        scratch_shapes=[pltpu.SemaphoreType.DMA((3 * n,)), pltpu.SemaphoreType.DMA((3 * n,))],
    )(*parts)


def _pair_share(reds):
    n = len(reds)

    def body(*refs):
        r_refs, o_refs = refs[:n], refs[n:2 * n]
        send_sems, recv_sems = refs[2 * n:]
        x, y, c, _ = _place()
        sends = [_remote(r_refs[a], o_refs[a], send_sems, recv_sems, a, (x, y, 1 - c)) for a in range(n)]
        for cp in sends:
            cp.start()
        for cp in sends:
            cp.wait_recv()
        for cp in sends:
            cp.wait_send()

    return pl.pallas_call(
        body, name="pair_share", in_specs=[_ANY] * n, out_specs=[_ANY] * n,
        out_shape=[jax.ShapeDtypeStruct(v.shape, v.dtype) for v in reds],
        scratch_shapes=[pltpu.SemaphoreType.DMA((n,)), pltpu.SemaphoreType.DMA((n,))],
    )(*reds)


def _all_reduce_small(pack):
    p, cols = pack.shape

    def body(x_ref, o_ref, sib, chipbuf, send_sems, recv_sems):
        x, y, c, chips = _place()
        k = 2 * x + y
        sibling = (x, y, 1 - c)
        pair = _remote(x_ref, sib, send_sems, recv_sems, 0, sibling)
        pair.start()
        pair.wait_recv()
        chipbuf[k] = x_ref[...] + sib[...]
        sends = [_remote(chipbuf.at[k], chipbuf.at[k], send_sems, recv_sems, 1 + j, (*chip, c)) for j, chip in enumerate(chips)]
        for cp in sends:
            cp.start()
        for j, chip in enumerate(chips):
            landed = chipbuf.at[2 * chip[0] + chip[1]]
            _remote(landed, landed, send_sems, recv_sems, 1 + j, (*chip, c)).wait_recv()
        pair.wait_send()
        for cp in sends:
            cp.wait_send()
        o_ref[...] = (chipbuf[0] + chipbuf[1]) + (chipbuf[2] + chipbuf[3])

    vm = pl.BlockSpec(memory_space=pltpu.VMEM)
    return pl.pallas_call(
        body, name="all_reduce_small", in_specs=[vm], out_specs=vm,
        out_shape=jax.ShapeDtypeStruct(pack.shape, F32),
        scratch_shapes=[pltpu.VMEM((p, cols), F32), pltpu.VMEM((4, p, cols), F32),
                        pltpu.SemaphoreType.DMA((4,)), pltpu.SemaphoreType.DMA((4,))],
    )(pack)


_BIG = ("w_in", "w_out", "w_up", "w_down")
_SMALL = ("norm1_g", "conv_dw_b", "conv_ln_g", "conv_ln_b", "pool_w", "pool_scale", "norm2_g", "final_g",
          "meta_tokens", "conv_dw_k", "ffn_dw_k")


def _rows8(v):
    return jnp.pad(v, ((0, -v.shape[0] % 8), (0, 0)))


def _pack_flat(arrs, rows):
    flat = jnp.concatenate([a.reshape(-1) for a in arrs])
    return jnp.pad(flat, (0, rows * D_MODEL - flat.shape[0])).reshape(1, rows, D_MODEL)


def _unpack_flat(packed, like):
    flat = packed.reshape(-1)
    out, off = [], 0
    for a in like:
        out.append(flat[off:off + a.size].reshape(a.shape))
        off += a.size
    return out


def kernel(x, meta_tokens, norm1_g, w_in, conv_dw_k, conv_dw_b, conv_ln_g, conv_ln_b, pool_w, pool_scale, w_out, norm2_g, w_up, ffn_dw_k, w_down, final_g, loss_target, m_meta_tokens, m_norm1_g, m_w_in, m_conv_dw_k, m_conv_dw_b, m_conv_ln_g, m_conv_ln_b, m_pool_w, m_pool_scale, m_w_out, m_norm2_g, m_w_up, m_ffn_dw_k, m_w_down, m_final_g, v_meta_tokens, v_norm1_g, v_w_in, v_conv_dw_k, v_conv_dw_b, v_conv_ln_g, v_conv_ln_b, v_pool_w, v_pool_scale, v_w_out, v_norm2_g, v_w_up, v_ffn_dw_k, v_w_down, v_final_g):
    weights = dict(meta_tokens=meta_tokens, norm1_g=norm1_g, w_in=w_in, conv_dw_k=conv_dw_k, conv_dw_b=conv_dw_b,
                   conv_ln_g=conv_ln_g, conv_ln_b=conv_ln_b, pool_w=pool_w, pool_scale=pool_scale, w_out=w_out,
                   norm2_g=norm2_g, w_up=w_up, ffn_dw_k=ffn_dw_k, w_down=w_down, final_g=final_g)
    mom1 = dict(meta_tokens=m_meta_tokens, norm1_g=m_norm1_g, w_in=m_w_in, conv_dw_k=m_conv_dw_k, conv_dw_b=m_conv_dw_b,
                conv_ln_g=m_conv_ln_g, conv_ln_b=m_conv_ln_b, pool_w=m_pool_w, pool_scale=m_pool_scale, w_out=m_w_out,
                norm2_g=m_norm2_g, w_up=m_w_up, ffn_dw_k=m_ffn_dw_k, w_down=m_w_down, final_g=m_final_g)
    mom2 = dict(meta_tokens=v_meta_tokens, norm1_g=v_norm1_g, w_in=v_w_in, conv_dw_k=v_conv_dw_k, conv_dw_b=v_conv_dw_b,
                conv_ln_g=v_conv_ln_g, conv_ln_b=v_conv_ln_b, pool_w=v_pool_w, pool_scale=v_pool_scale, w_out=v_w_out,
                norm2_g=v_norm2_g, w_up=v_w_up, ffn_dw_k=v_ffn_dw_k, w_down=v_w_down, final_g=v_final_g)
    order = list(weights)
    depth = w_in.shape[0]
    seq = x.shape[1]
    tm = ROW_TILE
    rows = -(-(N_META + seq) // tm) * tm
    chip = 2 * lax.axis_index("x") + lax.axis_index("y")
    core = lax.axis_index("c")
    chip_arr = chip.astype(jnp.int32).reshape(1)
    core_arr = core.astype(jnp.int32).reshape(1)

    small_sharded = dict(conv_dw_k=jnp.pad(conv_dw_k, ((0, 0), (0, CONV_HIST - CONV_TAPS), (0, 0))),
                         ffn_dw_k=jnp.pad(ffn_dw_k, ((0, 0), (0, 8 - FFN_TAPS), (0, 0))),
                         meta_tokens=meta_tokens.reshape(2, N_META // 2, D_MODEL // 4))
    g_in, g_out, g_up, g_down, g_cdk, g_fdk, g_meta = _gather_chips(
        [_place_own(chip_arr, weights[nm], BF16, "place_" + nm) for nm in _BIG]
        + [_place_own(chip_arr, v, F32, "place_" + nm) for nm, v in small_sharded.items()])
    meta_full = g_meta.transpose(1, 2, 0, 3).reshape(N_META, D_MODEL)
    layers = []
    for l in range(depth):
        w_in_l = g_in[:, l].transpose(1, 0, 2).reshape(D_MODEL, IN_COLS)
        w_out_l = g_out[:, l].reshape(D_MODEL, D_MODEL)
        w_down_l = g_down[:, l].reshape(2, FF_CHUNK, D_MODEL)
        pw = pool_w[l].astype(BF16)
        conv_v = jnp.pad(jnp.stack([conv_dw_b[l], conv_ln_g[l], conv_ln_b[l], pool_scale[l]]), ((0, 4), (0, 0)))
        layers.append(dict(
            w_in=w_in_l, w_in_t=w_in_l.T, conv_k=g_cdk[:, l].transpose(1, 0, 2).reshape(CONV_HIST, CONV_W), conv_v=conv_v,
            pool_w=pw, pool_w_t=pw.transpose(0, 2, 1), w_out=w_out_l, w_out_t=w_out_l.T,
            w_up_t=g_up[:, l].transpose(0, 2, 1), kf=g_fdk[:, l],
            w_down=w_down_l, w_down_t=w_down_l.transpose(0, 2, 1), g1=norm1_g[l][None], g2=norm2_g[l][None]))

    loss_cols, dx, dmeta, grads, dfinal_g = _local_step(x[0], meta_full, loss_target[0], layers, g_up, final_g[None], tm)
    grad_x = dx[None]

    keys = ("dw_in", "dw_out", "dw_up", "dw_down")
    g0s = [grads[0][kk] for kk in keys]
    g1s = [grads[1][kk] for kk in keys]
    theirs = _pair_exchange(g0s, g1s)
    parts = [_pair_sum_bf16(core_arr, g0s[a], g1s[a], theirs[a], "pair_sum_" + _BIG[a]) for a in range(4)]
    received = _chip_exchange(parts)
    reds = [_chip_sum(chip_arr, parts[a], received[a], "chip_sum_" + _BIG[a]) for a in range(4)]
    grad, delta, new_m, new_v = {}, {}, {}, {}
    for a, other in enumerate(_pair_share(reds)):
        nm = _BIG[a]
        grad[nm], delta[nm], new_m[nm], new_v[nm] = _adamw_pair(core_arr, weights[nm], reds[a], other, mom1[nm], mom2[nm],
                                                                "adamw_" + nm)

    pack = jnp.concatenate([
        _rows8(jnp.concatenate([grads[l]["dg1"] for l in range(depth)])),
        _rows8(jnp.concatenate([grads[l]["dg2"] for l in range(depth)])),
        _rows8(jnp.concatenate([dfinal_g, loss_cols])),
        jnp.concatenate([grads[l]["dsmall"] for l in range(depth)], axis=1),
        jnp.stack([grads[l]["dpw"] for l in range(depth)]).reshape(-1, D_MODEL),
        dmeta,
        jnp.concatenate([grads[l]["dk"] for l in range(depth)], axis=1),
        jnp.stack([grads[l]["dkf"] for l in range(depth)]).reshape(-1, D_MODEL),
    ])
    red = _all_reduce_small(pack)
    o = 0
    grad["norm1_g"] = red[o:o + depth]
    o += 8
    grad["norm2_g"] = red[o:o + depth]
    o += 8
    grad["final_g"] = red[o]
    loss = red[o + 1, 0]
    o += 8
    sm = red[o:o + 8].reshape(8, depth, CONV_W)
    grad["conv_dw_b"], grad["conv_ln_g"], grad["conv_ln_b"], grad["pool_scale"] = sm[0], sm[1], sm[2], sm[3]
    o += 8
    n_pw = depth * 4 * POOL_G * POOL_G // D_MODEL
    grad["pool_w"] = red[o:o + n_pw].reshape(pool_w.shape)
    o += n_pw
    grad["meta_tokens"] = lax.dynamic_slice_in_dim(red[o:o + N_META], chip * (D_MODEL // 4), D_MODEL // 4, axis=1)
    o += N_META
    dk_all = red[o:o + CONV_HIST].reshape(CONV_HIST, depth, 4, CONV_W // 4)
    grad["conv_dw_k"] = lax.dynamic_index_in_dim(dk_all, chip, axis=2, keepdims=False)[:CONV_TAPS].transpose(1, 0, 2)
    o += CONV_HIST
    dkf_all = red[o:].reshape(depth, 4, 8, FF_CHUNK)
    grad["ffn_dw_k"] = lax.dynamic_index_in_dim(dkf_all, chip, axis=1, keepdims=False)[:, :FFN_TAPS]

    small_rows = -(-sum(weights[nm].size for nm in _SMALL) // (8 * D_MODEL)) * 8
    packed = [_pack_flat([d[nm] for nm in _SMALL], small_rows) for d in (weights, grad, mom1, mom2)]
    for res, packed_out in zip((delta, new_m, new_v), _adamw(*packed, "adamw_small")):
        for nm, val in zip(_SMALL, _unpack_flat(packed_out, [weights[nm] for nm in _SMALL])):
            res[nm] = val

    return (loss, grad_x, *[grad[nm] for nm in order], *[delta[nm] for nm in order],
            *[new_m[nm] for nm in order], *[new_v[nm] for nm in order])
```

```python
import math

import jax
import jax.numpy as jnp
from jax import lax
from jax.experimental import pallas as pl
from jax.experimental.pallas import tpu as pltpu

F32 = jnp.float32
BF16 = jnp.bfloat16

D_MODEL = 1024
CONV_W = 512
POOL_W = 512
POOL_G = 128
POOL_WINDOWS = (2, 4, 8, 16)
IN_COLS = 1536
D_FF = 2816
FF_CHUNK = 1408
CONV_TAPS = 31
CONV_HIST = 32
POOL_HIST = 16
FFN_TAPS = 3
N_META = 16
EPS = 1e-6

ADAM_LR = 0.001
ADAM_B1 = 0.9
ADAM_B2 = 0.999
ADAM_EPS = 1e-08
ADAM_WD = 0.01
ADAM_STEP = 10

ROW_TILE = 256
FFN_ROW_TILE = 256
CONV_ROW_BLOCK = 64
FFN_ROW_BLOCK = 32
V7X_VMEM_LIMIT = 56 * 1024 * 1024

MESH = pl.DeviceIdType.MESH


def _cparams(n_axes, vmem=None):
    return pltpu.CompilerParams(dimension_semantics=("arbitrary",) * n_axes, vmem_limit_bytes=vmem)


def _whole(shape, single=False):
    zeros = (0,) * len(shape)
    if single:
        return pl.BlockSpec(shape, lambda *_: zeros, pipeline_mode=pl.Buffered(1))
    return pl.BlockSpec(shape, lambda *_: zeros)


def _sigmoid(x):
    return 0.5 * jnp.tanh(0.5 * x) + 0.5


def _dot(a, b):
    return jnp.dot(a, b, preferred_element_type=F32)


def _dot_tn(a, b):
    return lax.dot_general(a, b, (((0,), (0,)), ((), ())), preferred_element_type=F32)


def _split_dot(v, a_ref):
    hi = v.astype(BF16)
    lo = (v - hi.astype(F32)).astype(BF16)
    return _dot(hi, a_ref[...]) + _dot(lo, a_ref[...])


def _rms(x):
    r = lax.rsqrt(jnp.mean(x * x, axis=-1, keepdims=True) + EPS)
    return x * r, r


def _rms_bwd(dy, xhat, r, g):
    gd = dy * g
    return r * (gd - xhat * jnp.mean(gd * xhat, axis=-1, keepdims=True)), dy * xhat


def _colsum(v):
    return jnp.sum(v, axis=0, keepdims=True)


def _shifted(window, s):
    return window if s == 0 else pltpu.roll(window, window.shape[0] - s, 0)


def _conv3(window, kf_ref, cc, ls, hist):
    x2 = window[hist:]
    x1 = pltpu.roll(window, 1, 0)[hist:]
    x0 = pltpu.roll(window, 2, 0)[hist:]
    return x0, x1, x2, kf_ref[cc, 0:1, ls] * x0 + kf_ref[cc, 1:2, ls] * x1 + kf_ref[cc, 2:3, ls] * x2


def _for_row_blocks(n, rb, fn, unroll):
    if unroll:
        for r in range(n):
            fn(r * rb)
    else:
        def step(r, keep):
            fn(pl.multiple_of(r * rb, rb))
            return keep

        lax.fori_loop(0, n, step, 0)


def _fold8(v):
    part = v[0:8]
    for k in range(1, v.shape[0] // 8):
        part = part + v[8 * k:8 * k + 8]
    return part


def _inv_count(tile, tm, w):
    t = tile * tm + lax.broadcasted_iota(jnp.int32, (tm, POOL_G), 0)
    return 1.0 / jnp.minimum(t + 1, w).astype(F32)


def _embed(x, meta, g, rows, tm):
    seq = x.shape[0]
    x_tiles = -(-seq // tm)

    def body(xprev_ref, x_ref, meta_ref, g_ref, h_ref, hn_ref):
        i = pl.program_id(0)
        t = i * tm + lax.broadcasted_iota(jnp.int32, (tm, 1), 0)
        head = jnp.where(i == 0, meta_ref[...], xprev_ref[...])
        h = jnp.concatenate([head, x_ref[0:tm - N_META, :]], axis=0)
        h = jnp.where(t < N_META + seq, h, 0.0)
        h_ref[...] = h
        xhat, _ = _rms(h)
        hn_ref[...] = (xhat * g_ref[...]).astype(BF16)

    tile = pl.BlockSpec((tm, D_MODEL), lambda i: (i, 0))
    x_prev = pl.BlockSpec((N_META, D_MODEL), lambda i: (jnp.maximum(i * (tm // N_META) - 1, 0), 0))
    x_own = pl.BlockSpec((tm, D_MODEL), lambda i: (jnp.minimum(i, x_tiles - 1), 0))
    return pl.pallas_call(
        body, name="embed", grid=(rows // tm,),
        in_specs=[x_prev, x_own, _whole((N_META, D_MODEL)), _whole((1, D_MODEL))],
        out_specs=[tile, tile],
        out_shape=[jax.ShapeDtypeStruct((rows, D_MODEL), F32), jax.ShapeDtypeStruct((rows, D_MODEL), BF16)],
        compiler_params=_cparams(1),
    )(x, x, meta, g)


def _row_block(r):
    for cand in (256, 176, 128, 64, 32, 16):
        if r % cand == 0:
            return cand
    return r


def _place_own(chip, w, dtype, name):
    n, r, c = w.shape
    rb = _row_block(r)

    def body(chip_ref, w_ref, o_ref):
        o_ref[0] = w_ref[...].astype(dtype)

    return pl.pallas_call(
        body, name=name,
        grid_spec=pltpu.PrefetchScalarGridSpec(
            num_scalar_prefetch=1, grid=(n, r // rb),
            in_specs=[pl.BlockSpec((1, rb, c), lambda i, j, chip_ref: (i, j, 0))],
            out_specs=pl.BlockSpec((1, 1, rb, c), lambda i, j, chip_ref: (chip_ref[0], i, j, 0))),
        out_shape=jax.ShapeDtypeStruct((4,) + w.shape, dtype), compiler_params=_cparams(2),
    )(chip, w)


def _pair_sum_bf16(core, g0, g1, other, name):
    n, r, c = g0.shape
    rb = _row_block(r)

    def body(core_ref, g0_ref, g1_ref, o_ref, out_ref):
        mine = jnp.where(core_ref[0] == 0, g0_ref[...], g1_ref[...])
        out_ref[...] = (mine + o_ref[...]).astype(BF16)

    spec = pl.BlockSpec((1, rb, c), lambda i, j: (i, j, 0))
    return pl.pallas_call(
        body, name=name, grid=(n, r // rb),
        in_specs=[pl.BlockSpec(memory_space=pltpu.SMEM), spec, spec, spec], out_specs=spec,
        out_shape=jax.ShapeDtypeStruct(g0.shape, BF16), compiler_params=_cparams(2),
    )(core, g0, g1, other)


def _chip_sum(chip, parts, recv, name):
    _, r, c = parts.shape
    rb = _row_block(r)

    def body(chip_ref, p_ref, r_ref, out_ref):
        got = r_ref[...].astype(F32)
        out_ref[...] = (p_ref[0].astype(F32) + got[0]) + (got[1] + got[2])

    return pl.pallas_call(
        body, name=name,
        grid_spec=pltpu.PrefetchScalarGridSpec(
            num_scalar_prefetch=1, grid=(r // rb,),
            in_specs=[pl.BlockSpec((1, rb, c), lambda j, chip_ref: (chip_ref[0], j, 0)),
                      pl.BlockSpec((3, rb, c), lambda j, chip_ref: (0, j, 0))],
            out_specs=pl.BlockSpec((rb, c), lambda j, chip_ref: (j, 0))),
        out_shape=jax.ShapeDtypeStruct((r, c), F32), compiler_params=_cparams(1),
    )(chip, parts, recv)


def _adamw_update(w, g, m, v):
    nm = ADAM_B1 * m + (1.0 - ADAM_B1) * g
    nv = ADAM_B2 * v + (1.0 - ADAM_B2) * (g * g)
    m_hat = nm / (1.0 - ADAM_B1 ** ADAM_STEP)
    v_hat = nv / (1.0 - ADAM_B2 ** ADAM_STEP)
    return -ADAM_LR * (m_hat / (jnp.sqrt(v_hat) + ADAM_EPS) + ADAM_WD * w), nm, nv


def _adamw(w, g, m, v, name):
    n, r, c = w.shape
    rb = _row_block(r)

    def body(w_ref, g_ref, m_ref, v_ref, d_ref, nm_ref, nv_ref):
        d_ref[...], nm_ref[...], nv_ref[...] = _adamw_update(w_ref[...], g_ref[...], m_ref[...], v_ref[...])

    spec = pl.BlockSpec((1, rb, c), lambda i, j: (i, j, 0))
    shp = jax.ShapeDtypeStruct(w.shape, F32)
    return pl.pallas_call(
        body, name=name, grid=(n, r // rb), in_specs=[spec] * 4, out_specs=[spec] * 3,
        out_shape=[shp] * 3, compiler_params=_cparams(2),
    )(w, g, m, v)


def _adamw_pair(core, w, mine, theirs, m, v, name):
    n, r, c = w.shape
    rb = _row_block(r)

    def body(core_ref, w_ref, a_ref, b_ref, m_ref, v_ref, g_ref, d_ref, nm_ref, nv_ref):
        g = jnp.where(pl.program_id(0) == core_ref[0], a_ref[...], b_ref[...])
        g_ref[0] = g
        d_ref[0], nm_ref[0], nv_ref[0] = _adamw_update(w_ref[0], g, m_ref[0], v_ref[0])

    spec = pl.BlockSpec((1, rb, c), lambda i, j, core_ref: (i, j, 0))
    flat = pl.BlockSpec((rb, c), lambda i, j, core_ref: (j, 0))
    shp = jax.ShapeDtypeStruct(w.shape, F32)
    return pl.pallas_call(
        body, name=name,
        grid_spec=pltpu.PrefetchScalarGridSpec(num_scalar_prefetch=1, grid=(n, r // rb),
                                               in_specs=[spec, flat, flat, spec, spec], out_specs=[spec] * 4),
        out_shape=[shp] * 4, compiler_params=_cparams(2),
    )(core, w, mine, theirs, m, v)


def _mixer_fwd(h0, hn, w_in, conv_k, conv_v, avg, pool_w, w_out, g2, tm):
    rows = h0.shape[0]
    rb = CONV_ROW_BLOCK

    def body(h0_ref, hn_ref, win_ref, ck_ref, cv_ref, avg_ref, pw_ref, wout_ref, g2_ref,
             z_ref, u1_ref, m_ref, h1_ref, hn2_ref, ubuf, pbuf):
        i = pl.program_id(0)

        @pl.when(i == 0)
        def _():
            ubuf[pl.ds(0, CONV_HIST), :] = jnp.zeros((CONV_HIST, CONV_W), F32)
            pbuf[pl.ds(0, POOL_HIST), :] = jnp.zeros((POOL_HIST, POOL_W), F32)

        z = _dot(hn_ref[...], win_ref[...])
        z_ref[...] = z
        ubuf[pl.ds(CONV_HIST, tm), :] = z[:, :CONV_W] * _sigmoid(z[:, CONV_W:2 * CONV_W])
        p = z[:, 2 * CONV_W:]
        pbuf[pl.ds(POOL_HIST, tm), :] = p

        def conv_block(r0):
            for l in range(CONV_W // 128):
                ls = pl.ds(128 * l, 128)
                window = ubuf[pl.ds(r0, rb + CONV_HIST), ls]
                acc = jnp.broadcast_to(cv_ref[0:1, ls], (rb, 128))
                for s in range(8):
                    ws = _shifted(window, s)
                    for q in range(CONV_HIST // 8 + 1):
                        j = 8 * q + s - 2
                        if 0 <= j < CONV_TAPS:
                            acc = acc + ck_ref[j:j + 1, ls] * ws[8 * q:8 * q + rb]
                u1_ref[pl.ds(r0, rb), ls] = acc

        _for_row_blocks(tm // rb, rb, conv_block, unroll=True)
        ubuf[pl.ds(0, CONV_HIST), :] = ubuf[pl.ds(tm, CONV_HIST), :]

        u1 = u1_ref[...]
        cen = u1 - _split_dot(u1, avg_ref)
        xhat = cen * lax.rsqrt(_split_dot(cen * cen, avg_ref) + EPS)
        u2 = xhat * cv_ref[1:2, :] + cv_ref[2:3, :]
        m_ref[:, 0:CONV_W] = (u2 * _sigmoid(u2)).astype(BF16)

        for gi, w in enumerate(POOL_WINDOWS):
            ls = pl.ds(POOL_G * gi, POOL_G)
            s = pbuf[pl.ds(POOL_HIST, tm), ls]
            for j in range(1, w):
                s = s + pbuf[pl.ds(POOL_HIST - j, tm), ls]
            d = s * _inv_count(i, tm, w) - p[:, POOL_G * gi:POOL_G * (gi + 1)]
            y = _dot(d.astype(BF16), pw_ref[gi]) * cv_ref[3:4, ls]
            m_ref[:, pl.ds(CONV_W + POOL_G * gi, POOL_G)] = y.astype(BF16)
        pbuf[pl.ds(0, POOL_HIST), :] = pbuf[pl.ds(tm, POOL_HIST), :]

        h1 = h0_ref[...] + _dot(m_ref[...], wout_ref[...])
        h1_ref[...] = h1
        xh, _ = _rms(h1)
        hn2_ref[...] = (xh * g2_ref[...]).astype(BF16)

    def tile(c):
        return pl.BlockSpec((tm, c), lambda i: (i, 0))

    return pl.pallas_call(
        body, name="mixer_fwd", grid=(rows // tm,),
        in_specs=[tile(D_MODEL), tile(D_MODEL), _whole((D_MODEL, IN_COLS)), _whole((CONV_HIST, CONV_W)),
                  _whole((8, CONV_W)), _whole((CONV_W, CONV_W)), _whole((4, POOL_G, POOL_G)),
                  _whole((D_MODEL, D_MODEL)), _whole((1, D_MODEL))],
        out_specs=[tile(IN_COLS), tile(CONV_W), tile(D_MODEL), tile(D_MODEL), tile(D_MODEL)],
        out_shape=[jax.ShapeDtypeStruct((rows, IN_COLS), F32), jax.ShapeDtypeStruct((rows, CONV_W), F32),
                   jax.ShapeDtypeStruct((rows, D_MODEL), BF16), jax.ShapeDtypeStruct((rows, D_MODEL), F32),
                   jax.ShapeDtypeStruct((rows, D_MODEL), BF16)],
        scratch_shapes=[pltpu.VMEM((CONV_HIST + tm, CONV_W), F32), pltpu.VMEM((POOL_HIST + tm, POOL_W), F32)],
        compiler_params=_cparams(1, V7X_VMEM_LIMIT),
    )(h0, hn, w_in, conv_k, conv_v, avg, pool_w, w_out, g2)


def _ffn_fwd(h1, hn2, w_up, layer, kf, w_down, g_next, tm):
    rows = h1.shape[0]
    hist = 8
    w_up_spec = pl.BlockSpec((4, None, D_MODEL, FF_CHUNK), lambda i: (0, layer, 0, 0), pipeline_mode=pl.Buffered(1))

    rb = FFN_ROW_BLOCK

    def body(h1_ref, hn2_ref, wup_ref, kf_ref, wdn_ref, gn_ref, ug_ref, h2_ref, hnn_ref, wg, wv, carry, act_s, acc):
        i = pl.program_id(0)

        @pl.when(i == 0)
        def _():
            carry[...] = jnp.zeros(carry.shape, F32)

        acc[...] = h1_ref[...]
        for c in range(2):
            for buf, cc in ((wg, c), (wv, c + 2)):
                ug = _dot(hn2_ref[...], wup_ref[cc])
                ug_ref[cc] = ug.astype(BF16)
                buf[pl.ds(0, hist), :] = carry[cc]
                buf[pl.ds(hist, tm), :] = ug
                carry[cc] = buf[pl.ds(tm, hist), :]

            def act_block(r0):
                for l in range(FF_CHUNK // 128):
                    ls = pl.ds(128 * l, 128)
                    gate = _conv3(wg[pl.ds(r0, rb + hist), ls], kf_ref, c, ls, hist)[3]
                    val = _conv3(wv[pl.ds(r0, rb + hist), ls], kf_ref, c + 2, ls, hist)[3]
                    act_s[pl.ds(r0, rb), ls] = (gate * _sigmoid(gate) * val).astype(BF16)

            _for_row_blocks(tm // rb, rb, act_block, unroll=True)
            acc[...] += _dot(act_s[...], wdn_ref[c])
        h2 = acc[...]
        h2_ref[...] = h2
        xh, _ = _rms(h2)
        hnn_ref[...] = (xh * gn_ref[...]).astype(BF16)

    def tile(c):
        return pl.BlockSpec((tm, c), lambda i: (i, 0))

    return pl.pallas_call(
        body, name="ffn_fwd", grid=(rows // tm,),
        in_specs=[tile(D_MODEL), tile(D_MODEL), w_up_spec, _whole((4, 8, FF_CHUNK)),
                  _whole((2, FF_CHUNK, D_MODEL), single=True), _whole((1, D_MODEL))],
        out_specs=[pl.BlockSpec((4, tm, FF_CHUNK), lambda i: (0, i, 0)), tile(D_MODEL), tile(D_MODEL)],
        out_shape=[jax.ShapeDtypeStruct((4, rows, FF_CHUNK), BF16), jax.ShapeDtypeStruct((rows, D_MODEL), F32),
                   jax.ShapeDtypeStruct((rows, D_MODEL), BF16)],
        scratch_shapes=[pltpu.VMEM((hist + tm, FF_CHUNK), F32), pltpu.VMEM((hist + tm, FF_CHUNK), F32),
                        pltpu.VMEM((4, hist, FF_CHUNK), F32), pltpu.VMEM((tm, FF_CHUNK), BF16),
                        pltpu.VMEM((tm, D_MODEL), F32)],
        compiler_params=_cparams(1, V7X_VMEM_LIMIT),
    )(h1, hn2, w_up, kf, w_down, g_next)


def _loss_head(h, tgt, g, seq, tm):
    rows = h.shape[0]
    tgt_tiles = -(-seq // tm)

    def body(h_ref, tprev_ref, t_ref, g_ref, dh_ref, loss_ref, dg_ref):
        i = pl.program_id(0)

        @pl.when(i == 0)
        def _():
            loss_ref[...] = jnp.zeros(loss_ref.shape, F32)
            dg_ref[...] = jnp.zeros(dg_ref.shape, F32)

        t = i * tm + lax.broadcasted_iota(jnp.int32, (tm, 1), 0)
        inside = jnp.logical_and(t >= N_META, t < N_META + seq)
        tgt = jnp.concatenate([tprev_ref[...], t_ref[0:tm - N_META, :]], axis=0)
        xhat, r = _rms(h_ref[...])
        err = jnp.where(inside, xhat * g_ref[...] - tgt, 0.0)
        loss_ref[...] += _colsum(err * err)
        dh, dg_rows = _rms_bwd(err * (1.0 / D_MODEL), xhat, r, g_ref[...])
        dh_ref[...] = dh
        dg_ref[...] += _colsum(dg_rows)

        @pl.when(i == rows // tm - 1)
        def _():
            total = jnp.sum(loss_ref[...], axis=1, keepdims=True) * (0.5 / D_MODEL)
            loss_ref[...] = jnp.broadcast_to(total, loss_ref.shape)

    tile = pl.BlockSpec((tm, D_MODEL), lambda i: (i, 0))
    t_prev = pl.BlockSpec((N_META, D_MODEL), lambda i: (jnp.maximum(i * (tm // N_META) - 1, 0), 0))
    t_own = pl.BlockSpec((tm, D_MODEL), lambda i: (jnp.minimum(i, tgt_tiles - 1), 0))
    vec = jax.ShapeDtypeStruct((1, D_MODEL), F32)
    return pl.pallas_call(
        body, name="loss_head", grid=(rows // tm,),
        in_specs=[tile, t_prev, t_own, _whole((1, D_MODEL))],
        out_specs=[tile, _whole((1, D_MODEL)), _whole((1, D_MODEL))],
        out_shape=[jax.ShapeDtypeStruct((rows, D_MODEL), F32), vec, vec],
        compiler_params=_cparams(1),
    )(h, tgt, tgt, g)


def _ffn_bwd(dh2, hn2, ug0, w_down_t, w_up_t, kf, tm):
    rows = dh2.shape[0]
    nt = rows // tm
    hist = 16
    fut = 8
    rb = FFN_ROW_BLOCK
    near = 8

    def body(dh2_ref, hn2_ref, ugg_ref, ugv_ref, hg_ref, hv_ref, wdt_ref, wutg_ref, wutv_ref, kf_ref,
             dhn_ref, dkf_ref, dwup_ref, dwdn_ref, wg, wv, dgb, dvb, carry, dkacc, act_s, dug_s, acc_up, acc_dn):
        c = pl.program_id(0)
        i = pl.program_id(1)
        first_tile = jnp.where(i == nt - 1, 1.0, 0.0)

        @pl.when(i == 0)
        def _():
            carry[...] = jnp.zeros(carry.shape, F32)
            dkacc[...] = jnp.zeros(dkacc.shape, F32)
            acc_up[...] = jnp.zeros(acc_up.shape, F32)
            acc_dn[...] = jnp.zeros(acc_dn.shape, F32)

        def run():
            n_blocks = tm // rb
            dh2b = dh2_ref[...].astype(BF16)
            sides = ((wg, dgb, ugg_ref, hg_ref, 0, c), (wv, dvb, ugv_ref, hv_ref, 1, c + 2))
            for buf, dbuf, u_ref, h_ref, s, cc in sides:
                buf[pl.ds(0, hist), :] = h_ref[0].astype(F32) * (1.0 - first_tile)
                buf[pl.ds(hist, tm), :] = u_ref[0].astype(F32)
                dbuf[pl.ds(tm, fut), :] = carry[s]
            dgb[pl.ds(0, tm), :] = _dot(dh2b, wdt_ref[0])

            def grad_block(r0):
                for l in range(FF_CHUNK // 128):
                    ls = pl.ds(128 * l, 128)
                    g0, g1, g2, gate = _conv3(wg[pl.ds(r0 + hist - near, rb + near), ls], kf_ref, c, ls, near)
                    v0, v1, v2, val = _conv3(wv[pl.ds(r0 + hist - near, rb + near), ls], kf_ref, c + 2, ls, near)
                    sg = _sigmoid(gate)
                    silu = gate * sg
                    act_s[pl.ds(r0, rb), ls] = (silu * val).astype(BF16)
                    dact = dgb[pl.ds(r0, rb), ls]
                    dgate = dact * val * (sg * (1.0 + gate * (1.0 - sg)))
                    dval = dact * silu
                    dgb[pl.ds(r0, rb), ls] = dgate
                    dvb[pl.ds(r0, rb), ls] = dval
                    for s, dv, taps in ((0, dgate, (g0, g1, g2)), (1, dval, (v0, v1, v2))):
                        for j in range(FFN_TAPS):
                            dkacc[s, pl.ds(8 * j, 8), ls] += _fold8(dv * taps[j])

            _for_row_blocks(n_blocks, rb, grad_block, unroll=True)
            for buf, dbuf, u_ref, h_ref, s, cc in sides:
                carry[s] = dbuf[pl.ds(0, fut), :]

            def conv_block(r0):
                for l in range(FF_CHUNK // 128):
                    ls = pl.ds(128 * l, 128)
                    for buf, dbuf, u_ref, h_ref, s, cc in sides:
                        window = dbuf[pl.ds(r0, rb + fut), ls]
                        dug0 = (kf_ref[cc, 0:1, ls] * _shifted(window, 2)[0:rb] + kf_ref[cc, 1:2, ls] * _shifted(window, 1)[0:rb]
                                + kf_ref[cc, 2:3, ls] * window[0:rb])
                        dug_s[s, pl.ds(r0, rb), ls] = dug0.astype(BF16)

            _for_row_blocks(n_blocks, rb, conv_block, unroll=True)
            dhn_ref[0] = _dot(dug_s[0], wutg_ref[0]) + _dot(dug_s[1], wutv_ref[0])
            acc_up[0] += _dot_tn(hn2_ref[...], dug_s[0])
            acc_up[1] += _dot_tn(hn2_ref[...], dug_s[1])
            acc_dn[...] += _dot_tn(act_s[...], dh2b)

        run()

        @pl.when(i == nt - 1)
        def _():
            for s in range(2):
                for j in range(FFN_TAPS):
                    dkf_ref[c + 2 * s, j:j + 1, :] = _colsum(dkacc[s, pl.ds(8 * j, 8), :])
                dkf_ref[c + 2 * s, FFN_TAPS:8, :] = jnp.zeros((8 - FFN_TAPS, FF_CHUNK), F32)
            pltpu.sync_copy(acc_up, dwup_ref.at[c])
            pltpu.sync_copy(acc_dn, dwdn_ref.at[c])

    def tile(cols):
        return pl.BlockSpec((tm, cols), lambda c, i: (nt - 1 - i, 0))

    def chunk(off, r, halo_rows=None):
        if halo_rows is None:
            return pl.BlockSpec((1, r, FF_CHUNK), lambda c, i: (c + off, nt - 1 - i, 0))
        return pl.BlockSpec((1, r, FF_CHUNK), lambda c, i: (c + off, jnp.maximum((nt - 1 - i) * (tm // r) - 1, 0), 0))

    def weight(off, r, cols):
        return pl.BlockSpec((1, r, cols), lambda c, i: (c + off, 0, 0), pipeline_mode=pl.Buffered(1))

    return pl.pallas_call(
        body, name="ffn_bwd", grid=(2, nt),
        in_specs=[tile(D_MODEL), tile(D_MODEL), chunk(0, tm), chunk(2, tm), chunk(0, hist, True), chunk(2, hist, True),
                  weight(0, D_MODEL, FF_CHUNK), weight(0, FF_CHUNK, D_MODEL), weight(2, FF_CHUNK, D_MODEL),
                  _whole((4, 8, FF_CHUNK))],
        out_specs=[pl.BlockSpec((1, tm, D_MODEL), lambda c, i: (c, nt - 1 - i, 0)), _whole((4, 8, FF_CHUNK)), _ANY, _ANY],
        out_shape=[jax.ShapeDtypeStruct((2, rows, D_MODEL), F32), jax.ShapeDtypeStruct((4, 8, FF_CHUNK), F32),
                   jax.ShapeDtypeStruct((2, 2, D_MODEL, FF_CHUNK), F32), jax.ShapeDtypeStruct((2, FF_CHUNK, D_MODEL), F32)],
        scratch_shapes=[pltpu.VMEM((hist + tm, FF_CHUNK), F32), pltpu.VMEM((hist + tm, FF_CHUNK), F32),
                        pltpu.VMEM((tm + fut, FF_CHUNK), F32), pltpu.VMEM((tm + fut, FF_CHUNK), F32),
                        pltpu.VMEM((2, fut, FF_CHUNK), F32), pltpu.VMEM((2, 8 * FFN_TAPS, FF_CHUNK), F32),
                        pltpu.VMEM((tm, FF_CHUNK), BF16), pltpu.VMEM((2, tm, FF_CHUNK), BF16),
                        pltpu.VMEM((2, D_MODEL, FF_CHUNK), F32), pltpu.VMEM((FF_CHUNK, D_MODEL), F32)],
        compiler_params=_cparams(2, V7X_VMEM_LIMIT),
    )(dh2, hn2, ug0, ug0, ug0, ug0, w_down_t, w_up_t, w_up_t, kf)


def _mixer_bwd(dh2, dhn2, h1, g2, h0, z, u1, hn1, m, w_out_t, w_in_t, conv_k, conv_v, avg, pool_w, pool_w_t, g1, tm,
               x_rows=None):
    rows = dh2.shape[0]
    nt = rows // tm
    rb = CONV_ROW_BLOCK
    in_shard = IN_COLS // 4

    def body(dh2_ref, dhn2_ref, h1_ref, g2_ref, h0_ref, z_ref, zh_ref, u1_ref, hn1_ref, m_ref, wot_ref, wit_ref, ck_ref,
             cv_ref, avg_ref, pw_ref, pwt_ref, g1_ref,
             dh0_ref, dtop_ref, dk_ref, ds_ref, dpw_ref, dg1_ref, dg2_ref, dwo_ref, dwi_ref,
             ubuf, dbuf, pbuf, ebuf, dcarry, ecarry, dkacc, dzs, dz_s, dh1_s, acc_out, acc_in, xcarry):
        i = pl.program_id(0)
        ti = nt - 1 - i
        has_past = jnp.where(ti > 0, 1.0, 0.0)

        @pl.when(i == 0)
        def _():
            dcarry[...] = jnp.zeros(dcarry.shape, F32)
            ecarry[...] = jnp.zeros(ecarry.shape, F32)
            dkacc[...] = jnp.zeros(dkacc.shape, F32)
            ds_ref[...] = jnp.zeros(ds_ref.shape, F32)
            dpw_ref[...] = jnp.zeros(dpw_ref.shape, F32)
            dg1_ref[...] = jnp.zeros(dg1_ref.shape, F32)
            dg2_ref[...] = jnp.zeros(dg2_ref.shape, F32)
            acc_out[...] = jnp.zeros(acc_out.shape, F32)
            acc_in[...] = jnp.zeros(acc_in.shape, F32)
            xcarry[...] = jnp.zeros(xcarry.shape, F32)

        xh1, r1 = _rms(h1_ref[...])
        dx1, dg2_rows = _rms_bwd(dhn2_ref[0] + dhn2_ref[1], xh1, r1, g2_ref[...])
        dh1_s[...] = dh2_ref[...] + dx1
        dg2_ref[...] += _colsum(dg2_rows)
        dh1b = dh1_s[...].astype(BF16)
        acc_out[...] += _dot_tn(m_ref[...], dh1b)
        dm = _dot(dh1b, wot_ref[...])
        z = z_ref[...]
        a = z[:, :CONV_W]
        sg = _sigmoid(z[:, CONV_W:2 * CONV_W])
        p = z[:, 2 * CONV_W:]
        zh = zh_ref[...] * has_past
        ubuf[pl.ds(0, CONV_HIST), :] = zh[:, :CONV_W] * _sigmoid(zh[:, CONV_W:2 * CONV_W])
        ubuf[pl.ds(CONV_HIST, tm), :] = a * sg
        pbuf[pl.ds(0, POOL_HIST), :] = zh[CONV_HIST - POOL_HIST:, 2 * CONV_W:]
        pbuf[pl.ds(POOL_HIST, tm), :] = p

        u1 = u1_ref[...]
        cen = u1 - _split_dot(u1, avg_ref)
        rstd = lax.rsqrt(_split_dot(cen * cen, avg_ref) + EPS)
        xhat = cen * rstd
        u2 = xhat * cv_ref[1:2, :] + cv_ref[2:3, :]
        s2 = _sigmoid(u2)
        du2 = dm[:, :CONV_W] * (s2 * (1.0 + u2 * (1.0 - s2)))
        ds_ref[1:2, :] += _colsum(du2 * xhat)
        ds_ref[2:3, :] += _colsum(du2)
        dxh = du2 * cv_ref[1:2, :]
        du1 = rstd * (dxh - _split_dot(dxh, avg_ref) - xhat * _split_dot(dxh * xhat, avg_ref))
        ds_ref[0:1, :] += _colsum(du1)
        dbuf[pl.ds(0, tm), :] = du1
        dbuf[pl.ds(tm, CONV_HIST), :] = dcarry[...]
        dcarry[...] = dbuf[pl.ds(0, CONV_HIST), :]

        def conv_block(r0):
            for l in range(CONV_W // 128):
                ls = pl.ds(128 * l, 128)
                dwin = dbuf[pl.ds(r0, rb + CONV_HIST), ls]
                uwin = ubuf[pl.ds(r0, rb + CONV_HIST), ls]
                dblk = dwin[0:rb]
                du0 = jnp.zeros((rb, 128), F32)
                for s in range(8):
                    ds_ = _shifted(dwin, s)
                    us_ = _shifted(uwin, s)
                    for q in range(CONV_HIST // 8 + 1):
                        o = 8 * q + s
                        if 0 <= CONV_TAPS - 1 - o < CONV_TAPS:
                            j = CONV_TAPS - 1 - o
                            du0 = du0 + ck_ref[j:j + 1, ls] * ds_[8 * q:8 * q + rb]
                        j = o - 2
                        if 0 <= j < CONV_TAPS:
                            prod = dblk * us_[8 * q:8 * q + rb]
                            part = prod[0:8]
                            for v in range(1, rb // 8):
                                part = part + prod[8 * v:8 * v + 8]
                            dkacc[pl.ds(8 * j, 8), ls] += part
                dzs[pl.ds(r0, rb), ls] = du0

        _for_row_blocks(tm // rb, rb, conv_block, unroll=True)
        du0 = dzs[:, 0:CONV_W]
        dz_s[:, 0:CONV_W] = (du0 * sg).astype(BF16)
        dz_s[:, CONV_W:2 * CONV_W] = (du0 * a * sg * (1.0 - sg)).astype(BF16)

        for gi, w in enumerate(POOL_WINDOWS):
            ls = pl.ds(POOL_G * gi, POOL_G)
            cols = slice(CONV_W + POOL_G * gi, CONV_W + POOL_G * (gi + 1))
            inv = _inv_count(ti, tm, w)
            s = pbuf[pl.ds(POOL_HIST, tm), ls]
            for j in range(1, w):
                s = s + pbuf[pl.ds(POOL_HIST - j, tm), ls]
            d = (s * inv - p[:, POOL_G * gi:POOL_G * (gi + 1)]).astype(BF16)
            dyp = dm[:, cols]
            ds_ref[3:4, ls] += _colsum(dyp * _dot(d, pw_ref[gi]))
            dyb = (dyp * cv_ref[3:4, ls]).astype(BF16)
            dpw_ref[gi] += _dot_tn(d, dyb)
            dd = _dot(dyb, pwt_ref[gi])
            ebuf[pl.ds(0, tm), ls] = dd * inv
            ebuf[pl.ds(tm, POOL_HIST), ls] = ecarry[:, ls]
            dp = ebuf[pl.ds(0, tm), ls] - dd
            for j in range(1, w):
                dp = dp + ebuf[pl.ds(j, tm), ls]
            dz_s[:, pl.ds(2 * CONV_W + POOL_G * gi, POOL_G)] = dp.astype(BF16)
        ecarry[...] = ebuf[pl.ds(0, POOL_HIST), :]

        acc_in[...] += _dot_tn(hn1_ref[...], dz_s[...])
        dhn = _dot(dz_s[...], wit_ref[...])
        xh, r = _rms(h0_ref[...])
        dx, dg_rows = _rms_bwd(dhn, xh, r, g1_ref[...])
        dh0 = dh1_s[...] + dx
        if x_rows is None:
            dh0_ref[...] = dh0
        else:
            dh0_ref[0:tm - N_META, :] = dh0[N_META:]
            dh0_ref[tm - N_META:tm, :] = xcarry[...]
            xcarry[...] = dh0[0:N_META]
        dg1_ref[...] += _colsum(dg_rows)

        @pl.when(i == nt - 1)
        def _():
            dtop_ref[...] = dh0[0:N_META]
            for j in range(CONV_TAPS):
                dk_ref[j:j + 1, :] = _colsum(dkacc[pl.ds(8 * j, 8), :])
            dk_ref[CONV_TAPS:CONV_HIST, :] = jnp.zeros((CONV_HIST - CONV_TAPS, CONV_W), F32)
            pltpu.sync_copy(acc_out, dwo_ref)
            for k in range(4):
                pltpu.sync_copy(acc_in.at[:, pl.ds(in_shard * k, in_shard)], dwi_ref.at[k])

    def tile(c):
        return pl.BlockSpec((tm, c), lambda i: (nt - 1 - i, 0))

    halo = pl.BlockSpec((CONV_HIST, IN_COLS), lambda i: (jnp.maximum((nt - 1 - i) * (tm // CONV_HIST) - 1, 0), 0))
    vec = jax.ShapeDtypeStruct((1, D_MODEL), F32)
    if x_rows is None:
        first_spec, first_shape = tile(D_MODEL), jax.ShapeDtypeStruct((rows, D_MODEL), F32)
    else:
        x_tiles = -(-x_rows // tm)
        first_spec = pl.BlockSpec((tm, D_MODEL), lambda i: (jnp.minimum(nt - 1 - i, x_tiles - 1), 0))
        first_shape = jax.ShapeDtypeStruct((x_rows, D_MODEL), F32)
    return pl.pallas_call(
        body, name="mixer_bwd", grid=(nt,),
        in_specs=[tile(D_MODEL), pl.BlockSpec((2, tm, D_MODEL), lambda i: (0, nt - 1 - i, 0)), tile(D_MODEL),
                  _whole((1, D_MODEL)), tile(D_MODEL), tile(IN_COLS), halo, tile(CONV_W), tile(D_MODEL), tile(D_MODEL),
                  _whole((D_MODEL, D_MODEL)), _whole((IN_COLS, D_MODEL)), _whole((CONV_HIST, CONV_W)), _whole((8, CONV_W)),
                  _whole((CONV_W, CONV_W)), _whole((4, POOL_G, POOL_G)), _whole((4, POOL_G, POOL_G)), _whole((1, D_MODEL))],
        out_specs=[first_spec, _whole((N_META, D_MODEL)), _whole((CONV_HIST, CONV_W)), _whole((8, CONV_W)),
                   _whole((4, POOL_G, POOL_G)), _whole((1, D_MODEL)), _whole((1, D_MODEL)), _ANY, _ANY],
        out_shape=[first_shape, jax.ShapeDtypeStruct((N_META, D_MODEL), F32), jax.ShapeDtypeStruct((CONV_HIST, CONV_W), F32),
                   jax.ShapeDtypeStruct((8, CONV_W), F32), jax.ShapeDtypeStruct((4, POOL_G, POOL_G), F32), vec, vec,
                   jax.ShapeDtypeStruct((D_MODEL, D_MODEL), F32), jax.ShapeDtypeStruct((4, D_MODEL, in_shard), F32)],
        scratch_shapes=[pltpu.VMEM((CONV_HIST + tm, CONV_W), F32), pltpu.VMEM((tm + CONV_HIST, CONV_W), F32),
                        pltpu.VMEM((POOL_HIST + tm, POOL_W), F32), pltpu.VMEM((tm + POOL_HIST, POOL_W), F32),
                        pltpu.VMEM((CONV_HIST, CONV_W), F32), pltpu.VMEM((POOL_HIST, POOL_W), F32),
                        pltpu.VMEM((8 * CONV_HIST, CONV_W), F32), pltpu.VMEM((tm, CONV_W), F32),
                        pltpu.VMEM((tm, IN_COLS), BF16), pltpu.VMEM((tm, D_MODEL), F32),
                        pltpu.VMEM((D_MODEL, D_MODEL), F32), pltpu.VMEM((D_MODEL, IN_COLS), F32),
                        pltpu.VMEM((N_META, D_MODEL), F32)],
        compiler_params=_cparams(1, V7X_VMEM_LIMIT),
    )(dh2, dhn2, h1, g2, h0, z, z, u1, hn1, m, w_out_t, w_in_t, conv_k, conv_v, avg, pool_w, pool_w_t, g1)


def _head_average():
    head = lax.broadcasted_iota(jnp.int32, (CONV_W, CONV_W), 0) // 64
    return jnp.where(head == head.T, 1.0 / 64, 0.0).astype(BF16)


def _local_step(x, meta, tgt, layers, ffn_weights, final_g, tm, tm_ffn):
    avg = _head_average()
    depth = len(layers)
    saved = []
    seq = x.shape[0]
    step = math.lcm(tm, tm_ffn)
    rows = -(-(N_META + seq) // step) * step
    h, hn = _embed(x, meta, layers[0]["g1"], rows, tm)
    for l, w in enumerate(layers):
        g_next = layers[l + 1]["g1"] if l + 1 < depth else final_g
        z, u1, m, h1, hn2 = _mixer_fwd(h, hn, w["w_in"], w["conv_k"], w["conv_v"], avg, w["pool_w"], w["w_out"], w["g2"], tm)
        if l == 0:
            w_up_all, ffn = ffn_weights(hn2)
        ug0, h2, hn_next = _ffn_fwd(h1, hn2, w_up_all, l, w["kf"], ffn[l]["w_down"], g_next, tm_ffn)
        saved.append((h, hn, z, u1, m, h1, hn2, ug0))
        h, hn = h2, hn_next
    dh, loss_cols, dfinal_g = _loss_head(h, tgt, final_g, seq, tm)

    grads = [None] * depth
    for l in reversed(range(depth)):
        w = layers[l]
        h0, hn1, z, u1, m, h1, hn2, ug0 = saved[l]
        dhn2, dkf, dw_up, dw_down = _ffn_bwd(dh, hn2, ug0, ffn[l]["w_down_t"], ffn[l]["w_up_t"], w["kf"], tm_ffn)
        dh0, dtop, dk, dsmall, dpw, dg1, dg2, dw_out, dw_in = _mixer_bwd(
            dh, dhn2, h1, w["g2"], h0, z, u1, hn1, m, w["w_out_t"], w["w_in_t"], w["conv_k"], w["conv_v"], avg,
            w["pool_w"], w["pool_w_t"], w["g1"], tm, x_rows=seq if l == 0 else None)
        grads[l] = dict(dw_in=dw_in, dw_out=dw_out.reshape(4, D_MODEL // 4, D_MODEL),
                        dw_up=dw_up.transpose(1, 0, 2, 3).reshape(4, D_MODEL, FF_CHUNK),
                        dw_down=dw_down.reshape(4, D_FF // 4, D_MODEL), dk=dk, dsmall=dsmall, dpw=dpw, dg1=dg1, dg2=dg2, dkf=dkf)
        dh = dh0
    return loss_cols, dh, dtop, grads, dfinal_g


_ANY = pl.BlockSpec(memory_space=pl.ANY)


def _place():
    x, y, c = lax.axis_index("x"), lax.axis_index("y"), lax.axis_index("c")
    chips = [(1 - x, y), (x, 1 - y), (1 - x, 1 - y)]
    return x, y, c, chips


def _remote(src, dst, send_sems, recv_sems, idx, to):
    return pltpu.make_async_remote_copy(src_ref=src, dst_ref=dst, send_sem=send_sems.at[idx], recv_sem=recv_sems.at[idx],
                                        device_id=to, device_id_type=MESH)


def _gather_chips(xs):
    n = len(xs)

    def body(*refs):
        x_refs, o_refs = refs[:n], refs[n:2 * n]
        send_sems, recv_sems = refs[2 * n:]
        x, y, c, chips = _place()
        k = 2 * x + y
        sibling = (x, y, 1 - c)
        sends = []
        for j, chip in enumerate(chips):
            for a in range(n):
                sends.append(_remote(x_refs[a].at[k, c], o_refs[a].at[k, c], send_sems, recv_sems, 3 * a + j, (*chip, c)))
                sends[-1].start()
        for j, chip in enumerate(chips):
            kj = 2 * chip[0] + chip[1]
            for a in range(n):
                landed = o_refs[a].at[kj, c]
                _remote(landed, landed, send_sems, recv_sems, 3 * a + j, sibling).wait_recv()
                sends.append(_remote(landed, landed, send_sems, recv_sems, 3 * n + 3 * a + j, sibling))
                sends[-1].start()
        for j, chip in enumerate(chips):
            kj = 2 * chip[0] + chip[1]
            for a in range(n):
                passed = o_refs[a].at[kj, 1 - c]
                _remote(passed, passed, send_sems, recv_sems, 3 * n + 3 * a + j, sibling).wait_recv()
        for cp in sends:
            cp.wait_send()

    return pl.pallas_call(
        body, name="gather_chips", in_specs=[_ANY] * n, out_specs=[_ANY] * n,
        out_shape=[jax.ShapeDtypeStruct(v.shape, v.dtype) for v in xs],
        input_output_aliases={a: a for a in range(n)},
        scratch_shapes=[pltpu.SemaphoreType.DMA((6 * n,)), pltpu.SemaphoreType.DMA((6 * n,))],
    )(*xs)


_HBM = pl.BlockSpec(memory_space=pltpu.HBM)
_SEM = pl.BlockSpec(memory_space=pltpu.SEMAPHORE)
_SPLIT_COPY = pltpu.CompilerParams(has_side_effects=pltpu.SideEffectType.DATAFLOW_SIDE_EFFECTING)


def _gather_start(xs, after):
    n = len(xs)

    def body(*refs):
        x_refs = refs[:n]
        send_sems, recv_sems = refs[n + 1], refs[n + 2]
        token = refs[2 * n + 3]
        x, y, c, chips = _place()
        k = 2 * x + y
        for j, chip in enumerate(chips):
            for a in range(n):
                mine = x_refs[a].at[k, c]
                _remote(mine, mine, send_sems, recv_sems, 3 * a + j, (*chip, c)).start()
        token[...] = jnp.zeros(token.shape, F32)

    return pl.pallas_call(
        body, name="gather_start", in_specs=[_HBM] * n + [_ANY],
        out_specs=(_SEM, _SEM, *[_HBM] * n, pl.BlockSpec(memory_space=pltpu.VMEM)),
        out_shape=(pltpu.SemaphoreType.DMA((3 * n,)), pltpu.SemaphoreType.DMA((3 * n,)),
                   *[pltpu.HBM(v.shape, v.dtype) for v in xs], jax.ShapeDtypeStruct((8, 128), F32)),
        input_output_aliases={a: 2 + a for a in range(n)}, compiler_params=_SPLIT_COPY,
    )(*[pltpu.with_memory_space_constraint(v, pltpu.HBM) for v in xs], after)


def _gather_wait(send_sems, recv_sems, xs, after):
    n = len(xs)

    def body(*refs):
        x_refs = refs[:n]
        send_sems, recv_sems = refs[n], refs[n + 1]
        x, y, c, chips = _place()
        k = 2 * x + y
        for j, chip in enumerate(chips):
            kj = 2 * chip[0] + chip[1]
            for a in range(n):
                cp = _remote(x_refs[a].at[k, c], x_refs[a].at[kj, c], send_sems, recv_sems, 3 * a + j, (*chip, c))
                cp.wait_send()
                cp.wait_recv()

    return pl.pallas_call(
        body, name="gather_wait", in_specs=[_HBM] * n + [_SEM, _SEM, _ANY], out_specs=[_HBM] * n,
        out_shape=[pltpu.HBM(v.shape, v.dtype) for v in xs],
        input_output_aliases={a: a for a in range(n)}, compiler_params=_SPLIT_COPY,
    )(*xs, send_sems, recv_sems, after)


def _gather_forward(xs):
    n = len(xs)

    def body(*refs):
        x_refs, o_refs = refs[:n], refs[n:2 * n]
        send_sems, recv_sems = refs[2 * n:]
        x, y, c, chips = _place()
        sibling = (x, y, 1 - c)
        sends = []
        for j, chip in enumerate(chips):
            kj = 2 * chip[0] + chip[1]
            for a in range(n):
                landed = x_refs[a].at[kj, c]
                sends.append(_remote(landed, o_refs[a].at[kj, c], send_sems, recv_sems, 3 * a + j, sibling))
                sends[-1].start()
        for j, chip in enumerate(chips):
            kj = 2 * chip[0] + chip[1]
            for a in range(n):
                passed = o_refs[a].at[kj, 1 - c]
                _remote(passed, passed, send_sems, recv_sems, 3 * a + j, sibling).wait_recv()
        for cp in sends:
            cp.wait_send()

    return pl.pallas_call(
        body, name="gather_forward", in_specs=[_ANY] * n, out_specs=[_ANY] * n,
        out_shape=[jax.ShapeDtypeStruct(v.shape, v.dtype) for v in xs],
        input_output_aliases={a: a for a in range(n)},
        scratch_shapes=[pltpu.SemaphoreType.DMA((3 * n,)), pltpu.SemaphoreType.DMA((3 * n,))],
    )(*xs)


def _pair_exchange(g0s, g1s):
    n = len(g0s)

    def body(*refs):
        g0, g1, out = refs[:n], refs[n:2 * n], refs[2 * n:3 * n]
        send_sems, recv_sems = refs[3 * n:]
        x, y, c, _ = _place()
        sibling = (x, y, 1 - c)
        for a in range(n):
            @pl.when(c == 0)
            def _():
                _remote(g1[a], out[a], send_sems, recv_sems, a, sibling).start()

            @pl.when(c == 1)
            def _():
                _remote(g0[a], out[a], send_sems, recv_sems, a, sibling).start()
        for a in range(n):
            cp = _remote(g0[a], out[a], send_sems, recv_sems, a, sibling)
            cp.wait_recv()
            cp.wait_send()

    return pl.pallas_call(
        body, name="pair_exchange", in_specs=[_ANY] * (2 * n), out_specs=[_ANY] * n,
        out_shape=[jax.ShapeDtypeStruct(v.shape, v.dtype) for v in g0s],
        scratch_shapes=[pltpu.SemaphoreType.DMA((n,)), pltpu.SemaphoreType.DMA((n,))],
    )(*g0s, *g1s)


def _chip_exchange(parts):
    n = len(parts)

    def body(*refs):
        p_refs, o_refs = refs[:n], refs[n:2 * n]
        send_sems, recv_sems = refs[2 * n:]
        x, y, c, chips = _place()
        sends = []
        for j, chip in enumerate(chips):
            kj = 2 * chip[0] + chip[1]
            for a in range(n):
                sends.append(_remote(p_refs[a].at[kj], o_refs[a].at[j], send_sems, recv_sems, 3 * a + j, (*chip, c)))
                sends[-1].start()
        for j, chip in enumerate(chips):
            for a in range(n):
                landed = o_refs[a].at[j]
                _remote(landed, landed, send_sems, recv_sems, 3 * a + j, (*chip, c)).wait_recv()
        for cp in sends:
            cp.wait_send()

    return pl.pallas_call(
        body, name="chip_exchange", in_specs=[_ANY] * n, out_specs=[_ANY] * n,
        out_shape=[jax.ShapeDtypeStruct((3,) + v.shape[1:], v.dtype) for v in parts],
        scratch_shapes=[pltpu.SemaphoreType.DMA((3 * n,)), pltpu.SemaphoreType.DMA((3 * n,))],
    )(*parts)


def _pair_share(reds):
    n = len(reds)

    def body(*refs):
        r_refs, o_refs = refs[:n], refs[n:2 * n]
        send_sems, recv_sems = refs[2 * n:]
        x, y, c, _ = _place()
        sends = [_remote(r_refs[a], o_refs[a], send_sems, recv_sems, a, (x, y, 1 - c)) for a in range(n)]
        for cp in sends:
            cp.start()
        for cp in sends:
            cp.wait_recv()
        for cp in sends:
            cp.wait_send()

    return pl.pallas_call(
        body, name="pair_share", in_specs=[_ANY] * n, out_specs=[_ANY] * n,
        out_shape=[jax.ShapeDtypeStruct(v.shape, v.dtype) for v in reds],
        scratch_shapes=[pltpu.SemaphoreType.DMA((n,)), pltpu.SemaphoreType.DMA((n,))],
    )(*reds)


def _all_reduce_small(pack):
    p, cols = pack.shape

    def body(x_ref, o_ref, sib, chipbuf, send_sems, recv_sems):
        x, y, c, chips = _place()
        k = 2 * x + y
        sibling = (x, y, 1 - c)
        pair = _remote(x_ref, sib, send_sems, recv_sems, 0, sibling)
        pair.start()
        pair.wait_recv()
        chipbuf[k] = x_ref[...] + sib[...]
        sends = [_remote(chipbuf.at[k], chipbuf.at[k], send_sems, recv_sems, 1 + j, (*chip, c)) for j, chip in enumerate(chips)]
        for cp in sends:
            cp.start()
        for j, chip in enumerate(chips):
            landed = chipbuf.at[2 * chip[0] + chip[1]]
            _remote(landed, landed, send_sems, recv_sems, 1 + j, (*chip, c)).wait_recv()
        pair.wait_send()
        for cp in sends:
            cp.wait_send()
        o_ref[...] = (chipbuf[0] + chipbuf[1]) + (chipbuf[2] + chipbuf[3])

    vm = pl.BlockSpec(memory_space=pltpu.VMEM)
    return pl.pallas_call(
        body, name="all_reduce_small", in_specs=[vm], out_specs=vm,
        out_shape=jax.ShapeDtypeStruct(pack.shape, F32),
        scratch_shapes=[pltpu.VMEM((p, cols), F32), pltpu.VMEM((4, p, cols), F32),
                        pltpu.SemaphoreType.DMA((4,)), pltpu.SemaphoreType.DMA((4,))],
    )(pack)


_BIG = ("w_in", "w_out", "w_up", "w_down")
_SMALL = ("norm1_g", "conv_dw_b", "conv_ln_g", "conv_ln_b", "pool_w", "pool_scale", "norm2_g", "final_g",
          "meta_tokens", "conv_dw_k", "ffn_dw_k")


def _rows8(v):
    return jnp.pad(v, ((0, -v.shape[0] % 8), (0, 0)))


def _pack_flat(arrs, rows):
    flat = jnp.concatenate([a.reshape(-1) for a in arrs])
    return jnp.pad(flat, (0, rows * D_MODEL - flat.shape[0])).reshape(1, rows, D_MODEL)


def _unpack_flat(packed, like):
    flat = packed.reshape(-1)
    out, off = [], 0
    for a in like:
        out.append(flat[off:off + a.size].reshape(a.shape))
        off += a.size
    return out


def kernel(x, meta_tokens, norm1_g, w_in, conv_dw_k, conv_dw_b, conv_ln_g, conv_ln_b, pool_w, pool_scale, w_out, norm2_g, w_up, ffn_dw_k, w_down, final_g, loss_target, m_meta_tokens, m_norm1_g, m_w_in, m_conv_dw_k, m_conv_dw_b, m_conv_ln_g, m_conv_ln_b, m_pool_w, m_pool_scale, m_w_out, m_norm2_g, m_w_up, m_ffn_dw_k, m_w_down, m_final_g, v_meta_tokens, v_norm1_g, v_w_in, v_conv_dw_k, v_conv_dw_b, v_conv_ln_g, v_conv_ln_b, v_pool_w, v_pool_scale, v_w_out, v_norm2_g, v_w_up, v_ffn_dw_k, v_w_down, v_final_g):
    weights = dict(meta_tokens=meta_tokens, norm1_g=norm1_g, w_in=w_in, conv_dw_k=conv_dw_k, conv_dw_b=conv_dw_b,
                   conv_ln_g=conv_ln_g, conv_ln_b=conv_ln_b, pool_w=pool_w, pool_scale=pool_scale, w_out=w_out,
                   norm2_g=norm2_g, w_up=w_up, ffn_dw_k=ffn_dw_k, w_down=w_down, final_g=final_g)
    mom1 = dict(meta_tokens=m_meta_tokens, norm1_g=m_norm1_g, w_in=m_w_in, conv_dw_k=m_conv_dw_k, conv_dw_b=m_conv_dw_b,
                conv_ln_g=m_conv_ln_g, conv_ln_b=m_conv_ln_b, pool_w=m_pool_w, pool_scale=m_pool_scale, w_out=m_w_out,
                norm2_g=m_norm2_g, w_up=m_w_up, ffn_dw_k=m_ffn_dw_k, w_down=m_w_down, final_g=m_final_g)
    mom2 = dict(meta_tokens=v_meta_tokens, norm1_g=v_norm1_g, w_in=v_w_in, conv_dw_k=v_conv_dw_k, conv_dw_b=v_conv_dw_b,
                conv_ln_g=v_conv_ln_g, conv_ln_b=v_conv_ln_b, pool_w=v_pool_w, pool_scale=v_pool_scale, w_out=v_w_out,
                norm2_g=v_norm2_g, w_up=v_w_up, ffn_dw_k=v_ffn_dw_k, w_down=v_w_down, final_g=v_final_g)
    order = list(weights)
    depth = w_in.shape[0]
    seq = x.shape[1]
    chip = 2 * lax.axis_index("x") + lax.axis_index("y")
    core = lax.axis_index("c")
    chip_arr = chip.astype(jnp.int32).reshape(1)
    core_arr = core.astype(jnp.int32).reshape(1)

    small_sharded = dict(conv_dw_k=jnp.pad(conv_dw_k, ((0, 0), (0, CONV_HIST - CONV_TAPS), (0, 0))),
                         ffn_dw_k=jnp.pad(ffn_dw_k, ((0, 0), (0, 8 - FFN_TAPS), (0, 0))),
                         meta_tokens=meta_tokens.reshape(2, N_META // 2, D_MODEL // 4))
    placed = {nm: _place_own(chip_arr, weights[nm], BF16, "place_" + nm) for nm in _BIG}
    g_in, g_out, g_cdk, g_fdk, g_meta = _gather_chips(
        [placed["w_in"], placed["w_out"]] + [_place_own(chip_arr, v, F32, "place_" + nm) for nm, v in small_sharded.items()])
    send_sems, recv_sems, up_buf, down_buf, token = _gather_start([placed["w_up"], placed["w_down"]], g_meta)
    meta_full = g_meta.transpose(1, 2, 0, 3).reshape(N_META, D_MODEL)
    layers = []
    for l in range(depth):
        w_in_l = g_in[:, l].transpose(1, 0, 2).reshape(D_MODEL, IN_COLS)
        w_out_l = g_out[:, l].reshape(D_MODEL, D_MODEL)
        pw = pool_w[l].astype(BF16)
        conv_v = jnp.pad(jnp.stack([conv_dw_b[l], conv_ln_g[l], conv_ln_b[l], pool_scale[l]]), ((0, 4), (0, 0)))
        layers.append(dict(
            w_in=w_in_l, w_in_t=w_in_l.T, conv_k=g_cdk[:, l].transpose(1, 0, 2).reshape(CONV_HIST, CONV_W), conv_v=conv_v,
            pool_w=pw, pool_w_t=pw.transpose(0, 2, 1), w_out=w_out_l, w_out_t=w_out_l.T, kf=g_fdk[:, l],
            g1=norm1_g[l][None], g2=norm2_g[l][None]))
    layers[0]["g1"] = layers[0]["g1"] + token[0, 0]

    def ffn_weights(after):
        g_up, g_down = _gather_forward(_gather_wait(send_sems, recv_sems, [up_buf, down_buf], after))
        per_layer = []
        for l in range(depth):
            w_down_l = g_down[:, l].reshape(2, FF_CHUNK, D_MODEL)
            per_layer.append(dict(w_up_t=g_up[:, l].transpose(0, 2, 1), w_down=w_down_l, w_down_t=w_down_l.transpose(0, 2, 1)))
        return g_up, per_layer

    loss_cols, dx, dmeta, grads, dfinal_g = _local_step(x[0], meta_full, loss_target[0], layers, ffn_weights, final_g[None],
                                                         ROW_TILE, FFN_ROW_TILE)
    grad_x = dx[None]

    keys = ("dw_in", "dw_out", "dw_up", "dw_down")
    g0s = [grads[0][kk] for kk in keys]
    g1s = [grads[1][kk] for kk in keys]
    theirs = _pair_exchange(g0s, g1s)
    parts = [_pair_sum_bf16(core_arr, g0s[a], g1s[a], theirs[a], "pair_sum_" + _BIG[a]) for a in range(4)]
    received = _chip_exchange(parts)
    reds = [_chip_sum(chip_arr, parts[a], received[a], "chip_sum_" + _BIG[a]) for a in range(4)]
    grad, delta, new_m, new_v = {}, {}, {}, {}
    for a, other in enumerate(_pair_share(reds)):
        nm = _BIG[a]
        grad[nm], delta[nm], new_m[nm], new_v[nm] = _adamw_pair(core_arr, weights[nm], reds[a], other, mom1[nm], mom2[nm],
                                                                "adamw_" + nm)

    pack = jnp.concatenate([
        _rows8(jnp.concatenate([grads[l]["dg1"] for l in range(depth)])),
        _rows8(jnp.concatenate([grads[l]["dg2"] for l in range(depth)])),
        _rows8(jnp.concatenate([dfinal_g, loss_cols])),
        jnp.concatenate([grads[l]["dsmall"] for l in range(depth)], axis=1),
        jnp.stack([grads[l]["dpw"] for l in range(depth)]).reshape(-1, D_MODEL),
        dmeta,
        jnp.concatenate([grads[l]["dk"] for l in range(depth)], axis=1),
        jnp.stack([grads[l]["dkf"] for l in range(depth)]).reshape(-1, D_MODEL),
    ])
    red = _all_reduce_small(pack)
    o = 0
    grad["norm1_g"] = red[o:o + depth]
    o += 8
    grad["norm2_g"] = red[o:o + depth]
    o += 8
    grad["final_g"] = red[o]
    loss = red[o + 1, 0]
    o += 8
    sm = red[o:o + 8].reshape(8, depth, CONV_W)
    grad["conv_dw_b"], grad["conv_ln_g"], grad["conv_ln_b"], grad["pool_scale"] = sm[0], sm[1], sm[2], sm[3]
    o += 8
    n_pw = depth * 4 * POOL_G * POOL_G // D_MODEL
    grad["pool_w"] = red[o:o + n_pw].reshape(pool_w.shape)
    o += n_pw
    grad["meta_tokens"] = lax.dynamic_slice_in_dim(red[o:o + N_META], chip * (D_MODEL // 4), D_MODEL // 4, axis=1)
    o += N_META
    dk_all = red[o:o + CONV_HIST].reshape(CONV_HIST, depth, 4, CONV_W // 4)
    grad["conv_dw_k"] = lax.dynamic_index_in_dim(dk_all, chip, axis=2, keepdims=False)[:CONV_TAPS].transpose(1, 0, 2)
    o += CONV_HIST
    dkf_all = red[o:].reshape(depth, 4, 8, FF_CHUNK)
    grad["ffn_dw_k"] = lax.dynamic_index_in_dim(dkf_all, chip, axis=1, keepdims=False)[:, :FFN_TAPS]

    small_rows = -(-sum(weights[nm].size for nm in _SMALL) // (8 * D_MODEL)) * 8
    packed = [_pack_flat([d[nm] for nm in _SMALL], small_rows) for d in (weights, grad, mom1, mom2)]
    for res, packed_out in zip((delta, new_m, new_v), _adamw(*packed, "adamw_small")):
        for nm, val in zip(_SMALL, _unpack_flat(packed_out, [weights[nm] for nm in _SMALL])):
            res[nm] = val

    return (loss, grad_x, *[grad[nm] for nm in order], *[delta[nm] for nm in order],
            *[new_m[nm] for nm in order], *[new_v[nm] for nm in order])
```

```python
import math

import jax
import jax.numpy as jnp
from jax import lax
from jax.experimental import pallas as pl
from jax.experimental.pallas import tpu as pltpu

F32 = jnp.float32
BF16 = jnp.bfloat16

D_MODEL = 1024
CONV_W = 512
POOL_W = 512
POOL_G = 128
POOL_WINDOWS = (2, 4, 8, 16)
IN_COLS = 1536
D_FF = 2816
FF_CHUNK = 1408
CONV_TAPS = 31
CONV_HIST = 32
POOL_HIST = 16
FFN_TAPS = 3
N_META = 16
EPS = 1e-6

ADAM_LR = 0.001
ADAM_B1 = 0.9
ADAM_B2 = 0.999
ADAM_EPS = 1e-08
ADAM_WD = 0.01
ADAM_STEP = 10

ROW_TILE = 256
FFN_ROW_TILE = 256
CONV_ROW_BLOCK = 64
FFN_ROW_BLOCK = 32
V7X_VMEM_LIMIT = 56 * 1024 * 1024

MESH = pl.DeviceIdType.MESH


def _cparams(n_axes, vmem=None):
    return pltpu.CompilerParams(dimension_semantics=("arbitrary",) * n_axes, vmem_limit_bytes=vmem)


def _whole(shape, single=False):
    zeros = (0,) * len(shape)
    if single:
        return pl.BlockSpec(shape, lambda *_: zeros, pipeline_mode=pl.Buffered(1))
    return pl.BlockSpec(shape, lambda *_: zeros)


def _sigmoid(x):
    return 0.5 * jnp.tanh(0.5 * x) + 0.5


def _dot(a, b):
    return jnp.dot(a, b, preferred_element_type=F32)


def _dot_tn(a, b):
    return lax.dot_general(a, b, (((0,), (0,)), ((), ())), preferred_element_type=F32)


def _split_dot(v, a_ref):
    hi = v.astype(BF16)
    lo = (v - hi.astype(F32)).astype(BF16)
    return _dot(hi, a_ref[...]) + _dot(lo, a_ref[...])


def _rms(x):
    r = lax.rsqrt(jnp.mean(x * x, axis=-1, keepdims=True) + EPS)
    return x * r, r


def _rms_bwd(dy, xhat, r, g):
    gd = dy * g
    return r * (gd - xhat * jnp.mean(gd * xhat, axis=-1, keepdims=True)), dy * xhat


def _colsum(v):
    return jnp.sum(v, axis=0, keepdims=True)


def _shifted(window, s):
    return window if s == 0 else pltpu.roll(window, window.shape[0] - s, 0)


def _conv3(window, kf_ref, cc, ls, hist):
    x2 = window[hist:]
    x1 = pltpu.roll(window, 1, 0)[hist:]
    x0 = pltpu.roll(window, 2, 0)[hist:]
    return x0, x1, x2, kf_ref[cc, 0:1, ls] * x0 + kf_ref[cc, 1:2, ls] * x1 + kf_ref[cc, 2:3, ls] * x2


def _for_row_blocks(n, rb, fn, unroll):
    if unroll:
        for r in range(n):
            fn(r * rb)
    else:
        def step(r, keep):
            fn(pl.multiple_of(r * rb, rb))
            return keep

        lax.fori_loop(0, n, step, 0)


def _fold8(v):
    part = v[0:8]
    for k in range(1, v.shape[0] // 8):
        part = part + v[8 * k:8 * k + 8]
    return part


def _inv_count(tile, tm, w):
    t = tile * tm + lax.broadcasted_iota(jnp.int32, (tm, POOL_G), 0)
    return 1.0 / jnp.minimum(t + 1, w).astype(F32)


def _embed(x, meta, g, rows, tm):
    seq = x.shape[0]
    x_tiles = -(-seq // tm)

    def body(xprev_ref, x_ref, meta_ref, g_ref, h_ref, hn_ref):
        i = pl.program_id(0)
        t = i * tm + lax.broadcasted_iota(jnp.int32, (tm, 1), 0)
        head = jnp.where(i == 0, meta_ref[...], xprev_ref[...])
        h = jnp.concatenate([head, x_ref[0:tm - N_META, :]], axis=0)
        h = jnp.where(t < N_META + seq, h, 0.0)
        h_ref[...] = h
        xhat, _ = _rms(h)
        hn_ref[...] = (xhat * g_ref[...]).astype(BF16)

    tile = pl.BlockSpec((tm, D_MODEL), lambda i: (i, 0))
    x_prev = pl.BlockSpec((N_META, D_MODEL), lambda i: (jnp.maximum(i * (tm // N_META) - 1, 0), 0))
    x_own = pl.BlockSpec((tm, D_MODEL), lambda i: (jnp.minimum(i, x_tiles - 1), 0))
    return pl.pallas_call(
        body, name="embed", grid=(rows // tm,),
        in_specs=[x_prev, x_own, _whole((N_META, D_MODEL)), _whole((1, D_MODEL))],
        out_specs=[tile, tile],
        out_shape=[jax.ShapeDtypeStruct((rows, D_MODEL), F32), jax.ShapeDtypeStruct((rows, D_MODEL), BF16)],
        compiler_params=_cparams(1),
    )(x, x, meta, g)


def _row_block(r):
    for cand in (256, 176, 128, 64, 32, 16):
        if r % cand == 0:
            return cand
    return r


def _place_own(chip, w, dtype, name):
    n, r, c = w.shape
    rb = _row_block(r)

    def body(chip_ref, w_ref, o_ref):
        o_ref[0] = w_ref[...].astype(dtype)

    return pl.pallas_call(
        body, name=name,
        grid_spec=pltpu.PrefetchScalarGridSpec(
            num_scalar_prefetch=1, grid=(n, r // rb),
            in_specs=[pl.BlockSpec((1, rb, c), lambda i, j, chip_ref: (i, j, 0))],
            out_specs=pl.BlockSpec((1, 1, rb, c), lambda i, j, chip_ref: (chip_ref[0], i, j, 0))),
        out_shape=jax.ShapeDtypeStruct((4,) + w.shape, dtype), compiler_params=_cparams(2),
    )(chip, w)


def _pair_sum_bf16(core, g0, g1, other, name):
    n, r, c = g0.shape
    rb = _row_block(r)

    def body(core_ref, g0_ref, g1_ref, o_ref, out_ref):
        mine = jnp.where(core_ref[0] == 0, g0_ref[...], g1_ref[...])
        out_ref[...] = (mine + o_ref[...]).astype(BF16)

    spec = pl.BlockSpec((1, rb, c), lambda i, j: (i, j, 0))
    return pl.pallas_call(
        body, name=name, grid=(n, r // rb),
        in_specs=[pl.BlockSpec(memory_space=pltpu.SMEM), spec, spec, spec], out_specs=spec,
        out_shape=jax.ShapeDtypeStruct(g0.shape, BF16), compiler_params=_cparams(2),
    )(core, g0, g1, other)


def _chip_sum(chip, parts, recv, name):
    _, r, c = parts.shape
    rb = _row_block(r)

    def body(chip_ref, p_ref, r_ref, out_ref):
        got = r_ref[...].astype(F32)
        out_ref[...] = (p_ref[0].astype(F32) + got[0]) + (got[1] + got[2])

    return pl.pallas_call(
        body, name=name,
        grid_spec=pltpu.PrefetchScalarGridSpec(
            num_scalar_prefetch=1, grid=(r // rb,),
            in_specs=[pl.BlockSpec((1, rb, c), lambda j, chip_ref: (chip_ref[0], j, 0)),
                      pl.BlockSpec((3, rb, c), lambda j, chip_ref: (0, j, 0))],
            out_specs=pl.BlockSpec((rb, c), lambda j, chip_ref: (j, 0))),
        out_shape=jax.ShapeDtypeStruct((r, c), F32), compiler_params=_cparams(1),
    )(chip, parts, recv)


def _adamw_update(w, g, m, v):
    nm = ADAM_B1 * m + (1.0 - ADAM_B1) * g
    nv = ADAM_B2 * v + (1.0 - ADAM_B2) * (g * g)
    m_hat = nm / (1.0 - ADAM_B1 ** ADAM_STEP)
    v_hat = nv / (1.0 - ADAM_B2 ** ADAM_STEP)
    return -ADAM_LR * (m_hat / (jnp.sqrt(v_hat) + ADAM_EPS) + ADAM_WD * w), nm, nv


def _adamw(w, g, m, v, name):
    n, r, c = w.shape
    rb = _row_block(r)

    def body(w_ref, g_ref, m_ref, v_ref, d_ref, nm_ref, nv_ref):
        d_ref[...], nm_ref[...], nv_ref[...] = _adamw_update(w_ref[...], g_ref[...], m_ref[...], v_ref[...])

    spec = pl.BlockSpec((1, rb, c), lambda i, j: (i, j, 0))
    shp = jax.ShapeDtypeStruct(w.shape, F32)
    return pl.pallas_call(
        body, name=name, grid=(n, r // rb), in_specs=[spec] * 4, out_specs=[spec] * 3,
        out_shape=[shp] * 3, compiler_params=_cparams(2),
    )(w, g, m, v)


def _adamw_pair(core, w, mine, theirs, m, v, name):
    n, r, c = w.shape
    rb = _row_block(r)

    def body(core_ref, w_ref, a_ref, b_ref, m_ref, v_ref, g_ref, d_ref, nm_ref, nv_ref):
        g = jnp.where(pl.program_id(0) == core_ref[0], a_ref[...], b_ref[...])
        g_ref[0] = g
        d_ref[0], nm_ref[0], nv_ref[0] = _adamw_update(w_ref[0], g, m_ref[0], v_ref[0])

    spec = pl.BlockSpec((1, rb, c), lambda i, j, core_ref: (i, j, 0))
    flat = pl.BlockSpec((rb, c), lambda i, j, core_ref: (j, 0))
    shp = jax.ShapeDtypeStruct(w.shape, F32)
    return pl.pallas_call(
        body, name=name,
        grid_spec=pltpu.PrefetchScalarGridSpec(num_scalar_prefetch=1, grid=(n, r // rb),
                                               in_specs=[spec, flat, flat, spec, spec], out_specs=[spec] * 4),
        out_shape=[shp] * 4, compiler_params=_cparams(2),
    )(core, w, mine, theirs, m, v)


def _mixer_fwd(h0, hn, w_in, conv_k, conv_v, avg, pool_w, w_out, g2, tm):
    rows = h0.shape[0]
    rb = CONV_ROW_BLOCK

    def body(h0_ref, hn_ref, win_ref, ck_ref, cv_ref, avg_ref, pw_ref, wout_ref, g2_ref,
             z_ref, u1_ref, m_ref, h1_ref, hn2_ref, ubuf, pbuf):
        i = pl.program_id(0)

        @pl.when(i == 0)
        def _():
            ubuf[pl.ds(0, CONV_HIST), :] = jnp.zeros((CONV_HIST, CONV_W), F32)
            pbuf[pl.ds(0, POOL_HIST), :] = jnp.zeros((POOL_HIST, POOL_W), F32)

        z = _dot(hn_ref[...], win_ref[...])
        z_ref[...] = z
        ubuf[pl.ds(CONV_HIST, tm), :] = z[:, :CONV_W] * _sigmoid(z[:, CONV_W:2 * CONV_W])
        p = z[:, 2 * CONV_W:]
        pbuf[pl.ds(POOL_HIST, tm), :] = p

        def conv_block(r0):
            for l in range(CONV_W // 128):
                ls = pl.ds(128 * l, 128)
                window = ubuf[pl.ds(r0, rb + CONV_HIST), ls]
                acc = jnp.broadcast_to(cv_ref[0:1, ls], (rb, 128))
                for s in range(8):
                    ws = _shifted(window, s)
                    for q in range(CONV_HIST // 8 + 1):
                        j = 8 * q + s - 2
                        if 0 <= j < CONV_TAPS:
                            acc = acc + ck_ref[j:j + 1, ls] * ws[8 * q:8 * q + rb]
                u1_ref[pl.ds(r0, rb), ls] = acc

        _for_row_blocks(tm // rb, rb, conv_block, unroll=True)
        ubuf[pl.ds(0, CONV_HIST), :] = ubuf[pl.ds(tm, CONV_HIST), :]

        u1 = u1_ref[...]
        cen = u1 - _split_dot(u1, avg_ref)
        xhat = cen * lax.rsqrt(_split_dot(cen * cen, avg_ref) + EPS)
        u2 = xhat * cv_ref[1:2, :] + cv_ref[2:3, :]
        m_ref[:, 0:CONV_W] = (u2 * _sigmoid(u2)).astype(BF16)

        for gi, w in enumerate(POOL_WINDOWS):
            ls = pl.ds(POOL_G * gi, POOL_G)
            s = pbuf[pl.ds(POOL_HIST, tm), ls]
            for j in range(1, w):
                s = s + pbuf[pl.ds(POOL_HIST - j, tm), ls]
            d = s * _inv_count(i, tm, w) - p[:, POOL_G * gi:POOL_G * (gi + 1)]
            y = _dot(d.astype(BF16), pw_ref[gi]) * cv_ref[3:4, ls]
            m_ref[:, pl.ds(CONV_W + POOL_G * gi, POOL_G)] = y.astype(BF16)
        pbuf[pl.ds(0, POOL_HIST), :] = pbuf[pl.ds(tm, POOL_HIST), :]

        h1 = h0_ref[...] + _dot(m_ref[...], wout_ref[...])
        h1_ref[...] = h1
        xh, _ = _rms(h1)
        hn2_ref[...] = (xh * g2_ref[...]).astype(BF16)

    def tile(c):
        return pl.BlockSpec((tm, c), lambda i: (i, 0))

    return pl.pallas_call(
        body, name="mixer_fwd", grid=(rows // tm,),
        in_specs=[tile(D_MODEL), tile(D_MODEL), _whole((D_MODEL, IN_COLS)), _whole((CONV_HIST, CONV_W)),
                  _whole((8, CONV_W)), _whole((CONV_W, CONV_W)), _whole((4, POOL_G, POOL_G)),
                  _whole((D_MODEL, D_MODEL)), _whole((1, D_MODEL))],
        out_specs=[tile(IN_COLS), tile(CONV_W), tile(D_MODEL), tile(D_MODEL), tile(D_MODEL)],
        out_shape=[jax.ShapeDtypeStruct((rows, IN_COLS), F32), jax.ShapeDtypeStruct((rows, CONV_W), F32),
                   jax.ShapeDtypeStruct((rows, D_MODEL), BF16), jax.ShapeDtypeStruct((rows, D_MODEL), F32),
                   jax.ShapeDtypeStruct((rows, D_MODEL), BF16)],
        scratch_shapes=[pltpu.VMEM((CONV_HIST + tm, CONV_W), F32), pltpu.VMEM((POOL_HIST + tm, POOL_W), F32)],
        compiler_params=_cparams(1, V7X_VMEM_LIMIT),
    )(h0, hn, w_in, conv_k, conv_v, avg, pool_w, w_out, g2)


def _ffn_fwd(h1, hn2, w_up, layer, kf, w_down, g_next, tm):
    rows = h1.shape[0]
    hist = 8
    w_up_spec = pl.BlockSpec((4, None, D_MODEL, FF_CHUNK), lambda i: (0, layer, 0, 0), pipeline_mode=pl.Buffered(1))

    rb = FFN_ROW_BLOCK

    def body(h1_ref, hn2_ref, wup_ref, kf_ref, wdn_ref, gn_ref, ug_ref, h2_ref, hnn_ref, wg, wv, carry, act_s, acc):
        i = pl.program_id(0)

        @pl.when(i == 0)
        def _():
            carry[...] = jnp.zeros(carry.shape, F32)

        acc[...] = h1_ref[...]
        for c in range(2):
            for buf, cc in ((wg, c), (wv, c + 2)):
                ug = _dot(hn2_ref[...], wup_ref[cc])
                ug_ref[cc] = ug.astype(BF16)
                buf[pl.ds(0, hist), :] = carry[cc]
                buf[pl.ds(hist, tm), :] = ug
                carry[cc] = buf[pl.ds(tm, hist), :]

            def act_block(r0):
                for l in range(FF_CHUNK // 128):
                    ls = pl.ds(128 * l, 128)
                    gate = _conv3(wg[pl.ds(r0, rb + hist), ls], kf_ref, c, ls, hist)[3]
                    val = _conv3(wv[pl.ds(r0, rb + hist), ls], kf_ref, c + 2, ls, hist)[3]
                    act_s[pl.ds(r0, rb), ls] = (gate * _sigmoid(gate) * val).astype(BF16)

            _for_row_blocks(tm // rb, rb, act_block, unroll=True)
            acc[...] += _dot(act_s[...], wdn_ref[c])
        h2 = acc[...]
        h2_ref[...] = h2
        xh, _ = _rms(h2)
        hnn_ref[...] = (xh * gn_ref[...]).astype(BF16)

    def tile(c):
        return pl.BlockSpec((tm, c), lambda i: (i, 0))

    return pl.pallas_call(
        body, name="ffn_fwd", grid=(rows // tm,),
        in_specs=[tile(D_MODEL), tile(D_MODEL), w_up_spec, _whole((4, 8, FF_CHUNK)),
                  _whole((2, FF_CHUNK, D_MODEL), single=True), _whole((1, D_MODEL))],
        out_specs=[pl.BlockSpec((4, tm, FF_CHUNK), lambda i: (0, i, 0)), tile(D_MODEL), tile(D_MODEL)],
        out_shape=[jax.ShapeDtypeStruct((4, rows, FF_CHUNK), BF16), jax.ShapeDtypeStruct((rows, D_MODEL), F32),
                   jax.ShapeDtypeStruct((rows, D_MODEL), BF16)],
        scratch_shapes=[pltpu.VMEM((hist + tm, FF_CHUNK), F32), pltpu.VMEM((hist + tm, FF_CHUNK), F32),
                        pltpu.VMEM((4, hist, FF_CHUNK), F32), pltpu.VMEM((tm, FF_CHUNK), BF16),
                        pltpu.VMEM((tm, D_MODEL), F32)],
        compiler_params=_cparams(1, V7X_VMEM_LIMIT),
    )(h1, hn2, w_up, kf, w_down, g_next)


def _loss_head(h, tgt, g, seq, tm):
    rows = h.shape[0]
    tgt_tiles = -(-seq // tm)

    def body(h_ref, tprev_ref, t_ref, g_ref, dh_ref, loss_ref, dg_ref):
        i = pl.program_id(0)

        @pl.when(i == 0)
        def _():
            loss_ref[...] = jnp.zeros(loss_ref.shape, F32)
            dg_ref[...] = jnp.zeros(dg_ref.shape, F32)

        t = i * tm + lax.broadcasted_iota(jnp.int32, (tm, 1), 0)
        inside = jnp.logical_and(t >= N_META, t < N_META + seq)
        tgt = jnp.concatenate([tprev_ref[...], t_ref[0:tm - N_META, :]], axis=0)
        xhat, r = _rms(h_ref[...])
        err = jnp.where(inside, xhat * g_ref[...] - tgt, 0.0)
        loss_ref[...] += _colsum(err * err)
        dh, dg_rows = _rms_bwd(err * (1.0 / D_MODEL), xhat, r, g_ref[...])
        dh_ref[...] = dh
        dg_ref[...] += _colsum(dg_rows)

        @pl.when(i == rows // tm - 1)
        def _():
            total = jnp.sum(loss_ref[...], axis=1, keepdims=True) * (0.5 / D_MODEL)
            loss_ref[...] = jnp.broadcast_to(total, loss_ref.shape)

    tile = pl.BlockSpec((tm, D_MODEL), lambda i: (i, 0))
    t_prev = pl.BlockSpec((N_META, D_MODEL), lambda i: (jnp.maximum(i * (tm // N_META) - 1, 0), 0))
    t_own = pl.BlockSpec((tm, D_MODEL), lambda i: (jnp.minimum(i, tgt_tiles - 1), 0))
    vec = jax.ShapeDtypeStruct((1, D_MODEL), F32)
    return pl.pallas_call(
        body, name="loss_head", grid=(rows // tm,),
        in_specs=[tile, t_prev, t_own, _whole((1, D_MODEL))],
        out_specs=[tile, _whole((1, D_MODEL)), _whole((1, D_MODEL))],
        out_shape=[jax.ShapeDtypeStruct((rows, D_MODEL), F32), vec, vec],
        compiler_params=_cparams(1),
    )(h, tgt, tgt, g)


def _ffn_bwd(dh2, hn2, ug0, w_down_t, w_up_t, kf, tm):
    rows = dh2.shape[0]
    nt = rows // tm
    hist = 16
    fut = 8
    rb = FFN_ROW_BLOCK
    near = 8

    def body(dh2_ref, hn2_ref, ugg_ref, ugv_ref, hg_ref, hv_ref, wdt_ref, wutg_ref, wutv_ref, kf_ref,
             dhn_ref, dkf_ref, dwup_ref, dwdn_ref, wg, wv, dgb, dvb, carry, dkacc, act_s, dug_s, acc_up, acc_dn):
        c = pl.program_id(0)
        i = pl.program_id(1)
        first_tile = jnp.where(i == nt - 1, 1.0, 0.0)

        @pl.when(i == 0)
        def _():
            carry[...] = jnp.zeros(carry.shape, F32)
            dkacc[...] = jnp.zeros(dkacc.shape, F32)
            acc_up[...] = jnp.zeros(acc_up.shape, F32)
            acc_dn[...] = jnp.zeros(acc_dn.shape, F32)

        def run():
            n_blocks = tm // rb
            dh2b = dh2_ref[...].astype(BF16)
            sides = ((wg, dgb, ugg_ref, hg_ref, 0, c), (wv, dvb, ugv_ref, hv_ref, 1, c + 2))
            for buf, dbuf, u_ref, h_ref, s, cc in sides:
                buf[pl.ds(0, hist), :] = h_ref[0].astype(F32) * (1.0 - first_tile)
                buf[pl.ds(hist, tm), :] = u_ref[0].astype(F32)
                dbuf[pl.ds(tm, fut), :] = carry[s]
            dgb[pl.ds(0, tm), :] = _dot(dh2b, wdt_ref[0])

            def grad_block(r0):
                for l in range(FF_CHUNK // 128):
                    ls = pl.ds(128 * l, 128)
                    g0, g1, g2, gate = _conv3(wg[pl.ds(r0 + hist - near, rb + near), ls], kf_ref, c, ls, near)
                    v0, v1, v2, val = _conv3(wv[pl.ds(r0 + hist - near, rb + near), ls], kf_ref, c + 2, ls, near)
                    sg = _sigmoid(gate)
                    silu = gate * sg
                    act_s[pl.ds(r0, rb), ls] = (silu * val).astype(BF16)
                    dact = dgb[pl.ds(r0, rb), ls]
                    dgate = dact * val * (sg * (1.0 + gate * (1.0 - sg)))
                    dval = dact * silu
                    dgb[pl.ds(r0, rb), ls] = dgate
                    dvb[pl.ds(r0, rb), ls] = dval
                    for s, dv, taps in ((0, dgate, (g0, g1, g2)), (1, dval, (v0, v1, v2))):
                        for j in range(FFN_TAPS):
                            dkacc[s, pl.ds(8 * j, 8), ls] += _fold8(dv * taps[j])

            _for_row_blocks(n_blocks, rb, grad_block, unroll=True)
            for buf, dbuf, u_ref, h_ref, s, cc in sides:
                carry[s] = dbuf[pl.ds(0, fut), :]

            def conv_block(r0):
                for l in range(FF_CHUNK // 128):
                    ls = pl.ds(128 * l, 128)
                    for buf, dbuf, u_ref, h_ref, s, cc in sides:
                        window = dbuf[pl.ds(r0, rb + fut), ls]
                        dug0 = (kf_ref[cc, 0:1, ls] * _shifted(window, 2)[0:rb] + kf_ref[cc, 1:2, ls] * _shifted(window, 1)[0:rb]
                                + kf_ref[cc, 2:3, ls] * window[0:rb])
                        dug_s[s, pl.ds(r0, rb), ls] = dug0.astype(BF16)

            _for_row_blocks(n_blocks, rb, conv_block, unroll=True)
            dhn_ref[0] = _dot(dug_s[0], wutg_ref[0]) + _dot(dug_s[1], wutv_ref[0])
            acc_up[0] += _dot_tn(hn2_ref[...], dug_s[0])
            acc_up[1] += _dot_tn(hn2_ref[...], dug_s[1])
            acc_dn[...] += _dot_tn(act_s[...], dh2b)

        run()

        @pl.when(i == nt - 1)
        def _():
            for s in range(2):
                for j in range(FFN_TAPS):
                    dkf_ref[c + 2 * s, j:j + 1, :] = _colsum(dkacc[s, pl.ds(8 * j, 8), :])
                dkf_ref[c + 2 * s, FFN_TAPS:8, :] = jnp.zeros((8 - FFN_TAPS, FF_CHUNK), F32)
            pltpu.sync_copy(acc_up, dwup_ref.at[c])
            pltpu.sync_copy(acc_dn, dwdn_ref.at[c])

    def tile(cols):
        return pl.BlockSpec((tm, cols), lambda c, i: (nt - 1 - i, 0))

    def chunk(off, r, halo_rows=None):
        if halo_rows is None:
            return pl.BlockSpec((1, r, FF_CHUNK), lambda c, i: (c + off, nt - 1 - i, 0))
        return pl.BlockSpec((1, r, FF_CHUNK), lambda c, i: (c + off, jnp.maximum((nt - 1 - i) * (tm // r) - 1, 0), 0))

    def weight(off, r, cols):
        return pl.BlockSpec((1, r, cols), lambda c, i: (c + off, 0, 0), pipeline_mode=pl.Buffered(1))

    return pl.pallas_call(
        body, name="ffn_bwd", grid=(2, nt),
        in_specs=[tile(D_MODEL), tile(D_MODEL), chunk(0, tm), chunk(2, tm), chunk(0, hist, True), chunk(2, hist, True),
                  weight(0, D_MODEL, FF_CHUNK), weight(0, FF_CHUNK, D_MODEL), weight(2, FF_CHUNK, D_MODEL),
                  _whole((4, 8, FF_CHUNK))],
        out_specs=[pl.BlockSpec((1, tm, D_MODEL), lambda c, i: (c, nt - 1 - i, 0)), _whole((4, 8, FF_CHUNK)), _ANY, _ANY],
        out_shape=[jax.ShapeDtypeStruct((2, rows, D_MODEL), F32), jax.ShapeDtypeStruct((4, 8, FF_CHUNK), F32),
                   jax.ShapeDtypeStruct((2, 2, D_MODEL, FF_CHUNK), F32), jax.ShapeDtypeStruct((2, FF_CHUNK, D_MODEL), F32)],
        scratch_shapes=[pltpu.VMEM((hist + tm, FF_CHUNK), F32), pltpu.VMEM((hist + tm, FF_CHUNK), F32),
                        pltpu.VMEM((tm + fut, FF_CHUNK), F32), pltpu.VMEM((tm + fut, FF_CHUNK), F32),
                        pltpu.VMEM((2, fut, FF_CHUNK), F32), pltpu.VMEM((2, 8 * FFN_TAPS, FF_CHUNK), F32),
                        pltpu.VMEM((tm, FF_CHUNK), BF16), pltpu.VMEM((2, tm, FF_CHUNK), BF16),
                        pltpu.VMEM((2, D_MODEL, FF_CHUNK), F32), pltpu.VMEM((FF_CHUNK, D_MODEL), F32)],
        compiler_params=_cparams(2, V7X_VMEM_LIMIT),
    )(dh2, hn2, ug0, ug0, ug0, ug0, w_down_t, w_up_t, w_up_t, kf)


def _mixer_bwd(dh2, dhn2, h1, g2, h0, z, u1, hn1, m, w_out_t, w_in_t, conv_k, conv_v, avg, pool_w, pool_w_t, g1, tm,
               x_rows=None):
    rows = dh2.shape[0]
    nt = rows // tm
    rb = CONV_ROW_BLOCK
    in_shard = IN_COLS // 4

    def body(dh2_ref, dhn2_ref, h1_ref, g2_ref, h0_ref, z_ref, zh_ref, u1_ref, hn1_ref, m_ref, wot_ref, wit_ref, ck_ref,
             cv_ref, avg_ref, pw_ref, pwt_ref, g1_ref,
             dh0_ref, dtop_ref, dk_ref, ds_ref, dpw_ref, dg1_ref, dg2_ref, dwo_ref, dwi_ref,
             ubuf, dbuf, pbuf, ebuf, dcarry, ecarry, dkacc, dzs, dz_s, dh1_s, acc_out, acc_in, xcarry):
        i = pl.program_id(0)
        ti = nt - 1 - i
        has_past = jnp.where(ti > 0, 1.0, 0.0)

        @pl.when(i == 0)
        def _():
            dcarry[...] = jnp.zeros(dcarry.shape, F32)
            ecarry[...] = jnp.zeros(ecarry.shape, F32)
            dkacc[...] = jnp.zeros(dkacc.shape, F32)
            ds_ref[...] = jnp.zeros(ds_ref.shape, F32)
            dpw_ref[...] = jnp.zeros(dpw_ref.shape, F32)
            dg1_ref[...] = jnp.zeros(dg1_ref.shape, F32)
            dg2_ref[...] = jnp.zeros(dg2_ref.shape, F32)
            acc_out[...] = jnp.zeros(acc_out.shape, F32)
            acc_in[...] = jnp.zeros(acc_in.shape, F32)
            xcarry[...] = jnp.zeros(xcarry.shape, F32)

        xh1, r1 = _rms(h1_ref[...])
        dx1, dg2_rows = _rms_bwd(dhn2_ref[0] + dhn2_ref[1], xh1, r1, g2_ref[...])
        dh1_s[...] = dh2_ref[...] + dx1
        dg2_ref[...] += _colsum(dg2_rows)
        dh1b = dh1_s[...].astype(BF16)
        acc_out[...] += _dot_tn(m_ref[...], dh1b)
        dm = _dot(dh1b, wot_ref[...])
        z = z_ref[...]
        a = z[:, :CONV_W]
        sg = _sigmoid(z[:, CONV_W:2 * CONV_W])
        p = z[:, 2 * CONV_W:]
        zh = zh_ref[...] * has_past
        ubuf[pl.ds(0, CONV_HIST), :] = zh[:, :CONV_W] * _sigmoid(zh[:, CONV_W:2 * CONV_W])
        ubuf[pl.ds(CONV_HIST, tm), :] = a * sg
        pbuf[pl.ds(0, POOL_HIST), :] = zh[CONV_HIST - POOL_HIST:, 2 * CONV_W:]
        pbuf[pl.ds(POOL_HIST, tm), :] = p

        u1 = u1_ref[...]
        cen = u1 - _split_dot(u1, avg_ref)
        rstd = lax.rsqrt(_split_dot(cen * cen, avg_ref) + EPS)
        xhat = cen * rstd
        u2 = xhat * cv_ref[1:2, :] + cv_ref[2:3, :]
        s2 = _sigmoid(u2)
        du2 = dm[:, :CONV_W] * (s2 * (1.0 + u2 * (1.0 - s2)))
        ds_ref[1:2, :] += _colsum(du2 * xhat)
        ds_ref[2:3, :] += _colsum(du2)
        dxh = du2 * cv_ref[1:2, :]
        du1 = rstd * (dxh - _split_dot(dxh, avg_ref) - xhat * _split_dot(dxh * xhat, avg_ref))
        ds_ref[0:1, :] += _colsum(du1)
        dbuf[pl.ds(0, tm), :] = du1
        dbuf[pl.ds(tm, CONV_HIST), :] = dcarry[...]
        dcarry[...] = dbuf[pl.ds(0, CONV_HIST), :]

        def conv_block(r0):
            for l in range(CONV_W // 128):
                ls = pl.ds(128 * l, 128)
                dwin = dbuf[pl.ds(r0, rb + CONV_HIST), ls]
                uwin = ubuf[pl.ds(r0, rb + CONV_HIST), ls]
                dblk = dwin[0:rb]
                du0 = jnp.zeros((rb, 128), F32)
                for s in range(8):
                    ds_ = _shifted(dwin, s)
                    us_ = _shifted(uwin, s)
                    for q in range(CONV_HIST // 8 + 1):
                        o = 8 * q + s
                        if 0 <= CONV_TAPS - 1 - o < CONV_TAPS:
                            j = CONV_TAPS - 1 - o
                            du0 = du0 + ck_ref[j:j + 1, ls] * ds_[8 * q:8 * q + rb]
                        j = o - 2
                        if 0 <= j < CONV_TAPS:
                            prod = dblk * us_[8 * q:8 * q + rb]
                            part = prod[0:8]
                            for v in range(1, rb // 8):
                                part = part + prod[8 * v:8 * v + 8]
                            dkacc[pl.ds(8 * j, 8), ls] += part
                dzs[pl.ds(r0, rb), ls] = du0

        _for_row_blocks(tm // rb, rb, conv_block, unroll=True)
        du0 = dzs[:, 0:CONV_W]
        dz_s[:, 0:CONV_W] = (du0 * sg).astype(BF16)
        dz_s[:, CONV_W:2 * CONV_W] = (du0 * a * sg * (1.0 - sg)).astype(BF16)

        for gi, w in enumerate(POOL_WINDOWS):
            ls = pl.ds(POOL_G * gi, POOL_G)
            cols = slice(CONV_W + POOL_G * gi, CONV_W + POOL_G * (gi + 1))
            inv = _inv_count(ti, tm, w)
            s = pbuf[pl.ds(POOL_HIST, tm), ls]
            for j in range(1, w):
                s = s + pbuf[pl.ds(POOL_HIST - j, tm), ls]
            d = (s * inv - p[:, POOL_G * gi:POOL_G * (gi + 1)]).astype(BF16)
            dyp = dm[:, cols]
            ds_ref[3:4, ls] += _colsum(dyp * _dot(d, pw_ref[gi]))
            dyb = (dyp * cv_ref[3:4, ls]).astype(BF16)
            dpw_ref[gi] += _dot_tn(d, dyb)
            dd = _dot(dyb, pwt_ref[gi])
            ebuf[pl.ds(0, tm), ls] = dd * inv
            ebuf[pl.ds(tm, POOL_HIST), ls] = ecarry[:, ls]
            dp = ebuf[pl.ds(0, tm), ls] - dd
            for j in range(1, w):
                dp = dp + ebuf[pl.ds(j, tm), ls]
            dz_s[:, pl.ds(2 * CONV_W + POOL_G * gi, POOL_G)] = dp.astype(BF16)
        ecarry[...] = ebuf[pl.ds(0, POOL_HIST), :]

        acc_in[...] += _dot_tn(hn1_ref[...], dz_s[...])
        dhn = _dot(dz_s[...], wit_ref[...])
        xh, r = _rms(h0_ref[...])
        dx, dg_rows = _rms_bwd(dhn, xh, r, g1_ref[...])
        dh0 = dh1_s[...] + dx
        if x_rows is None:
            dh0_ref[...] = dh0
        else:
            dh0_ref[0:tm - N_META, :] = dh0[N_META:]
            dh0_ref[tm - N_META:tm, :] = xcarry[...]
            xcarry[...] = dh0[0:N_META]
        dg1_ref[...] += _colsum(dg_rows)

        @pl.when(i == nt - 1)
        def _():
            dtop_ref[...] = dh0[0:N_META]
            for j in range(CONV_TAPS):
                dk_ref[j:j + 1, :] = _colsum(dkacc[pl.ds(8 * j, 8), :])
            dk_ref[CONV_TAPS:CONV_HIST, :] = jnp.zeros((CONV_HIST - CONV_TAPS, CONV_W), F32)
            pltpu.sync_copy(acc_out, dwo_ref)
            for k in range(4):
                pltpu.sync_copy(acc_in.at[:, pl.ds(in_shard * k, in_shard)], dwi_ref.at[k])

    def tile(c):
        return pl.BlockSpec((tm, c), lambda i: (nt - 1 - i, 0))

    halo = pl.BlockSpec((CONV_HIST, IN_COLS), lambda i: (jnp.maximum((nt - 1 - i) * (tm // CONV_HIST) - 1, 0), 0))
    vec = jax.ShapeDtypeStruct((1, D_MODEL), F32)
    if x_rows is None:
        first_spec, first_shape = tile(D_MODEL), jax.ShapeDtypeStruct((rows, D_MODEL), F32)
    else:
        x_tiles = -(-x_rows // tm)
        first_spec = pl.BlockSpec((tm, D_MODEL), lambda i: (jnp.minimum(nt - 1 - i, x_tiles - 1), 0))
        first_shape = jax.ShapeDtypeStruct((x_rows, D_MODEL), F32)
    return pl.pallas_call(
        body, name="mixer_bwd", grid=(nt,),
        in_specs=[tile(D_MODEL), pl.BlockSpec((2, tm, D_MODEL), lambda i: (0, nt - 1 - i, 0)), tile(D_MODEL),
                  _whole((1, D_MODEL)), tile(D_MODEL), tile(IN_COLS), halo, tile(CONV_W), tile(D_MODEL), tile(D_MODEL),
                  _whole((D_MODEL, D_MODEL)), _whole((IN_COLS, D_MODEL)), _whole((CONV_HIST, CONV_W)), _whole((8, CONV_W)),
                  _whole((CONV_W, CONV_W)), _whole((4, POOL_G, POOL_G)), _whole((4, POOL_G, POOL_G)), _whole((1, D_MODEL))],
        out_specs=[first_spec, _whole((N_META, D_MODEL)), _whole((CONV_HIST, CONV_W)), _whole((8, CONV_W)),
                   _whole((4, POOL_G, POOL_G)), _whole((1, D_MODEL)), _whole((1, D_MODEL)), _ANY, _ANY],
        out_shape=[first_shape, jax.ShapeDtypeStruct((N_META, D_MODEL), F32), jax.ShapeDtypeStruct((CONV_HIST, CONV_W), F32),
                   jax.ShapeDtypeStruct((8, CONV_W), F32), jax.ShapeDtypeStruct((4, POOL_G, POOL_G), F32), vec, vec,
                   jax.ShapeDtypeStruct((D_MODEL, D_MODEL), F32), jax.ShapeDtypeStruct((4, D_MODEL, in_shard), F32)],
        scratch_shapes=[pltpu.VMEM((CONV_HIST + tm, CONV_W), F32), pltpu.VMEM((tm + CONV_HIST, CONV_W), F32),
                        pltpu.VMEM((POOL_HIST + tm, POOL_W), F32), pltpu.VMEM((tm + POOL_HIST, POOL_W), F32),
                        pltpu.VMEM((CONV_HIST, CONV_W), F32), pltpu.VMEM((POOL_HIST, POOL_W), F32),
                        pltpu.VMEM((8 * CONV_HIST, CONV_W), F32), pltpu.VMEM((tm, CONV_W), F32),
                        pltpu.VMEM((tm, IN_COLS), BF16), pltpu.VMEM((tm, D_MODEL), F32),
                        pltpu.VMEM((D_MODEL, D_MODEL), F32), pltpu.VMEM((D_MODEL, IN_COLS), F32),
                        pltpu.VMEM((N_META, D_MODEL), F32)],
        compiler_params=_cparams(1, V7X_VMEM_LIMIT),
    )(dh2, dhn2, h1, g2, h0, z, z, u1, hn1, m, w_out_t, w_in_t, conv_k, conv_v, avg, pool_w, pool_w_t, g1)


def _head_average():
    head = lax.broadcasted_iota(jnp.int32, (CONV_W, CONV_W), 0) // 64
    return jnp.where(head == head.T, 1.0 / 64, 0.0).astype(BF16)


def _local_step(x, meta, tgt, layers, ffn_weights, ffn_grads_done, final_g, tm, tm_ffn):
    avg = _head_average()
    depth = len(layers)
    saved = []
    seq = x.shape[0]
    step = math.lcm(tm, tm_ffn)
    rows = -(-(N_META + seq) // step) * step
    h, hn = _embed(x, meta, layers[0]["g1"], rows, tm)
    for l, w in enumerate(layers):
        g_next = layers[l + 1]["g1"] if l + 1 < depth else final_g
        z, u1, m, h1, hn2 = _mixer_fwd(h, hn, w["w_in"], w["conv_k"], w["conv_v"], avg, w["pool_w"], w["w_out"], w["g2"], tm)
        if l == 0:
            w_up_all, ffn = ffn_weights(hn2)
        ug0, h2, hn_next = _ffn_fwd(h1, hn2, w_up_all, l, w["kf"], ffn[l]["w_down"], g_next, tm_ffn)
        saved.append((h, hn, z, u1, m, h1, hn2, ug0))
        h, hn = h2, hn_next
    dh, loss_cols, dfinal_g = _loss_head(h, tgt, final_g, seq, tm)

    grads = [None] * depth
    ffn_grads = [None] * depth
    for l in reversed(range(depth)):
        w = layers[l]
        h0, hn1, z, u1, m, h1, hn2, ug0 = saved[l]
        dhn2, dkf, dw_up, dw_down = _ffn_bwd(dh, hn2, ug0, ffn[l]["w_down_t"], ffn[l]["w_up_t"], w["kf"], tm_ffn)
        ffn_grads[l] = (dw_up.transpose(1, 0, 2, 3).reshape(4, D_MODEL, FF_CHUNK), dw_down.reshape(4, D_FF // 4, D_MODEL))
        g2 = w["g2"] + ffn_grads_done(ffn_grads) if l == 0 else w["g2"]
        dh0, dtop, dk, dsmall, dpw, dg1, dg2, dw_out, dw_in = _mixer_bwd(
            dh, dhn2, h1, g2, h0, z, u1, hn1, m, w["w_out_t"], w["w_in_t"], w["conv_k"], w["conv_v"], avg,
            w["pool_w"], w["pool_w_t"], w["g1"], tm, x_rows=seq if l == 0 else None)
        grads[l] = dict(dw_in=dw_in, dw_out=dw_out.reshape(4, D_MODEL // 4, D_MODEL), dk=dk, dsmall=dsmall, dpw=dpw,
                        dg1=dg1, dg2=dg2, dkf=dkf)
        dh = dh0
    return loss_cols, dh, dtop, grads, dfinal_g


_ANY = pl.BlockSpec(memory_space=pl.ANY)


def _place():
    x, y, c = lax.axis_index("x"), lax.axis_index("y"), lax.axis_index("c")
    chips = [(1 - x, y), (x, 1 - y), (1 - x, 1 - y)]
    return x, y, c, chips


def _remote(src, dst, send_sems, recv_sems, idx, to):
    return pltpu.make_async_remote_copy(src_ref=src, dst_ref=dst, send_sem=send_sems.at[idx], recv_sem=recv_sems.at[idx],
                                        device_id=to, device_id_type=MESH)


def _gather_chips(xs):
    n = len(xs)

    def body(*refs):
        x_refs, o_refs = refs[:n], refs[n:2 * n]
        send_sems, recv_sems = refs[2 * n:]
        x, y, c, chips = _place()
        k = 2 * x + y
        sibling = (x, y, 1 - c)
        sends = []
        for j, chip in enumerate(chips):
            for a in range(n):
                sends.append(_remote(x_refs[a].at[k, c], o_refs[a].at[k, c], send_sems, recv_sems, 3 * a + j, (*chip, c)))
                sends[-1].start()
        for j, chip in enumerate(chips):
            kj = 2 * chip[0] + chip[1]
            for a in range(n):
                landed = o_refs[a].at[kj, c]
                _remote(landed, landed, send_sems, recv_sems, 3 * a + j, sibling).wait_recv()
                sends.append(_remote(landed, landed, send_sems, recv_sems, 3 * n + 3 * a + j, sibling))
                sends[-1].start()
        for j, chip in enumerate(chips):
            kj = 2 * chip[0] + chip[1]
            for a in range(n):
                passed = o_refs[a].at[kj, 1 - c]
                _remote(passed, passed, send_sems, recv_sems, 3 * n + 3 * a + j, sibling).wait_recv()
        for cp in sends:
            cp.wait_send()

    return pl.pallas_call(
        body, name="gather_chips", in_specs=[_ANY] * n, out_specs=[_ANY] * n,
        out_shape=[jax.ShapeDtypeStruct(v.shape, v.dtype) for v in xs],
        input_output_aliases={a: a for a in range(n)},
        scratch_shapes=[pltpu.SemaphoreType.DMA((6 * n,)), pltpu.SemaphoreType.DMA((6 * n,))],
    )(*xs)


_HBM = pl.BlockSpec(memory_space=pltpu.HBM)
_SEM = pl.BlockSpec(memory_space=pltpu.SEMAPHORE)
_SPLIT_COPY = pltpu.CompilerParams(has_side_effects=pltpu.SideEffectType.DATAFLOW_SIDE_EFFECTING)


def _gather_start(xs, after):
    n = len(xs)

    def body(*refs):
        x_refs = refs[:n]
        send_sems, recv_sems = refs[n + 1], refs[n + 2]
        token = refs[2 * n + 3]
        x, y, c, chips = _place()
        k = 2 * x + y
        for j, chip in enumerate(chips):
            for a in range(n):
                mine = x_refs[a].at[k, c]
                _remote(mine, mine, send_sems, recv_sems, 3 * a + j, (*chip, c)).start()
        token[...] = jnp.zeros(token.shape, F32)

    return pl.pallas_call(
        body, name="gather_start", in_specs=[_HBM] * n + [_ANY],
        out_specs=(_SEM, _SEM, *[_HBM] * n, pl.BlockSpec(memory_space=pltpu.VMEM)),
        out_shape=(pltpu.SemaphoreType.DMA((3 * n,)), pltpu.SemaphoreType.DMA((3 * n,)),
                   *[pltpu.HBM(v.shape, v.dtype) for v in xs], jax.ShapeDtypeStruct((8, 128), F32)),
        input_output_aliases={a: 2 + a for a in range(n)}, compiler_params=_SPLIT_COPY,
    )(*[pltpu.with_memory_space_constraint(v, pltpu.HBM) for v in xs], after)


def _gather_wait(send_sems, recv_sems, xs, after):
    n = len(xs)

    def body(*refs):
        x_refs = refs[:n]
        send_sems, recv_sems = refs[n], refs[n + 1]
        x, y, c, chips = _place()
        k = 2 * x + y
        for j, chip in enumerate(chips):
            kj = 2 * chip[0] + chip[1]
            for a in range(n):
                cp = _remote(x_refs[a].at[k, c], x_refs[a].at[kj, c], send_sems, recv_sems, 3 * a + j, (*chip, c))
                cp.wait_send()
                cp.wait_recv()

    return pl.pallas_call(
        body, name="gather_wait", in_specs=[_HBM] * n + [_SEM, _SEM, _ANY], out_specs=[_HBM] * n,
        out_shape=[pltpu.HBM(v.shape, v.dtype) for v in xs],
        input_output_aliases={a: a for a in range(n)}, compiler_params=_SPLIT_COPY,
    )(*xs, send_sems, recv_sems, after)


def _gather_forward(xs):
    n = len(xs)

    def body(*refs):
        x_refs, o_refs = refs[:n], refs[n:2 * n]
        send_sems, recv_sems = refs[2 * n:]
        x, y, c, chips = _place()
        sibling = (x, y, 1 - c)
        sends = []
        for j, chip in enumerate(chips):
            kj = 2 * chip[0] + chip[1]
            for a in range(n):
                landed = x_refs[a].at[kj, c]
                sends.append(_remote(landed, o_refs[a].at[kj, c], send_sems, recv_sems, 3 * a + j, sibling))
                sends[-1].start()
        for j, chip in enumerate(chips):
            kj = 2 * chip[0] + chip[1]
            for a in range(n):
                passed = o_refs[a].at[kj, 1 - c]
                _remote(passed, passed, send_sems, recv_sems, 3 * a + j, sibling).wait_recv()
        for cp in sends:
            cp.wait_send()

    return pl.pallas_call(
        body, name="gather_forward", in_specs=[_ANY] * n, out_specs=[_ANY] * n,
        out_shape=[jax.ShapeDtypeStruct(v.shape, v.dtype) for v in xs],
        input_output_aliases={a: a for a in range(n)},
        scratch_shapes=[pltpu.SemaphoreType.DMA((3 * n,)), pltpu.SemaphoreType.DMA((3 * n,))],
    )(*xs)


def _pair_exchange(g0s, g1s, name):
    n = len(g0s)

    def body(*refs):
        g0, g1, out = refs[:n], refs[n:2 * n], refs[2 * n:3 * n]
        send_sems, recv_sems = refs[3 * n:]
        x, y, c, _ = _place()
        sibling = (x, y, 1 - c)
        for a in range(n):
            @pl.when(c == 0)
            def _():
                _remote(g1[a], out[a], send_sems, recv_sems, a, sibling).start()

            @pl.when(c == 1)
            def _():
                _remote(g0[a], out[a], send_sems, recv_sems, a, sibling).start()
        for a in range(n):
            cp = _remote(g0[a], out[a], send_sems, recv_sems, a, sibling)
            cp.wait_recv()
            cp.wait_send()

    return pl.pallas_call(
        body, name=name, in_specs=[_ANY] * (2 * n), out_specs=[_ANY] * n,
        out_shape=[jax.ShapeDtypeStruct(v.shape, v.dtype) for v in g0s],
        scratch_shapes=[pltpu.SemaphoreType.DMA((n,)), pltpu.SemaphoreType.DMA((n,))],
    )(*g0s, *g1s)


def _chip_exchange(parts, name):
    n = len(parts)

    def body(*refs):
        p_refs, o_refs = refs[:n], refs[n:2 * n]
        send_sems, recv_sems = refs[2 * n:]
        x, y, c, chips = _place()
        sends = []
        for j, chip in enumerate(chips):
            kj = 2 * chip[0] + chip[1]
            for a in range(n):
                sends.append(_remote(p_refs[a].at[kj], o_refs[a].at[j], send_sems, recv_sems, 3 * a + j, (*chip, c)))
                sends[-1].start()
        for j, chip in enumerate(chips):
            for a in range(n):
                landed = o_refs[a].at[j]
                _remote(landed, landed, send_sems, recv_sems, 3 * a + j, (*chip, c)).wait_recv()
        for cp in sends:
            cp.wait_send()

    return pl.pallas_call(
        body, name=name, in_specs=[_ANY] * n, out_specs=[_ANY] * n,
        out_shape=[jax.ShapeDtypeStruct((3,) + v.shape[1:], v.dtype) for v in parts],
        scratch_shapes=[pltpu.SemaphoreType.DMA((3 * n,)), pltpu.SemaphoreType.DMA((3 * n,))],
    )(*parts)


def _chip_exchange_start(parts):
    n = len(parts)
    lands = [lax.empty((3,) + v.shape[1:], v.dtype) for v in parts]

    def body(*refs):
        p_refs, l_refs = refs[:n], refs[n:2 * n]
        send_sems, recv_sems = refs[2 * n], refs[2 * n + 1]
        token = refs[4 * n + 2]
        x, y, c, chips = _place()
        for j, chip in enumerate(chips):
            kj = 2 * chip[0] + chip[1]
            for a in range(n):
                _remote(p_refs[a].at[kj], l_refs[a].at[j], send_sems, recv_sems, 3 * a + j, (*chip, c)).start()
        token[...] = jnp.zeros(token.shape, F32)

    hbm = [pltpu.with_memory_space_constraint(v, pltpu.HBM) for v in parts + lands]
    return pl.pallas_call(
        body, name="chip_exchange_start", in_specs=[_HBM] * (2 * n),
        out_specs=(_SEM, _SEM, *[_HBM] * (2 * n), pl.BlockSpec(memory_space=pltpu.VMEM)),
        out_shape=(pltpu.SemaphoreType.DMA((3 * n,)), pltpu.SemaphoreType.DMA((3 * n,)),
                   *[pltpu.HBM(v.shape, v.dtype) for v in parts + lands], jax.ShapeDtypeStruct((8, 128), F32)),
        input_output_aliases={a: 2 + a for a in range(2 * n)}, compiler_params=_SPLIT_COPY,
    )(*hbm)


def _chip_exchange_wait(send_sems, recv_sems, parts, lands, after):
    n = len(parts)

    def body(*refs):
        p_refs, l_refs = refs[:n], refs[n:2 * n]
        send_sems, recv_sems = refs[2 * n], refs[2 * n + 1]
        x, y, c, chips = _place()
        for j, chip in enumerate(chips):
            kj = 2 * chip[0] + chip[1]
            for a in range(n):
                cp = _remote(p_refs[a].at[kj], l_refs[a].at[j], send_sems, recv_sems, 3 * a + j, (*chip, c))
                cp.wait_send()
                cp.wait_recv()

    return pl.pallas_call(
        body, name="chip_exchange_wait", in_specs=[_HBM] * (2 * n) + [_SEM, _SEM, _ANY], out_specs=[_HBM] * (2 * n),
        out_shape=[pltpu.HBM(v.shape, v.dtype) for v in parts + lands],
        input_output_aliases={a: a for a in range(2 * n)}, compiler_params=_SPLIT_COPY,
    )(*parts, *lands, send_sems, recv_sems, after)[n:]


def _pair_share(reds):
    n = len(reds)

    def body(*refs):
        r_refs, o_refs = refs[:n], refs[n:2 * n]
        send_sems, recv_sems = refs[2 * n:]
        x, y, c, _ = _place()
        sends = [_remote(r_refs[a], o_refs[a], send_sems, recv_sems, a, (x, y, 1 - c)) for a in range(n)]
        for cp in sends:
            cp.start()
        for cp in sends:
            cp.wait_recv()
        for cp in sends:
            cp.wait_send()

    return pl.pallas_call(
        body, name="pair_share", in_specs=[_ANY] * n, out_specs=[_ANY] * n,
        out_shape=[jax.ShapeDtypeStruct(v.shape, v.dtype) for v in reds],
        scratch_shapes=[pltpu.SemaphoreType.DMA((n,)), pltpu.SemaphoreType.DMA((n,))],
    )(*reds)


def _all_reduce_small(pack):
    p, cols = pack.shape

    def body(x_ref, o_ref, sib, chipbuf, send_sems, recv_sems):
        x, y, c, chips = _place()
        k = 2 * x + y
        sibling = (x, y, 1 - c)
        pair = _remote(x_ref, sib, send_sems, recv_sems, 0, sibling)
        pair.start()
        pair.wait_recv()
        chipbuf[k] = x_ref[...] + sib[...]
        sends = [_remote(chipbuf.at[k], chipbuf.at[k], send_sems, recv_sems, 1 + j, (*chip, c)) for j, chip in enumerate(chips)]
        for cp in sends:
            cp.start()
        for j, chip in enumerate(chips):
            landed = chipbuf.at[2 * chip[0] + chip[1]]
            _remote(landed, landed, send_sems, recv_sems, 1 + j, (*chip, c)).wait_recv()
        pair.wait_send()
        for cp in sends:
            cp.wait_send()
        o_ref[...] = (chipbuf[0] + chipbuf[1]) + (chipbuf[2] + chipbuf[3])

    vm = pl.BlockSpec(memory_space=pltpu.VMEM)
    return pl.pallas_call(
        body, name="all_reduce_small", in_specs=[vm], out_specs=vm,
        out_shape=jax.ShapeDtypeStruct(pack.shape, F32),
        scratch_shapes=[pltpu.VMEM((p, cols), F32), pltpu.VMEM((4, p, cols), F32),
                        pltpu.SemaphoreType.DMA((4,)), pltpu.SemaphoreType.DMA((4,))],
    )(pack)


_BIG = ("w_in", "w_out", "w_up", "w_down")
_SMALL = ("norm1_g", "conv_dw_b", "conv_ln_g", "conv_ln_b", "pool_w", "pool_scale", "norm2_g", "final_g",
          "meta_tokens", "conv_dw_k", "ffn_dw_k")


def _rows8(v):
    return jnp.pad(v, ((0, -v.shape[0] % 8), (0, 0)))


def _pack_flat(arrs, rows):
    flat = jnp.concatenate([a.reshape(-1) for a in arrs])
    return jnp.pad(flat, (0, rows * D_MODEL - flat.shape[0])).reshape(1, rows, D_MODEL)


def _unpack_flat(packed, like):
    flat = packed.reshape(-1)
    out, off = [], 0
    for a in like:
        out.append(flat[off:off + a.size].reshape(a.shape))
        off += a.size
    return out


def kernel(x, meta_tokens, norm1_g, w_in, conv_dw_k, conv_dw_b, conv_ln_g, conv_ln_b, pool_w, pool_scale, w_out, norm2_g, w_up, ffn_dw_k, w_down, final_g, loss_target, m_meta_tokens, m_norm1_g, m_w_in, m_conv_dw_k, m_conv_dw_b, m_conv_ln_g, m_conv_ln_b, m_pool_w, m_pool_scale, m_w_out, m_norm2_g, m_w_up, m_ffn_dw_k, m_w_down, m_final_g, v_meta_tokens, v_norm1_g, v_w_in, v_conv_dw_k, v_conv_dw_b, v_conv_ln_g, v_conv_ln_b, v_pool_w, v_pool_scale, v_w_out, v_norm2_g, v_w_up, v_ffn_dw_k, v_w_down, v_final_g):
    weights = dict(meta_tokens=meta_tokens, norm1_g=norm1_g, w_in=w_in, conv_dw_k=conv_dw_k, conv_dw_b=conv_dw_b,
                   conv_ln_g=conv_ln_g, conv_ln_b=conv_ln_b, pool_w=pool_w, pool_scale=pool_scale, w_out=w_out,
                   norm2_g=norm2_g, w_up=w_up, ffn_dw_k=ffn_dw_k, w_down=w_down, final_g=final_g)
    mom1 = dict(meta_tokens=m_meta_tokens, norm1_g=m_norm1_g, w_in=m_w_in, conv_dw_k=m_conv_dw_k, conv_dw_b=m_conv_dw_b,
                conv_ln_g=m_conv_ln_g, conv_ln_b=m_conv_ln_b, pool_w=m_pool_w, pool_scale=m_pool_scale, w_out=m_w_out,
                norm2_g=m_norm2_g, w_up=m_w_up, ffn_dw_k=m_ffn_dw_k, w_down=m_w_down, final_g=m_final_g)
    mom2 = dict(meta_tokens=v_meta_tokens, norm1_g=v_norm1_g, w_in=v_w_in, conv_dw_k=v_conv_dw_k, conv_dw_b=v_conv_dw_b,
                conv_ln_g=v_conv_ln_g, conv_ln_b=v_conv_ln_b, pool_w=v_pool_w, pool_scale=v_pool_scale, w_out=v_w_out,
                norm2_g=v_norm2_g, w_up=v_w_up, ffn_dw_k=v_ffn_dw_k, w_down=v_w_down, final_g=v_final_g)
    order = list(weights)
    depth = w_in.shape[0]
    seq = x.shape[1]
    chip = 2 * lax.axis_index("x") + lax.axis_index("y")
    core = lax.axis_index("c")
    chip_arr = chip.astype(jnp.int32).reshape(1)
    core_arr = core.astype(jnp.int32).reshape(1)

    small_sharded = dict(conv_dw_k=jnp.pad(conv_dw_k, ((0, 0), (0, CONV_HIST - CONV_TAPS), (0, 0))),
                         ffn_dw_k=jnp.pad(ffn_dw_k, ((0, 0), (0, 8 - FFN_TAPS), (0, 0))),
                         meta_tokens=meta_tokens.reshape(2, N_META // 2, D_MODEL // 4))
    placed = {nm: _place_own(chip_arr, weights[nm], BF16, "place_" + nm) for nm in _BIG}
    g_in, g_out, g_cdk, g_fdk, g_meta = _gather_chips(
        [placed["w_in"], placed["w_out"]] + [_place_own(chip_arr, v, F32, "place_" + nm) for nm, v in small_sharded.items()])
    send_sems, recv_sems, up_buf, down_buf, token = _gather_start([placed["w_up"], placed["w_down"]], g_meta)
    meta_full = g_meta.transpose(1, 2, 0, 3).reshape(N_META, D_MODEL)
    layers = []
    for l in range(depth):
        w_in_l = g_in[:, l].transpose(1, 0, 2).reshape(D_MODEL, IN_COLS)
        w_out_l = g_out[:, l].reshape(D_MODEL, D_MODEL)
        pw = pool_w[l].astype(BF16)
        conv_v = jnp.pad(jnp.stack([conv_dw_b[l], conv_ln_g[l], conv_ln_b[l], pool_scale[l]]), ((0, 4), (0, 0)))
        layers.append(dict(
            w_in=w_in_l, w_in_t=w_in_l.T, conv_k=g_cdk[:, l].transpose(1, 0, 2).reshape(CONV_HIST, CONV_W), conv_v=conv_v,
            pool_w=pw, pool_w_t=pw.transpose(0, 2, 1), w_out=w_out_l, w_out_t=w_out_l.T, kf=g_fdk[:, l],
            g1=norm1_g[l][None], g2=norm2_g[l][None]))
    layers[0]["g1"] = layers[0]["g1"] + token[0, 0]

    def ffn_weights(after):
        g_up, g_down = _gather_forward(_gather_wait(send_sems, recv_sems, [up_buf, down_buf], after))
        per_layer = []
        for l in range(depth):
            w_down_l = g_down[:, l].reshape(2, FF_CHUNK, D_MODEL)
            per_layer.append(dict(w_up_t=g_up[:, l].transpose(0, 2, 1), w_down=w_down_l, w_down_t=w_down_l.transpose(0, 2, 1)))
        return g_up, per_layer

    def pair_reduced(names, g0s, g1s, tag):
        theirs = _pair_exchange(g0s, g1s, "pair_exchange_" + tag)
        return [_pair_sum_bf16(core_arr, g0s[a], g1s[a], theirs[a], "pair_sum_" + nm) for a, nm in enumerate(names)]

    in_flight = {}

    def ffn_grads_done(ffn_grads):
        parts = pair_reduced(("w_up", "w_down"), list(ffn_grads[0]), list(ffn_grads[1]), "ffn")
        send, recv, up_parts, down_parts, up_land, down_land, zero = _chip_exchange_start(parts)
        in_flight.update(sems=(send, recv), parts=[up_parts, down_parts], lands=[up_land, down_land])
        return zero[0, 0]

    loss_cols, dx, dmeta, grads, dfinal_g = _local_step(x[0], meta_full, loss_target[0], layers, ffn_weights, ffn_grads_done,
                                                         final_g[None], ROW_TILE, FFN_ROW_TILE)
    grad_x = dx[None]
    mixer_parts = pair_reduced(("w_in", "w_out"), [grads[0]["dw_in"], grads[0]["dw_out"]],
                               [grads[1]["dw_in"], grads[1]["dw_out"]], "mixer")
    parts = mixer_parts + in_flight["parts"]
    received = list(_chip_exchange(mixer_parts, "chip_exchange_mixer")) + list(_chip_exchange_wait(
        *in_flight["sems"], in_flight["parts"], in_flight["lands"], dx))
    reds = [_chip_sum(chip_arr, parts[a], received[a], "chip_sum_" + _BIG[a]) for a in range(4)]
    grad, delta, new_m, new_v = {}, {}, {}, {}
    for a, other in enumerate(_pair_share(reds)):
        nm = _BIG[a]
        grad[nm], delta[nm], new_m[nm], new_v[nm] = _adamw_pair(core_arr, weights[nm], reds[a], other, mom1[nm], mom2[nm],
                                                                "adamw_" + nm)

    pack = jnp.concatenate([
        _rows8(jnp.concatenate([grads[l]["dg1"] for l in range(depth)])),
        _rows8(jnp.concatenate([grads[l]["dg2"] for l in range(depth)])),
        _rows8(jnp.concatenate([dfinal_g, loss_cols])),
        jnp.concatenate([grads[l]["dsmall"] for l in range(depth)], axis=1),
        jnp.stack([grads[l]["dpw"] for l in range(depth)]).reshape(-1, D_MODEL),
        dmeta,
        jnp.concatenate([grads[l]["dk"] for l in range(depth)], axis=1),
        jnp.stack([grads[l]["dkf"] for l in range(depth)]).reshape(-1, D_MODEL),
    ])
    red = _all_reduce_small(pack)
    o = 0
    grad["norm1_g"] = red[o:o + depth]
    o += 8
    grad["norm2_g"] = red[o:o + depth]
    o += 8
    grad["final_g"] = red[o]
    loss = red[o + 1, 0]
    o += 8
    sm = red[o:o + 8].reshape(8, depth, CONV_W)
    grad["conv_dw_b"], grad["conv_ln_g"], grad["conv_ln_b"], grad["pool_scale"] = sm[0], sm[1], sm[2], sm[3]
    o += 8
    n_pw = depth * 4 * POOL_G * POOL_G // D_MODEL
    grad["pool_w"] = red[o:o + n_pw].reshape(pool_w.shape)
    o += n_pw
    grad["meta_tokens"] = lax.dynamic_slice_in_dim(red[o:o + N_META], chip * (D_MODEL // 4), D_MODEL // 4, axis=1)
    o += N_META
    dk_all = red[o:o + CONV_HIST].reshape(CONV_HIST, depth, 4, CONV_W // 4)
    grad["conv_dw_k"] = lax.dynamic_index_in_dim(dk_all, chip, axis=2, keepdims=False)[:CONV_TAPS].transpose(1, 0, 2)
    o += CONV_HIST
    dkf_all = red[o:].reshape(depth, 4, 8, FF_CHUNK)
    grad["ffn_dw_k"] = lax.dynamic_index_in_dim(dkf_all, chip, axis=1, keepdims=False)[:, :FFN_TAPS]

    small_rows = -(-sum(weights[nm].size for nm in _SMALL) // (8 * D_MODEL)) * 8
    packed = [_pack_flat([d[nm] for nm in _SMALL], small_rows) for d in (weights, grad, mom1, mom2)]
    for res, packed_out in zip((delta, new_m, new_v), _adamw(*packed, "adamw_small")):
        for nm, val in zip(_SMALL, _unpack_flat(packed_out, [weights[nm] for nm in _SMALL])):
            res[nm] = val

    return (loss, grad_x, *[grad[nm] for nm in order], *[delta[nm] for nm in order],
            *[new_m[nm] for nm in order], *[new_v[nm] for nm in order])
```

```python
import math

import jax
import jax.numpy as jnp
from jax import lax
from jax.experimental import pallas as pl
from jax.experimental.pallas import tpu as pltpu

F32 = jnp.float32
BF16 = jnp.bfloat16

D_MODEL = 1024
CONV_W = 512
POOL_W = 512
POOL_G = 128
POOL_WINDOWS = (2, 4, 8, 16)
IN_COLS = 1536
D_FF = 2816
FF_CHUNK = 1408
CONV_TAPS = 31
CONV_HIST = 32
POOL_HIST = 16
FFN_TAPS = 3
N_META = 16
EPS = 1e-6

ADAM_LR = 0.001
ADAM_B1 = 0.9
ADAM_B2 = 0.999
ADAM_EPS = 1e-08
ADAM_WD = 0.01
ADAM_STEP = 10

ROW_TILE = 256
FFN_ROW_TILE = 384
CONV_ROW_BLOCK = 64
FFN_ROW_BLOCK = 32
V7X_VMEM_LIMIT = 56 * 1024 * 1024

MESH = pl.DeviceIdType.MESH


def _cparams(n_axes, vmem=None):
    return pltpu.CompilerParams(dimension_semantics=("arbitrary",) * n_axes, vmem_limit_bytes=vmem)


def _whole(shape, single=False):
    zeros = (0,) * len(shape)
    if single:
        return pl.BlockSpec(shape, lambda *_: zeros, pipeline_mode=pl.Buffered(1))
    return pl.BlockSpec(shape, lambda *_: zeros)


def _sigmoid(x):
    return 0.5 * jnp.tanh(0.5 * x) + 0.5


def _dot(a, b):
    return jnp.dot(a, b, preferred_element_type=F32)


def _dot_tn(a, b):
    return lax.dot_general(a, b, (((0,), (0,)), ((), ())), preferred_element_type=F32)


def _split_dot(v, a_ref):
    hi = v.astype(BF16)
    lo = (v - hi.astype(F32)).astype(BF16)
    return _dot(hi, a_ref[...]) + _dot(lo, a_ref[...])


def _rms(x):
    r = lax.rsqrt(jnp.mean(x * x, axis=-1, keepdims=True) + EPS)
    return x * r, r


def _rms_bwd(dy, xhat, r, g):
    gd = dy * g
    return r * (gd - xhat * jnp.mean(gd * xhat, axis=-1, keepdims=True)), dy * xhat


def _colsum(v):
    return jnp.sum(v, axis=0, keepdims=True)


def _shifted(window, s):
    return window if s == 0 else pltpu.roll(window, window.shape[0] - s, 0)


def _conv3(window, kf_ref, cc, ls, hist):
    x2 = window[hist:]
    x1 = pltpu.roll(window, 1, 0)[hist:]
    x0 = pltpu.roll(window, 2, 0)[hist:]
    return x0, x1, x2, kf_ref[cc, 0:1, ls] * x0 + kf_ref[cc, 1:2, ls] * x1 + kf_ref[cc, 2:3, ls] * x2


def _for_row_blocks(n, rb, fn, unroll):
    if unroll:
        for r in range(n):
            fn(r * rb)
    else:
        def step(r, keep):
            fn(pl.multiple_of(r * rb, rb))
            return keep

        lax.fori_loop(0, n, step, 0)


def _fold8(v):
    part = v[0:8]
    for k in range(1, v.shape[0] // 8):
        part = part + v[8 * k:8 * k + 8]
    return part


def _inv_count(tile, tm, w):
    t = tile * tm + lax.broadcasted_iota(jnp.int32, (tm, POOL_G), 0)
    return 1.0 / jnp.minimum(t + 1, w).astype(F32)


def _embed(x, meta, g, rows, tm):
    seq = x.shape[0]
    x_tiles = -(-seq // tm)

    def body(xprev_ref, x_ref, meta_ref, g_ref, h_ref, hn_ref):
        i = pl.program_id(0)
        t = i * tm + lax.broadcasted_iota(jnp.int32, (tm, 1), 0)
        head = jnp.where(i == 0, meta_ref[...], xprev_ref[...])
        h = jnp.concatenate([head, x_ref[0:tm - N_META, :]], axis=0)
        h = jnp.where(t < N_META + seq, h, 0.0)
        h_ref[...] = h
        xhat, _ = _rms(h)
        hn_ref[...] = (xhat * g_ref[...]).astype(BF16)

    tile = pl.BlockSpec((tm, D_MODEL), lambda i: (i, 0))
    x_prev = pl.BlockSpec((N_META, D_MODEL), lambda i: (jnp.maximum(i * (tm // N_META) - 1, 0), 0))
    x_own = pl.BlockSpec((tm, D_MODEL), lambda i: (jnp.minimum(i, x_tiles - 1), 0))
    return pl.pallas_call(
        body, name="embed", grid=(rows // tm,),
        in_specs=[x_prev, x_own, _whole((N_META, D_MODEL)), _whole((1, D_MODEL))],
        out_specs=[tile, tile],
        out_shape=[jax.ShapeDtypeStruct((rows, D_MODEL), F32), jax.ShapeDtypeStruct((rows, D_MODEL), BF16)],
        compiler_params=_cparams(1),
    )(x, x, meta, g)


def _row_block(r):
    for cand in (256, 176, 128, 64, 32, 16):
        if r % cand == 0:
            return cand
    return r


def _place_own(chip, w, dtype, name):
    n, r, c = w.shape
    rb = _row_block(r)

    def body(chip_ref, w_ref, o_ref):
        o_ref[0] = w_ref[...].astype(dtype)

    return pl.pallas_call(
        body, name=name,
        grid_spec=pltpu.PrefetchScalarGridSpec(
            num_scalar_prefetch=1, grid=(n, r // rb),
            in_specs=[pl.BlockSpec((1, rb, c), lambda i, j, chip_ref: (i, j, 0))],
            out_specs=pl.BlockSpec((1, 1, rb, c), lambda i, j, chip_ref: (chip_ref[0], i, j, 0))),
        out_shape=jax.ShapeDtypeStruct((4,) + w.shape, dtype), compiler_params=_cparams(2),
    )(chip, w)


def _pair_sum_bf16(core, g0, g1, other, name):
    n, r, c = g0.shape
    rb = _row_block(r)

    def body(core_ref, g0_ref, g1_ref, o_ref, out_ref):
        mine = jnp.where(core_ref[0] == 0, g0_ref[...], g1_ref[...])
        out_ref[...] = (mine + o_ref[...]).astype(BF16)

    def layer_spec(layer):
        return pl.BlockSpec((1, rb, c), lambda i, j, core_ref: (jnp.where(core_ref[0] == layer, i, 0),
                                                                jnp.where(core_ref[0] == layer, j, 0), 0))

    spec = pl.BlockSpec((1, rb, c), lambda i, j, core_ref: (i, j, 0))
    return pl.pallas_call(
        body, name=name,
        grid_spec=pltpu.PrefetchScalarGridSpec(num_scalar_prefetch=1, grid=(n, r // rb),
                                               in_specs=[layer_spec(0), layer_spec(1), spec], out_specs=spec),
        out_shape=jax.ShapeDtypeStruct(g0.shape, BF16), compiler_params=_cparams(2),
    )(core, g0, g1, other)


def _chip_sum(chip, parts, recv, name):
    _, r, c = parts.shape
    rb = _row_block(r)

    def body(chip_ref, p_ref, r_ref, out_ref):
        got = r_ref[...].astype(F32)
        out_ref[...] = (p_ref[0].astype(F32) + got[0]) + (got[1] + got[2])

    return pl.pallas_call(
        body, name=name,
        grid_spec=pltpu.PrefetchScalarGridSpec(
            num_scalar_prefetch=1, grid=(r // rb,),
            in_specs=[pl.BlockSpec((1, rb, c), lambda j, chip_ref: (chip_ref[0], j, 0)),
                      pl.BlockSpec((3, rb, c), lambda j, chip_ref: (0, j, 0))],
            out_specs=pl.BlockSpec((rb, c), lambda j, chip_ref: (j, 0))),
        out_shape=jax.ShapeDtypeStruct((r, c), F32), compiler_params=_cparams(1),
    )(chip, parts, recv)


def _adamw_update(w, g, m, v):
    nm = ADAM_B1 * m + (1.0 - ADAM_B1) * g
    nv = ADAM_B2 * v + (1.0 - ADAM_B2) * (g * g)
    m_hat = nm / (1.0 - ADAM_B1 ** ADAM_STEP)
    v_hat = nv / (1.0 - ADAM_B2 ** ADAM_STEP)
    return -ADAM_LR * (m_hat / (jnp.sqrt(v_hat) + ADAM_EPS) + ADAM_WD * w), nm, nv


def _adamw(w, g, m, v, name):
    n, r, c = w.shape
    rb = _row_block(r)

    def body(w_ref, g_ref, m_ref, v_ref, d_ref, nm_ref, nv_ref):
        d_ref[...], nm_ref[...], nv_ref[...] = _adamw_update(w_ref[...], g_ref[...], m_ref[...], v_ref[...])

    spec = pl.BlockSpec((1, rb, c), lambda i, j: (i, j, 0))
    shp = jax.ShapeDtypeStruct(w.shape, F32)
    return pl.pallas_call(
        body, name=name, grid=(n, r // rb), in_specs=[spec] * 4, out_specs=[spec] * 3,
        out_shape=[shp] * 3, compiler_params=_cparams(2),
    )(w, g, m, v)


def _adamw_pair(core, w, mine, theirs, m, v, name):
    n, r, c = w.shape
    rb = _row_block(r)

    def body(core_ref, w_ref, a_ref, b_ref, m_ref, v_ref, g_ref, d_ref, nm_ref, nv_ref):
        g = jnp.where(pl.program_id(0) == core_ref[0], a_ref[...], b_ref[...])
        g_ref[0] = g
        d_ref[0], nm_ref[0], nv_ref[0] = _adamw_update(w_ref[0], g, m_ref[0], v_ref[0])

    spec = pl.BlockSpec((1, rb, c), lambda i, j, core_ref: (i, j, 0))
    flat = pl.BlockSpec((rb, c), lambda i, j, core_ref: (j, 0))
    shp = jax.ShapeDtypeStruct(w.shape, F32)
    return pl.pallas_call(
        body, name=name,
        grid_spec=pltpu.PrefetchScalarGridSpec(num_scalar_prefetch=1, grid=(n, r // rb),
                                               in_specs=[spec, flat, flat, spec, spec], out_specs=[spec] * 4),
        out_shape=[shp] * 4, compiler_params=_cparams(2),
    )(core, w, mine, theirs, m, v)


def _mixer_fwd(h0, hn, w_in, conv_k, conv_v, avg, pool_w, w_out, g2, tm):
    rows = h0.shape[0]
    rb = CONV_ROW_BLOCK

    def body(h0_ref, hn_ref, win_ref, ck_ref, cv_ref, avg_ref, pw_ref, wout_ref, g2_ref,
             z_ref, u1_ref, m_ref, h1_ref, hn2_ref, ubuf, pbuf):
        i = pl.program_id(0)

        @pl.when(i == 0)
        def _():
            ubuf[pl.ds(0, CONV_HIST), :] = jnp.zeros((CONV_HIST, CONV_W), F32)
            pbuf[pl.ds(0, POOL_HIST), :] = jnp.zeros((POOL_HIST, POOL_W), F32)

        z = _dot(hn_ref[...], win_ref[...])
        z_ref[...] = z
        ubuf[pl.ds(CONV_HIST, tm), :] = z[:, :CONV_W] * _sigmoid(z[:, CONV_W:2 * CONV_W])
        p = z[:, 2 * CONV_W:]
        pbuf[pl.ds(POOL_HIST, tm), :] = p

        def conv_block(r0):
            for l in range(CONV_W // 128):
                ls = pl.ds(128 * l, 128)
                window = ubuf[pl.ds(r0, rb + CONV_HIST), ls]
                acc = jnp.broadcast_to(cv_ref[0:1, ls], (rb, 128))
                for s in range(8):
                    ws = _shifted(window, s)
                    for q in range(CONV_HIST // 8 + 1):
                        j = 8 * q + s - 2
                        if 0 <= j < CONV_TAPS:
                            acc = acc + ck_ref[j:j + 1, ls] * ws[8 * q:8 * q + rb]
                u1_ref[pl.ds(r0, rb), ls] = acc

        _for_row_blocks(tm // rb, rb, conv_block, unroll=True)
        ubuf[pl.ds(0, CONV_HIST), :] = ubuf[pl.ds(tm, CONV_HIST), :]

        u1 = u1_ref[...]
        cen = u1 - _split_dot(u1, avg_ref)
        xhat = cen * lax.rsqrt(_split_dot(cen * cen, avg_ref) + EPS)
        u2 = xhat * cv_ref[1:2, :] + cv_ref[2:3, :]
        m_ref[:, 0:CONV_W] = (u2 * _sigmoid(u2)).astype(BF16)

        for gi, w in enumerate(POOL_WINDOWS):
            ls = pl.ds(POOL_G * gi, POOL_G)
            s = pbuf[pl.ds(POOL_HIST, tm), ls]
            for j in range(1, w):
                s = s + pbuf[pl.ds(POOL_HIST - j, tm), ls]
            d = s * _inv_count(i, tm, w) - p[:, POOL_G * gi:POOL_G * (gi + 1)]
            y = _dot(d.astype(BF16), pw_ref[gi]) * cv_ref[3:4, ls]
            m_ref[:, pl.ds(CONV_W + POOL_G * gi, POOL_G)] = y.astype(BF16)
        pbuf[pl.ds(0, POOL_HIST), :] = pbuf[pl.ds(tm, POOL_HIST), :]

        h1 = h0_ref[...] + _dot(m_ref[...], wout_ref[...])
        h1_ref[...] = h1
        xh, _ = _rms(h1)
        hn2_ref[...] = (xh * g2_ref[...]).astype(BF16)

    def tile(c):
        return pl.BlockSpec((tm, c), lambda i: (i, 0))

    return pl.pallas_call(
        body, name="mixer_fwd", grid=(rows // tm,),
        in_specs=[tile(D_MODEL), tile(D_MODEL), _whole((D_MODEL, IN_COLS)), _whole((CONV_HIST, CONV_W)),
                  _whole((8, CONV_W)), _whole((CONV_W, CONV_W)), _whole((4, POOL_G, POOL_G)),
                  _whole((D_MODEL, D_MODEL)), _whole((1, D_MODEL))],
        out_specs=[tile(IN_COLS), tile(CONV_W), tile(D_MODEL), tile(D_MODEL), tile(D_MODEL)],
        out_shape=[jax.ShapeDtypeStruct((rows, IN_COLS), F32), jax.ShapeDtypeStruct((rows, CONV_W), F32),
                   jax.ShapeDtypeStruct((rows, D_MODEL), BF16), jax.ShapeDtypeStruct((rows, D_MODEL), F32),
                   jax.ShapeDtypeStruct((rows, D_MODEL), BF16)],
        scratch_shapes=[pltpu.VMEM((CONV_HIST + tm, CONV_W), F32), pltpu.VMEM((POOL_HIST + tm, POOL_W), F32)],
        compiler_params=_cparams(1, V7X_VMEM_LIMIT),
    )(h0, hn, w_in, conv_k, conv_v, avg, pool_w, w_out, g2)


def _ffn_fwd(h1, hn2, w_up, layer, kf, w_down, g_next, tm):
    rows = h1.shape[0]
    hist = 8
    w_up_spec = pl.BlockSpec((4, None, D_MODEL, FF_CHUNK), lambda i: (0, layer, 0, 0), pipeline_mode=pl.Buffered(1))

    rb = FFN_ROW_BLOCK

    def body(h1_ref, hn2_ref, wup_ref, kf_ref, wdn_ref, gn_ref, ug_ref, h2_ref, hnn_ref, wg, wv, carry, act_s, acc):
        i = pl.program_id(0)

        @pl.when(i == 0)
        def _():
            carry[...] = jnp.zeros(carry.shape, F32)

        acc[...] = h1_ref[...]
        for c in range(2):
            for buf, cc in ((wg, c), (wv, c + 2)):
                ug = _dot(hn2_ref[...], wup_ref[cc])
                ug_ref[cc] = ug.astype(BF16)
                buf[pl.ds(0, hist), :] = carry[cc]
                buf[pl.ds(hist, tm), :] = ug
                carry[cc] = buf[pl.ds(tm, hist), :]

            def act_block(r0):
                for l in range(FF_CHUNK // 128):
                    ls = pl.ds(128 * l, 128)
                    gate = _conv3(wg[pl.ds(r0, rb + hist), ls], kf_ref, c, ls, hist)[3]
                    val = _conv3(wv[pl.ds(r0, rb + hist), ls], kf_ref, c + 2, ls, hist)[3]
                    act_s[pl.ds(r0, rb), ls] = (gate * _sigmoid(gate) * val).astype(BF16)

            _for_row_blocks(tm // rb, rb, act_block, unroll=True)
            acc[...] += _dot(act_s[...], wdn_ref[c])
        h2 = acc[...]
        h2_ref[...] = h2
        xh, _ = _rms(h2)
        hnn_ref[...] = (xh * gn_ref[...]).astype(BF16)

    def tile(c):
        return pl.BlockSpec((tm, c), lambda i: (i, 0))

    return pl.pallas_call(
        body, name="ffn_fwd", grid=(rows // tm,),
        in_specs=[tile(D_MODEL), tile(D_MODEL), w_up_spec, _whole((4, 8, FF_CHUNK)),
                  _whole((2, FF_CHUNK, D_MODEL), single=True), _whole((1, D_MODEL))],
        out_specs=[pl.BlockSpec((4, tm, FF_CHUNK), lambda i: (0, i, 0)), tile(D_MODEL), tile(D_MODEL)],
        out_shape=[jax.ShapeDtypeStruct((4, rows, FF_CHUNK), BF16), jax.ShapeDtypeStruct((rows, D_MODEL), F32),
                   jax.ShapeDtypeStruct((rows, D_MODEL), BF16)],
        scratch_shapes=[pltpu.VMEM((hist + tm, FF_CHUNK), F32), pltpu.VMEM((hist + tm, FF_CHUNK), F32),
                        pltpu.VMEM((4, hist, FF_CHUNK), F32), pltpu.VMEM((tm, FF_CHUNK), BF16),
                        pltpu.VMEM((tm, D_MODEL), F32)],
        compiler_params=_cparams(1, V7X_VMEM_LIMIT),
    )(h1, hn2, w_up, kf, w_down, g_next)


def _loss_head(h, tgt, g, seq, tm):
    rows = h.shape[0]
    tgt_tiles = -(-seq // tm)

    def body(h_ref, tprev_ref, t_ref, g_ref, dh_ref, loss_ref, dg_ref):
        i = pl.program_id(0)

        @pl.when(i == 0)
        def _():
            loss_ref[...] = jnp.zeros(loss_ref.shape, F32)
            dg_ref[...] = jnp.zeros(dg_ref.shape, F32)

        t = i * tm + lax.broadcasted_iota(jnp.int32, (tm, 1), 0)
        inside = jnp.logical_and(t >= N_META, t < N_META + seq)
        tgt = jnp.concatenate([tprev_ref[...], t_ref[0:tm - N_META, :]], axis=0)
        xhat, r = _rms(h_ref[...])
        err = jnp.where(inside, xhat * g_ref[...] - tgt, 0.0)
        loss_ref[...] += _colsum(err * err)
        dh, dg_rows = _rms_bwd(err * (1.0 / D_MODEL), xhat, r, g_ref[...])
        dh_ref[...] = dh
        dg_ref[...] += _colsum(dg_rows)

        @pl.when(i == rows // tm - 1)
        def _():
            total = jnp.sum(loss_ref[...], axis=1, keepdims=True) * (0.5 / D_MODEL)
            loss_ref[...] = jnp.broadcast_to(total, loss_ref.shape)

    tile = pl.BlockSpec((tm, D_MODEL), lambda i: (i, 0))
    t_prev = pl.BlockSpec((N_META, D_MODEL), lambda i: (jnp.maximum(i * (tm // N_META) - 1, 0), 0))
    t_own = pl.BlockSpec((tm, D_MODEL), lambda i: (jnp.minimum(i, tgt_tiles - 1), 0))
    vec = jax.ShapeDtypeStruct((1, D_MODEL), F32)
    return pl.pallas_call(
        body, name="loss_head", grid=(rows // tm,),
        in_specs=[tile, t_prev, t_own, _whole((1, D_MODEL))],
        out_specs=[tile, _whole((1, D_MODEL)), _whole((1, D_MODEL))],
        out_shape=[jax.ShapeDtypeStruct((rows, D_MODEL), F32), vec, vec],
        compiler_params=_cparams(1),
    )(h, tgt, tgt, g)


def _ffn_bwd(dh2, hn2, ug0, w_down_t, w_up_t, kf, tm):
    rows = dh2.shape[0]
    nt = rows // tm
    hist = 16
    fut = 8
    rb = FFN_ROW_BLOCK
    near = 8

    def body(dh2_ref, hn2_ref, ugg_ref, ugv_ref, hg_ref, hv_ref, wdt_ref, wutg_ref, wutv_ref, kf_ref,
             dhn_ref, dkf_ref, dwup_ref, dwdn_ref, wg, wv, dgb, dvb, carry, dkacc, act_s, dug_s, acc_up, acc_dn):
        c = pl.program_id(0)
        i = pl.program_id(1)
        first_tile = jnp.where(i == nt - 1, 1.0, 0.0)

        @pl.when(i == 0)
        def _():
            carry[...] = jnp.zeros(carry.shape, F32)
            dkacc[...] = jnp.zeros(dkacc.shape, F32)
            acc_up[...] = jnp.zeros(acc_up.shape, F32)
            acc_dn[...] = jnp.zeros(acc_dn.shape, F32)

        def run():
            n_blocks = tm // rb
            dh2b = dh2_ref[...].astype(BF16)
            sides = ((wg, dgb, ugg_ref, hg_ref, 0, c), (wv, dvb, ugv_ref, hv_ref, 1, c + 2))
            for buf, dbuf, u_ref, h_ref, s, cc in sides:
                buf[pl.ds(0, hist), :] = h_ref[0].astype(F32) * (1.0 - first_tile)
                buf[pl.ds(hist, tm), :] = u_ref[0].astype(F32)
                dbuf[pl.ds(tm, fut), :] = carry[s]
            dgb[pl.ds(0, tm), :] = _dot(dh2b, wdt_ref[0])

            def grad_block(r0):
                for l in range(FF_CHUNK // 128):
                    ls = pl.ds(128 * l, 128)
                    g0, g1, g2, gate = _conv3(wg[pl.ds(r0 + hist - near, rb + near), ls], kf_ref, c, ls, near)
                    v0, v1, v2, val = _conv3(wv[pl.ds(r0 + hist - near, rb + near), ls], kf_ref, c + 2, ls, near)
                    sg = _sigmoid(gate)
                    silu = gate * sg
                    act_s[pl.ds(r0, rb), ls] = (silu * val).astype(BF16)
                    dact = dgb[pl.ds(r0, rb), ls]
                    dgate = dact * val * (sg * (1.0 + gate * (1.0 - sg)))
                    dval = dact * silu
                    dgb[pl.ds(r0, rb), ls] = dgate
                    dvb[pl.ds(r0, rb), ls] = dval
                    for s, dv, taps in ((0, dgate, (g0, g1, g2)), (1, dval, (v0, v1, v2))):
                        for j in range(FFN_TAPS):
                            dkacc[s, pl.ds(8 * j, 8), ls] += _fold8(dv * taps[j])

            _for_row_blocks(n_blocks, rb, grad_block, unroll=True)
            for buf, dbuf, u_ref, h_ref, s, cc in sides:
                carry[s] = dbuf[pl.ds(0, fut), :]

            def conv_block(r0):
                for l in range(FF_CHUNK // 128):
                    ls = pl.ds(128 * l, 128)
                    for buf, dbuf, u_ref, h_ref, s, cc in sides:
                        window = dbuf[pl.ds(r0, rb + fut), ls]
                        dug0 = (kf_ref[cc, 0:1, ls] * _shifted(window, 2)[0:rb] + kf_ref[cc, 1:2, ls] * _shifted(window, 1)[0:rb]
                                + kf_ref[cc, 2:3, ls] * window[0:rb])
                        dug_s[s, pl.ds(r0, rb), ls] = dug0.astype(BF16)

            _for_row_blocks(n_blocks, rb, conv_block, unroll=True)
            dhn_ref[0] = _dot(dug_s[0], wutg_ref[0]) + _dot(dug_s[1], wutv_ref[0])
            acc_up[0] += _dot_tn(hn2_ref[...], dug_s[0])
            acc_up[1] += _dot_tn(hn2_ref[...], dug_s[1])
            acc_dn[...] += _dot_tn(act_s[...], dh2b)

        run()

        @pl.when(i == nt - 1)
        def _():
            for s in range(2):
                for j in range(FFN_TAPS):
                    dkf_ref[c + 2 * s, j:j + 1, :] = _colsum(dkacc[s, pl.ds(8 * j, 8), :])
                dkf_ref[c + 2 * s, FFN_TAPS:8, :] = jnp.zeros((8 - FFN_TAPS, FF_CHUNK), F32)
            pltpu.sync_copy(acc_up, dwup_ref.at[c])
            pltpu.sync_copy(acc_dn, dwdn_ref.at[c])

    def tile(cols):
        return pl.BlockSpec((tm, cols), lambda c, i: (nt - 1 - i, 0))

    def chunk(off, r, halo_rows=None):
        if halo_rows is None:
            return pl.BlockSpec((1, r, FF_CHUNK), lambda c, i: (c + off, nt - 1 - i, 0))
        return pl.BlockSpec((1, r, FF_CHUNK), lambda c, i: (c + off, jnp.maximum((nt - 1 - i) * (tm // r) - 1, 0), 0))

    def weight(off, r, cols):
        return pl.BlockSpec((1, r, cols), lambda c, i: (c + off, 0, 0), pipeline_mode=pl.Buffered(1))

    return pl.pallas_call(
        body, name="ffn_bwd", grid=(2, nt),
        in_specs=[tile(D_MODEL), tile(D_MODEL), chunk(0, tm), chunk(2, tm), chunk(0, hist, True), chunk(2, hist, True),
                  weight(0, D_MODEL, FF_CHUNK), weight(0, FF_CHUNK, D_MODEL), weight(2, FF_CHUNK, D_MODEL),
                  _whole((4, 8, FF_CHUNK))],
        out_specs=[pl.BlockSpec((1, tm, D_MODEL), lambda c, i: (c, nt - 1 - i, 0)), _whole((4, 8, FF_CHUNK)), _ANY, _ANY],
        out_shape=[jax.ShapeDtypeStruct((2, rows, D_MODEL), F32), jax.ShapeDtypeStruct((4, 8, FF_CHUNK), F32),
                   jax.ShapeDtypeStruct((2, 2, D_MODEL, FF_CHUNK), F32), jax.ShapeDtypeStruct((2, FF_CHUNK, D_MODEL), F32)],
        scratch_shapes=[pltpu.VMEM((hist + tm, FF_CHUNK), F32), pltpu.VMEM((hist + tm, FF_CHUNK), F32),
                        pltpu.VMEM((tm + fut, FF_CHUNK), F32), pltpu.VMEM((tm + fut, FF_CHUNK), F32),
                        pltpu.VMEM((2, fut, FF_CHUNK), F32), pltpu.VMEM((2, 8 * FFN_TAPS, FF_CHUNK), F32),
                        pltpu.VMEM((tm, FF_CHUNK), BF16), pltpu.VMEM((2, tm, FF_CHUNK), BF16),
                        pltpu.VMEM((2, D_MODEL, FF_CHUNK), F32), pltpu.VMEM((FF_CHUNK, D_MODEL), F32)],
        compiler_params=_cparams(2, V7X_VMEM_LIMIT),
    )(dh2, hn2, ug0, ug0, ug0, ug0, w_down_t, w_up_t, w_up_t, kf)


def _mixer_bwd(dh2, dhn2, h1, g2, h0, z, u1, hn1, m, w_out_t, w_in_t, conv_k, conv_v, avg, pool_w, pool_w_t, g1, tm,
               x_rows=None):
    rows = dh2.shape[0]
    nt = rows // tm
    rb = CONV_ROW_BLOCK
    in_shard = IN_COLS // 4

    def body(dh2_ref, dhn2_ref, h1_ref, g2_ref, h0_ref, z_ref, zh_ref, u1_ref, hn1_ref, m_ref, wot_ref, wit_ref, ck_ref,
             cv_ref, avg_ref, pw_ref, pwt_ref, g1_ref,
             dh0_ref, dtop_ref, dk_ref, ds_ref, dpw_ref, dg1_ref, dg2_ref, dwo_ref, dwi_ref,
             ubuf, dbuf, pbuf, ebuf, dcarry, ecarry, dkacc, dzs, dz_s, dh1_s, acc_out, acc_in, xcarry):
        i = pl.program_id(0)
        ti = nt - 1 - i
        has_past = jnp.where(ti > 0, 1.0, 0.0)

        @pl.when(i == 0)
        def _():
            dcarry[...] = jnp.zeros(dcarry.shape, F32)
            ecarry[...] = jnp.zeros(ecarry.shape, F32)
            dkacc[...] = jnp.zeros(dkacc.shape, F32)
            ds_ref[...] = jnp.zeros(ds_ref.shape, F32)
            dpw_ref[...] = jnp.zeros(dpw_ref.shape, F32)
            dg1_ref[...] = jnp.zeros(dg1_ref.shape, F32)
            dg2_ref[...] = jnp.zeros(dg2_ref.shape, F32)
            acc_out[...] = jnp.zeros(acc_out.shape, F32)
            acc_in[...] = jnp.zeros(acc_in.shape, F32)
            xcarry[...] = jnp.zeros(xcarry.shape, F32)

        xh1, r1 = _rms(h1_ref[...])
        dx1, dg2_rows = _rms_bwd(dhn2_ref[0] + dhn2_ref[1], xh1, r1, g2_ref[...])
        dh1_s[...] = dh2_ref[...] + dx1
        dg2_ref[...] += _colsum(dg2_rows)
        dh1b = dh1_s[...].astype(BF16)
        acc_out[...] += _dot_tn(m_ref[...], dh1b)
        dm = _dot(dh1b, wot_ref[...])
        z = z_ref[...]
        a = z[:, :CONV_W]
        sg = _sigmoid(z[:, CONV_W:2 * CONV_W])
        p = z[:, 2 * CONV_W:]
        zh = zh_ref[...] * has_past
        ubuf[pl.ds(0, CONV_HIST), :] = zh[:, :CONV_W] * _sigmoid(zh[:, CONV_W:2 * CONV_W])
        ubuf[pl.ds(CONV_HIST, tm), :] = a * sg
        pbuf[pl.ds(0, POOL_HIST), :] = zh[CONV_HIST - POOL_HIST:, 2 * CONV_W:]
        pbuf[pl.ds(POOL_HIST, tm), :] = p

        u1 = u1_ref[...]
        cen = u1 - _split_dot(u1, avg_ref)
        rstd = lax.rsqrt(_split_dot(cen * cen, avg_ref) + EPS)
        xhat = cen * rstd
        u2 = xhat * cv_ref[1:2, :] + cv_ref[2:3, :]
        s2 = _sigmoid(u2)
        du2 = dm[:, :CONV_W] * (s2 * (1.0 + u2 * (1.0 - s2)))
        ds_ref[1:2, :] += _colsum(du2 * xhat)
        ds_ref[2:3, :] += _colsum(du2)
        dxh = du2 * cv_ref[1:2, :]
        du1 = rstd * (dxh - _split_dot(dxh, avg_ref) - xhat * _split_dot(dxh * xhat, avg_ref))
        ds_ref[0:1, :] += _colsum(du1)
        dbuf[pl.ds(0, tm), :] = du1
        dbuf[pl.ds(tm, CONV_HIST), :] = dcarry[...]
        dcarry[...] = dbuf[pl.ds(0, CONV_HIST), :]

        def conv_block(r0):
            for l in range(CONV_W // 128):
                ls = pl.ds(128 * l, 128)
                dwin = dbuf[pl.ds(r0, rb + CONV_HIST), ls]
                uwin = ubuf[pl.ds(r0, rb + CONV_HIST), ls]
                dblk = dwin[0:rb]
                du0 = jnp.zeros((rb, 128), F32)
                for s in range(8):
                    ds_ = _shifted(dwin, s)
                    us_ = _shifted(uwin, s)
                    for q in range(CONV_HIST // 8 + 1):
                        o = 8 * q + s
                        if 0 <= CONV_TAPS - 1 - o < CONV_TAPS:
                            j = CONV_TAPS - 1 - o
                            du0 = du0 + ck_ref[j:j + 1, ls] * ds_[8 * q:8 * q + rb]
                        j = o - 2
                        if 0 <= j < CONV_TAPS:
                            prod = dblk * us_[8 * q:8 * q + rb]
                            part = prod[0:8]
                            for v in range(1, rb // 8):
                                part = part + prod[8 * v:8 * v + 8]
                            dkacc[pl.ds(8 * j, 8), ls] += part
                dzs[pl.ds(r0, rb), ls] = du0

        _for_row_blocks(tm // rb, rb, conv_block, unroll=True)
        du0 = dzs[:, 0:CONV_W]
        dz_s[:, 0:CONV_W] = (du0 * sg).astype(BF16)
        dz_s[:, CONV_W:2 * CONV_W] = (du0 * a * sg * (1.0 - sg)).astype(BF16)

        for gi, w in enumerate(POOL_WINDOWS):
            ls = pl.ds(POOL_G * gi, POOL_G)
            cols = slice(CONV_W + POOL_G * gi, CONV_W + POOL_G * (gi + 1))
            inv = _inv_count(ti, tm, w)
            s = pbuf[pl.ds(POOL_HIST, tm), ls]
            for j in range(1, w):
                s = s + pbuf[pl.ds(POOL_HIST - j, tm), ls]
            d = (s * inv - p[:, POOL_G * gi:POOL_G * (gi + 1)]).astype(BF16)
            dyp = dm[:, cols]
            ds_ref[3:4, ls] += _colsum(dyp * _dot(d, pw_ref[gi]))
            dyb = (dyp * cv_ref[3:4, ls]).astype(BF16)
            dpw_ref[gi] += _dot_tn(d, dyb)
            dd = _dot(dyb, pwt_ref[gi])
            ebuf[pl.ds(0, tm), ls] = dd * inv
            ebuf[pl.ds(tm, POOL_HIST), ls] = ecarry[:, ls]
            dp = ebuf[pl.ds(0, tm), ls] - dd
            for j in range(1, w):
                dp = dp + ebuf[pl.ds(j, tm), ls]
            dz_s[:, pl.ds(2 * CONV_W + POOL_G * gi, POOL_G)] = dp.astype(BF16)
        ecarry[...] = ebuf[pl.ds(0, POOL_HIST), :]

        acc_in[...] += _dot_tn(hn1_ref[...], dz_s[...])
        dhn = _dot(dz_s[...], wit_ref[...])
        xh, r = _rms(h0_ref[...])
        dx, dg_rows = _rms_bwd(dhn, xh, r, g1_ref[...])
        dh0 = dh1_s[...] + dx
        if x_rows is None:
            dh0_ref[...] = dh0
        else:
            dh0_ref[0:tm - N_META, :] = dh0[N_META:]
            dh0_ref[tm - N_META:tm, :] = xcarry[...]
            xcarry[...] = dh0[0:N_META]
        dg1_ref[...] += _colsum(dg_rows)

        @pl.when(i == nt - 1)
        def _():
            dtop_ref[...] = dh0[0:N_META]
            for j in range(CONV_TAPS):
                dk_ref[j:j + 1, :] = _colsum(dkacc[pl.ds(8 * j, 8), :])
            dk_ref[CONV_TAPS:CONV_HIST, :] = jnp.zeros((CONV_HIST - CONV_TAPS, CONV_W), F32)
            pltpu.sync_copy(acc_out, dwo_ref)
            for k in range(4):
                pltpu.sync_copy(acc_in.at[:, pl.ds(in_shard * k, in_shard)], dwi_ref.at[k])

    def tile(c):
        return pl.BlockSpec((tm, c), lambda i: (nt - 1 - i, 0))

    halo = pl.BlockSpec((CONV_HIST, IN_COLS), lambda i: (jnp.maximum((nt - 1 - i) * (tm // CONV_HIST) - 1, 0), 0))
    vec = jax.ShapeDtypeStruct((1, D_MODEL), F32)
    if x_rows is None:
        first_spec, first_shape = tile(D_MODEL), jax.ShapeDtypeStruct((rows, D_MODEL), F32)
    else:
        x_tiles = -(-x_rows // tm)
        first_spec = pl.BlockSpec((tm, D_MODEL), lambda i: (jnp.minimum(nt - 1 - i, x_tiles - 1), 0))
        first_shape = jax.ShapeDtypeStruct((x_rows, D_MODEL), F32)
    return pl.pallas_call(
        body, name="mixer_bwd", grid=(nt,),
        in_specs=[tile(D_MODEL), pl.BlockSpec((2, tm, D_MODEL), lambda i: (0, nt - 1 - i, 0)), tile(D_MODEL),
                  _whole((1, D_MODEL)), tile(D_MODEL), tile(IN_COLS), halo, tile(CONV_W), tile(D_MODEL), tile(D_MODEL),
                  _whole((D_MODEL, D_MODEL)), _whole((IN_COLS, D_MODEL)), _whole((CONV_HIST, CONV_W)), _whole((8, CONV_W)),
                  _whole((CONV_W, CONV_W)), _whole((4, POOL_G, POOL_G)), _whole((4, POOL_G, POOL_G)), _whole((1, D_MODEL))],
        out_specs=[first_spec, _whole((N_META, D_MODEL)), _whole((CONV_HIST, CONV_W)), _whole((8, CONV_W)),
                   _whole((4, POOL_G, POOL_G)), _whole((1, D_MODEL)), _whole((1, D_MODEL)), _ANY, _ANY],
        out_shape=[first_shape, jax.ShapeDtypeStruct((N_META, D_MODEL), F32), jax.ShapeDtypeStruct((CONV_HIST, CONV_W), F32),
                   jax.ShapeDtypeStruct((8, CONV_W), F32), jax.ShapeDtypeStruct((4, POOL_G, POOL_G), F32), vec, vec,
                   jax.ShapeDtypeStruct((D_MODEL, D_MODEL), F32), jax.ShapeDtypeStruct((4, D_MODEL, in_shard), F32)],
        scratch_shapes=[pltpu.VMEM((CONV_HIST + tm, CONV_W), F32), pltpu.VMEM((tm + CONV_HIST, CONV_W), F32),
                        pltpu.VMEM((POOL_HIST + tm, POOL_W), F32), pltpu.VMEM((tm + POOL_HIST, POOL_W), F32),
                        pltpu.VMEM((CONV_HIST, CONV_W), F32), pltpu.VMEM((POOL_HIST, POOL_W), F32),
                        pltpu.VMEM((8 * CONV_HIST, CONV_W), F32), pltpu.VMEM((tm, CONV_W), F32),
                        pltpu.VMEM((tm, IN_COLS), BF16), pltpu.VMEM((tm, D_MODEL), F32),
                        pltpu.VMEM((D_MODEL, D_MODEL), F32), pltpu.VMEM((D_MODEL, IN_COLS), F32),
                        pltpu.VMEM((N_META, D_MODEL), F32)],
        compiler_params=_cparams(1, V7X_VMEM_LIMIT),
    )(dh2, dhn2, h1, g2, h0, z, z, u1, hn1, m, w_out_t, w_in_t, conv_k, conv_v, avg, pool_w, pool_w_t, g1)


def _head_average():
    head = lax.broadcasted_iota(jnp.int32, (CONV_W, CONV_W), 0) // 64
    return jnp.where(head == head.T, 1.0 / 64, 0.0).astype(BF16)


def _local_step(x, meta, tgt, layers, ffn_weights, ffn_grads_done, final_g, tm, tm_ffn):
    avg = _head_average()
    depth = len(layers)
    saved = []
    seq = x.shape[0]
    step = math.lcm(tm, tm_ffn)
    rows = -(-(N_META + seq) // step) * step
    h, hn = _embed(x, meta, layers[0]["g1"], rows, tm)
    for l, w in enumerate(layers):
        g_next = layers[l + 1]["g1"] if l + 1 < depth else final_g
        z, u1, m, h1, hn2 = _mixer_fwd(h, hn, w["w_in"], w["conv_k"], w["conv_v"], avg, w["pool_w"], w["w_out"], w["g2"], tm)
        if l == 0:
            w_up_all, ffn = ffn_weights(hn2)
        ug0, h2, hn_next = _ffn_fwd(h1, hn2, w_up_all, l, w["kf"], ffn[l]["w_down"], g_next, tm_ffn)
        saved.append((h, hn, z, u1, m, h1, hn2, ug0))
        h, hn = h2, hn_next
    dh, loss_cols, dfinal_g = _loss_head(h, tgt, final_g, seq, tm)

    grads = [None] * depth
    ffn_grads = [None] * depth
    for l in reversed(range(depth)):
        w = layers[l]
        h0, hn1, z, u1, m, h1, hn2, ug0 = saved[l]
        dhn2, dkf, dw_up, dw_down = _ffn_bwd(dh, hn2, ug0, ffn[l]["w_down_t"], ffn[l]["w_up_t"], w["kf"], tm)
        ffn_grads[l] = (dw_up.transpose(1, 0, 2, 3).reshape(4, D_MODEL, FF_CHUNK), dw_down.reshape(4, D_FF // 4, D_MODEL))
        g2 = w["g2"] + ffn_grads_done(ffn_grads) if l == 0 else w["g2"]
        dh0, dtop, dk, dsmall, dpw, dg1, dg2, dw_out, dw_in = _mixer_bwd(
            dh, dhn2, h1, g2, h0, z, u1, hn1, m, w["w_out_t"], w["w_in_t"], w["conv_k"], w["conv_v"], avg,
            w["pool_w"], w["pool_w_t"], w["g1"], tm, x_rows=seq if l == 0 else None)
        grads[l] = dict(dw_in=dw_in, dw_out=dw_out.reshape(4, D_MODEL // 4, D_MODEL), dk=dk, dsmall=dsmall, dpw=dpw,
                        dg1=dg1, dg2=dg2, dkf=dkf)
        dh = dh0
    return loss_cols, dh, dtop, grads, dfinal_g


_ANY = pl.BlockSpec(memory_space=pl.ANY)


def _place():
    x, y, c = lax.axis_index("x"), lax.axis_index("y"), lax.axis_index("c")
    chips = [(1 - x, y), (x, 1 - y), (1 - x, 1 - y)]
    return x, y, c, chips


def _remote(src, dst, send_sems, recv_sems, idx, to):
    return pltpu.make_async_remote_copy(src_ref=src, dst_ref=dst, send_sem=send_sems.at[idx], recv_sem=recv_sems.at[idx],
                                        device_id=to, device_id_type=MESH)


def _gather_chips(xs):
    n = len(xs)

    def body(*refs):
        x_refs, o_refs = refs[:n], refs[n:2 * n]
        send_sems, recv_sems = refs[2 * n:]
        x, y, c, chips = _place()
        k = 2 * x + y
        sibling = (x, y, 1 - c)
        sends = []
        for j, chip in enumerate(chips):
            for a in range(n):
                sends.append(_remote(x_refs[a].at[k, c], o_refs[a].at[k, c], send_sems, recv_sems, 3 * a + j, (*chip, c)))
                sends[-1].start()
        for j, chip in enumerate(chips):
            kj = 2 * chip[0] + chip[1]
            for a in range(n):
                landed = o_refs[a].at[kj, c]
                _remote(landed, landed, send_sems, recv_sems, 3 * a + j, sibling).wait_recv()
                sends.append(_remote(landed, landed, send_sems, recv_sems, 3 * n + 3 * a + j, sibling))
                sends[-1].start()
        for j, chip in enumerate(chips):
            kj = 2 * chip[0] + chip[1]
            for a in range(n):
                passed = o_refs[a].at[kj, 1 - c]
                _remote(passed, passed, send_sems, recv_sems, 3 * n + 3 * a + j, sibling).wait_recv()
        for cp in sends:
            cp.wait_send()

    return pl.pallas_call(
        body, name="gather_chips", in_specs=[_ANY] * n, out_specs=[_ANY] * n,
        out_shape=[jax.ShapeDtypeStruct(v.shape, v.dtype) for v in xs],
        input_output_aliases={a: a for a in range(n)},
        scratch_shapes=[pltpu.SemaphoreType.DMA((6 * n,)), pltpu.SemaphoreType.DMA((6 * n,))],
    )(*xs)


_HBM = pl.BlockSpec(memory_space=pltpu.HBM)
_SEM = pl.BlockSpec(memory_space=pltpu.SEMAPHORE)
_SPLIT_COPY = pltpu.CompilerParams(has_side_effects=pltpu.SideEffectType.DATAFLOW_SIDE_EFFECTING)


def _gather_start(xs, after):
    n = len(xs)

    def body(*refs):
        x_refs = refs[:n]
        send_sems, recv_sems = refs[n + 1], refs[n + 2]
        token = refs[2 * n + 3]
        x, y, c, chips = _place()
        k = 2 * x + y
        for j, chip in enumerate(chips):
            for a in range(n):
                mine = x_refs[a].at[k, c]
                _remote(mine, mine, send_sems, recv_sems, 3 * a + j, (*chip, c)).start()
        token[...] = jnp.zeros(token.shape, F32)

    return pl.pallas_call(
        body, name="gather_start", in_specs=[_HBM] * n + [_ANY],
        out_specs=(_SEM, _SEM, *[_HBM] * n, pl.BlockSpec(memory_space=pltpu.VMEM)),
        out_shape=(pltpu.SemaphoreType.DMA((3 * n,)), pltpu.SemaphoreType.DMA((3 * n,)),
                   *[pltpu.HBM(v.shape, v.dtype) for v in xs], jax.ShapeDtypeStruct((8, 128), F32)),
        input_output_aliases={a: 2 + a for a in range(n)}, compiler_params=_SPLIT_COPY,
    )(*[pltpu.with_memory_space_constraint(v, pltpu.HBM) for v in xs], after)


def _gather_wait(send_sems, recv_sems, xs, after):
    n = len(xs)

    def body(*refs):
        x_refs = refs[:n]
        send_sems, recv_sems = refs[n], refs[n + 1]
        x, y, c, chips = _place()
        k = 2 * x + y
        for j, chip in enumerate(chips):
            kj = 2 * chip[0] + chip[1]
            for a in range(n):
                cp = _remote(x_refs[a].at[k, c], x_refs[a].at[kj, c], send_sems, recv_sems, 3 * a + j, (*chip, c))
                cp.wait_send()
                cp.wait_recv()

    return pl.pallas_call(
        body, name="gather_wait", in_specs=[_HBM] * n + [_SEM, _SEM, _ANY], out_specs=[_HBM] * n,
        out_shape=[pltpu.HBM(v.shape, v.dtype) for v in xs],
        input_output_aliases={a: a for a in range(n)}, compiler_params=_SPLIT_COPY,
    )(*xs, send_sems, recv_sems, after)


def _gather_forward(xs):
    n = len(xs)

    def body(*refs):
        x_refs, o_refs = refs[:n], refs[n:2 * n]
        send_sems, recv_sems = refs[2 * n:]
        x, y, c, chips = _place()
        sibling = (x, y, 1 - c)
        sends = []
        for j, chip in enumerate(chips):
            kj = 2 * chip[0] + chip[1]
            for a in range(n):
                landed = x_refs[a].at[kj, c]
                sends.append(_remote(landed, o_refs[a].at[kj, c], send_sems, recv_sems, 3 * a + j, sibling))
                sends[-1].start()
        for j, chip in enumerate(chips):
            kj = 2 * chip[0] + chip[1]
            for a in range(n):
                passed = o_refs[a].at[kj, 1 - c]
                _remote(passed, passed, send_sems, recv_sems, 3 * a + j, sibling).wait_recv()
        for cp in sends:
            cp.wait_send()

    return pl.pallas_call(
        body, name="gather_forward", in_specs=[_ANY] * n, out_specs=[_ANY] * n,
        out_shape=[jax.ShapeDtypeStruct(v.shape, v.dtype) for v in xs],
        input_output_aliases={a: a for a in range(n)},
        scratch_shapes=[pltpu.SemaphoreType.DMA((3 * n,)), pltpu.SemaphoreType.DMA((3 * n,))],
    )(*xs)


def _pair_exchange(g0s, g1s, name):
    n = len(g0s)

    def body(*refs):
        g0, g1, out = refs[:n], refs[n:2 * n], refs[2 * n:3 * n]
        send_sems, recv_sems = refs[3 * n:]
        x, y, c, _ = _place()
        sibling = (x, y, 1 - c)
        for a in range(n):
            @pl.when(c == 0)
            def _():
                _remote(g1[a], out[a], send_sems, recv_sems, a, sibling).start()

            @pl.when(c == 1)
            def _():
                _remote(g0[a], out[a], send_sems, recv_sems, a, sibling).start()
        for a in range(n):
            cp = _remote(g0[a], out[a], send_sems, recv_sems, a, sibling)
            cp.wait_recv()
            cp.wait_send()

    return pl.pallas_call(
        body, name=name, in_specs=[_ANY] * (2 * n), out_specs=[_ANY] * n,
        out_shape=[jax.ShapeDtypeStruct(v.shape, v.dtype) for v in g0s],
        scratch_shapes=[pltpu.SemaphoreType.DMA((n,)), pltpu.SemaphoreType.DMA((n,))],
    )(*g0s, *g1s)


def _chip_exchange(parts, name):
    n = len(parts)

    def body(*refs):
        p_refs, o_refs = refs[:n], refs[n:2 * n]
        send_sems, recv_sems = refs[2 * n:]
        x, y, c, chips = _place()
        sends = []
        for j, chip in enumerate(chips):
            kj = 2 * chip[0] + chip[1]
            for a in range(n):
                sends.append(_remote(p_refs[a].at[kj], o_refs[a].at[j], send_sems, recv_sems, 3 * a + j, (*chip, c)))
                sends[-1].start()
        for j, chip in enumerate(chips):
            for a in range(n):
                landed = o_refs[a].at[j]
                _remote(landed, landed, send_sems, recv_sems, 3 * a + j, (*chip, c)).wait_recv()
        for cp in sends:
            cp.wait_send()

    return pl.pallas_call(
        body, name=name, in_specs=[_ANY] * n, out_specs=[_ANY] * n,
        out_shape=[jax.ShapeDtypeStruct((3,) + v.shape[1:], v.dtype) for v in parts],
        scratch_shapes=[pltpu.SemaphoreType.DMA((3 * n,)), pltpu.SemaphoreType.DMA((3 * n,))],
    )(*parts)


def _chip_exchange_start(parts):
    n = len(parts)
    lands = [lax.empty((3,) + v.shape[1:], v.dtype) for v in parts]

    def body(*refs):
        p_refs, l_refs = refs[:n], refs[n:2 * n]
        send_sems, recv_sems = refs[2 * n], refs[2 * n + 1]
        token = refs[4 * n + 2]
        x, y, c, chips = _place()
        for j, chip in enumerate(chips):
            kj = 2 * chip[0] + chip[1]
            for a in range(n):
                _remote(p_refs[a].at[kj], l_refs[a].at[j], send_sems, recv_sems, 3 * a + j, (*chip, c)).start()
        token[...] = jnp.zeros(token.shape, F32)

    hbm = [pltpu.with_memory_space_constraint(v, pltpu.HBM) for v in parts + lands]
    return pl.pallas_call(
        body, name="chip_exchange_start", in_specs=[_HBM] * (2 * n),
        out_specs=(_SEM, _SEM, *[_HBM] * (2 * n), pl.BlockSpec(memory_space=pltpu.VMEM)),
        out_shape=(pltpu.SemaphoreType.DMA((3 * n,)), pltpu.SemaphoreType.DMA((3 * n,)),
                   *[pltpu.HBM(v.shape, v.dtype) for v in parts + lands], jax.ShapeDtypeStruct((8, 128), F32)),
        input_output_aliases={a: 2 + a for a in range(2 * n)}, compiler_params=_SPLIT_COPY,
    )(*hbm)


def _chip_exchange_wait(send_sems, recv_sems, parts, lands, after):
    n = len(parts)

    def body(*refs):
        p_refs, l_refs = refs[:n], refs[n:2 * n]
        send_sems, recv_sems = refs[2 * n], refs[2 * n + 1]
        x, y, c, chips = _place()
        for j, chip in enumerate(chips):
            kj = 2 * chip[0] + chip[1]
            for a in range(n):
                cp = _remote(p_refs[a].at[kj], l_refs[a].at[j], send_sems, recv_sems, 3 * a + j, (*chip, c))
                cp.wait_send()
                cp.wait_recv()

    return pl.pallas_call(
        body, name="chip_exchange_wait", in_specs=[_HBM] * (2 * n) + [_SEM, _SEM, _ANY], out_specs=[_HBM] * (2 * n),
        out_shape=[pltpu.HBM(v.shape, v.dtype) for v in parts + lands],
        input_output_aliases={a: a for a in range(2 * n)}, compiler_params=_SPLIT_COPY,
    )(*parts, *lands, send_sems, recv_sems, after)[n:]


def _pair_share(reds):
    n = len(reds)

    def body(*refs):
        r_refs, o_refs = refs[:n], refs[n:2 * n]
        send_sems, recv_sems = refs[2 * n:]
        x, y, c, _ = _place()
        sends = [_remote(r_refs[a], o_refs[a], send_sems, recv_sems, a, (x, y, 1 - c)) for a in range(n)]
        for cp in sends:
            cp.start()
        for cp in sends:
            cp.wait_recv()
        for cp in sends:
            cp.wait_send()

    return pl.pallas_call(
        body, name="pair_share", in_specs=[_ANY] * n, out_specs=[_ANY] * n,
        out_shape=[jax.ShapeDtypeStruct(v.shape, v.dtype) for v in reds],
        scratch_shapes=[pltpu.SemaphoreType.DMA((n,)), pltpu.SemaphoreType.DMA((n,))],
    )(*reds)


def _all_reduce_small(pack):
    p, cols = pack.shape
    half = p // 2

    def body(x_ref, o_ref, sib, chipbuf, send_sems, recv_sems):
        x, y, c, chips = _place()
        k = 2 * x + y
        sibling = (x, y, 1 - c)
        mine = pl.ds(pl.multiple_of(c * half, 8), half)
        other = pl.ds(pl.multiple_of((1 - c) * half, 8), half)
        pair = _remote(x_ref.at[other], sib, send_sems, recv_sems, 0, sibling)
        pair.start()
        pair.wait_recv()
        chipbuf[k] = x_ref[mine, :] + sib[...]
        sends = [_remote(chipbuf.at[k], chipbuf.at[k], send_sems, recv_sems, 1 + j, (*chip, c)) for j, chip in enumerate(chips)]
        for cp in sends:
            cp.start()
        for j, chip in enumerate(chips):
            landed = chipbuf.at[2 * chip[0] + chip[1]]
            _remote(landed, landed, send_sems, recv_sems, 1 + j, (*chip, c)).wait_recv()
        o_ref[mine, :] = (chipbuf[0] + chipbuf[1]) + (chipbuf[2] + chipbuf[3])
        share = _remote(o_ref.at[mine], o_ref.at[mine], send_sems, recv_sems, 4, sibling)
        share.start()
        _remote(o_ref.at[other], o_ref.at[other], send_sems, recv_sems, 4, sibling).wait_recv()
        for cp in [pair, share] + sends:
            cp.wait_send()

    vm = pl.BlockSpec(memory_space=pltpu.VMEM)
    return pl.pallas_call(
        body, name="all_reduce_small", in_specs=[vm], out_specs=vm,
        out_shape=jax.ShapeDtypeStruct(pack.shape, F32),
        scratch_shapes=[pltpu.VMEM((half, cols), F32), pltpu.VMEM((4, half, cols), F32),
                        pltpu.SemaphoreType.DMA((5,)), pltpu.SemaphoreType.DMA((5,))],
    )(pack)


_BIG = ("w_in", "w_out", "w_up", "w_down")
_SMALL = ("norm1_g", "conv_dw_b", "conv_ln_g", "conv_ln_b", "pool_w", "pool_scale", "norm2_g", "final_g",
          "meta_tokens", "conv_dw_k", "ffn_dw_k")


def _rows8(v):
    return jnp.pad(v, ((0, -v.shape[0] % 8), (0, 0)))


def _pack_flat(arrs, rows):
    flat = jnp.concatenate([a.reshape(-1) for a in arrs])
    return jnp.pad(flat, (0, rows * D_MODEL - flat.shape[0])).reshape(1, rows, D_MODEL)


def _unpack_flat(packed, like):
    flat = packed.reshape(-1)
    out, off = [], 0
    for a in like:
        out.append(flat[off:off + a.size].reshape(a.shape))
        off += a.size
    return out


def kernel(x, meta_tokens, norm1_g, w_in, conv_dw_k, conv_dw_b, conv_ln_g, conv_ln_b, pool_w, pool_scale, w_out, norm2_g, w_up, ffn_dw_k, w_down, final_g, loss_target, m_meta_tokens, m_norm1_g, m_w_in, m_conv_dw_k, m_conv_dw_b, m_conv_ln_g, m_conv_ln_b, m_pool_w, m_pool_scale, m_w_out, m_norm2_g, m_w_up, m_ffn_dw_k, m_w_down, m_final_g, v_meta_tokens, v_norm1_g, v_w_in, v_conv_dw_k, v_conv_dw_b, v_conv_ln_g, v_conv_ln_b, v_pool_w, v_pool_scale, v_w_out, v_norm2_g, v_w_up, v_ffn_dw_k, v_w_down, v_final_g):
    weights = dict(meta_tokens=meta_tokens, norm1_g=norm1_g, w_in=w_in, conv_dw_k=conv_dw_k, conv_dw_b=conv_dw_b,
                   conv_ln_g=conv_ln_g, conv_ln_b=conv_ln_b, pool_w=pool_w, pool_scale=pool_scale, w_out=w_out,
                   norm2_g=norm2_g, w_up=w_up, ffn_dw_k=ffn_dw_k, w_down=w_down, final_g=final_g)
    mom1 = dict(meta_tokens=m_meta_tokens, norm1_g=m_norm1_g, w_in=m_w_in, conv_dw_k=m_conv_dw_k, conv_dw_b=m_conv_dw_b,
                conv_ln_g=m_conv_ln_g, conv_ln_b=m_conv_ln_b, pool_w=m_pool_w, pool_scale=m_pool_scale, w_out=m_w_out,
                norm2_g=m_norm2_g, w_up=m_w_up, ffn_dw_k=m_ffn_dw_k, w_down=m_w_down, final_g=m_final_g)
    mom2 = dict(meta_tokens=v_meta_tokens, norm1_g=v_norm1_g, w_in=v_w_in, conv_dw_k=v_conv_dw_k, conv_dw_b=v_conv_dw_b,
                conv_ln_g=v_conv_ln_g, conv_ln_b=v_conv_ln_b, pool_w=v_pool_w, pool_scale=v_pool_scale, w_out=v_w_out,
                norm2_g=v_norm2_g, w_up=v_w_up, ffn_dw_k=v_ffn_dw_k, w_down=v_w_down, final_g=v_final_g)
    order = list(weights)
    depth = w_in.shape[0]
    seq = x.shape[1]
    chip = 2 * lax.axis_index("x") + lax.axis_index("y")
    core = lax.axis_index("c")
    chip_arr = chip.astype(jnp.int32).reshape(1)
    core_arr = core.astype(jnp.int32).reshape(1)

    small_sharded = dict(conv_dw_k=jnp.pad(conv_dw_k, ((0, 0), (0, CONV_HIST - CONV_TAPS), (0, 0))),
                         ffn_dw_k=jnp.pad(ffn_dw_k, ((0, 0), (0, 8 - FFN_TAPS), (0, 0))),
                         meta_tokens=meta_tokens.reshape(2, N_META // 2, D_MODEL // 4))
    placed = {nm: _place_own(chip_arr, weights[nm], BF16, "place_" + nm) for nm in _BIG}
    g_in, g_out, g_cdk, g_fdk, g_meta = _gather_chips(
        [placed["w_in"], placed["w_out"]] + [_place_own(chip_arr, v, F32, "place_" + nm) for nm, v in small_sharded.items()])
    send_sems, recv_sems, up_buf, down_buf, token = _gather_start([placed["w_up"], placed["w_down"]], g_meta)
    meta_full = g_meta.transpose(1, 2, 0, 3).reshape(N_META, D_MODEL)
    layers = []
    for l in range(depth):
        w_in_l = g_in[:, l].transpose(1, 0, 2).reshape(D_MODEL, IN_COLS)
        w_out_l = g_out[:, l].reshape(D_MODEL, D_MODEL)
        pw = pool_w[l].astype(BF16)
        conv_v = jnp.pad(jnp.stack([conv_dw_b[l], conv_ln_g[l], conv_ln_b[l], pool_scale[l]]), ((0, 4), (0, 0)))
        layers.append(dict(
            w_in=w_in_l, w_in_t=w_in_l.T, conv_k=g_cdk[:, l].transpose(1, 0, 2).reshape(CONV_HIST, CONV_W), conv_v=conv_v,
            pool_w=pw, pool_w_t=pw.transpose(0, 2, 1), w_out=w_out_l, w_out_t=w_out_l.T, kf=g_fdk[:, l],
            g1=norm1_g[l][None], g2=norm2_g[l][None]))
    layers[0]["g1"] = layers[0]["g1"] + token[0, 0]

    def ffn_weights(after):
        g_up, g_down = _gather_forward(_gather_wait(send_sems, recv_sems, [up_buf, down_buf], after))
        per_layer = []
        for l in range(depth):
            w_down_l = g_down[:, l].reshape(2, FF_CHUNK, D_MODEL)
            per_layer.append(dict(w_up_t=g_up[:, l].transpose(0, 2, 1), w_down=w_down_l, w_down_t=w_down_l.transpose(0, 2, 1)))
        return g_up, per_layer

    def pair_reduced(names, g0s, g1s, tag):
        theirs = _pair_exchange(g0s, g1s, "pair_exchange_" + tag)
        return [_pair_sum_bf16(core_arr, g0s[a], g1s[a], theirs[a], "pair_sum_" + nm) for a, nm in enumerate(names)]

    in_flight = {}

    def ffn_grads_done(ffn_grads):
        parts = pair_reduced(("w_up", "w_down"), list(ffn_grads[0]), list(ffn_grads[1]), "ffn")
        send, recv, up_parts, down_parts, up_land, down_land, zero = _chip_exchange_start(parts)
        in_flight.update(sems=(send, recv), parts=[up_parts, down_parts], lands=[up_land, down_land])
        return zero[0, 0]

    loss_cols, dx, dmeta, grads, dfinal_g = _local_step(x[0], meta_full, loss_target[0], layers, ffn_weights, ffn_grads_done,
                                                         final_g[None], ROW_TILE, FFN_ROW_TILE)
    grad_x = dx[None]
    mixer_parts = pair_reduced(("w_in", "w_out"), [grads[0]["dw_in"], grads[0]["dw_out"]],
                               [grads[1]["dw_in"], grads[1]["dw_out"]], "mixer")
    parts = mixer_parts + in_flight["parts"]
    received = list(_chip_exchange(mixer_parts, "chip_exchange_mixer")) + list(_chip_exchange_wait(
        *in_flight["sems"], in_flight["parts"], in_flight["lands"], dx))
    reds = [_chip_sum(chip_arr, parts[a], received[a], "chip_sum_" + _BIG[a]) for a in range(4)]
    grad, delta, new_m, new_v = {}, {}, {}, {}
    for a, other in enumerate(_pair_share(reds)):
        nm = _BIG[a]
        grad[nm], delta[nm], new_m[nm], new_v[nm] = _adamw_pair(core_arr, weights[nm], reds[a], other, mom1[nm], mom2[nm],
                                                                "adamw_" + nm)

    pack = jnp.concatenate([
        _rows8(jnp.concatenate([grads[l]["dg1"] for l in range(depth)])),
        _rows8(jnp.concatenate([grads[l]["dg2"] for l in range(depth)])),
        _rows8(jnp.concatenate([dfinal_g, loss_cols])),
        jnp.concatenate([grads[l]["dsmall"] for l in range(depth)], axis=1),
        jnp.stack([grads[l]["dpw"] for l in range(depth)]).reshape(-1, D_MODEL),
        dmeta,
        jnp.concatenate([grads[l]["dk"] for l in range(depth)], axis=1),
        jnp.stack([grads[l]["dkf"] for l in range(depth)]).reshape(-1, D_MODEL),
    ])
    red = _all_reduce_small(jnp.pad(pack, ((0, -pack.shape[0] % 16), (0, 0))))
    o = 0
    grad["norm1_g"] = red[o:o + depth]
    o += 8
    grad["norm2_g"] = red[o:o + depth]
    o += 8
    grad["final_g"] = red[o]
    loss = red[o + 1, 0]
    o += 8
    sm = red[o:o + 8].reshape(8, depth, CONV_W)
    grad["conv_dw_b"], grad["conv_ln_g"], grad["conv_ln_b"], grad["pool_scale"] = sm[0], sm[1], sm[2], sm[3]
    o += 8
    n_pw = depth * 4 * POOL_G * POOL_G // D_MODEL
    grad["pool_w"] = red[o:o + n_pw].reshape(pool_w.shape)
    o += n_pw
    grad["meta_tokens"] = lax.dynamic_slice_in_dim(red[o:o + N_META], chip * (D_MODEL // 4), D_MODEL // 4, axis=1)
    o += N_META
    dk_all = red[o:o + CONV_HIST].reshape(CONV_HIST, depth, 4, CONV_W // 4)
    grad["conv_dw_k"] = lax.dynamic_index_in_dim(dk_all, chip, axis=2, keepdims=False)[:CONV_TAPS].transpose(1, 0, 2)
    o += CONV_HIST
    dkf_all = red[o:pack.shape[0]].reshape(depth, 4, 8, FF_CHUNK)
    grad["ffn_dw_k"] = lax.dynamic_index_in_dim(dkf_all, chip, axis=1, keepdims=False)[:, :FFN_TAPS]

    small_rows = -(-sum(weights[nm].size for nm in _SMALL) // (8 * D_MODEL)) * 8
    packed = [_pack_flat([d[nm] for nm in _SMALL], small_rows) for d in (weights, grad, mom1, mom2)]
    for res, packed_out in zip((delta, new_m, new_v), _adamw(*packed, "adamw_small")):
        for nm, val in zip(_SMALL, _unpack_flat(packed_out, [weights[nm] for nm in _SMALL])):
            res[nm] = val

    return (loss, grad_x, *[grad[nm] for nm in order], *[delta[nm] for nm in order],
            *[new_m[nm] for nm in order], *[new_v[nm] for nm in order])
```

```python
import math

import jax
import jax.numpy as jnp
from jax import lax
from jax.experimental import pallas as pl
from jax.experimental.pallas import tpu as pltpu

F32 = jnp.float32
BF16 = jnp.bfloat16

D_MODEL = 1024
CONV_W = 512
POOL_W = 512
POOL_G = 128
POOL_WINDOWS = (2, 4, 8, 16)
IN_COLS = 1536
D_FF = 2816
FF_CHUNK = 1408
CONV_TAPS = 31
CONV_HIST = 32
POOL_HIST = 16
FFN_TAPS = 3
N_META = 16
EPS = 1e-6

ADAM_LR = 0.001
ADAM_B1 = 0.9
ADAM_B2 = 0.999
ADAM_EPS = 1e-08
ADAM_WD = 0.01
ADAM_STEP = 10

ROW_TILE = 256
FFN_ROW_TILE = 384
CONV_ROW_BLOCK = 64
FFN_ROW_BLOCK = 32
V7X_VMEM_LIMIT = 56 * 1024 * 1024

MESH = pl.DeviceIdType.MESH


def _cparams(n_axes, vmem=None):
    return pltpu.CompilerParams(dimension_semantics=("arbitrary",) * n_axes, vmem_limit_bytes=vmem)


def _whole(shape, single=False):
    zeros = (0,) * len(shape)
    if single:
        return pl.BlockSpec(shape, lambda *_: zeros, pipeline_mode=pl.Buffered(1))
    return pl.BlockSpec(shape, lambda *_: zeros)


def _sigmoid(x):
    return 0.5 * jnp.tanh(0.5 * x) + 0.5


def _dot(a, b):
    return jnp.dot(a, b, preferred_element_type=F32)


def _dot_nt(a, b):
    return lax.dot_general(a, b, (((1,), (1,)), ((), ())), preferred_element_type=F32)


def _dot_tn(a, b):
    return lax.dot_general(a, b, (((0,), (0,)), ((), ())), preferred_element_type=F32)


def _split_dot(v, a_ref):
    hi = v.astype(BF16)
    lo = (v - hi.astype(F32)).astype(BF16)
    return _dot(hi, a_ref[...]) + _dot(lo, a_ref[...])


def _rms(x):
    r = lax.rsqrt(jnp.mean(x * x, axis=-1, keepdims=True) + EPS)
    return x * r, r


def _rms_bwd(dy, xhat, r, g):
    gd = dy * g
    return r * (gd - xhat * jnp.mean(gd * xhat, axis=-1, keepdims=True)), dy * xhat


def _colsum(v):
    return jnp.sum(v, axis=0, keepdims=True)


def _shifted(window, s):
    return window if s == 0 else pltpu.roll(window, window.shape[0] - s, 0)


def _conv3(window, kf_ref, cc, ls, hist):
    x2 = window[hist:]
    x1 = pltpu.roll(window, 1, 0)[hist:]
    x0 = pltpu.roll(window, 2, 0)[hist:]
    return x0, x1, x2, kf_ref[cc, 0:1, ls] * x0 + kf_ref[cc, 1:2, ls] * x1 + kf_ref[cc, 2:3, ls] * x2


def _for_row_blocks(n, rb, fn, unroll):
    if unroll:
        for r in range(n):
            fn(r * rb)
    else:
        def step(r, keep):
            fn(pl.multiple_of(r * rb, rb))
            return keep

        lax.fori_loop(0, n, step, 0)


def _fold8(v):
    part = v[0:8]
    for k in range(1, v.shape[0] // 8):
        part = part + v[8 * k:8 * k + 8]
    return part


def _inv_count(tile, tm, w):
    t = tile * tm + lax.broadcasted_iota(jnp.int32, (tm, POOL_G), 0)
    return 1.0 / jnp.minimum(t + 1, w).astype(F32)


def _embed(x, meta, g, rows, tm):
    seq = x.shape[0]
    x_tiles = -(-seq // tm)

    def body(xprev_ref, x_ref, meta_ref, g_ref, h_ref, hn_ref):
        i = pl.program_id(0)
        t = i * tm + lax.broadcasted_iota(jnp.int32, (tm, 1), 0)
        head = jnp.where(i == 0, meta_ref[...], xprev_ref[...])
        h = jnp.concatenate([head, x_ref[0:tm - N_META, :]], axis=0)
        h = jnp.where(t < N_META + seq, h, 0.0)
        h_ref[...] = h
        xhat, _ = _rms(h)
        hn_ref[...] = (xhat * g_ref[...]).astype(BF16)

    tile = pl.BlockSpec((tm, D_MODEL), lambda i: (i, 0))
    x_prev = pl.BlockSpec((N_META, D_MODEL), lambda i: (jnp.maximum(i * (tm // N_META) - 1, 0), 0))
    x_own = pl.BlockSpec((tm, D_MODEL), lambda i: (jnp.minimum(i, x_tiles - 1), 0))
    return pl.pallas_call(
        body, name="embed", grid=(rows // tm,),
        in_specs=[x_prev, x_own, _whole((N_META, D_MODEL)), _whole((1, D_MODEL))],
        out_specs=[tile, tile],
        out_shape=[jax.ShapeDtypeStruct((rows, D_MODEL), F32), jax.ShapeDtypeStruct((rows, D_MODEL), BF16)],
        compiler_params=_cparams(1),
    )(x, x, meta, g)


def _row_block(r):
    for cand in (256, 176, 128, 64, 32, 16):
        if r % cand == 0:
            return cand
    return r


def _place_own(chip, w, dtype, name):
    n, r, c = w.shape
    rb = _row_block(r)

    def body(chip_ref, w_ref, o_ref):
        o_ref[0] = w_ref[...].astype(dtype)

    return pl.pallas_call(
        body, name=name,
        grid_spec=pltpu.PrefetchScalarGridSpec(
            num_scalar_prefetch=1, grid=(n, r // rb),
            in_specs=[pl.BlockSpec((1, rb, c), lambda i, j, chip_ref: (i, j, 0))],
            out_specs=pl.BlockSpec((1, 1, rb, c), lambda i, j, chip_ref: (chip_ref[0], i, j, 0))),
        out_shape=jax.ShapeDtypeStruct((4,) + w.shape, dtype), compiler_params=_cparams(2),
    )(chip, w)


def _pair_sum_bf16(core, g0, g1, other, name):
    n, r, c = g0.shape
    rb = _row_block(r)

    def body(core_ref, g0_ref, g1_ref, o_ref, out_ref):
        mine = jnp.where(core_ref[0] == 0, g0_ref[...], g1_ref[...])
        out_ref[...] = (mine + o_ref[...]).astype(BF16)

    def layer_spec(layer):
        return pl.BlockSpec((1, rb, c), lambda i, j, core_ref: (jnp.where(core_ref[0] == layer, i, 0),
                                                                jnp.where(core_ref[0] == layer, j, 0), 0))

    spec = pl.BlockSpec((1, rb, c), lambda i, j, core_ref: (i, j, 0))
    return pl.pallas_call(
        body, name=name,
        grid_spec=pltpu.PrefetchScalarGridSpec(num_scalar_prefetch=1, grid=(n, r // rb),
                                               in_specs=[layer_spec(0), layer_spec(1), spec], out_specs=spec),
        out_shape=jax.ShapeDtypeStruct(g0.shape, BF16), compiler_params=_cparams(2),
    )(core, g0, g1, other)


def _chip_sum(chip, parts, recv, name):
    _, r, c = parts.shape
    rb = _row_block(r)

    def body(chip_ref, p_ref, r_ref, out_ref):
        got = r_ref[...].astype(F32)
        out_ref[...] = (p_ref[0].astype(F32) + got[0]) + (got[1] + got[2])

    return pl.pallas_call(
        body, name=name,
        grid_spec=pltpu.PrefetchScalarGridSpec(
            num_scalar_prefetch=1, grid=(r // rb,),
            in_specs=[pl.BlockSpec((1, rb, c), lambda j, chip_ref: (chip_ref[0], j, 0)),
                      pl.BlockSpec((3, rb, c), lambda j, chip_ref: (0, j, 0))],
            out_specs=pl.BlockSpec((rb, c), lambda j, chip_ref: (j, 0))),
        out_shape=jax.ShapeDtypeStruct((r, c), F32), compiler_params=_cparams(1),
    )(chip, parts, recv)


def _adamw_update(w, g, m, v):
    nm = ADAM_B1 * m + (1.0 - ADAM_B1) * g
    nv = ADAM_B2 * v + (1.0 - ADAM_B2) * (g * g)
    m_hat = nm / (1.0 - ADAM_B1 ** ADAM_STEP)
    v_hat = nv / (1.0 - ADAM_B2 ** ADAM_STEP)
    return -ADAM_LR * (m_hat / (jnp.sqrt(v_hat) + ADAM_EPS) + ADAM_WD * w), nm, nv


def _adamw(w, g, m, v, name):
    n, r, c = w.shape
    rb = _row_block(r)

    def body(w_ref, g_ref, m_ref, v_ref, d_ref, nm_ref, nv_ref):
        d_ref[...], nm_ref[...], nv_ref[...] = _adamw_update(w_ref[...], g_ref[...], m_ref[...], v_ref[...])

    spec = pl.BlockSpec((1, rb, c), lambda i, j: (i, j, 0))
    shp = jax.ShapeDtypeStruct(w.shape, F32)
    return pl.pallas_call(
        body, name=name, grid=(n, r // rb), in_specs=[spec] * 4, out_specs=[spec] * 3,
        out_shape=[shp] * 3, compiler_params=_cparams(2),
    )(w, g, m, v)


def _adamw_pair(core, w, mine, theirs, m, v, name):
    n, r, c = w.shape
    rb = _row_block(r)

    def body(core_ref, w_ref, a_ref, b_ref, m_ref, v_ref, g_ref, d_ref, nm_ref, nv_ref):
        g = jnp.where(pl.program_id(0) == core_ref[0], a_ref[...], b_ref[...])
        g_ref[0] = g
        d_ref[0], nm_ref[0], nv_ref[0] = _adamw_update(w_ref[0], g, m_ref[0], v_ref[0])

    spec = pl.BlockSpec((1, rb, c), lambda i, j, core_ref: (i, j, 0))
    flat = pl.BlockSpec((rb, c), lambda i, j, core_ref: (j, 0))
    shp = jax.ShapeDtypeStruct(w.shape, F32)
    return pl.pallas_call(
        body, name=name,
        grid_spec=pltpu.PrefetchScalarGridSpec(num_scalar_prefetch=1, grid=(n, r // rb),
                                               in_specs=[spec, flat, flat, spec, spec], out_specs=[spec] * 4),
        out_shape=[shp] * 4, compiler_params=_cparams(2),
    )(core, w, mine, theirs, m, v)


def _mixer_fwd(h0, hn, w_in, conv_k, conv_v, avg, pool_w, w_out, g2, tm):
    rows = h0.shape[0]
    rb = CONV_ROW_BLOCK

    def body(h0_ref, hn_ref, win_ref, ck_ref, cv_ref, avg_ref, pw_ref, wout_ref, g2_ref,
             z_ref, u1_ref, m_ref, h1_ref, hn2_ref, ubuf, pbuf):
        i = pl.program_id(0)

        @pl.when(i == 0)
        def _():
            ubuf[pl.ds(0, CONV_HIST), :] = jnp.zeros((CONV_HIST, CONV_W), F32)
            pbuf[pl.ds(0, POOL_HIST), :] = jnp.zeros((POOL_HIST, POOL_W), F32)

        z = _dot(hn_ref[...], win_ref[...])
        z_ref[...] = z
        ubuf[pl.ds(CONV_HIST, tm), :] = z[:, :CONV_W] * _sigmoid(z[:, CONV_W:2 * CONV_W])
        p = z[:, 2 * CONV_W:]
        pbuf[pl.ds(POOL_HIST, tm), :] = p

        def conv_block(r0):
            for l in range(CONV_W // 128):
                ls = pl.ds(128 * l, 128)
                window = ubuf[pl.ds(r0, rb + CONV_HIST), ls]
                acc = jnp.broadcast_to(cv_ref[0:1, ls], (rb, 128))
                for s in range(8):
                    ws = _shifted(window, s)
                    for q in range(CONV_HIST // 8 + 1):
                        j = 8 * q + s - 2
                        if 0 <= j < CONV_TAPS:
                            acc = acc + ck_ref[j:j + 1, ls] * ws[8 * q:8 * q + rb]
                u1_ref[pl.ds(r0, rb), ls] = acc

        _for_row_blocks(tm // rb, rb, conv_block, unroll=True)
        ubuf[pl.ds(0, CONV_HIST), :] = ubuf[pl.ds(tm, CONV_HIST), :]

        u1 = u1_ref[...]
        cen = u1 - _split_dot(u1, avg_ref)
        xhat = cen * lax.rsqrt(_split_dot(cen * cen, avg_ref) + EPS)
        u2 = xhat * cv_ref[1:2, :] + cv_ref[2:3, :]
        m_ref[:, 0:CONV_W] = (u2 * _sigmoid(u2)).astype(BF16)

        for gi, w in enumerate(POOL_WINDOWS):
            ls = pl.ds(POOL_G * gi, POOL_G)
            s = pbuf[pl.ds(POOL_HIST, tm), ls]
            for j in range(1, w):
                s = s + pbuf[pl.ds(POOL_HIST - j, tm), ls]
            d = s * _inv_count(i, tm, w) - p[:, POOL_G * gi:POOL_G * (gi + 1)]
            y = _dot(d.astype(BF16), pw_ref[gi]) * cv_ref[3:4, ls]
            m_ref[:, pl.ds(CONV_W + POOL_G * gi, POOL_G)] = y.astype(BF16)
        pbuf[pl.ds(0, POOL_HIST), :] = pbuf[pl.ds(tm, POOL_HIST), :]

        h1 = h0_ref[...] + _dot(m_ref[...], wout_ref[...])
        h1_ref[...] = h1
        xh, _ = _rms(h1)
        hn2_ref[...] = (xh * g2_ref[...]).astype(BF16)

    def tile(c):
        return pl.BlockSpec((tm, c), lambda i: (i, 0))

    return pl.pallas_call(
        body, name="mixer_fwd", grid=(rows // tm,),
        in_specs=[tile(D_MODEL), tile(D_MODEL), _whole((D_MODEL, IN_COLS)), _whole((CONV_HIST, CONV_W)),
                  _whole((8, CONV_W)), _whole((CONV_W, CONV_W)), _whole((4, POOL_G, POOL_G)),
                  _whole((D_MODEL, D_MODEL)), _whole((1, D_MODEL))],
        out_specs=[tile(IN_COLS), tile(CONV_W), tile(D_MODEL), tile(D_MODEL), tile(D_MODEL)],
        out_shape=[jax.ShapeDtypeStruct((rows, IN_COLS), F32), jax.ShapeDtypeStruct((rows, CONV_W), F32),
                   jax.ShapeDtypeStruct((rows, D_MODEL), BF16), jax.ShapeDtypeStruct((rows, D_MODEL), F32),
                   jax.ShapeDtypeStruct((rows, D_MODEL), BF16)],
        scratch_shapes=[pltpu.VMEM((CONV_HIST + tm, CONV_W), F32), pltpu.VMEM((POOL_HIST + tm, POOL_W), F32)],
        compiler_params=_cparams(1, V7X_VMEM_LIMIT),
    )(h0, hn, w_in, conv_k, conv_v, avg, pool_w, w_out, g2)


def _ffn_fwd(h1, hn2, w_up, layer, kf, w_down, g_next, tm):
    rows = h1.shape[0]
    hist = 8
    w_up_spec = pl.BlockSpec((4, None, D_MODEL, FF_CHUNK), lambda i: (0, layer, 0, 0), pipeline_mode=pl.Buffered(1))

    rb = FFN_ROW_BLOCK

    def body(h1_ref, hn2_ref, wup_ref, kf_ref, wdn_ref, gn_ref, ug_ref, h2_ref, hnn_ref, wg, wv, carry, act_s, acc):
        i = pl.program_id(0)

        @pl.when(i == 0)
        def _():
            carry[...] = jnp.zeros(carry.shape, F32)

        acc[...] = h1_ref[...]
        for c in range(2):
            for buf, cc in ((wg, c), (wv, c + 2)):
                ug = _dot(hn2_ref[...], wup_ref[cc])
                ug_ref[cc] = ug.astype(BF16)
                buf[pl.ds(0, hist), :] = carry[cc]
                buf[pl.ds(hist, tm), :] = ug
                carry[cc] = buf[pl.ds(tm, hist), :]

            def act_block(r0):
                for l in range(FF_CHUNK // 128):
                    ls = pl.ds(128 * l, 128)
                    gate = _conv3(wg[pl.ds(r0, rb + hist), ls], kf_ref, c, ls, hist)[3]
                    val = _conv3(wv[pl.ds(r0, rb + hist), ls], kf_ref, c + 2, ls, hist)[3]
                    act_s[pl.ds(r0, rb), ls] = (gate * _sigmoid(gate) * val).astype(BF16)

            _for_row_blocks(tm // rb, rb, act_block, unroll=True)
            acc[...] += _dot(act_s[...], wdn_ref[c])
        h2 = acc[...]
        h2_ref[...] = h2
        xh, _ = _rms(h2)
        hnn_ref[...] = (xh * gn_ref[...]).astype(BF16)

    def tile(c):
        return pl.BlockSpec((tm, c), lambda i: (i, 0))

    return pl.pallas_call(
        body, name="ffn_fwd", grid=(rows // tm,),
        in_specs=[tile(D_MODEL), tile(D_MODEL), w_up_spec, _whole((4, 8, FF_CHUNK)),
                  _whole((2, FF_CHUNK, D_MODEL), single=True), _whole((1, D_MODEL))],
        out_specs=[pl.BlockSpec((4, tm, FF_CHUNK), lambda i: (0, i, 0)), tile(D_MODEL), tile(D_MODEL)],
        out_shape=[jax.ShapeDtypeStruct((4, rows, FF_CHUNK), BF16), jax.ShapeDtypeStruct((rows, D_MODEL), F32),
                   jax.ShapeDtypeStruct((rows, D_MODEL), BF16)],
        scratch_shapes=[pltpu.VMEM((hist + tm, FF_CHUNK), F32), pltpu.VMEM((hist + tm, FF_CHUNK), F32),
                        pltpu.VMEM((4, hist, FF_CHUNK), F32), pltpu.VMEM((tm, FF_CHUNK), BF16),
                        pltpu.VMEM((tm, D_MODEL), F32)],
        compiler_params=_cparams(1, V7X_VMEM_LIMIT),
    )(h1, hn2, w_up, kf, w_down, g_next)


def _loss_head(h, tgt, g, seq, tm):
    rows = h.shape[0]
    tgt_tiles = -(-seq // tm)

    def body(h_ref, tprev_ref, t_ref, g_ref, dh_ref, loss_ref, dg_ref):
        i = pl.program_id(0)

        @pl.when(i == 0)
        def _():
            loss_ref[...] = jnp.zeros(loss_ref.shape, F32)
            dg_ref[...] = jnp.zeros(dg_ref.shape, F32)

        t = i * tm + lax.broadcasted_iota(jnp.int32, (tm, 1), 0)
        inside = jnp.logical_and(t >= N_META, t < N_META + seq)
        tgt = jnp.concatenate([tprev_ref[...], t_ref[0:tm - N_META, :]], axis=0)
        xhat, r = _rms(h_ref[...])
        err = jnp.where(inside, xhat * g_ref[...] - tgt, 0.0)
        loss_ref[...] += _colsum(err * err)
        dh, dg_rows = _rms_bwd(err * (1.0 / D_MODEL), xhat, r, g_ref[...])
        dh_ref[...] = dh
        dg_ref[...] += _colsum(dg_rows)

        @pl.when(i == rows // tm - 1)
        def _():
            total = jnp.sum(loss_ref[...], axis=1, keepdims=True) * (0.5 / D_MODEL)
            loss_ref[...] = jnp.broadcast_to(total, loss_ref.shape)

    tile = pl.BlockSpec((tm, D_MODEL), lambda i: (i, 0))
    t_prev = pl.BlockSpec((N_META, D_MODEL), lambda i: (jnp.maximum(i * (tm // N_META) - 1, 0), 0))
    t_own = pl.BlockSpec((tm, D_MODEL), lambda i: (jnp.minimum(i, tgt_tiles - 1), 0))
    vec = jax.ShapeDtypeStruct((1, D_MODEL), F32)
    return pl.pallas_call(
        body, name="loss_head", grid=(rows // tm,),
        in_specs=[tile, t_prev, t_own, _whole((1, D_MODEL))],
        out_specs=[tile, _whole((1, D_MODEL)), _whole((1, D_MODEL))],
        out_shape=[jax.ShapeDtypeStruct((rows, D_MODEL), F32), vec, vec],
        compiler_params=_cparams(1),
    )(h, tgt, tgt, g)


def _ffn_bwd(dh2, hn2, ug0, w_down, w_up, layer, kf, tm):
    rows = dh2.shape[0]
    nt = rows // tm
    hist = 16
    fut = 8
    rb = FFN_ROW_BLOCK
    near = 8

    def body(dh2_ref, hn2_ref, ugg_ref, ugv_ref, hg_ref, hv_ref, wd_ref, wug_ref, wuv_ref, kf_ref,
             dhn_ref, dkf_ref, dwup_ref, dwdn_ref, wg, wv, dgb, dvb, carry, dkacc, act_s, dug_s, acc_up, acc_dn):
        c = pl.program_id(0)
        i = pl.program_id(1)
        first_tile = jnp.where(i == nt - 1, 1.0, 0.0)

        @pl.when(i == 0)
        def _():
            carry[...] = jnp.zeros(carry.shape, F32)
            dkacc[...] = jnp.zeros(dkacc.shape, F32)
            acc_up[...] = jnp.zeros(acc_up.shape, F32)
            acc_dn[...] = jnp.zeros(acc_dn.shape, F32)

        def run():
            n_blocks = tm // rb
            dh2b = dh2_ref[...].astype(BF16)
            sides = ((wg, dgb, ugg_ref, hg_ref, 0, c), (wv, dvb, ugv_ref, hv_ref, 1, c + 2))
            for buf, dbuf, u_ref, h_ref, s, cc in sides:
                buf[pl.ds(0, hist), :] = h_ref[0].astype(F32) * (1.0 - first_tile)
                buf[pl.ds(hist, tm), :] = u_ref[0].astype(F32)
                dbuf[pl.ds(tm, fut), :] = carry[s]
            dgb[pl.ds(0, tm), :] = _dot_nt(dh2b, wd_ref[0])

            def grad_block(r0):
                for l in range(FF_CHUNK // 128):
                    ls = pl.ds(128 * l, 128)
                    g0, g1, g2, gate = _conv3(wg[pl.ds(r0 + hist - near, rb + near), ls], kf_ref, c, ls, near)
                    v0, v1, v2, val = _conv3(wv[pl.ds(r0 + hist - near, rb + near), ls], kf_ref, c + 2, ls, near)
                    sg = _sigmoid(gate)
                    silu = gate * sg
                    act_s[pl.ds(r0, rb), ls] = (silu * val).astype(BF16)
                    dact = dgb[pl.ds(r0, rb), ls]
                    dgate = dact * val * (sg * (1.0 + gate * (1.0 - sg)))
                    dval = dact * silu
                    dgb[pl.ds(r0, rb), ls] = dgate
                    dvb[pl.ds(r0, rb), ls] = dval
                    for s, dv, taps in ((0, dgate, (g0, g1, g2)), (1, dval, (v0, v1, v2))):
                        for j in range(FFN_TAPS):
                            dkacc[s, pl.ds(8 * j, 8), ls] += _fold8(dv * taps[j])

            _for_row_blocks(n_blocks, rb, grad_block, unroll=True)
            for buf, dbuf, u_ref, h_ref, s, cc in sides:
                carry[s] = dbuf[pl.ds(0, fut), :]

            def conv_block(r0):
                for l in range(FF_CHUNK // 128):
                    ls = pl.ds(128 * l, 128)
                    for buf, dbuf, u_ref, h_ref, s, cc in sides:
                        window = dbuf[pl.ds(r0, rb + fut), ls]
                        dug0 = (kf_ref[cc, 0:1, ls] * _shifted(window, 2)[0:rb] + kf_ref[cc, 1:2, ls] * _shifted(window, 1)[0:rb]
                                + kf_ref[cc, 2:3, ls] * window[0:rb])
                        dug_s[s, pl.ds(r0, rb), ls] = dug0.astype(BF16)

            _for_row_blocks(n_blocks, rb, conv_block, unroll=True)
            dhn_ref[0] = _dot_nt(dug_s[0], wug_ref[0]) + _dot_nt(dug_s[1], wuv_ref[0])
            acc_up[0] += _dot_tn(hn2_ref[...], dug_s[0])
            acc_up[1] += _dot_tn(hn2_ref[...], dug_s[1])
            acc_dn[...] += _dot_tn(act_s[...], dh2b)

        run()

        @pl.when(i == nt - 1)
        def _():
            for s in range(2):
                for j in range(FFN_TAPS):
                    dkf_ref[c + 2 * s, j:j + 1, :] = _colsum(dkacc[s, pl.ds(8 * j, 8), :])
                dkf_ref[c + 2 * s, FFN_TAPS:8, :] = jnp.zeros((8 - FFN_TAPS, FF_CHUNK), F32)
            pltpu.sync_copy(acc_up, dwup_ref.at[c])
            pltpu.sync_copy(acc_dn, dwdn_ref.at[c])

    def tile(cols):
        return pl.BlockSpec((tm, cols), lambda c, i: (nt - 1 - i, 0))

    def chunk(off, r, halo_rows=None):
        if halo_rows is None:
            return pl.BlockSpec((1, r, FF_CHUNK), lambda c, i: (c + off, nt - 1 - i, 0))
        return pl.BlockSpec((1, r, FF_CHUNK), lambda c, i: (c + off, jnp.maximum((nt - 1 - i) * (tm // r) - 1, 0), 0))

    def up_chunk(off):
        return pl.BlockSpec((1, None, D_MODEL, FF_CHUNK), lambda c, i: (c + off, layer, 0, 0), pipeline_mode=pl.Buffered(1))

    down_chunk = pl.BlockSpec((1, FF_CHUNK, D_MODEL), lambda c, i: (c, 0, 0), pipeline_mode=pl.Buffered(1))

    return pl.pallas_call(
        body, name="ffn_bwd", grid=(2, nt),
        in_specs=[tile(D_MODEL), tile(D_MODEL), chunk(0, tm), chunk(2, tm), chunk(0, hist, True), chunk(2, hist, True),
                  down_chunk, up_chunk(0), up_chunk(2), _whole((4, 8, FF_CHUNK))],
        out_specs=[pl.BlockSpec((1, tm, D_MODEL), lambda c, i: (c, nt - 1 - i, 0)), _whole((4, 8, FF_CHUNK)), _ANY, _ANY],
        out_shape=[jax.ShapeDtypeStruct((2, rows, D_MODEL), F32), jax.ShapeDtypeStruct((4, 8, FF_CHUNK), F32),
                   jax.ShapeDtypeStruct((2, 2, D_MODEL, FF_CHUNK), F32), jax.ShapeDtypeStruct((2, FF_CHUNK, D_MODEL), F32)],
        scratch_shapes=[pltpu.VMEM((hist + tm, FF_CHUNK), F32), pltpu.VMEM((hist + tm, FF_CHUNK), F32),
                        pltpu.VMEM((tm + fut, FF_CHUNK), F32), pltpu.VMEM((tm + fut, FF_CHUNK), F32),
                        pltpu.VMEM((2, fut, FF_CHUNK), F32), pltpu.VMEM((2, 8 * FFN_TAPS, FF_CHUNK), F32),
                        pltpu.VMEM((tm, FF_CHUNK), BF16), pltpu.VMEM((2, tm, FF_CHUNK), BF16),
                        pltpu.VMEM((2, D_MODEL, FF_CHUNK), F32), pltpu.VMEM((FF_CHUNK, D_MODEL), F32)],
        compiler_params=_cparams(2, V7X_VMEM_LIMIT),
    )(dh2, hn2, ug0, ug0, ug0, ug0, w_down, w_up, w_up, kf)


def _mixer_bwd(dh2, dhn2, h1, g2, h0, z, u1, hn1, m, w_out, w_in, conv_k, conv_v, avg, pool_w, g1, tm,
               x_rows=None):
    rows = dh2.shape[0]
    nt = rows // tm
    rb = CONV_ROW_BLOCK
    in_shard = IN_COLS // 4

    def body(dh2_ref, dhn2_ref, h1_ref, g2_ref, h0_ref, z_ref, zh_ref, u1_ref, hn1_ref, m_ref, wo_ref, wi_ref, ck_ref,
             cv_ref, avg_ref, pw_ref, g1_ref,
             dh0_ref, dtop_ref, dk_ref, ds_ref, dpw_ref, dg1_ref, dg2_ref, dwo_ref, dwi_ref,
             ubuf, dbuf, pbuf, ebuf, dcarry, ecarry, dkacc, dzs, dz_s, dh1_s, acc_out, acc_in, xcarry):
        i = pl.program_id(0)
        ti = nt - 1 - i
        has_past = jnp.where(ti > 0, 1.0, 0.0)

        @pl.when(i == 0)
        def _():
            dcarry[...] = jnp.zeros(dcarry.shape, F32)
            ecarry[...] = jnp.zeros(ecarry.shape, F32)
            dkacc[...] = jnp.zeros(dkacc.shape, F32)
            ds_ref[...] = jnp.zeros(ds_ref.shape, F32)
            dpw_ref[...] = jnp.zeros(dpw_ref.shape, F32)
            dg1_ref[...] = jnp.zeros(dg1_ref.shape, F32)
            dg2_ref[...] = jnp.zeros(dg2_ref.shape, F32)
            acc_out[...] = jnp.zeros(acc_out.shape, F32)
            acc_in[...] = jnp.zeros(acc_in.shape, F32)
            xcarry[...] = jnp.zeros(xcarry.shape, F32)

        xh1, r1 = _rms(h1_ref[...])
        dx1, dg2_rows = _rms_bwd(dhn2_ref[0] + dhn2_ref[1], xh1, r1, g2_ref[...])
        dh1_s[...] = dh2_ref[...] + dx1
        dg2_ref[...] += _colsum(dg2_rows)
        dh1b = dh1_s[...].astype(BF16)
        acc_out[...] += _dot_tn(m_ref[...], dh1b)
        dm = _dot_nt(dh1b, wo_ref[...])
        z = z_ref[...]
        a = z[:, :CONV_W]
        sg = _sigmoid(z[:, CONV_W:2 * CONV_W])
        p = z[:, 2 * CONV_W:]
        zh = zh_ref[...] * has_past
        ubuf[pl.ds(0, CONV_HIST), :] = zh[:, :CONV_W] * _sigmoid(zh[:, CONV_W:2 * CONV_W])
        ubuf[pl.ds(CONV_HIST, tm), :] = a * sg
        pbuf[pl.ds(0, POOL_HIST), :] = zh[CONV_HIST - POOL_HIST:, 2 * CONV_W:]
        pbuf[pl.ds(POOL_HIST, tm), :] = p

        u1 = u1_ref[...]
        cen = u1 - _split_dot(u1, avg_ref)
        rstd = lax.rsqrt(_split_dot(cen * cen, avg_ref) + EPS)
        xhat = cen * rstd
        u2 = xhat * cv_ref[1:2, :] + cv_ref[2:3, :]
        s2 = _sigmoid(u2)
        du2 = dm[:, :CONV_W] * (s2 * (1.0 + u2 * (1.0 - s2)))
        ds_ref[1:2, :] += _colsum(du2 * xhat)
        ds_ref[2:3, :] += _colsum(du2)
        dxh = du2 * cv_ref[1:2, :]
        du1 = rstd * (dxh - _split_dot(dxh, avg_ref) - xhat * _split_dot(dxh * xhat, avg_ref))
        ds_ref[0:1, :] += _colsum(du1)
        dbuf[pl.ds(0, tm), :] = du1
        dbuf[pl.ds(tm, CONV_HIST), :] = dcarry[...]
        dcarry[...] = dbuf[pl.ds(0, CONV_HIST), :]

        def conv_block(r0):
            for l in range(CONV_W // 128):
                ls = pl.ds(128 * l, 128)
                dwin = dbuf[pl.ds(r0, rb + CONV_HIST), ls]
                uwin = ubuf[pl.ds(r0, rb + CONV_HIST), ls]
                dblk = dwin[0:rb]
                du0 = jnp.zeros((rb, 128), F32)
                for s in range(8):
                    ds_ = _shifted(dwin, s)
                    us_ = _shifted(uwin, s)
                    for q in range(CONV_HIST // 8 + 1):
                        o = 8 * q + s
                        if 0 <= CONV_TAPS - 1 - o < CONV_TAPS:
                            j = CONV_TAPS - 1 - o
                            du0 = du0 + ck_ref[j:j + 1, ls] * ds_[8 * q:8 * q + rb]
                        j = o - 2
                        if 0 <= j < CONV_TAPS:
                            prod = dblk * us_[8 * q:8 * q + rb]
                            part = prod[0:8]
                            for v in range(1, rb // 8):
                                part = part + prod[8 * v:8 * v + 8]
                            dkacc[pl.ds(8 * j, 8), ls] += part
                dzs[pl.ds(r0, rb), ls] = du0

        _for_row_blocks(tm // rb, rb, conv_block, unroll=True)
        du0 = dzs[:, 0:CONV_W]
        dz_s[:, 0:CONV_W] = (du0 * sg).astype(BF16)
        dz_s[:, CONV_W:2 * CONV_W] = (du0 * a * sg * (1.0 - sg)).astype(BF16)

        for gi, w in enumerate(POOL_WINDOWS):
            ls = pl.ds(POOL_G * gi, POOL_G)
            cols = slice(CONV_W + POOL_G * gi, CONV_W + POOL_G * (gi + 1))
            inv = _inv_count(ti, tm, w)
            s = pbuf[pl.ds(POOL_HIST, tm), ls]
            for j in range(1, w):
                s = s + pbuf[pl.ds(POOL_HIST - j, tm), ls]
            d = (s * inv - p[:, POOL_G * gi:POOL_G * (gi + 1)]).astype(BF16)
            dyp = dm[:, cols]
            ds_ref[3:4, ls] += _colsum(dyp * _dot(d, pw_ref[gi]))
            dyb = (dyp * cv_ref[3:4, ls]).astype(BF16)
            dpw_ref[gi] += _dot_tn(d, dyb)
            dd = _dot_nt(dyb, pw_ref[gi])
            ebuf[pl.ds(0, tm), ls] = dd * inv
            ebuf[pl.ds(tm, POOL_HIST), ls] = ecarry[:, ls]
            dp = ebuf[pl.ds(0, tm), ls] - dd
            for j in range(1, w):
                dp = dp + ebuf[pl.ds(j, tm), ls]
            dz_s[:, pl.ds(2 * CONV_W + POOL_G * gi, POOL_G)] = dp.astype(BF16)
        ecarry[...] = ebuf[pl.ds(0, POOL_HIST), :]

        acc_in[...] += _dot_tn(hn1_ref[...], dz_s[...])
        dhn = _dot_nt(dz_s[...], wi_ref[...])
        xh, r = _rms(h0_ref[...])
        dx, dg_rows = _rms_bwd(dhn, xh, r, g1_ref[...])
        dh0 = dh1_s[...] + dx
        if x_rows is None:
            dh0_ref[...] = dh0
        else:
            dh0_ref[0:tm - N_META, :] = dh0[N_META:]
            dh0_ref[tm - N_META:tm, :] = xcarry[...]
            xcarry[...] = dh0[0:N_META]
        dg1_ref[...] += _colsum(dg_rows)

        @pl.when(i == nt - 1)
        def _():
            dtop_ref[...] = dh0[0:N_META]
            for j in range(CONV_TAPS):
                dk_ref[j:j + 1, :] = _colsum(dkacc[pl.ds(8 * j, 8), :])
            dk_ref[CONV_TAPS:CONV_HIST, :] = jnp.zeros((CONV_HIST - CONV_TAPS, CONV_W), F32)
            pltpu.sync_copy(acc_out, dwo_ref)
            for k in range(4):
                pltpu.sync_copy(acc_in.at[:, pl.ds(in_shard * k, in_shard)], dwi_ref.at[k])

    def tile(c):
        return pl.BlockSpec((tm, c), lambda i: (nt - 1 - i, 0))

    halo = pl.BlockSpec((CONV_HIST, IN_COLS), lambda i: (jnp.maximum((nt - 1 - i) * (tm // CONV_HIST) - 1, 0), 0))
    vec = jax.ShapeDtypeStruct((1, D_MODEL), F32)
    if x_rows is None:
        first_spec, first_shape = tile(D_MODEL), jax.ShapeDtypeStruct((rows, D_MODEL), F32)
    else:
        x_tiles = -(-x_rows // tm)
        first_spec = pl.BlockSpec((tm, D_MODEL), lambda i: (jnp.minimum(nt - 1 - i, x_tiles - 1), 0))
        first_shape = jax.ShapeDtypeStruct((x_rows, D_MODEL), F32)
    return pl.pallas_call(
        body, name="mixer_bwd", grid=(nt,),
        in_specs=[tile(D_MODEL), pl.BlockSpec((2, tm, D_MODEL), lambda i: (0, nt - 1 - i, 0)), tile(D_MODEL),
                  _whole((1, D_MODEL)), tile(D_MODEL), tile(IN_COLS), halo, tile(CONV_W), tile(D_MODEL), tile(D_MODEL),
                  _whole((D_MODEL, D_MODEL)), _whole((D_MODEL, IN_COLS)), _whole((CONV_HIST, CONV_W)), _whole((8, CONV_W)),
                  _whole((CONV_W, CONV_W)), _whole((4, POOL_G, POOL_G)), _whole((1, D_MODEL))],
        out_specs=[first_spec, _whole((N_META, D_MODEL)), _whole((CONV_HIST, CONV_W)), _whole((8, CONV_W)),
                   _whole((4, POOL_G, POOL_G)), _whole((1, D_MODEL)), _whole((1, D_MODEL)), _ANY, _ANY],
        out_shape=[first_shape, jax.ShapeDtypeStruct((N_META, D_MODEL), F32), jax.ShapeDtypeStruct((CONV_HIST, CONV_W), F32),
                   jax.ShapeDtypeStruct((8, CONV_W), F32), jax.ShapeDtypeStruct((4, POOL_G, POOL_G), F32), vec, vec,
                   jax.ShapeDtypeStruct((D_MODEL, D_MODEL), F32), jax.ShapeDtypeStruct((4, D_MODEL, in_shard), F32)],
        scratch_shapes=[pltpu.VMEM((CONV_HIST + tm, CONV_W), F32), pltpu.VMEM((tm + CONV_HIST, CONV_W), F32),
                        pltpu.VMEM((POOL_HIST + tm, POOL_W), F32), pltpu.VMEM((tm + POOL_HIST, POOL_W), F32),
                        pltpu.VMEM((CONV_HIST, CONV_W), F32), pltpu.VMEM((POOL_HIST, POOL_W), F32),
                        pltpu.VMEM((8 * CONV_HIST, CONV_W), F32), pltpu.VMEM((tm, CONV_W), F32),
                        pltpu.VMEM((tm, IN_COLS), BF16), pltpu.VMEM((tm, D_MODEL), F32),
                        pltpu.VMEM((D_MODEL, D_MODEL), F32), pltpu.VMEM((D_MODEL, IN_COLS), F32),
                        pltpu.VMEM((N_META, D_MODEL), F32)],
        compiler_params=_cparams(1, V7X_VMEM_LIMIT),
    )(dh2, dhn2, h1, g2, h0, z, z, u1, hn1, m, w_out, w_in, conv_k, conv_v, avg, pool_w, g1)


def _head_average():
    head = lax.broadcasted_iota(jnp.int32, (CONV_W, CONV_W), 0) // 64
    return jnp.where(head == head.T, 1.0 / 64, 0.0).astype(BF16)


def _local_step(x, meta, tgt, layers, ffn_weights, ffn_grads_done, final_g, tm, tm_ffn):
    avg = _head_average()
    depth = len(layers)
    saved = []
    seq = x.shape[0]
    step = math.lcm(tm, tm_ffn)
    rows = -(-(N_META + seq) // step) * step
    h, hn = _embed(x, meta, layers[0]["g1"], rows, tm)
    for l, w in enumerate(layers):
        g_next = layers[l + 1]["g1"] if l + 1 < depth else final_g
        z, u1, m, h1, hn2 = _mixer_fwd(h, hn, w["w_in"], w["conv_k"], w["conv_v"], avg, w["pool_w"], w["w_out"], w["g2"], tm)
        if l == 0:
            w_up_all, ffn = ffn_weights(hn2)
        ug0, h2, hn_next = _ffn_fwd(h1, hn2, w_up_all, l, w["kf"], ffn[l]["w_down"], g_next, tm_ffn)
        saved.append((h, hn, z, u1, m, h1, hn2, ug0))
        h, hn = h2, hn_next
    dh, loss_cols, dfinal_g = _loss_head(h, tgt, final_g, seq, tm)

    grads = [None] * depth
    ffn_grads = [None] * depth
    for l in reversed(range(depth)):
        w = layers[l]
        h0, hn1, z, u1, m, h1, hn2, ug0 = saved[l]
        dhn2, dkf, dw_up, dw_down = _ffn_bwd(dh, hn2, ug0, ffn[l]["w_down"], w_up_all, l, w["kf"], tm)
        ffn_grads[l] = (dw_up.transpose(1, 0, 2, 3).reshape(4, D_MODEL, FF_CHUNK), dw_down.reshape(4, D_FF // 4, D_MODEL))
        g2 = w["g2"] + ffn_grads_done(ffn_grads) if l == 0 else w["g2"]
        dh0, dtop, dk, dsmall, dpw, dg1, dg2, dw_out, dw_in = _mixer_bwd(
            dh, dhn2, h1, g2, h0, z, u1, hn1, m, w["w_out"], w["w_in"], w["conv_k"], w["conv_v"], avg,
            w["pool_w"], w["g1"], tm, x_rows=seq if l == 0 else None)
        grads[l] = dict(dw_in=dw_in, dw_out=dw_out.reshape(4, D_MODEL // 4, D_MODEL), dk=dk, dsmall=dsmall, dpw=dpw,
                        dg1=dg1, dg2=dg2, dkf=dkf)
        dh = dh0
    return loss_cols, dh, dtop, grads, dfinal_g


_ANY = pl.BlockSpec(memory_space=pl.ANY)


def _place():
    x, y, c = lax.axis_index("x"), lax.axis_index("y"), lax.axis_index("c")
    chips = [(1 - x, y), (x, 1 - y), (1 - x, 1 - y)]
    return x, y, c, chips


def _remote(src, dst, send_sems, recv_sems, idx, to):
    return pltpu.make_async_remote_copy(src_ref=src, dst_ref=dst, send_sem=send_sems.at[idx], recv_sem=recv_sems.at[idx],
                                        device_id=to, device_id_type=MESH)


def _gather_chips(xs):
    n = len(xs)

    def body(*refs):
        x_refs, o_refs = refs[:n], refs[n:2 * n]
        send_sems, recv_sems = refs[2 * n:]
        x, y, c, chips = _place()
        k = 2 * x + y
        sibling = (x, y, 1 - c)
        sends = []
        for j, chip in enumerate(chips):
            for a in range(n):
                sends.append(_remote(x_refs[a].at[k, c], o_refs[a].at[k, c], send_sems, recv_sems, 3 * a + j, (*chip, c)))
                sends[-1].start()
        for j, chip in enumerate(chips):
            kj = 2 * chip[0] + chip[1]
            for a in range(n):
                landed = o_refs[a].at[kj, c]
                _remote(landed, landed, send_sems, recv_sems, 3 * a + j, sibling).wait_recv()
                sends.append(_remote(landed, landed, send_sems, recv_sems, 3 * n + 3 * a + j, sibling))
                sends[-1].start()
        for j, chip in enumerate(chips):
            kj = 2 * chip[0] + chip[1]
            for a in range(n):
                passed = o_refs[a].at[kj, 1 - c]
                _remote(passed, passed, send_sems, recv_sems, 3 * n + 3 * a + j, sibling).wait_recv()
        for cp in sends:
            cp.wait_send()

    return pl.pallas_call(
        body, name="gather_chips", in_specs=[_ANY] * n, out_specs=[_ANY] * n,
        out_shape=[jax.ShapeDtypeStruct(v.shape, v.dtype) for v in xs],
        input_output_aliases={a: a for a in range(n)},
        scratch_shapes=[pltpu.SemaphoreType.DMA((6 * n,)), pltpu.SemaphoreType.DMA((6 * n,))],
    )(*xs)


_HBM = pl.BlockSpec(memory_space=pltpu.HBM)
_SEM = pl.BlockSpec(memory_space=pltpu.SEMAPHORE)
_SPLIT_COPY = pltpu.CompilerParams(has_side_effects=pltpu.SideEffectType.DATAFLOW_SIDE_EFFECTING)


def _gather_start(xs, after):
    n = len(xs)

    def body(*refs):
        x_refs = refs[:n]
        send_sems, recv_sems = refs[n + 1], refs[n + 2]
        token = refs[2 * n + 3]
        x, y, c, chips = _place()
        k = 2 * x + y
        for j, chip in enumerate(chips):
            for a in range(n):
                mine = x_refs[a].at[k, c]
                _remote(mine, mine, send_sems, recv_sems, 3 * a + j, (*chip, c)).start()
        token[...] = jnp.zeros(token.shape, F32)

    return pl.pallas_call(
        body, name="gather_start", in_specs=[_HBM] * n + [_ANY],
        out_specs=(_SEM, _SEM, *[_HBM] * n, pl.BlockSpec(memory_space=pltpu.VMEM)),
        out_shape=(pltpu.SemaphoreType.DMA((3 * n,)), pltpu.SemaphoreType.DMA((3 * n,)),
                   *[pltpu.HBM(v.shape, v.dtype) for v in xs], jax.ShapeDtypeStruct((8, 128), F32)),
        input_output_aliases={a: 2 + a for a in range(n)}, compiler_params=_SPLIT_COPY,
    )(*[pltpu.with_memory_space_constraint(v, pltpu.HBM) for v in xs], after)


def _gather_wait(send_sems, recv_sems, xs, after):
    n = len(xs)

    def body(*refs):
        x_refs = refs[:n]
        send_sems, recv_sems = refs[n], refs[n + 1]
        x, y, c, chips = _place()
        k = 2 * x + y
        for j, chip in enumerate(chips):
            kj = 2 * chip[0] + chip[1]
            for a in range(n):
                cp = _remote(x_refs[a].at[k, c], x_refs[a].at[kj, c], send_sems, recv_sems, 3 * a + j, (*chip, c))
                cp.wait_send()
                cp.wait_recv()

    return pl.pallas_call(
        body, name="gather_wait", in_specs=[_HBM] * n + [_SEM, _SEM, _ANY], out_specs=[_HBM] * n,
        out_shape=[pltpu.HBM(v.shape, v.dtype) for v in xs],
        input_output_aliases={a: a for a in range(n)}, compiler_params=_SPLIT_COPY,
    )(*xs, send_sems, recv_sems, after)


def _gather_forward(xs):
    n = len(xs)

    def body(*refs):
        x_refs, o_refs = refs[:n], refs[n:2 * n]
        send_sems, recv_sems = refs[2 * n:]
        x, y, c, chips = _place()
        sibling = (x, y, 1 - c)
        sends = []
        for j, chip in enumerate(chips):
            kj = 2 * chip[0] + chip[1]
            for a in range(n):
                landed = x_refs[a].at[kj, c]
                sends.append(_remote(landed, o_refs[a].at[kj, c], send_sems, recv_sems, 3 * a + j, sibling))
                sends[-1].start()
        for j, chip in enumerate(chips):
            kj = 2 * chip[0] + chip[1]
            for a in range(n):
                passed = o_refs[a].at[kj, 1 - c]
                _remote(passed, passed, send_sems, recv_sems, 3 * a + j, sibling).wait_recv()
        for cp in sends:
            cp.wait_send()

    return pl.pallas_call(
        body, name="gather_forward", in_specs=[_ANY] * n, out_specs=[_ANY] * n,
        out_shape=[jax.ShapeDtypeStruct(v.shape, v.dtype) for v in xs],
        input_output_aliases={a: a for a in range(n)},
        scratch_shapes=[pltpu.SemaphoreType.DMA((3 * n,)), pltpu.SemaphoreType.DMA((3 * n,))],
    )(*xs)


def _pair_exchange(g0s, g1s, name):
    n = len(g0s)

    def body(*refs):
        g0, g1, out = refs[:n], refs[n:2 * n], refs[2 * n:3 * n]
        send_sems, recv_sems = refs[3 * n:]
        x, y, c, _ = _place()
        sibling = (x, y, 1 - c)
        for a in range(n):
            @pl.when(c == 0)
            def _():
                _remote(g1[a], out[a], send_sems, recv_sems, a, sibling).start()

            @pl.when(c == 1)
            def _():
                _remote(g0[a], out[a], send_sems, recv_sems, a, sibling).start()
        for a in range(n):
            cp = _remote(g0[a], out[a], send_sems, recv_sems, a, sibling)
            cp.wait_recv()
            cp.wait_send()

    return pl.pallas_call(
        body, name=name, in_specs=[_ANY] * (2 * n), out_specs=[_ANY] * n,
        out_shape=[jax.ShapeDtypeStruct(v.shape, v.dtype) for v in g0s],
        scratch_shapes=[pltpu.SemaphoreType.DMA((n,)), pltpu.SemaphoreType.DMA((n,))],
    )(*g0s, *g1s)


def _chip_exchange(parts, name):
    n = len(parts)

    def body(*refs):
        p_refs, o_refs = refs[:n], refs[n:2 * n]
        send_sems, recv_sems = refs[2 * n:]
        x, y, c, chips = _place()
        sends = []
        for j, chip in enumerate(chips):
            kj = 2 * chip[0] + chip[1]
            for a in range(n):
                sends.append(_remote(p_refs[a].at[kj], o_refs[a].at[j], send_sems, recv_sems, 3 * a + j, (*chip, c)))
                sends[-1].start()
        for j, chip in enumerate(chips):
            for a in range(n):
                landed = o_refs[a].at[j]
                _remote(landed, landed, send_sems, recv_sems, 3 * a + j, (*chip, c)).wait_recv()
        for cp in sends:
            cp.wait_send()

    return pl.pallas_call(
        body, name=name, in_specs=[_ANY] * n, out_specs=[_ANY] * n,
        out_shape=[jax.ShapeDtypeStruct((3,) + v.shape[1:], v.dtype) for v in parts],
        scratch_shapes=[pltpu.SemaphoreType.DMA((3 * n,)), pltpu.SemaphoreType.DMA((3 * n,))],
    )(*parts)


def _chip_exchange_start(parts):
    n = len(parts)
    lands = [lax.empty((3,) + v.shape[1:], v.dtype) for v in parts]

    def body(*refs):
        p_refs, l_refs = refs[:n], refs[n:2 * n]
        send_sems, recv_sems = refs[2 * n], refs[2 * n + 1]
        token = refs[4 * n + 2]
        x, y, c, chips = _place()
        for j, chip in enumerate(chips):
            kj = 2 * chip[0] + chip[1]
            for a in range(n):
                _remote(p_refs[a].at[kj], l_refs[a].at[j], send_sems, recv_sems, 3 * a + j, (*chip, c)).start()
        token[...] = jnp.zeros(token.shape, F32)

    hbm = [pltpu.with_memory_space_constraint(v, pltpu.HBM) for v in parts + lands]
    return pl.pallas_call(
        body, name="chip_exchange_start", in_specs=[_HBM] * (2 * n),
        out_specs=(_SEM, _SEM, *[_HBM] * (2 * n), pl.BlockSpec(memory_space=pltpu.VMEM)),
        out_shape=(pltpu.SemaphoreType.DMA((3 * n,)), pltpu.SemaphoreType.DMA((3 * n,)),
                   *[pltpu.HBM(v.shape, v.dtype) for v in parts + lands], jax.ShapeDtypeStruct((8, 128), F32)),
        input_output_aliases={a: 2 + a for a in range(2 * n)}, compiler_params=_SPLIT_COPY,
    )(*hbm)


def _chip_exchange_wait(send_sems, recv_sems, parts, lands, after):
    n = len(parts)

    def body(*refs):
        p_refs, l_refs = refs[:n], refs[n:2 * n]
        send_sems, recv_sems = refs[2 * n], refs[2 * n + 1]
        x, y, c, chips = _place()
        for j, chip in enumerate(chips):
            kj = 2 * chip[0] + chip[1]
            for a in range(n):
                cp = _remote(p_refs[a].at[kj], l_refs[a].at[j], send_sems, recv_sems, 3 * a + j, (*chip, c))
                cp.wait_send()
                cp.wait_recv()

    return pl.pallas_call(
        body, name="chip_exchange_wait", in_specs=[_HBM] * (2 * n) + [_SEM, _SEM, _ANY], out_specs=[_HBM] * (2 * n),
        out_shape=[pltpu.HBM(v.shape, v.dtype) for v in parts + lands],
        input_output_aliases={a: a for a in range(2 * n)}, compiler_params=_SPLIT_COPY,
    )(*parts, *lands, send_sems, recv_sems, after)[n:]


def _pair_share(reds):
    n = len(reds)

    def body(*refs):
        r_refs, o_refs = refs[:n], refs[n:2 * n]
        send_sems, recv_sems = refs[2 * n:]
        x, y, c, _ = _place()
        sends = [_remote(r_refs[a], o_refs[a], send_sems, recv_sems, a, (x, y, 1 - c)) for a in range(n)]
        for cp in sends:
            cp.start()
        for cp in sends:
            cp.wait_recv()
        for cp in sends:
            cp.wait_send()

    return pl.pallas_call(
        body, name="pair_share", in_specs=[_ANY] * n, out_specs=[_ANY] * n,
        out_shape=[jax.ShapeDtypeStruct(v.shape, v.dtype) for v in reds],
        scratch_shapes=[pltpu.SemaphoreType.DMA((n,)), pltpu.SemaphoreType.DMA((n,))],
    )(*reds)


def _all_reduce_small(pack):
    p, cols = pack.shape
    half = p // 2

    def body(x_ref, o_ref, sib, chipbuf, send_sems, recv_sems):
        x, y, c, chips = _place()
        k = 2 * x + y
        sibling = (x, y, 1 - c)
        mine = pl.ds(pl.multiple_of(c * half, 8), half)
        other = pl.ds(pl.multiple_of((1 - c) * half, 8), half)
        pair = _remote(x_ref.at[other], sib, send_sems, recv_sems, 0, sibling)
        pair.start()
        pair.wait_recv()
        chipbuf[k] = x_ref[mine, :] + sib[...]
        sends = [_remote(chipbuf.at[k], chipbuf.at[k], send_sems, recv_sems, 1 + j, (*chip, c)) for j, chip in enumerate(chips)]
        for cp in sends:
            cp.start()
        for j, chip in enumerate(chips):
            landed = chipbuf.at[2 * chip[0] + chip[1]]
            _remote(landed, landed, send_sems, recv_sems, 1 + j, (*chip, c)).wait_recv()
        o_ref[mine, :] = (chipbuf[0] + chipbuf[1]) + (chipbuf[2] + chipbuf[3])
        share = _remote(o_ref.at[mine], o_ref.at[mine], send_sems, recv_sems, 4, sibling)
        share.start()
        _remote(o_ref.at[other], o_ref.at[other], send_sems, recv_sems, 4, sibling).wait_recv()
        for cp in [pair, share] + sends:
            cp.wait_send()

    vm = pl.BlockSpec(memory_space=pltpu.VMEM)
    return pl.pallas_call(
        body, name="all_reduce_small", in_specs=[vm], out_specs=vm,
        out_shape=jax.ShapeDtypeStruct(pack.shape, F32),
        scratch_shapes=[pltpu.VMEM((half, cols), F32), pltpu.VMEM((4, half, cols), F32),
                        pltpu.SemaphoreType.DMA((5,)), pltpu.SemaphoreType.DMA((5,))],
    )(pack)


_BIG = ("w_in", "w_out", "w_up", "w_down")
_SMALL = ("norm1_g", "conv_dw_b", "conv_ln_g", "conv_ln_b", "pool_w", "pool_scale", "norm2_g", "final_g",
          "meta_tokens", "conv_dw_k", "ffn_dw_k")


def _rows8(v):
    return jnp.pad(v, ((0, -v.shape[0] % 8), (0, 0)))


def _pack_flat(arrs, rows):
    flat = jnp.concatenate([a.reshape(-1) for a in arrs])
    return jnp.pad(flat, (0, rows * D_MODEL - flat.shape[0])).reshape(1, rows, D_MODEL)


def _unpack_flat(packed, like):
    flat = packed.reshape(-1)
    out, off = [], 0
    for a in like:
        out.append(flat[off:off + a.size].reshape(a.shape))
        off += a.size
    return out


def kernel(x, meta_tokens, norm1_g, w_in, conv_dw_k, conv_dw_b, conv_ln_g, conv_ln_b, pool_w, pool_scale, w_out, norm2_g, w_up, ffn_dw_k, w_down, final_g, loss_target, m_meta_tokens, m_norm1_g, m_w_in, m_conv_dw_k, m_conv_dw_b, m_conv_ln_g, m_conv_ln_b, m_pool_w, m_pool_scale, m_w_out, m_norm2_g, m_w_up, m_ffn_dw_k, m_w_down, m_final_g, v_meta_tokens, v_norm1_g, v_w_in, v_conv_dw_k, v_conv_dw_b, v_conv_ln_g, v_conv_ln_b, v_pool_w, v_pool_scale, v_w_out, v_norm2_g, v_w_up, v_ffn_dw_k, v_w_down, v_final_g):
    weights = dict(meta_tokens=meta_tokens, norm1_g=norm1_g, w_in=w_in, conv_dw_k=conv_dw_k, conv_dw_b=conv_dw_b,
                   conv_ln_g=conv_ln_g, conv_ln_b=conv_ln_b, pool_w=pool_w, pool_scale=pool_scale, w_out=w_out,
                   norm2_g=norm2_g, w_up=w_up, ffn_dw_k=ffn_dw_k, w_down=w_down, final_g=final_g)
    mom1 = dict(meta_tokens=m_meta_tokens, norm1_g=m_norm1_g, w_in=m_w_in, conv_dw_k=m_conv_dw_k, conv_dw_b=m_conv_dw_b,
                conv_ln_g=m_conv_ln_g, conv_ln_b=m_conv_ln_b, pool_w=m_pool_w, pool_scale=m_pool_scale, w_out=m_w_out,
                norm2_g=m_norm2_g, w_up=m_w_up, ffn_dw_k=m_ffn_dw_k, w_down=m_w_down, final_g=m_final_g)
    mom2 = dict(meta_tokens=v_meta_tokens, norm1_g=v_norm1_g, w_in=v_w_in, conv_dw_k=v_conv_dw_k, conv_dw_b=v_conv_dw_b,
                conv_ln_g=v_conv_ln_g, conv_ln_b=v_conv_ln_b, pool_w=v_pool_w, pool_scale=v_pool_scale, w_out=v_w_out,
                norm2_g=v_norm2_g, w_up=v_w_up, ffn_dw_k=v_ffn_dw_k, w_down=v_w_down, final_g=v_final_g)
    order = list(weights)
    depth = w_in.shape[0]
    seq = x.shape[1]
    chip = 2 * lax.axis_index("x") + lax.axis_index("y")
    core = lax.axis_index("c")
    chip_arr = chip.astype(jnp.int32).reshape(1)
    core_arr = core.astype(jnp.int32).reshape(1)

    small_sharded = dict(conv_dw_k=jnp.pad(conv_dw_k, ((0, 0), (0, CONV_HIST - CONV_TAPS), (0, 0))),
                         ffn_dw_k=jnp.pad(ffn_dw_k, ((0, 0), (0, 8 - FFN_TAPS), (0, 0))),
                         meta_tokens=meta_tokens.reshape(2, N_META // 2, D_MODEL // 4))
    placed = {nm: _place_own(chip_arr, weights[nm], BF16, "place_" + nm) for nm in _BIG}
    g_in, g_out, g_cdk, g_fdk, g_meta = _gather_chips(
        [placed["w_in"], placed["w_out"]] + [_place_own(chip_arr, v, F32, "place_" + nm) for nm, v in small_sharded.items()])
    send_sems, recv_sems, up_buf, down_buf, token = _gather_start([placed["w_up"], placed["w_down"]], g_meta)
    meta_full = g_meta.transpose(1, 2, 0, 3).reshape(N_META, D_MODEL)
    layers = []
    for l in range(depth):
        w_in_l = g_in[:, l].transpose(1, 0, 2).reshape(D_MODEL, IN_COLS)
        w_out_l = g_out[:, l].reshape(D_MODEL, D_MODEL)
        pw = pool_w[l].astype(BF16)
        conv_v = jnp.pad(jnp.stack([conv_dw_b[l], conv_ln_g[l], conv_ln_b[l], pool_scale[l]]), ((0, 4), (0, 0)))
        layers.append(dict(
            w_in=w_in_l, conv_k=g_cdk[:, l].transpose(1, 0, 2).reshape(CONV_HIST, CONV_W), conv_v=conv_v,
            pool_w=pw, w_out=w_out_l, kf=g_fdk[:, l],
            g1=norm1_g[l][None], g2=norm2_g[l][None]))
    layers[0]["g1"] = layers[0]["g1"] + token[0, 0]

    def ffn_weights(after):
        g_up, g_down = _gather_forward(_gather_wait(send_sems, recv_sems, [up_buf, down_buf], after))
        per_layer = []
        for l in range(depth):
            per_layer.append(dict(w_down=g_down[:, l].reshape(2, FF_CHUNK, D_MODEL)))
        return g_up, per_layer

    def pair_reduced(names, g0s, g1s, tag):
        theirs = _pair_exchange(g0s, g1s, "pair_exchange_" + tag)
        return [_pair_sum_bf16(core_arr, g0s[a], g1s[a], theirs[a], "pair_sum_" + nm) for a, nm in enumerate(names)]

    in_flight = {}

    def ffn_grads_done(ffn_grads):
        parts = pair_reduced(("w_up", "w_down"), list(ffn_grads[0]), list(ffn_grads[1]), "ffn")
        send, recv, up_parts, down_parts, up_land, down_land, zero = _chip_exchange_start(parts)
        in_flight.update(sems=(send, recv), parts=[up_parts, down_parts], lands=[up_land, down_land])
        return zero[0, 0]

    loss_cols, dx, dmeta, grads, dfinal_g = _local_step(x[0], meta_full, loss_target[0], layers, ffn_weights, ffn_grads_done,
                                                         final_g[None], ROW_TILE, FFN_ROW_TILE)
    grad_x = dx[None]
    mixer_parts = pair_reduced(("w_in", "w_out"), [grads[0]["dw_in"], grads[0]["dw_out"]],
                               [grads[1]["dw_in"], grads[1]["dw_out"]], "mixer")
    parts = mixer_parts + in_flight["parts"]
    received = list(_chip_exchange(mixer_parts, "chip_exchange_mixer")) + list(_chip_exchange_wait(
        *in_flight["sems"], in_flight["parts"], in_flight["lands"], dx))
    reds = [_chip_sum(chip_arr, parts[a], received[a], "chip_sum_" + _BIG[a]) for a in range(4)]
    grad, delta, new_m, new_v = {}, {}, {}, {}
    for a, other in enumerate(_pair_share(reds)):
        nm = _BIG[a]
        grad[nm], delta[nm], new_m[nm], new_v[nm] = _adamw_pair(core_arr, weights[nm], reds[a], other, mom1[nm], mom2[nm],
                                                                "adamw_" + nm)

    pack = jnp.concatenate([
        _rows8(jnp.concatenate([grads[l]["dg1"] for l in range(depth)])),
        _rows8(jnp.concatenate([grads[l]["dg2"] for l in range(depth)])),
        _rows8(jnp.concatenate([dfinal_g, loss_cols])),
        jnp.concatenate([grads[l]["dsmall"] for l in range(depth)], axis=1),
        jnp.stack([grads[l]["dpw"] for l in range(depth)]).reshape(-1, D_MODEL),
        dmeta,
        jnp.concatenate([grads[l]["dk"] for l in range(depth)], axis=1),
        jnp.stack([grads[l]["dkf"] for l in range(depth)]).reshape(-1, D_MODEL),
    ])
    red = _all_reduce_small(jnp.pad(pack, ((0, -pack.shape[0] % 16), (0, 0))))
    o = 0
    grad["norm1_g"] = red[o:o + depth]
    o += 8
    grad["norm2_g"] = red[o:o + depth]
    o += 8
    grad["final_g"] = red[o]
    loss = red[o + 1, 0]
    o += 8
    sm = red[o:o + 8].reshape(8, depth, CONV_W)
    grad["conv_dw_b"], grad["conv_ln_g"], grad["conv_ln_b"], grad["pool_scale"] = sm[0], sm[1], sm[2], sm[3]
    o += 8
    n_pw = depth * 4 * POOL_G * POOL_G // D_MODEL
    grad["pool_w"] = red[o:o + n_pw].reshape(pool_w.shape)
    o += n_pw
    grad["meta_tokens"] = lax.dynamic_slice_in_dim(red[o:o + N_META], chip * (D_MODEL // 4), D_MODEL // 4, axis=1)
    o += N_META
    dk_all = red[o:o + CONV_HIST].reshape(CONV_HIST, depth, 4, CONV_W // 4)
    grad["conv_dw_k"] = lax.dynamic_index_in_dim(dk_all, chip, axis=2, keepdims=False)[:CONV_TAPS].transpose(1, 0, 2)
    o += CONV_HIST
    dkf_all = red[o:pack.shape[0]].reshape(depth, 4, 8, FF_CHUNK)
    grad["ffn_dw_k"] = lax.dynamic_index_in_dim(dkf_all, chip, axis=1, keepdims=False)[:, :FFN_TAPS]

    small_rows = -(-sum(weights[nm].size for nm in _SMALL) // (8 * D_MODEL)) * 8
    packed = [_pack_flat([d[nm] for nm in _SMALL], small_rows) for d in (weights, grad, mom1, mom2)]
    for res, packed_out in zip((delta, new_m, new_v), _adamw(*packed, "adamw_small")):
        for nm, val in zip(_SMALL, _unpack_flat(packed_out, [weights[nm] for nm in _SMALL])):
            res[nm] = val

    return (loss, grad_x, *[grad[nm] for nm in order], *[delta[nm] for nm in order],
            *[new_m[nm] for nm in order], *[new_v[nm] for nm in order])
```

```python
import math

import jax
import jax.numpy as jnp
from jax import lax
from jax.experimental import pallas as pl
from jax.experimental.pallas import tpu as pltpu

F32 = jnp.float32
BF16 = jnp.bfloat16

D_MODEL = 1024
CONV_W = 512
POOL_W = 512
POOL_G = 128
POOL_WINDOWS = (2, 4, 8, 16)
IN_COLS = 1536
D_FF = 2816
FF_CHUNK = 1408
CONV_TAPS = 31
CONV_HIST = 32
POOL_HIST = 16
FFN_TAPS = 3
N_META = 16
EPS = 1e-6

ADAM_LR = 0.001
ADAM_B1 = 0.9
ADAM_B2 = 0.999
ADAM_EPS = 1e-08
ADAM_WD = 0.01
ADAM_STEP = 10

ROW_TILE = 256
FFN_ROW_TILE = 384
CONV_ROW_BLOCK = 64
FFN_ROW_BLOCK = 32
V7X_VMEM_LIMIT = 56 * 1024 * 1024

MESH = pl.DeviceIdType.MESH


def _cparams(n_axes, vmem=None):
    return pltpu.CompilerParams(dimension_semantics=("arbitrary",) * n_axes, vmem_limit_bytes=vmem)


def _whole(shape, single=False):
    zeros = (0,) * len(shape)
    if single:
        return pl.BlockSpec(shape, lambda *_: zeros, pipeline_mode=pl.Buffered(1))
    return pl.BlockSpec(shape, lambda *_: zeros)


def _sigmoid(x):
    return 0.5 * jnp.tanh(0.5 * x) + 0.5


def _dot(a, b):
    return jnp.dot(a, b, preferred_element_type=F32)


def _dot_nt(a, b):
    return lax.dot_general(a, b, (((1,), (1,)), ((), ())), preferred_element_type=F32)


def _dot_tn(a, b):
    return lax.dot_general(a, b, (((0,), (0,)), ((), ())), preferred_element_type=F32)


def _split_dot(v, a_ref):
    hi = v.astype(BF16)
    lo = (v - hi.astype(F32)).astype(BF16)
    return _dot(hi, a_ref[...]) + _dot(lo, a_ref[...])


def _rms(x):
    r = lax.rsqrt(jnp.mean(x * x, axis=-1, keepdims=True) + EPS)
    return x * r, r


def _rms_bwd(dy, xhat, r, g):
    gd = dy * g
    return r * (gd - xhat * jnp.mean(gd * xhat, axis=-1, keepdims=True)), dy * xhat


def _colsum(v):
    return jnp.sum(v, axis=0, keepdims=True)


def _shifted(window, s):
    return window if s == 0 else pltpu.roll(window, window.shape[0] - s, 0)


def _conv3(window, kf_ref, cc, ls, hist):
    x2 = window[hist:]
    x1 = pltpu.roll(window, 1, 0)[hist:]
    x0 = pltpu.roll(window, 2, 0)[hist:]
    return x0, x1, x2, kf_ref[cc, 0:1, ls] * x0 + kf_ref[cc, 1:2, ls] * x1 + kf_ref[cc, 2:3, ls] * x2


def _for_row_blocks(n, rb, fn, unroll):
    if unroll:
        for r in range(n):
            fn(r * rb)
    else:
        def step(r, keep):
            fn(pl.multiple_of(r * rb, rb))
            return keep

        lax.fori_loop(0, n, step, 0)


def _fold8(v):
    part = v[0:8]
    for k in range(1, v.shape[0] // 8):
        part = part + v[8 * k:8 * k + 8]
    return part


def _inv_count(tile, tm, w):
    t = tile * tm + lax.broadcasted_iota(jnp.int32, (tm, POOL_G), 0)
    return 1.0 / jnp.minimum(t + 1, w).astype(F32)


def _embed(x, meta, g, rows, tm):
    seq = x.shape[0]
    x_tiles = -(-seq // tm)

    def body(xprev_ref, x_ref, meta_ref, g_ref, h_ref, hn_ref):
        i = pl.program_id(0)
        t = i * tm + lax.broadcasted_iota(jnp.int32, (tm, 1), 0)
        head = jnp.where(i == 0, meta_ref[...], xprev_ref[...])
        h = jnp.concatenate([head, x_ref[0:tm - N_META, :]], axis=0)
        h = jnp.where(t < N_META + seq, h, 0.0)
        h_ref[...] = h
        xhat, _ = _rms(h)
        hn_ref[...] = (xhat * g_ref[...]).astype(BF16)

    tile = pl.BlockSpec((tm, D_MODEL), lambda i: (i, 0))
    x_prev = pl.BlockSpec((N_META, D_MODEL), lambda i: (jnp.maximum(i * (tm // N_META) - 1, 0), 0))
    x_own = pl.BlockSpec((tm, D_MODEL), lambda i: (jnp.minimum(i, x_tiles - 1), 0))
    return pl.pallas_call(
        body, name="embed", grid=(rows // tm,),
        in_specs=[x_prev, x_own, _whole((N_META, D_MODEL)), _whole((1, D_MODEL))],
        out_specs=[tile, tile],
        out_shape=[jax.ShapeDtypeStruct((rows, D_MODEL), F32), jax.ShapeDtypeStruct((rows, D_MODEL), BF16)],
        compiler_params=_cparams(1),
    )(x, x, meta, g)


def _row_block(r):
    for cand in (256, 176, 128, 64, 32, 16):
        if r % cand == 0:
            return cand
    return r


def _place_own(chip, w, dtype, name):
    n, r, c = w.shape
    rb = _row_block(r)

    def body(chip_ref, w_ref, o_ref):
        o_ref[0] = w_ref[...].astype(dtype)

    return pl.pallas_call(
        body, name=name,
        grid_spec=pltpu.PrefetchScalarGridSpec(
            num_scalar_prefetch=1, grid=(n, r // rb),
            in_specs=[pl.BlockSpec((1, rb, c), lambda i, j, chip_ref: (i, j, 0))],
            out_specs=pl.BlockSpec((1, 1, rb, c), lambda i, j, chip_ref: (chip_ref[0], i, j, 0))),
        out_shape=jax.ShapeDtypeStruct((4,) + w.shape, dtype), compiler_params=_cparams(2),
    )(chip, w)


def _pair_sum_bf16(core, g0, g1, other, name):
    n, r, c = g0.shape
    rb = _row_block(r)

    def body(core_ref, g0_ref, g1_ref, o_ref, out_ref):
        mine = jnp.where(core_ref[0] == 0, g0_ref[...], g1_ref[...])
        out_ref[...] = (mine + o_ref[...]).astype(BF16)

    def layer_spec(layer):
        return pl.BlockSpec((1, rb, c), lambda i, j, core_ref: (jnp.where(core_ref[0] == layer, i, 0),
                                                                jnp.where(core_ref[0] == layer, j, 0), 0))

    spec = pl.BlockSpec((1, rb, c), lambda i, j, core_ref: (i, j, 0))
    return pl.pallas_call(
        body, name=name,
        grid_spec=pltpu.PrefetchScalarGridSpec(num_scalar_prefetch=1, grid=(n, r // rb),
                                               in_specs=[layer_spec(0), layer_spec(1), spec], out_specs=spec),
        out_shape=jax.ShapeDtypeStruct(g0.shape, BF16), compiler_params=_cparams(2),
    )(core, g0, g1, other)


def _chip_sum(chip, parts, recv, name):
    _, r, c = parts.shape
    rb = _row_block(r)

    def body(chip_ref, p_ref, r_ref, out_ref):
        got = r_ref[...].astype(F32)
        out_ref[...] = (p_ref[0].astype(F32) + got[0]) + (got[1] + got[2])

    return pl.pallas_call(
        body, name=name,
        grid_spec=pltpu.PrefetchScalarGridSpec(
            num_scalar_prefetch=1, grid=(r // rb,),
            in_specs=[pl.BlockSpec((1, rb, c), lambda j, chip_ref: (chip_ref[0], j, 0)),
                      pl.BlockSpec((3, rb, c), lambda j, chip_ref: (0, j, 0))],
            out_specs=pl.BlockSpec((rb, c), lambda j, chip_ref: (j, 0))),
        out_shape=jax.ShapeDtypeStruct((r, c), F32), compiler_params=_cparams(1),
    )(chip, parts, recv)


def _adamw_update(w, g, m, v):
    nm = ADAM_B1 * m + (1.0 - ADAM_B1) * g
    nv = ADAM_B2 * v + (1.0 - ADAM_B2) * (g * g)
    m_hat = nm / (1.0 - ADAM_B1 ** ADAM_STEP)
    v_hat = nv / (1.0 - ADAM_B2 ** ADAM_STEP)
    return -ADAM_LR * (m_hat / (jnp.sqrt(v_hat) + ADAM_EPS) + ADAM_WD * w), nm, nv


def _adamw(w, g, m, v, name):
    n, r, c = w.shape
    rb = _row_block(r)

    def body(w_ref, g_ref, m_ref, v_ref, d_ref, nm_ref, nv_ref):
        d_ref[...], nm_ref[...], nv_ref[...] = _adamw_update(w_ref[...], g_ref[...], m_ref[...], v_ref[...])

    spec = pl.BlockSpec((1, rb, c), lambda i, j: (i, j, 0))
    shp = jax.ShapeDtypeStruct(w.shape, F32)
    return pl.pallas_call(
        body, name=name, grid=(n, r // rb), in_specs=[spec] * 4, out_specs=[spec] * 3,
        out_shape=[shp] * 3, compiler_params=_cparams(2),
    )(w, g, m, v)


def _adamw_pair(core, w, mine, theirs, m, v, name):
    n, r, c = w.shape
    rb = _row_block(r)

    def body(core_ref, w_ref, a_ref, b_ref, m_ref, v_ref, g_ref, d_ref, nm_ref, nv_ref):
        g = jnp.where(pl.program_id(0) == core_ref[0], a_ref[...], b_ref[...])
        g_ref[0] = g
        d_ref[0], nm_ref[0], nv_ref[0] = _adamw_update(w_ref[0], g, m_ref[0], v_ref[0])

    spec = pl.BlockSpec((1, rb, c), lambda i, j, core_ref: (i, j, 0))
    flat = pl.BlockSpec((rb, c), lambda i, j, core_ref: (j, 0))
    shp = jax.ShapeDtypeStruct(w.shape, F32)
    return pl.pallas_call(
        body, name=name,
        grid_spec=pltpu.PrefetchScalarGridSpec(num_scalar_prefetch=1, grid=(n, r // rb),
                                               in_specs=[spec, flat, flat, spec, spec], out_specs=[spec] * 4),
        out_shape=[shp] * 4, compiler_params=_cparams(2),
    )(core, w, mine, theirs, m, v)


def _mixer_fwd(h0, hn, w_in, conv_k, conv_v, avg, pool_w, w_out, g2, tm):
    rows = h0.shape[0]
    rb = CONV_ROW_BLOCK

    def body(h0_ref, hn_ref, win_ref, ck_ref, cv_ref, avg_ref, pw_ref, wout_ref, g2_ref,
             z_ref, u1_ref, m_ref, h1_ref, hn2_ref, ubuf, pbuf):
        i = pl.program_id(0)

        @pl.when(i == 0)
        def _():
            ubuf[pl.ds(0, CONV_HIST), :] = jnp.zeros((CONV_HIST, CONV_W), F32)
            pbuf[pl.ds(0, POOL_HIST), :] = jnp.zeros((POOL_HIST, POOL_W), F32)

        z = _dot(hn_ref[...], win_ref[...])
        z_ref[...] = z
        ubuf[pl.ds(CONV_HIST, tm), :] = z[:, :CONV_W] * _sigmoid(z[:, CONV_W:2 * CONV_W])
        p = z[:, 2 * CONV_W:]
        pbuf[pl.ds(POOL_HIST, tm), :] = p

        def conv_block(r0):
            for l in range(CONV_W // 128):
                ls = pl.ds(128 * l, 128)
                window = ubuf[pl.ds(r0, rb + CONV_HIST), ls]
                acc = jnp.broadcast_to(cv_ref[0:1, ls], (rb, 128))
                for s in range(8):
                    ws = _shifted(window, s)
                    for q in range(CONV_HIST // 8 + 1):
                        j = 8 * q + s - 2
                        if 0 <= j < CONV_TAPS:
                            acc = acc + ck_ref[j:j + 1, ls] * ws[8 * q:8 * q + rb]
                u1_ref[pl.ds(r0, rb), ls] = acc

        _for_row_blocks(tm // rb, rb, conv_block, unroll=True)
        ubuf[pl.ds(0, CONV_HIST), :] = ubuf[pl.ds(tm, CONV_HIST), :]

        u1 = u1_ref[...]
        cen = u1 - _split_dot(u1, avg_ref)
        xhat = cen * lax.rsqrt(_split_dot(cen * cen, avg_ref) + EPS)
        u2 = xhat * cv_ref[1:2, :] + cv_ref[2:3, :]
        m_ref[:, 0:CONV_W] = (u2 * _sigmoid(u2)).astype(BF16)

        for gi, w in enumerate(POOL_WINDOWS):
            ls = pl.ds(POOL_G * gi, POOL_G)
            s = pbuf[pl.ds(POOL_HIST, tm), ls]
            for j in range(1, w):
                s = s + pbuf[pl.ds(POOL_HIST - j, tm), ls]
            d = s * _inv_count(i, tm, w) - p[:, POOL_G * gi:POOL_G * (gi + 1)]
            y = _dot(d.astype(BF16), pw_ref[gi]) * cv_ref[3:4, ls]
            m_ref[:, pl.ds(CONV_W + POOL_G * gi, POOL_G)] = y.astype(BF16)
        pbuf[pl.ds(0, POOL_HIST), :] = pbuf[pl.ds(tm, POOL_HIST), :]

        h1 = h0_ref[...] + _dot(m_ref[...], wout_ref[...])
        h1_ref[...] = h1
        xh, _ = _rms(h1)
        hn2_ref[...] = (xh * g2_ref[...]).astype(BF16)

    def tile(c):
        return pl.BlockSpec((tm, c), lambda i: (i, 0))

    return pl.pallas_call(
        body, name="mixer_fwd", grid=(rows // tm,),
        in_specs=[tile(D_MODEL), tile(D_MODEL), _whole((D_MODEL, IN_COLS)), _whole((CONV_HIST, CONV_W)),
                  _whole((8, CONV_W)), _whole((CONV_W, CONV_W)), _whole((4, POOL_G, POOL_G)),
                  _whole((D_MODEL, D_MODEL)), _whole((1, D_MODEL))],
        out_specs=[tile(IN_COLS), tile(CONV_W), tile(D_MODEL), tile(D_MODEL), tile(D_MODEL)],
        out_shape=[jax.ShapeDtypeStruct((rows, IN_COLS), F32), jax.ShapeDtypeStruct((rows, CONV_W), F32),
                   jax.ShapeDtypeStruct((rows, D_MODEL), BF16), jax.ShapeDtypeStruct((rows, D_MODEL), F32),
                   jax.ShapeDtypeStruct((rows, D_MODEL), BF16)],
        scratch_shapes=[pltpu.VMEM((CONV_HIST + tm, CONV_W), F32), pltpu.VMEM((POOL_HIST + tm, POOL_W), F32)],
        compiler_params=_cparams(1, V7X_VMEM_LIMIT),
    )(h0, hn, w_in, conv_k, conv_v, avg, pool_w, w_out, g2)


def _ffn_fwd(h1, hn2, w_up, layer, kf, w_down, g_next, tm):
    rows = h1.shape[0]
    hist = 8
    w_up_spec = pl.BlockSpec((4, None, D_MODEL, FF_CHUNK), lambda i: (0, layer, 0, 0), pipeline_mode=pl.Buffered(1))

    rb = FFN_ROW_BLOCK

    def body(h1_ref, hn2_ref, wup_ref, kf_ref, wdn_ref, gn_ref, ug_ref, h2_ref, hnn_ref, wg, wv, carry, act_s, acc):
        i = pl.program_id(0)

        @pl.when(i == 0)
        def _():
            carry[...] = jnp.zeros(carry.shape, F32)

        acc[...] = h1_ref[...]
        for c in range(2):
            for buf, cc in ((wg, c), (wv, c + 2)):
                ug = _dot(hn2_ref[...], wup_ref[cc])
                ug_ref[cc] = ug.astype(BF16)
                buf[pl.ds(0, hist), :] = carry[cc]
                buf[pl.ds(hist, tm), :] = ug
                carry[cc] = buf[pl.ds(tm, hist), :]

            def act_block(r0):
                for l in range(FF_CHUNK // 128):
                    ls = pl.ds(128 * l, 128)
                    gate = _conv3(wg[pl.ds(r0, rb + hist), ls], kf_ref, c, ls, hist)[3]
                    val = _conv3(wv[pl.ds(r0, rb + hist), ls], kf_ref, c + 2, ls, hist)[3]
                    act_s[pl.ds(r0, rb), ls] = (gate * _sigmoid(gate) * val).astype(BF16)

            _for_row_blocks(tm // rb, rb, act_block, unroll=True)
            acc[...] += _dot(act_s[...], wdn_ref[c])
        h2 = acc[...]
        h2_ref[...] = h2
        xh, _ = _rms(h2)
        hnn_ref[...] = (xh * gn_ref[...]).astype(BF16)

    def tile(c):
        return pl.BlockSpec((tm, c), lambda i: (i, 0))

    return pl.pallas_call(
        body, name="ffn_fwd", grid=(rows // tm,),
        in_specs=[tile(D_MODEL), tile(D_MODEL), w_up_spec, _whole((4, 8, FF_CHUNK)),
                  _whole((2, FF_CHUNK, D_MODEL), single=True), _whole((1, D_MODEL))],
        out_specs=[pl.BlockSpec((4, tm, FF_CHUNK), lambda i: (0, i, 0)), tile(D_MODEL), tile(D_MODEL)],
        out_shape=[jax.ShapeDtypeStruct((4, rows, FF_CHUNK), BF16), jax.ShapeDtypeStruct((rows, D_MODEL), F32),
                   jax.ShapeDtypeStruct((rows, D_MODEL), BF16)],
        scratch_shapes=[pltpu.VMEM((hist + tm, FF_CHUNK), F32), pltpu.VMEM((hist + tm, FF_CHUNK), F32),
                        pltpu.VMEM((4, hist, FF_CHUNK), F32), pltpu.VMEM((tm, FF_CHUNK), BF16),
                        pltpu.VMEM((tm, D_MODEL), F32)],
        compiler_params=_cparams(1, V7X_VMEM_LIMIT),
    )(h1, hn2, w_up, kf, w_down, g_next)


def _loss_head(h, tgt, g, seq, tm):
    rows = h.shape[0]
    tgt_tiles = -(-seq // tm)

    def body(h_ref, tprev_ref, t_ref, g_ref, dh_ref, loss_ref, dg_ref):
        i = pl.program_id(0)

        @pl.when(i == 0)
        def _():
            loss_ref[...] = jnp.zeros(loss_ref.shape, F32)
            dg_ref[...] = jnp.zeros(dg_ref.shape, F32)

        t = i * tm + lax.broadcasted_iota(jnp.int32, (tm, 1), 0)
        inside = jnp.logical_and(t >= N_META, t < N_META + seq)
        tgt = jnp.concatenate([tprev_ref[...], t_ref[0:tm - N_META, :]], axis=0)
        xhat, r = _rms(h_ref[...])
        err = jnp.where(inside, xhat * g_ref[...] - tgt, 0.0)
        loss_ref[...] += _colsum(err * err)
        dh, dg_rows = _rms_bwd(err * (1.0 / D_MODEL), xhat, r, g_ref[...])
        dh_ref[...] = dh
        dg_ref[...] += _colsum(dg_rows)

        @pl.when(i == rows // tm - 1)
        def _():
            total = jnp.sum(loss_ref[...], axis=1, keepdims=True) * (0.5 / D_MODEL)
            loss_ref[...] = jnp.broadcast_to(total, loss_ref.shape)

    tile = pl.BlockSpec((tm, D_MODEL), lambda i: (i, 0))
    t_prev = pl.BlockSpec((N_META, D_MODEL), lambda i: (jnp.maximum(i * (tm // N_META) - 1, 0), 0))
    t_own = pl.BlockSpec((tm, D_MODEL), lambda i: (jnp.minimum(i, tgt_tiles - 1), 0))
    vec = jax.ShapeDtypeStruct((1, D_MODEL), F32)
    return pl.pallas_call(
        body, name="loss_head", grid=(rows // tm,),
        in_specs=[tile, t_prev, t_own, _whole((1, D_MODEL))],
        out_specs=[tile, _whole((1, D_MODEL)), _whole((1, D_MODEL))],
        out_shape=[jax.ShapeDtypeStruct((rows, D_MODEL), F32), vec, vec],
        compiler_params=_cparams(1),
    )(h, tgt, tgt, g)


def _ffn_bwd(dh2, hn2, ug0, w_down, w_up, layer, kf, tm):
    rows = dh2.shape[0]
    nt = rows // tm
    hist = 16
    fut = 8
    rb = FFN_ROW_BLOCK
    near = 8

    def body(dh2_ref, hn2_ref, ugg_ref, ugv_ref, hg_ref, hv_ref, wd_ref, wug_ref, wuv_ref, kf_ref,
             dhn_ref, dkf_ref, dwup_ref, dwdn_ref, wg, wv, dgb, dvb, carry, dkacc, act_s, dug_s, acc_up, acc_dn):
        c = pl.program_id(0)
        i = pl.program_id(1)
        first_tile = jnp.where(i == nt - 1, 1.0, 0.0)

        @pl.when(i == 0)
        def _():
            carry[...] = jnp.zeros(carry.shape, F32)
            dkacc[...] = jnp.zeros(dkacc.shape, F32)
            acc_up[...] = jnp.zeros(acc_up.shape, F32)
            acc_dn[...] = jnp.zeros(acc_dn.shape, F32)

        def run():
            n_blocks = tm // rb
            dh2b = dh2_ref[...].astype(BF16)
            sides = ((wg, dgb, ugg_ref, hg_ref, 0, c), (wv, dvb, ugv_ref, hv_ref, 1, c + 2))
            for buf, dbuf, u_ref, h_ref, s, cc in sides:
                buf[pl.ds(0, hist), :] = h_ref[0].astype(F32) * (1.0 - first_tile)
                buf[pl.ds(hist, tm), :] = u_ref[0].astype(F32)
                dbuf[pl.ds(tm, fut), :] = carry[s]
            dgb[pl.ds(0, tm), :] = _dot_nt(dh2b, wd_ref[0])

            def grad_block(r0):
                for l in range(FF_CHUNK // 128):
                    ls = pl.ds(128 * l, 128)
                    g0, g1, g2, gate = _conv3(wg[pl.ds(r0 + hist - near, rb + near), ls], kf_ref, c, ls, near)
                    v0, v1, v2, val = _conv3(wv[pl.ds(r0 + hist - near, rb + near), ls], kf_ref, c + 2, ls, near)
                    sg = _sigmoid(gate)
                    silu = gate * sg
                    act_s[pl.ds(r0, rb), ls] = (silu * val).astype(BF16)
                    dact = dgb[pl.ds(r0, rb), ls]
                    dgate = dact * val * (sg * (1.0 + gate * (1.0 - sg)))
                    dval = dact * silu
                    dgb[pl.ds(r0, rb), ls] = dgate
                    dvb[pl.ds(r0, rb), ls] = dval
                    for s, dv, taps in ((0, dgate, (g0, g1, g2)), (1, dval, (v0, v1, v2))):
                        for j in range(FFN_TAPS):
                            dkacc[s, pl.ds(8 * j, 8), ls] += _fold8(dv * taps[j])

            _for_row_blocks(n_blocks, rb, grad_block, unroll=True)
            for buf, dbuf, u_ref, h_ref, s, cc in sides:
                carry[s] = dbuf[pl.ds(0, fut), :]

            def conv_block(r0):
                for l in range(FF_CHUNK // 128):
                    ls = pl.ds(128 * l, 128)
                    for buf, dbuf, u_ref, h_ref, s, cc in sides:
                        window = dbuf[pl.ds(r0, rb + fut), ls]
                        dug0 = (kf_ref[cc, 0:1, ls] * _shifted(window, 2)[0:rb] + kf_ref[cc, 1:2, ls] * _shifted(window, 1)[0:rb]
                                + kf_ref[cc, 2:3, ls] * window[0:rb])
                        dug_s[s, pl.ds(r0, rb), ls] = dug0.astype(BF16)

            _for_row_blocks(n_blocks, rb, conv_block, unroll=True)
            dhn_ref[0] = _dot_nt(dug_s[0], wug_ref[0]) + _dot_nt(dug_s[1], wuv_ref[0])
            acc_up[0] += _dot_tn(hn2_ref[...], dug_s[0])
            acc_up[1] += _dot_tn(hn2_ref[...], dug_s[1])
            acc_dn[...] += _dot_tn(act_s[...], dh2b)

        run()

        @pl.when(i == nt - 1)
        def _():
            for s in range(2):
                for j in range(FFN_TAPS):
                    dkf_ref[c + 2 * s, j:j + 1, :] = _colsum(dkacc[s, pl.ds(8 * j, 8), :])
                dkf_ref[c + 2 * s, FFN_TAPS:8, :] = jnp.zeros((8 - FFN_TAPS, FF_CHUNK), F32)
            pltpu.sync_copy(acc_up, dwup_ref.at[c])
            pltpu.sync_copy(acc_dn, dwdn_ref.at[c])

    def tile(cols):
        return pl.BlockSpec((tm, cols), lambda c, i: (nt - 1 - i, 0))

    def chunk(off, r, halo_rows=None):
        if halo_rows is None:
            return pl.BlockSpec((1, r, FF_CHUNK), lambda c, i: (c + off, nt - 1 - i, 0))
        return pl.BlockSpec((1, r, FF_CHUNK), lambda c, i: (c + off, jnp.maximum((nt - 1 - i) * (tm // r) - 1, 0), 0))

    def up_chunk(off):
        return pl.BlockSpec((1, None, D_MODEL, FF_CHUNK), lambda c, i: (c + off, layer, 0, 0), pipeline_mode=pl.Buffered(1))

    down_chunk = pl.BlockSpec((1, FF_CHUNK, D_MODEL), lambda c, i: (c, 0, 0), pipeline_mode=pl.Buffered(1))

    return pl.pallas_call(
        body, name="ffn_bwd", grid=(2, nt),
        in_specs=[tile(D_MODEL), tile(D_MODEL), chunk(0, tm), chunk(2, tm), chunk(0, hist, True), chunk(2, hist, True),
                  down_chunk, up_chunk(0), up_chunk(2), _whole((4, 8, FF_CHUNK))],
        out_specs=[pl.BlockSpec((1, tm, D_MODEL), lambda c, i: (c, nt - 1 - i, 0)), _whole((4, 8, FF_CHUNK)), _ANY, _ANY],
        out_shape=[jax.ShapeDtypeStruct((2, rows, D_MODEL), F32), jax.ShapeDtypeStruct((4, 8, FF_CHUNK), F32),
                   jax.ShapeDtypeStruct((2, 2, D_MODEL, FF_CHUNK), F32), jax.ShapeDtypeStruct((2, FF_CHUNK, D_MODEL), F32)],
        scratch_shapes=[pltpu.VMEM((hist + tm, FF_CHUNK), F32), pltpu.VMEM((hist + tm, FF_CHUNK), F32),
                        pltpu.VMEM((tm + fut, FF_CHUNK), F32), pltpu.VMEM((tm + fut, FF_CHUNK), F32),
                        pltpu.VMEM((2, fut, FF_CHUNK), F32), pltpu.VMEM((2, 8 * FFN_TAPS, FF_CHUNK), F32),
                        pltpu.VMEM((tm, FF_CHUNK), BF16), pltpu.VMEM((2, tm, FF_CHUNK), BF16),
                        pltpu.VMEM((2, D_MODEL, FF_CHUNK), F32), pltpu.VMEM((FF_CHUNK, D_MODEL), F32)],
        compiler_params=_cparams(2, V7X_VMEM_LIMIT),
    )(dh2, hn2, ug0, ug0, ug0, ug0, w_down, w_up, w_up, kf)


def _mixer_bwd(dh2, dhn2, h1, g2, h0, z, u1, hn1, m, w_out, w_in, conv_k, conv_v, avg, pool_w, g1, tm,
               x_rows=None):
    rows = dh2.shape[0]
    nt = rows // tm
    rb = CONV_ROW_BLOCK
    in_shard = IN_COLS // 4

    def body(dh2_ref, dhn2_ref, h1_ref, g2_ref, h0_ref, z_ref, zh_ref, u1_ref, hn1_ref, m_ref, wo_ref, wi_ref, ck_ref,
             cv_ref, avg_ref, pw_ref, g1_ref,
             dh0_ref, dtop_ref, dk_ref, ds_ref, dpw_ref, dg1_ref, dg2_ref, dwo_ref, dwi_ref,
             ubuf, dbuf, pbuf, ebuf, dcarry, ecarry, dkacc, dzs, dz_s, dh1_s, acc_out, acc_in, xcarry):
        i = pl.program_id(0)
        ti = nt - 1 - i
        has_past = jnp.where(ti > 0, 1.0, 0.0)

        @pl.when(i == 0)
        def _():
            dcarry[...] = jnp.zeros(dcarry.shape, F32)
            ecarry[...] = jnp.zeros(ecarry.shape, F32)
            dkacc[...] = jnp.zeros(dkacc.shape, F32)
            ds_ref[...] = jnp.zeros(ds_ref.shape, F32)
            dpw_ref[...] = jnp.zeros(dpw_ref.shape, F32)
            dg1_ref[...] = jnp.zeros(dg1_ref.shape, F32)
            dg2_ref[...] = jnp.zeros(dg2_ref.shape, F32)
            acc_out[...] = jnp.zeros(acc_out.shape, F32)
            acc_in[...] = jnp.zeros(acc_in.shape, F32)
            xcarry[...] = jnp.zeros(xcarry.shape, F32)

        xh1, r1 = _rms(h1_ref[...])
        dx1, dg2_rows = _rms_bwd(dhn2_ref[0] + dhn2_ref[1], xh1, r1, g2_ref[...])
        dh1_s[...] = dh2_ref[...] + dx1
        dg2_ref[...] += _colsum(dg2_rows)
        dh1b = dh1_s[...].astype(BF16)
        acc_out[...] += _dot_tn(m_ref[...], dh1b)
        dm = _dot_nt(dh1b, wo_ref[...])
        z = z_ref[...]
        a = z[:, :CONV_W]
        sg = _sigmoid(z[:, CONV_W:2 * CONV_W])
        p = z[:, 2 * CONV_W:]
        zh = zh_ref[...] * has_past
        ubuf[pl.ds(0, CONV_HIST), :] = zh[:, :CONV_W] * _sigmoid(zh[:, CONV_W:2 * CONV_W])
        ubuf[pl.ds(CONV_HIST, tm), :] = a * sg
        pbuf[pl.ds(0, POOL_HIST), :] = zh[CONV_HIST - POOL_HIST:, 2 * CONV_W:]
        pbuf[pl.ds(POOL_HIST, tm), :] = p

        u1 = u1_ref[...]
        cen = u1 - _split_dot(u1, avg_ref)
        rstd = lax.rsqrt(_split_dot(cen * cen, avg_ref) + EPS)
        xhat = cen * rstd
        u2 = xhat * cv_ref[1:2, :] + cv_ref[2:3, :]
        s2 = _sigmoid(u2)
        du2 = dm[:, :CONV_W] * (s2 * (1.0 + u2 * (1.0 - s2)))
        ds_ref[1:2, :] += _colsum(du2 * xhat)
        ds_ref[2:3, :] += _colsum(du2)
        dxh = du2 * cv_ref[1:2, :]
        du1 = rstd * (dxh - _split_dot(dxh, avg_ref) - xhat * _split_dot(dxh * xhat, avg_ref))
        ds_ref[0:1, :] += _colsum(du1)
        dbuf[pl.ds(0, tm), :] = du1
        dbuf[pl.ds(tm, CONV_HIST), :] = dcarry[...]
        dcarry[...] = dbuf[pl.ds(0, CONV_HIST), :]

        def conv_block(r0):
            for l in range(CONV_W // 128):
                ls = pl.ds(128 * l, 128)
                dwin = dbuf[pl.ds(r0, rb + CONV_HIST), ls]
                uwin = ubuf[pl.ds(r0, rb + CONV_HIST), ls]
                dblk = dwin[0:rb]
                du0 = jnp.zeros((rb, 128), F32)
                for s in range(8):
                    ds_ = _shifted(dwin, s)
                    us_ = _shifted(uwin, s)
                    for q in range(CONV_HIST // 8 + 1):
                        o = 8 * q + s
                        if 0 <= CONV_TAPS - 1 - o < CONV_TAPS:
                            j = CONV_TAPS - 1 - o
                            du0 = du0 + ck_ref[j:j + 1, ls] * ds_[8 * q:8 * q + rb]
                        j = o - 2
                        if 0 <= j < CONV_TAPS:
                            prod = dblk * us_[8 * q:8 * q + rb]
                            part = prod[0:8]
                            for v in range(1, rb // 8):
                                part = part + prod[8 * v:8 * v + 8]
                            dkacc[pl.ds(8 * j, 8), ls] += part
                dzs[pl.ds(r0, rb), ls] = du0

        _for_row_blocks(tm // rb, rb, conv_block, unroll=True)
        du0 = dzs[:, 0:CONV_W]
        dz_s[:, 0:CONV_W] = (du0 * sg).astype(BF16)
        dz_s[:, CONV_W:2 * CONV_W] = (du0 * a * sg * (1.0 - sg)).astype(BF16)

        for gi, w in enumerate(POOL_WINDOWS):
            ls = pl.ds(POOL_G * gi, POOL_G)
            cols = slice(CONV_W + POOL_G * gi, CONV_W + POOL_G * (gi + 1))
            inv = _inv_count(ti, tm, w)
            s = pbuf[pl.ds(POOL_HIST, tm), ls]
            for j in range(1, w):
                s = s + pbuf[pl.ds(POOL_HIST - j, tm), ls]
            d = (s * inv - p[:, POOL_G * gi:POOL_G * (gi + 1)]).astype(BF16)
            dyp = dm[:, cols]
            ds_ref[3:4, ls] += _colsum(dyp * _dot(d, pw_ref[gi]))
            dyb = (dyp * cv_ref[3:4, ls]).astype(BF16)
            dpw_ref[gi] += _dot_tn(d, dyb)
            dd = _dot_nt(dyb, pw_ref[gi])
            ebuf[pl.ds(0, tm), ls] = dd * inv
            ebuf[pl.ds(tm, POOL_HIST), ls] = ecarry[:, ls]
            dp = ebuf[pl.ds(0, tm), ls] - dd
            for j in range(1, w):
                dp = dp + ebuf[pl.ds(j, tm), ls]
            dz_s[:, pl.ds(2 * CONV_W + POOL_G * gi, POOL_G)] = dp.astype(BF16)
        ecarry[...] = ebuf[pl.ds(0, POOL_HIST), :]

        acc_in[...] += _dot_tn(hn1_ref[...], dz_s[...])
        dhn = _dot_nt(dz_s[...], wi_ref[...])
        xh, r = _rms(h0_ref[...])
        dx, dg_rows = _rms_bwd(dhn, xh, r, g1_ref[...])
        dh0 = dh1_s[...] + dx
        if x_rows is None:
            dh0_ref[...] = dh0
        else:
            dh0_ref[0:tm - N_META, :] = dh0[N_META:]
            dh0_ref[tm - N_META:tm, :] = xcarry[...]
            xcarry[...] = dh0[0:N_META]
        dg1_ref[...] += _colsum(dg_rows)

        @pl.when(i == nt - 1)
        def _():
            dtop_ref[...] = dh0[0:N_META]
            for j in range(CONV_TAPS):
                dk_ref[j:j + 1, :] = _colsum(dkacc[pl.ds(8 * j, 8), :])
            dk_ref[CONV_TAPS:CONV_HIST, :] = jnp.zeros((CONV_HIST - CONV_TAPS, CONV_W), F32)
            pltpu.sync_copy(acc_out, dwo_ref)
            for k in range(4):
                pltpu.sync_copy(acc_in.at[:, pl.ds(in_shard * k, in_shard)], dwi_ref.at[k])

    def tile(c):
        return pl.BlockSpec((tm, c), lambda i: (nt - 1 - i, 0))

    halo = pl.BlockSpec((CONV_HIST, IN_COLS), lambda i: (jnp.maximum((nt - 1 - i) * (tm // CONV_HIST) - 1, 0), 0))
    vec = jax.ShapeDtypeStruct((1, D_MODEL), F32)
    if x_rows is None:
        first_spec, first_shape = tile(D_MODEL), jax.ShapeDtypeStruct((rows, D_MODEL), F32)
    else:
        x_tiles = -(-x_rows // tm)
        first_spec = pl.BlockSpec((tm, D_MODEL), lambda i: (jnp.minimum(nt - 1 - i, x_tiles - 1), 0))
        first_shape = jax.ShapeDtypeStruct((x_rows, D_MODEL), F32)
    return pl.pallas_call(
        body, name="mixer_bwd", grid=(nt,),
        in_specs=[tile(D_MODEL), pl.BlockSpec((2, tm, D_MODEL), lambda i: (0, nt - 1 - i, 0)), tile(D_MODEL),
                  _whole((1, D_MODEL)), tile(D_MODEL), tile(IN_COLS), halo, tile(CONV_W), tile(D_MODEL), tile(D_MODEL),
                  _whole((D_MODEL, D_MODEL)), _whole((D_MODEL, IN_COLS)), _whole((CONV_HIST, CONV_W)), _whole((8, CONV_W)),
                  _whole((CONV_W, CONV_W)), _whole((4, POOL_G, POOL_G)), _whole((1, D_MODEL))],
        out_specs=[first_spec, _whole((N_META, D_MODEL)), _whole((CONV_HIST, CONV_W)), _whole((8, CONV_W)),
                   _whole((4, POOL_G, POOL_G)), _whole((1, D_MODEL)), _whole((1, D_MODEL)), _ANY, _ANY],
        out_shape=[first_shape, jax.ShapeDtypeStruct((N_META, D_MODEL), F32), jax.ShapeDtypeStruct((CONV_HIST, CONV_W), F32),
                   jax.ShapeDtypeStruct((8, CONV_W), F32), jax.ShapeDtypeStruct((4, POOL_G, POOL_G), F32), vec, vec,
                   jax.ShapeDtypeStruct((D_MODEL, D_MODEL), F32), jax.ShapeDtypeStruct((4, D_MODEL, in_shard), F32)],
        scratch_shapes=[pltpu.VMEM((CONV_HIST + tm, CONV_W), F32), pltpu.VMEM((tm + CONV_HIST, CONV_W), F32),
                        pltpu.VMEM((POOL_HIST + tm, POOL_W), F32), pltpu.VMEM((tm + POOL_HIST, POOL_W), F32),
                        pltpu.VMEM((CONV_HIST, CONV_W), F32), pltpu.VMEM((POOL_HIST, POOL_W), F32),
                        pltpu.VMEM((8 * CONV_HIST, CONV_W), F32), pltpu.VMEM((tm, CONV_W), F32),
                        pltpu.VMEM((tm, IN_COLS), BF16), pltpu.VMEM((tm, D_MODEL), F32),
                        pltpu.VMEM((D_MODEL, D_MODEL), F32), pltpu.VMEM((D_MODEL, IN_COLS), F32),
                        pltpu.VMEM((N_META, D_MODEL), F32)],
        compiler_params=_cparams(1, V7X_VMEM_LIMIT),
    )(dh2, dhn2, h1, g2, h0, z, z, u1, hn1, m, w_out, w_in, conv_k, conv_v, avg, pool_w, g1)


def _head_average():
    head = lax.broadcasted_iota(jnp.int32, (CONV_W, CONV_W), 0) // 64
    return jnp.where(head == head.T, 1.0 / 64, 0.0).astype(BF16)


def _local_step(x, meta, tgt, layers, ffn_weights, ffn_grads_done, final_g, tm, tm_ffn):
    avg = _head_average()
    depth = len(layers)
    saved = []
    seq = x.shape[0]
    step = math.lcm(tm, tm_ffn)
    rows = -(-(N_META + seq) // step) * step
    h, hn = _embed(x, meta, layers[0]["g1"], rows, tm)
    for l, w in enumerate(layers):
        g_next = layers[l + 1]["g1"] if l + 1 < depth else final_g
        z, u1, m, h1, hn2 = _mixer_fwd(h, hn, w["w_in"], w["conv_k"], w["conv_v"], avg, w["pool_w"], w["w_out"], w["g2"], tm)
        if l == 0:
            w_up_all, ffn = ffn_weights(hn2)
        ug0, h2, hn_next = _ffn_fwd(h1, hn2, w_up_all, l, w["kf"], ffn[l]["w_down"], g_next, tm_ffn)
        saved.append((h, hn, z, u1, m, h1, hn2, ug0))
        h, hn = h2, hn_next
    dh, loss_cols, dfinal_g = _loss_head(h, tgt, final_g, seq, tm)

    grads = [None] * depth
    ffn_grads = [None] * depth
    for l in reversed(range(depth)):
        w = layers[l]
        h0, hn1, z, u1, m, h1, hn2, ug0 = saved[l]
        dhn2, dkf, dw_up, dw_down = _ffn_bwd(dh, hn2, ug0, ffn[l]["w_down"], w_up_all, l, w["kf"], tm)
        ffn_grads[l] = (dw_up.transpose(1, 0, 2, 3).reshape(4, D_MODEL, FF_CHUNK), dw_down.reshape(4, D_FF // 4, D_MODEL))
        g2 = w["g2"] + ffn_grads_done(ffn_grads) if l == 0 else w["g2"]
        dh0, dtop, dk, dsmall, dpw, dg1, dg2, dw_out, dw_in = _mixer_bwd(
            dh, dhn2, h1, g2, h0, z, u1, hn1, m, w["w_out"], w["w_in"], w["conv_k"], w["conv_v"], avg,
            w["pool_w"], w["g1"], tm, x_rows=seq if l == 0 else None)
        grads[l] = dict(dw_in=dw_in, dw_out=dw_out.reshape(4, D_MODEL // 4, D_MODEL), dk=dk, dsmall=dsmall, dpw=dpw,
                        dg1=dg1, dg2=dg2, dkf=dkf)
        dh = dh0
    return loss_cols, dh, dtop, grads, dfinal_g


_ANY = pl.BlockSpec(memory_space=pl.ANY)


def _place():
    x, y, c = lax.axis_index("x"), lax.axis_index("y"), lax.axis_index("c")
    chips = [(1 - x, y), (x, 1 - y), (1 - x, 1 - y)]
    return x, y, c, chips


def _remote(src, dst, send_sems, recv_sems, idx, to):
    return pltpu.make_async_remote_copy(src_ref=src, dst_ref=dst, send_sem=send_sems.at[idx], recv_sem=recv_sems.at[idx],
                                        device_id=to, device_id_type=MESH)


def _gather_chips(xs):
    n = len(xs)

    def body(*refs):
        x_refs, o_refs = refs[:n], refs[n:2 * n]
        send_sems, recv_sems = refs[2 * n:]
        x, y, c, chips = _place()
        k = 2 * x + y
        sibling = (x, y, 1 - c)
        sends = []
        for j, chip in enumerate(chips):
            for a in range(n):
                sends.append(_remote(x_refs[a].at[k, c], o_refs[a].at[k, c], send_sems, recv_sems, 3 * a + j, (*chip, c)))
                sends[-1].start()
        for j, chip in enumerate(chips):
            kj = 2 * chip[0] + chip[1]
            for a in range(n):
                landed = o_refs[a].at[kj, c]
                _remote(landed, landed, send_sems, recv_sems, 3 * a + j, sibling).wait_recv()
                sends.append(_remote(landed, landed, send_sems, recv_sems, 3 * n + 3 * a + j, sibling))
                sends[-1].start()
        for j, chip in enumerate(chips):
            kj = 2 * chip[0] + chip[1]
            for a in range(n):
                passed = o_refs[a].at[kj, 1 - c]
                _remote(passed, passed, send_sems, recv_sems, 3 * n + 3 * a + j, sibling).wait_recv()
        for cp in sends:
            cp.wait_send()

    return pl.pallas_call(
        body, name="gather_chips", in_specs=[_ANY] * n, out_specs=[_ANY] * n,
        out_shape=[jax.ShapeDtypeStruct(v.shape, v.dtype) for v in xs],
        input_output_aliases={a: a for a in range(n)},
        scratch_shapes=[pltpu.SemaphoreType.DMA((6 * n,)), pltpu.SemaphoreType.DMA((6 * n,))],
    )(*xs)


_HBM = pl.BlockSpec(memory_space=pltpu.HBM)
_SEM = pl.BlockSpec(memory_space=pltpu.SEMAPHORE)
_SPLIT_COPY = pltpu.CompilerParams(has_side_effects=pltpu.SideEffectType.DATAFLOW_SIDE_EFFECTING)


def _gather_start(xs, after):
    n = len(xs)

    def body(*refs):
        x_refs = refs[:n]
        send_sems, recv_sems = refs[n + 1], refs[n + 2]
        token = refs[2 * n + 3]
        x, y, c, chips = _place()
        k = 2 * x + y
        for j, chip in enumerate(chips):
            for a in range(n):
                mine = x_refs[a].at[k, c]
                _remote(mine, mine, send_sems, recv_sems, 3 * a + j, (*chip, c)).start()
        token[...] = jnp.zeros(token.shape, F32)

    return pl.pallas_call(
        body, name="gather_start", in_specs=[_HBM] * n + [_ANY],
        out_specs=(_SEM, _SEM, *[_HBM] * n, pl.BlockSpec(memory_space=pltpu.VMEM)),
        out_shape=(pltpu.SemaphoreType.DMA((3 * n,)), pltpu.SemaphoreType.DMA((3 * n,)),
                   *[pltpu.HBM(v.shape, v.dtype) for v in xs], jax.ShapeDtypeStruct((8, 128), F32)),
        input_output_aliases={a: 2 + a for a in range(n)}, compiler_params=_SPLIT_COPY,
    )(*[pltpu.with_memory_space_constraint(v, pltpu.HBM) for v in xs], after)


def _gather_wait(send_sems, recv_sems, xs, after):
    n = len(xs)

    def body(*refs):
        x_refs = refs[:n]
        send_sems, recv_sems = refs[n], refs[n + 1]
        x, y, c, chips = _place()
        k = 2 * x + y
        for j, chip in enumerate(chips):
            kj = 2 * chip[0] + chip[1]
            for a in range(n):
                cp = _remote(x_refs[a].at[k, c], x_refs[a].at[kj, c], send_sems, recv_sems, 3 * a + j, (*chip, c))
                cp.wait_send()
                cp.wait_recv()

    return pl.pallas_call(
        body, name="gather_wait", in_specs=[_HBM] * n + [_SEM, _SEM, _ANY], out_specs=[_HBM] * n,
        out_shape=[pltpu.HBM(v.shape, v.dtype) for v in xs],
        input_output_aliases={a: a for a in range(n)}, compiler_params=_SPLIT_COPY,
    )(*xs, send_sems, recv_sems, after)


def _gather_forward(xs):
    n = len(xs)

    def body(*refs):
        x_refs, o_refs = refs[:n], refs[n:2 * n]
        send_sems, recv_sems = refs[2 * n:]
        x, y, c, chips = _place()
        sibling = (x, y, 1 - c)
        sends = []
        for j, chip in enumerate(chips):
            kj = 2 * chip[0] + chip[1]
            for a in range(n):
                landed = x_refs[a].at[kj, c]
                sends.append(_remote(landed, o_refs[a].at[kj, c], send_sems, recv_sems, 3 * a + j, sibling))
                sends[-1].start()
        for j, chip in enumerate(chips):
            kj = 2 * chip[0] + chip[1]
            for a in range(n):
                passed = o_refs[a].at[kj, 1 - c]
                _remote(passed, passed, send_sems, recv_sems, 3 * a + j, sibling).wait_recv()
        for cp in sends:
            cp.wait_send()

    return pl.pallas_call(
        body, name="gather_forward", in_specs=[_ANY] * n, out_specs=[_ANY] * n,
        out_shape=[jax.ShapeDtypeStruct(v.shape, v.dtype) for v in xs],
        input_output_aliases={a: a for a in range(n)},
        scratch_shapes=[pltpu.SemaphoreType.DMA((3 * n,)), pltpu.SemaphoreType.DMA((3 * n,))],
    )(*xs)


def _pair_exchange(g0s, g1s, name):
    n = len(g0s)

    def body(*refs):
        g0, g1, out = refs[:n], refs[n:2 * n], refs[2 * n:3 * n]
        send_sems, recv_sems = refs[3 * n:]
        x, y, c, _ = _place()
        sibling = (x, y, 1 - c)
        for a in range(n):
            @pl.when(c == 0)
            def _():
                _remote(g1[a], out[a], send_sems, recv_sems, a, sibling).start()

            @pl.when(c == 1)
            def _():
                _remote(g0[a], out[a], send_sems, recv_sems, a, sibling).start()
        for a in range(n):
            cp = _remote(g0[a], out[a], send_sems, recv_sems, a, sibling)
            cp.wait_recv()
            cp.wait_send()

    return pl.pallas_call(
        body, name=name, in_specs=[_ANY] * (2 * n), out_specs=[_ANY] * n,
        out_shape=[jax.ShapeDtypeStruct(v.shape, v.dtype) for v in g0s],
        scratch_shapes=[pltpu.SemaphoreType.DMA((n,)), pltpu.SemaphoreType.DMA((n,))],
    )(*g0s, *g1s)


def _chip_exchange_start(parts, name):
    n = len(parts)
    lands = [lax.empty((3,) + v.shape[1:], v.dtype) for v in parts]

    def body(*refs):
        p_refs, l_refs = refs[:n], refs[n:2 * n]
        send_sems, recv_sems = refs[2 * n], refs[2 * n + 1]
        token = refs[4 * n + 2]
        x, y, c, chips = _place()
        for j, chip in enumerate(chips):
            kj = 2 * chip[0] + chip[1]
            for a in range(n):
                _remote(p_refs[a].at[kj], l_refs[a].at[j], send_sems, recv_sems, 3 * a + j, (*chip, c)).start()
        token[...] = jnp.zeros(token.shape, F32)

    hbm = [pltpu.with_memory_space_constraint(v, pltpu.HBM) for v in parts + lands]
    return pl.pallas_call(
        body, name=name, in_specs=[_HBM] * (2 * n),
        out_specs=(_SEM, _SEM, *[_HBM] * (2 * n), pl.BlockSpec(memory_space=pltpu.VMEM)),
        out_shape=(pltpu.SemaphoreType.DMA((3 * n,)), pltpu.SemaphoreType.DMA((3 * n,)),
                   *[pltpu.HBM(v.shape, v.dtype) for v in parts + lands], jax.ShapeDtypeStruct((8, 128), F32)),
        input_output_aliases={a: 2 + a for a in range(2 * n)}, compiler_params=_SPLIT_COPY,
    )(*hbm)


def _chip_exchange_wait(send_sems, recv_sems, parts, lands, after, name):
    n = len(parts)

    def body(*refs):
        p_refs, l_refs = refs[:n], refs[n:2 * n]
        send_sems, recv_sems = refs[2 * n], refs[2 * n + 1]
        x, y, c, chips = _place()
        for j, chip in enumerate(chips):
            kj = 2 * chip[0] + chip[1]
            for a in range(n):
                cp = _remote(p_refs[a].at[kj], l_refs[a].at[j], send_sems, recv_sems, 3 * a + j, (*chip, c))
                cp.wait_send()
                cp.wait_recv()

    return pl.pallas_call(
        body, name=name, in_specs=[_HBM] * (2 * n) + [_SEM, _SEM, _ANY], out_specs=[_HBM] * (2 * n),
        out_shape=[pltpu.HBM(v.shape, v.dtype) for v in parts + lands],
        input_output_aliases={a: a for a in range(2 * n)}, compiler_params=_SPLIT_COPY,
    )(*parts, *lands, send_sems, recv_sems, after)[n:]


def _pair_share(reds, name):
    n = len(reds)

    def body(*refs):
        r_refs, o_refs = refs[:n], refs[n:2 * n]
        send_sems, recv_sems = refs[2 * n:]
        x, y, c, _ = _place()
        sends = [_remote(r_refs[a], o_refs[a], send_sems, recv_sems, a, (x, y, 1 - c)) for a in range(n)]
        for cp in sends:
            cp.start()
        for cp in sends:
            cp.wait_recv()
        for cp in sends:
            cp.wait_send()

    return pl.pallas_call(
        body, name=name, in_specs=[_ANY] * n, out_specs=[_ANY] * n,
        out_shape=[jax.ShapeDtypeStruct(v.shape, v.dtype) for v in reds],
        scratch_shapes=[pltpu.SemaphoreType.DMA((n,)), pltpu.SemaphoreType.DMA((n,))],
    )(*reds)


def _all_reduce_small(pack):
    p, cols = pack.shape
    half = p // 2

    def body(x_ref, o_ref, sib, chipbuf, send_sems, recv_sems):
        x, y, c, chips = _place()
        k = 2 * x + y
        sibling = (x, y, 1 - c)
        mine = pl.ds(pl.multiple_of(c * half, 8), half)
        other = pl.ds(pl.multiple_of((1 - c) * half, 8), half)
        pair = _remote(x_ref.at[other], sib, send_sems, recv_sems, 0, sibling)
        pair.start()
        pair.wait_recv()
        chipbuf[k] = x_ref[mine, :] + sib[...]
        sends = [_remote(chipbuf.at[k], chipbuf.at[k], send_sems, recv_sems, 1 + j, (*chip, c)) for j, chip in enumerate(chips)]
        for cp in sends:
            cp.start()
        for j, chip in enumerate(chips):
            landed = chipbuf.at[2 * chip[0] + chip[1]]
            _remote(landed, landed, send_sems, recv_sems, 1 + j, (*chip, c)).wait_recv()
        o_ref[mine, :] = (chipbuf[0] + chipbuf[1]) + (chipbuf[2] + chipbuf[3])
        share = _remote(o_ref.at[mine], o_ref.at[mine], send_sems, recv_sems, 4, sibling)
        share.start()
        _remote(o_ref.at[other], o_ref.at[other], send_sems, recv_sems, 4, sibling).wait_recv()
        for cp in [pair, share] + sends:
            cp.wait_send()

    vm = pl.BlockSpec(memory_space=pltpu.VMEM)
    return pl.pallas_call(
        body, name="all_reduce_small", in_specs=[vm], out_specs=vm,
        out_shape=jax.ShapeDtypeStruct(pack.shape, F32),
        scratch_shapes=[pltpu.VMEM((half, cols), F32), pltpu.VMEM((4, half, cols), F32),
                        pltpu.SemaphoreType.DMA((5,)), pltpu.SemaphoreType.DMA((5,))],
    )(pack)


_BIG = ("w_in", "w_out", "w_up", "w_down")
_SMALL = ("norm1_g", "conv_dw_b", "conv_ln_g", "conv_ln_b", "pool_w", "pool_scale", "norm2_g", "final_g",
          "meta_tokens", "conv_dw_k", "ffn_dw_k")


def _rows8(v):
    return jnp.pad(v, ((0, -v.shape[0] % 8), (0, 0)))


def _pack_flat(arrs, rows):
    flat = jnp.concatenate([a.reshape(-1) for a in arrs])
    return jnp.pad(flat, (0, rows * D_MODEL - flat.shape[0])).reshape(1, rows, D_MODEL)


def _unpack_flat(packed, like):
    flat = packed.reshape(-1)
    out, off = [], 0
    for a in like:
        out.append(flat[off:off + a.size].reshape(a.shape))
        off += a.size
    return out


def kernel(x, meta_tokens, norm1_g, w_in, conv_dw_k, conv_dw_b, conv_ln_g, conv_ln_b, pool_w, pool_scale, w_out, norm2_g, w_up, ffn_dw_k, w_down, final_g, loss_target, m_meta_tokens, m_norm1_g, m_w_in, m_conv_dw_k, m_conv_dw_b, m_conv_ln_g, m_conv_ln_b, m_pool_w, m_pool_scale, m_w_out, m_norm2_g, m_w_up, m_ffn_dw_k, m_w_down, m_final_g, v_meta_tokens, v_norm1_g, v_w_in, v_conv_dw_k, v_conv_dw_b, v_conv_ln_g, v_conv_ln_b, v_pool_w, v_pool_scale, v_w_out, v_norm2_g, v_w_up, v_ffn_dw_k, v_w_down, v_final_g):
    weights = dict(meta_tokens=meta_tokens, norm1_g=norm1_g, w_in=w_in, conv_dw_k=conv_dw_k, conv_dw_b=conv_dw_b,
                   conv_ln_g=conv_ln_g, conv_ln_b=conv_ln_b, pool_w=pool_w, pool_scale=pool_scale, w_out=w_out,
                   norm2_g=norm2_g, w_up=w_up, ffn_dw_k=ffn_dw_k, w_down=w_down, final_g=final_g)
    mom1 = dict(meta_tokens=m_meta_tokens, norm1_g=m_norm1_g, w_in=m_w_in, conv_dw_k=m_conv_dw_k, conv_dw_b=m_conv_dw_b,
                conv_ln_g=m_conv_ln_g, conv_ln_b=m_conv_ln_b, pool_w=m_pool_w, pool_scale=m_pool_scale, w_out=m_w_out,
                norm2_g=m_norm2_g, w_up=m_w_up, ffn_dw_k=m_ffn_dw_k, w_down=m_w_down, final_g=m_final_g)
    mom2 = dict(meta_tokens=v_meta_tokens, norm1_g=v_norm1_g, w_in=v_w_in, conv_dw_k=v_conv_dw_k, conv_dw_b=v_conv_dw_b,
                conv_ln_g=v_conv_ln_g, conv_ln_b=v_conv_ln_b, pool_w=v_pool_w, pool_scale=v_pool_scale, w_out=v_w_out,
                norm2_g=v_norm2_g, w_up=v_w_up, ffn_dw_k=v_ffn_dw_k, w_down=v_w_down, final_g=v_final_g)
    order = list(weights)
    depth = w_in.shape[0]
    seq = x.shape[1]
    chip = 2 * lax.axis_index("x") + lax.axis_index("y")
    core = lax.axis_index("c")
    chip_arr = chip.astype(jnp.int32).reshape(1)
    core_arr = core.astype(jnp.int32).reshape(1)

    small_sharded = dict(conv_dw_k=jnp.pad(conv_dw_k, ((0, 0), (0, CONV_HIST - CONV_TAPS), (0, 0))),
                         ffn_dw_k=jnp.pad(ffn_dw_k, ((0, 0), (0, 8 - FFN_TAPS), (0, 0))),
                         meta_tokens=meta_tokens.reshape(2, N_META // 2, D_MODEL // 4))
    placed = {nm: _place_own(chip_arr, weights[nm], BF16, "place_" + nm) for nm in _BIG}
    g_in, g_out, g_cdk, g_fdk, g_meta = _gather_chips(
        [placed["w_in"], placed["w_out"]] + [_place_own(chip_arr, v, F32, "place_" + nm) for nm, v in small_sharded.items()])
    send_sems, recv_sems, up_buf, down_buf, token = _gather_start([placed["w_up"], placed["w_down"]], g_meta)
    meta_full = g_meta.transpose(1, 2, 0, 3).reshape(N_META, D_MODEL)
    layers = []
    for l in range(depth):
        w_in_l = g_in[:, l].transpose(1, 0, 2).reshape(D_MODEL, IN_COLS)
        w_out_l = g_out[:, l].reshape(D_MODEL, D_MODEL)
        pw = pool_w[l].astype(BF16)
        conv_v = jnp.pad(jnp.stack([conv_dw_b[l], conv_ln_g[l], conv_ln_b[l], pool_scale[l]]), ((0, 4), (0, 0)))
        layers.append(dict(
            w_in=w_in_l, conv_k=g_cdk[:, l].transpose(1, 0, 2).reshape(CONV_HIST, CONV_W), conv_v=conv_v,
            pool_w=pw, w_out=w_out_l, kf=g_fdk[:, l],
            g1=norm1_g[l][None], g2=norm2_g[l][None]))
    layers[0]["g1"] = layers[0]["g1"] + token[0, 0]

    def ffn_weights(after):
        g_up, g_down = _gather_forward(_gather_wait(send_sems, recv_sems, [up_buf, down_buf], after))
        per_layer = []
        for l in range(depth):
            per_layer.append(dict(w_down=g_down[:, l].reshape(2, FF_CHUNK, D_MODEL)))
        return g_up, per_layer

    def pair_reduced(names, g0s, g1s, tag):
        theirs = _pair_exchange(g0s, g1s, "pair_exchange_" + tag)
        return [_pair_sum_bf16(core_arr, g0s[a], g1s[a], theirs[a], "pair_sum_" + nm) for a, nm in enumerate(names)]

    in_flight = {}

    def ffn_grads_done(ffn_grads):
        parts = pair_reduced(("w_up", "w_down"), list(ffn_grads[0]), list(ffn_grads[1]), "ffn")
        send, recv, up_parts, down_parts, up_land, down_land, zero = _chip_exchange_start(parts, "chip_exchange_ffn_start")
        in_flight.update(sems=(send, recv), parts=[up_parts, down_parts], lands=[up_land, down_land])
        return zero[0, 0]

    loss_cols, dx, dmeta, grads, dfinal_g = _local_step(x[0], meta_full, loss_target[0], layers, ffn_weights, ffn_grads_done,
                                                         final_g[None], ROW_TILE, FFN_ROW_TILE)
    grad_x = dx[None]
    grad, delta, new_m, new_v = {}, {}, {}, {}

    def finish(names, parts, received):
        reds = [_chip_sum(chip_arr, parts[a], received[a], "chip_sum_" + nm) for a, nm in enumerate(names)]
        for nm, red, other in zip(names, reds, _pair_share(reds, "pair_share_" + names[0])):
            grad[nm], delta[nm], new_m[nm], new_v[nm] = _adamw_pair(core_arr, weights[nm], red, other, mom1[nm], mom2[nm],
                                                                    "adamw_" + nm)
        return delta[names[-1]]

    mixer_parts = pair_reduced(("w_in", "w_out"), [grads[0]["dw_in"], grads[0]["dw_out"]],
                               [grads[1]["dw_in"], grads[1]["dw_out"]], "mixer")
    m_send, m_recv, in_parts, out_parts, in_land, out_land, zero = _chip_exchange_start(mixer_parts, "chip_exchange_mixer_start")
    ffn_received = _chip_exchange_wait(*in_flight["sems"], in_flight["parts"], in_flight["lands"], zero, "chip_exchange_ffn_wait")
    ffn_done = finish(("w_up", "w_down"), in_flight["parts"], ffn_received)
    mixer_received = _chip_exchange_wait(m_send, m_recv, [in_parts, out_parts], [in_land, out_land], ffn_done,
                                         "chip_exchange_mixer_wait")
    finish(("w_in", "w_out"), [in_parts, out_parts], mixer_received)

    pack = jnp.concatenate([
        _rows8(jnp.concatenate([grads[l]["dg1"] for l in range(depth)])),
        _rows8(jnp.concatenate([grads[l]["dg2"] for l in range(depth)])),
        _rows8(jnp.concatenate([dfinal_g, loss_cols])),
        jnp.concatenate([grads[l]["dsmall"] for l in range(depth)], axis=1),
        jnp.stack([grads[l]["dpw"] for l in range(depth)]).reshape(-1, D_MODEL),
        dmeta,
        jnp.concatenate([grads[l]["dk"] for l in range(depth)], axis=1),
        jnp.stack([grads[l]["dkf"] for l in range(depth)]).reshape(-1, D_MODEL),
    ])
    red = _all_reduce_small(jnp.pad(pack, ((0, -pack.shape[0] % 16), (0, 0))))
    o = 0
    grad["norm1_g"] = red[o:o + depth]
    o += 8
    grad["norm2_g"] = red[o:o + depth]
    o += 8
    grad["final_g"] = red[o]
    loss = red[o + 1, 0]
    o += 8
    sm = red[o:o + 8].reshape(8, depth, CONV_W)
    grad["conv_dw_b"], grad["conv_ln_g"], grad["conv_ln_b"], grad["pool_scale"] = sm[0], sm[1], sm[2], sm[3]
    o += 8
    n_pw = depth * 4 * POOL_G * POOL_G // D_MODEL
    grad["pool_w"] = red[o:o + n_pw].reshape(pool_w.shape)
    o += n_pw
    grad["meta_tokens"] = lax.dynamic_slice_in_dim(red[o:o + N_META], chip * (D_MODEL // 4), D_MODEL // 4, axis=1)
    o += N_META
    dk_all = red[o:o + CONV_HIST].reshape(CONV_HIST, depth, 4, CONV_W // 4)
    grad["conv_dw_k"] = lax.dynamic_index_in_dim(dk_all, chip, axis=2, keepdims=False)[:CONV_TAPS].transpose(1, 0, 2)
    o += CONV_HIST
    dkf_all = red[o:pack.shape[0]].reshape(depth, 4, 8, FF_CHUNK)
    grad["ffn_dw_k"] = lax.dynamic_index_in_dim(dkf_all, chip, axis=1, keepdims=False)[:, :FFN_TAPS]

    small_rows = -(-sum(weights[nm].size for nm in _SMALL) // (8 * D_MODEL)) * 8
    packed = [_pack_flat([d[nm] for nm in _SMALL], small_rows) for d in (weights, grad, mom1, mom2)]
    for res, packed_out in zip((delta, new_m, new_v), _adamw(*packed, "adamw_small")):
        for nm, val in zip(_SMALL, _unpack_flat(packed_out, [weights[nm] for nm in _SMALL])):
            res[nm] = val

    return (loss, grad_x, *[grad[nm] for nm in order], *[delta[nm] for nm in order],
            *[new_m[nm] for nm in order], *[new_v[nm] for nm in order])
```

```python
import math

import jax
import jax.numpy as jnp
from jax import lax
from jax.experimental import pallas as pl
from jax.experimental.pallas import tpu as pltpu

F32 = jnp.float32
BF16 = jnp.bfloat16

D_MODEL = 1024
CONV_W = 512
POOL_W = 512
POOL_G = 128
POOL_WINDOWS = (2, 4, 8, 16)
IN_COLS = 1536
D_FF = 2816
FF_CHUNK = 1408
CONV_TAPS = 31
CONV_HIST = 32
POOL_HIST = 16
FFN_TAPS = 3
N_META = 16
EPS = 1e-6

ADAM_LR = 0.001
ADAM_B1 = 0.9
ADAM_B2 = 0.999
ADAM_EPS = 1e-08
ADAM_WD = 0.01
ADAM_STEP = 10

ROW_TILE = 256
FFN_ROW_TILE = 384
CONV_ROW_BLOCK = 64
FFN_ROW_BLOCK = 32
V7X_VMEM_LIMIT = 56 * 1024 * 1024

MESH = pl.DeviceIdType.MESH


def _cparams(n_axes, vmem=None):
    return pltpu.CompilerParams(dimension_semantics=("arbitrary",) * n_axes, vmem_limit_bytes=vmem)


def _whole(shape, single=False):
    zeros = (0,) * len(shape)
    if single:
        return pl.BlockSpec(shape, lambda *_: zeros, pipeline_mode=pl.Buffered(1))
    return pl.BlockSpec(shape, lambda *_: zeros)


def _sigmoid(x):
    return 0.5 * jnp.tanh(0.5 * x) + 0.5


def _dot(a, b):
    return jnp.dot(a, b, preferred_element_type=F32)


def _dot_nt(a, b):
    return lax.dot_general(a, b, (((1,), (1,)), ((), ())), preferred_element_type=F32)


def _dot_tn(a, b):
    return lax.dot_general(a, b, (((0,), (0,)), ((), ())), preferred_element_type=F32)


def _split_dot(v, a_ref):
    hi = v.astype(BF16)
    lo = (v - hi.astype(F32)).astype(BF16)
    return _dot(hi, a_ref[...]) + _dot(lo, a_ref[...])


def _rms(x):
    r = lax.rsqrt(jnp.mean(x * x, axis=-1, keepdims=True) + EPS)
    return x * r, r


def _rms_bwd(dy, xhat, r, g):
    gd = dy * g
    return r * (gd - xhat * jnp.mean(gd * xhat, axis=-1, keepdims=True)), dy * xhat


def _colsum(v):
    return jnp.sum(v, axis=0, keepdims=True)


def _shifted(window, s):
    return window if s == 0 else pltpu.roll(window, window.shape[0] - s, 0)


def _conv3(window, kf_ref, cc, ls, hist):
    x2 = window[hist:]
    x1 = pltpu.roll(window, 1, 0)[hist:]
    x0 = pltpu.roll(window, 2, 0)[hist:]
    return x0, x1, x2, kf_ref[cc, 0:1, ls] * x0 + kf_ref[cc, 1:2, ls] * x1 + kf_ref[cc, 2:3, ls] * x2


def _for_row_blocks(n, rb, fn, unroll):
    if unroll:
        for r in range(n):
            fn(r * rb)
    else:
        def step(r, keep):
            fn(pl.multiple_of(r * rb, rb))
            return keep

        lax.fori_loop(0, n, step, 0)


def _fold8(v):
    part = v[0:8]
    for k in range(1, v.shape[0] // 8):
        part = part + v[8 * k:8 * k + 8]
    return part


def _inv_count(tile, tm, w):
    t = tile * tm + lax.broadcasted_iota(jnp.int32, (tm, POOL_G), 0)
    return 1.0 / jnp.minimum(t + 1, w).astype(F32)


def _embed(x, meta, g, rows, tm):
    seq = x.shape[0]
    x_tiles = -(-seq // tm)

    def body(xprev_ref, x_ref, meta_ref, g_ref, h_ref, hn_ref):
        i = pl.program_id(0)
        t = i * tm + lax.broadcasted_iota(jnp.int32, (tm, 1), 0)
        head = jnp.where(i == 0, meta_ref[...], xprev_ref[...])
        h = jnp.concatenate([head, x_ref[0:tm - N_META, :]], axis=0)
        h = jnp.where(t < N_META + seq, h, 0.0)
        h_ref[...] = h
        xhat, _ = _rms(h)
        hn_ref[...] = (xhat * g_ref[...]).astype(BF16)

    tile = pl.BlockSpec((tm, D_MODEL), lambda i: (i, 0))
    x_prev = pl.BlockSpec((N_META, D_MODEL), lambda i: (jnp.maximum(i * (tm // N_META) - 1, 0), 0))
    x_own = pl.BlockSpec((tm, D_MODEL), lambda i: (jnp.minimum(i, x_tiles - 1), 0))
    return pl.pallas_call(
        body, name="embed", grid=(rows // tm,),
        in_specs=[x_prev, x_own, _whole((N_META, D_MODEL)), _whole((1, D_MODEL))],
        out_specs=[tile, tile],
        out_shape=[jax.ShapeDtypeStruct((rows, D_MODEL), F32), jax.ShapeDtypeStruct((rows, D_MODEL), BF16)],
        compiler_params=_cparams(1),
    )(x, x, meta, g)


def _row_block(r):
    for cand in (256, 176, 128, 64, 32, 16):
        if r % cand == 0:
            return cand
    return r


def _place_own(chip, w, dtype, name):
    n, r, c = w.shape
    rb = _row_block(r)

    def body(chip_ref, w_ref, o_ref):
        o_ref[0] = w_ref[...].astype(dtype)

    return pl.pallas_call(
        body, name=name,
        grid_spec=pltpu.PrefetchScalarGridSpec(
            num_scalar_prefetch=1, grid=(n, r // rb),
            in_specs=[pl.BlockSpec((1, rb, c), lambda i, j, chip_ref: (i, j, 0))],
            out_specs=pl.BlockSpec((1, 1, rb, c), lambda i, j, chip_ref: (chip_ref[0], i, j, 0))),
        out_shape=jax.ShapeDtypeStruct((4,) + w.shape, dtype), compiler_params=_cparams(2),
    )(chip, w)


def _pair_sum_bf16(core, g0, g1, other, name):
    n, r, c = g0.shape
    rb = _row_block(r)

    def body(core_ref, g0_ref, g1_ref, o_ref, out_ref):
        mine = jnp.where(core_ref[0] == 0, g0_ref[...], g1_ref[...])
        out_ref[...] = (mine + o_ref[...]).astype(BF16)

    def layer_spec(layer):
        return pl.BlockSpec((1, rb, c), lambda i, j, core_ref: (jnp.where(core_ref[0] == layer, i, 0),
                                                                jnp.where(core_ref[0] == layer, j, 0), 0))

    spec = pl.BlockSpec((1, rb, c), lambda i, j, core_ref: (i, j, 0))
    return pl.pallas_call(
        body, name=name,
        grid_spec=pltpu.PrefetchScalarGridSpec(num_scalar_prefetch=1, grid=(n, r // rb),
                                               in_specs=[layer_spec(0), layer_spec(1), spec], out_specs=spec),
        out_shape=jax.ShapeDtypeStruct(g0.shape, BF16), compiler_params=_cparams(2),
    )(core, g0, g1, other)


def _chip_sum(chip, parts, recv, name):
    _, r, c = parts.shape
    rb = _row_block(r)

    def body(chip_ref, p_ref, r_ref, out_ref):
        got = r_ref[...].astype(F32)
        out_ref[...] = (p_ref[0].astype(F32) + got[0]) + (got[1] + got[2])

    return pl.pallas_call(
        body, name=name,
        grid_spec=pltpu.PrefetchScalarGridSpec(
            num_scalar_prefetch=1, grid=(r // rb,),
            in_specs=[pl.BlockSpec((1, rb, c), lambda j, chip_ref: (chip_ref[0], j, 0)),
                      pl.BlockSpec((3, rb, c), lambda j, chip_ref: (0, j, 0))],
            out_specs=pl.BlockSpec((rb, c), lambda j, chip_ref: (j, 0))),
        out_shape=jax.ShapeDtypeStruct((r, c), F32), compiler_params=_cparams(1),
    )(chip, parts, recv)


def _adamw_update(w, g, m, v):
    nm = ADAM_B1 * m + (1.0 - ADAM_B1) * g
    nv = ADAM_B2 * v + (1.0 - ADAM_B2) * (g * g)
    m_hat = nm / (1.0 - ADAM_B1 ** ADAM_STEP)
    v_hat = nv / (1.0 - ADAM_B2 ** ADAM_STEP)
    return -ADAM_LR * (m_hat / (jnp.sqrt(v_hat) + ADAM_EPS) + ADAM_WD * w), nm, nv


def _adamw(w, g, m, v, name):
    n, r, c = w.shape
    rb = _row_block(r)

    def body(w_ref, g_ref, m_ref, v_ref, d_ref, nm_ref, nv_ref):
        d_ref[...], nm_ref[...], nv_ref[...] = _adamw_update(w_ref[...], g_ref[...], m_ref[...], v_ref[...])

    spec = pl.BlockSpec((1, rb, c), lambda i, j: (i, j, 0))
    shp = jax.ShapeDtypeStruct(w.shape, F32)
    return pl.pallas_call(
        body, name=name, grid=(n, r // rb), in_specs=[spec] * 4, out_specs=[spec] * 3,
        out_shape=[shp] * 3, compiler_params=_cparams(2),
    )(w, g, m, v)


def _adamw_pair(core, w, mine, theirs, m, v, name):
    n, r, c = w.shape
    rb = _row_block(r)

    def body(core_ref, w_ref, a_ref, b_ref, m_ref, v_ref, g_ref, d_ref, nm_ref, nv_ref):
        g = jnp.where(pl.program_id(0) == core_ref[0], a_ref[...], b_ref[...])
        g_ref[0] = g
        d_ref[0], nm_ref[0], nv_ref[0] = _adamw_update(w_ref[0], g, m_ref[0], v_ref[0])

    spec = pl.BlockSpec((1, rb, c), lambda i, j, core_ref: (i, j, 0))
    flat = pl.BlockSpec((rb, c), lambda i, j, core_ref: (j, 0))
    shp = jax.ShapeDtypeStruct(w.shape, F32)
    return pl.pallas_call(
        body, name=name,
        grid_spec=pltpu.PrefetchScalarGridSpec(num_scalar_prefetch=1, grid=(n, r // rb),
                                               in_specs=[spec, flat, flat, spec, spec], out_specs=[spec] * 4),
        out_shape=[shp] * 4, compiler_params=_cparams(2),
    )(core, w, mine, theirs, m, v)


def _mixer_fwd(h0, hn, w_in, conv_k, conv_v, avg, pool_w, w_out, g2, tm):
    rows = h0.shape[0]
    rb = CONV_ROW_BLOCK

    def body(h0_ref, hn_ref, win_ref, ck_ref, cv_ref, avg_ref, pw_ref, wout_ref, g2_ref,
             z_ref, u1_ref, m_ref, h1_ref, hn2_ref, ubuf, pbuf):
        i = pl.program_id(0)

        @pl.when(i == 0)
        def _():
            ubuf[pl.ds(0, CONV_HIST), :] = jnp.zeros((CONV_HIST, CONV_W), F32)
            pbuf[pl.ds(0, POOL_HIST), :] = jnp.zeros((POOL_HIST, POOL_W), F32)

        z = _dot(hn_ref[...], win_ref[...])
        z_ref[...] = z
        ubuf[pl.ds(CONV_HIST, tm), :] = z[:, :CONV_W] * _sigmoid(z[:, CONV_W:2 * CONV_W])
        p = z[:, 2 * CONV_W:]
        pbuf[pl.ds(POOL_HIST, tm), :] = p

        def conv_block(r0):
            for l in range(CONV_W // 128):
                ls = pl.ds(128 * l, 128)
                window = ubuf[pl.ds(r0, rb + CONV_HIST), ls]
                acc = jnp.broadcast_to(cv_ref[0:1, ls], (rb, 128))
                for s in range(8):
                    ws = _shifted(window, s)
                    for q in range(CONV_HIST // 8 + 1):
                        j = 8 * q + s - 2
                        if 0 <= j < CONV_TAPS:
                            acc = acc + ck_ref[j:j + 1, ls] * ws[8 * q:8 * q + rb]
                u1_ref[pl.ds(r0, rb), ls] = acc

        _for_row_blocks(tm // rb, rb, conv_block, unroll=True)
        ubuf[pl.ds(0, CONV_HIST), :] = ubuf[pl.ds(tm, CONV_HIST), :]

        for l in range(CONV_W // 128):
            ls = pl.ds(128 * l, 128)
            u1 = u1_ref[:, ls]
            cen = u1 - _split_dot(u1, avg_ref)
            xhat = cen * lax.rsqrt(_split_dot(cen * cen, avg_ref) + EPS)
            u2 = xhat * cv_ref[1:2, ls] + cv_ref[2:3, ls]
            m_ref[:, ls] = (u2 * _sigmoid(u2)).astype(BF16)

        for gi, w in enumerate(POOL_WINDOWS):
            ls = pl.ds(POOL_G * gi, POOL_G)
            s = pbuf[pl.ds(POOL_HIST, tm), ls]
            for j in range(1, w):
                s = s + pbuf[pl.ds(POOL_HIST - j, tm), ls]
            d = s * _inv_count(i, tm, w) - p[:, POOL_G * gi:POOL_G * (gi + 1)]
            y = _dot(d.astype(BF16), pw_ref[gi]) * cv_ref[3:4, ls]
            m_ref[:, pl.ds(CONV_W + POOL_G * gi, POOL_G)] = y.astype(BF16)
        pbuf[pl.ds(0, POOL_HIST), :] = pbuf[pl.ds(tm, POOL_HIST), :]

        h1 = h0_ref[...] + _dot(m_ref[...], wout_ref[...])
        h1_ref[...] = h1
        xh, _ = _rms(h1)
        hn2_ref[...] = (xh * g2_ref[...]).astype(BF16)

    def tile(c):
        return pl.BlockSpec((tm, c), lambda i: (i, 0))

    return pl.pallas_call(
        body, name="mixer_fwd", grid=(rows // tm,),
        in_specs=[tile(D_MODEL), tile(D_MODEL), _whole((D_MODEL, IN_COLS)), _whole((CONV_HIST, CONV_W)),
                  _whole((8, CONV_W)), _whole((128, 128)), _whole((4, POOL_G, POOL_G)),
                  _whole((D_MODEL, D_MODEL)), _whole((1, D_MODEL))],
        out_specs=[tile(IN_COLS), tile(CONV_W), tile(D_MODEL), tile(D_MODEL), tile(D_MODEL)],
        out_shape=[jax.ShapeDtypeStruct((rows, IN_COLS), F32), jax.ShapeDtypeStruct((rows, CONV_W), F32),
                   jax.ShapeDtypeStruct((rows, D_MODEL), BF16), jax.ShapeDtypeStruct((rows, D_MODEL), F32),
                   jax.ShapeDtypeStruct((rows, D_MODEL), BF16)],
        scratch_shapes=[pltpu.VMEM((CONV_HIST + tm, CONV_W), F32), pltpu.VMEM((POOL_HIST + tm, POOL_W), F32)],
        compiler_params=_cparams(1, V7X_VMEM_LIMIT),
    )(h0, hn, w_in, conv_k, conv_v, avg, pool_w, w_out, g2)


def _ffn_fwd(h1, hn2, w_up, layer, kf, w_down, g_next, tm):
    rows = h1.shape[0]
    hist = 8
    w_up_spec = pl.BlockSpec((4, None, D_MODEL, FF_CHUNK), lambda i: (0, layer, 0, 0), pipeline_mode=pl.Buffered(1))

    rb = FFN_ROW_BLOCK

    def body(h1_ref, hn2_ref, wup_ref, kf_ref, wdn_ref, gn_ref, ug_ref, h2_ref, hnn_ref, wg, wv, carry, act_s, acc):
        i = pl.program_id(0)

        @pl.when(i == 0)
        def _():
            carry[...] = jnp.zeros(carry.shape, F32)

        acc[...] = h1_ref[...]
        for c in range(2):
            for buf, cc in ((wg, c), (wv, c + 2)):
                ug = _dot(hn2_ref[...], wup_ref[cc])
                ug_ref[cc] = ug.astype(BF16)
                buf[pl.ds(0, hist), :] = carry[cc]
                buf[pl.ds(hist, tm), :] = ug
                carry[cc] = buf[pl.ds(tm, hist), :]

            def act_block(r0):
                for l in range(FF_CHUNK // 128):
                    ls = pl.ds(128 * l, 128)
                    gate = _conv3(wg[pl.ds(r0, rb + hist), ls], kf_ref, c, ls, hist)[3]
                    val = _conv3(wv[pl.ds(r0, rb + hist), ls], kf_ref, c + 2, ls, hist)[3]
                    act_s[pl.ds(r0, rb), ls] = (gate * _sigmoid(gate) * val).astype(BF16)

            _for_row_blocks(tm // rb, rb, act_block, unroll=True)
            acc[...] += _dot(act_s[...], wdn_ref[c])
        h2 = acc[...]
        h2_ref[...] = h2
        xh, _ = _rms(h2)
        hnn_ref[...] = (xh * gn_ref[...]).astype(BF16)

    def tile(c):
        return pl.BlockSpec((tm, c), lambda i: (i, 0))

    return pl.pallas_call(
        body, name="ffn_fwd", grid=(rows // tm,),
        in_specs=[tile(D_MODEL), tile(D_MODEL), w_up_spec, _whole((4, 8, FF_CHUNK)),
                  _whole((2, FF_CHUNK, D_MODEL), single=True), _whole((1, D_MODEL))],
        out_specs=[pl.BlockSpec((4, tm, FF_CHUNK), lambda i: (0, i, 0)), tile(D_MODEL), tile(D_MODEL)],
        out_shape=[jax.ShapeDtypeStruct((4, rows, FF_CHUNK), BF16), jax.ShapeDtypeStruct((rows, D_MODEL), F32),
                   jax.ShapeDtypeStruct((rows, D_MODEL), BF16)],
        scratch_shapes=[pltpu.VMEM((hist + tm, FF_CHUNK), F32), pltpu.VMEM((hist + tm, FF_CHUNK), F32),
                        pltpu.VMEM((4, hist, FF_CHUNK), F32), pltpu.VMEM((tm, FF_CHUNK), BF16),
                        pltpu.VMEM((tm, D_MODEL), F32)],
        compiler_params=_cparams(1, V7X_VMEM_LIMIT),
    )(h1, hn2, w_up, kf, w_down, g_next)


def _loss_head(h, tgt, g, seq, tm):
    rows = h.shape[0]
    tgt_tiles = -(-seq // tm)

    def body(h_ref, tprev_ref, t_ref, g_ref, dh_ref, loss_ref, dg_ref):
        i = pl.program_id(0)

        @pl.when(i == 0)
        def _():
            loss_ref[...] = jnp.zeros(loss_ref.shape, F32)
            dg_ref[...] = jnp.zeros(dg_ref.shape, F32)

        t = i * tm + lax.broadcasted_iota(jnp.int32, (tm, 1), 0)
        inside = jnp.logical_and(t >= N_META, t < N_META + seq)
        tgt = jnp.concatenate([tprev_ref[...], t_ref[0:tm - N_META, :]], axis=0)
        xhat, r = _rms(h_ref[...])
        err = jnp.where(inside, xhat * g_ref[...] - tgt, 0.0)
        loss_ref[...] += _colsum(err * err)
        dh, dg_rows = _rms_bwd(err * (1.0 / D_MODEL), xhat, r, g_ref[...])
        dh_ref[...] = dh
        dg_ref[...] += _colsum(dg_rows)

        @pl.when(i == rows // tm - 1)
        def _():
            total = jnp.sum(loss_ref[...], axis=1, keepdims=True) * (0.5 / D_MODEL)
            loss_ref[...] = jnp.broadcast_to(total, loss_ref.shape)

    tile = pl.BlockSpec((tm, D_MODEL), lambda i: (i, 0))
    t_prev = pl.BlockSpec((N_META, D_MODEL), lambda i: (jnp.maximum(i * (tm // N_META) - 1, 0), 0))
    t_own = pl.BlockSpec((tm, D_MODEL), lambda i: (jnp.minimum(i, tgt_tiles - 1), 0))
    vec = jax.ShapeDtypeStruct((1, D_MODEL), F32)
    return pl.pallas_call(
        body, name="loss_head", grid=(rows // tm,),
        in_specs=[tile, t_prev, t_own, _whole((1, D_MODEL))],
        out_specs=[tile, _whole((1, D_MODEL)), _whole((1, D_MODEL))],
        out_shape=[jax.ShapeDtypeStruct((rows, D_MODEL), F32), vec, vec],
        compiler_params=_cparams(1),
    )(h, tgt, tgt, g)


def _ffn_bwd(dh2, hn2, ug0, w_down, w_up, layer, kf, tm):
    rows = dh2.shape[0]
    nt = rows // tm
    hist = 16
    fut = 8
    rb = FFN_ROW_BLOCK
    near = 8

    def body(dh2_ref, hn2_ref, ugg_ref, ugv_ref, hg_ref, hv_ref, wd_ref, wug_ref, wuv_ref, kf_ref,
             dhn_ref, dkf_ref, dwup_ref, dwdn_ref, wg, wv, dgb, dvb, carry, dkacc, act_s, dug_s, acc_up, acc_dn):
        c = pl.program_id(0)
        i = pl.program_id(1)
        first_tile = jnp.where(i == nt - 1, 1.0, 0.0)

        @pl.when(i == 0)
        def _():
            carry[...] = jnp.zeros(carry.shape, F32)
            dkacc[...] = jnp.zeros(dkacc.shape, F32)
            acc_up[...] = jnp.zeros(acc_up.shape, F32)
            acc_dn[...] = jnp.zeros(acc_dn.shape, F32)

        def run():
            n_blocks = tm // rb
            dh2b = dh2_ref[...].astype(BF16)
            sides = ((wg, dgb, ugg_ref, hg_ref, 0, c), (wv, dvb, ugv_ref, hv_ref, 1, c + 2))
            for buf, dbuf, u_ref, h_ref, s, cc in sides:
                buf[pl.ds(0, hist), :] = h_ref[0].astype(F32) * (1.0 - first_tile)
                buf[pl.ds(hist, tm), :] = u_ref[0].astype(F32)
                dbuf[pl.ds(tm, fut), :] = carry[s]
            dgb[pl.ds(0, tm), :] = _dot_nt(dh2b, wd_ref[0])

            def grad_block(r0):
                for l in range(FF_CHUNK // 128):
                    ls = pl.ds(128 * l, 128)
                    g0, g1, g2, gate = _conv3(wg[pl.ds(r0 + hist - near, rb + near), ls], kf_ref, c, ls, near)
                    v0, v1, v2, val = _conv3(wv[pl.ds(r0 + hist - near, rb + near), ls], kf_ref, c + 2, ls, near)
                    sg = _sigmoid(gate)
                    silu = gate * sg
                    act_s[pl.ds(r0, rb), ls] = (silu * val).astype(BF16)
                    dact = dgb[pl.ds(r0, rb), ls]
                    dgate = dact * val * (sg * (1.0 + gate * (1.0 - sg)))
                    dval = dact * silu
                    dgb[pl.ds(r0, rb), ls] = dgate
                    dvb[pl.ds(r0, rb), ls] = dval
                    for s, dv, taps in ((0, dgate, (g0, g1, g2)), (1, dval, (v0, v1, v2))):
                        for j in range(FFN_TAPS):
                            dkacc[s, pl.ds(8 * j, 8), ls] += _fold8(dv * taps[j])

            _for_row_blocks(n_blocks, rb, grad_block, unroll=True)
            for buf, dbuf, u_ref, h_ref, s, cc in sides:
                carry[s] = dbuf[pl.ds(0, fut), :]

            def conv_block(r0):
                for l in range(FF_CHUNK // 128):
                    ls = pl.ds(128 * l, 128)
                    for buf, dbuf, u_ref, h_ref, s, cc in sides:
                        window = dbuf[pl.ds(r0, rb + fut), ls]
                        dug0 = (kf_ref[cc, 0:1, ls] * _shifted(window, 2)[0:rb] + kf_ref[cc, 1:2, ls] * _shifted(window, 1)[0:rb]
                                + kf_ref[cc, 2:3, ls] * window[0:rb])
                        dug_s[s, pl.ds(r0, rb), ls] = dug0.astype(BF16)

            _for_row_blocks(n_blocks, rb, conv_block, unroll=True)
            dhn_ref[0] = _dot_nt(dug_s[0], wug_ref[0]) + _dot_nt(dug_s[1], wuv_ref[0])
            acc_up[0] += _dot_tn(hn2_ref[...], dug_s[0])
            acc_up[1] += _dot_tn(hn2_ref[...], dug_s[1])
            acc_dn[...] += _dot_tn(act_s[...], dh2b)

        run()

        @pl.when(i == nt - 1)
        def _():
            for s in range(2):
                for j in range(FFN_TAPS):
                    dkf_ref[c + 2 * s, j:j + 1, :] = _colsum(dkacc[s, pl.ds(8 * j, 8), :])
                dkf_ref[c + 2 * s, FFN_TAPS:8, :] = jnp.zeros((8 - FFN_TAPS, FF_CHUNK), F32)
            pltpu.sync_copy(acc_up, dwup_ref.at[c])
            pltpu.sync_copy(acc_dn, dwdn_ref.at[c])

    def tile(cols):
        return pl.BlockSpec((tm, cols), lambda c, i: (nt - 1 - i, 0))

    def chunk(off, r, halo_rows=None):
        if halo_rows is None:
            return pl.BlockSpec((1, r, FF_CHUNK), lambda c, i: (c + off, nt - 1 - i, 0))
        return pl.BlockSpec((1, r, FF_CHUNK), lambda c, i: (c + off, jnp.maximum((nt - 1 - i) * (tm // r) - 1, 0), 0))

    def up_chunk(off):
        return pl.BlockSpec((1, None, D_MODEL, FF_CHUNK), lambda c, i: (c + off, layer, 0, 0), pipeline_mode=pl.Buffered(1))

    down_chunk = pl.BlockSpec((1, FF_CHUNK, D_MODEL), lambda c, i: (c, 0, 0), pipeline_mode=pl.Buffered(1))

    return pl.pallas_call(
        body, name="ffn_bwd", grid=(2, nt),
        in_specs=[tile(D_MODEL), tile(D_MODEL), chunk(0, tm), chunk(2, tm), chunk(0, hist, True), chunk(2, hist, True),
                  down_chunk, up_chunk(0), up_chunk(2), _whole((4, 8, FF_CHUNK))],
        out_specs=[pl.BlockSpec((1, tm, D_MODEL), lambda c, i: (c, nt - 1 - i, 0)), _whole((4, 8, FF_CHUNK)), _ANY, _ANY],
        out_shape=[jax.ShapeDtypeStruct((2, rows, D_MODEL), F32), jax.ShapeDtypeStruct((4, 8, FF_CHUNK), F32),
                   jax.ShapeDtypeStruct((2, 2, D_MODEL, FF_CHUNK), F32), jax.ShapeDtypeStruct((2, FF_CHUNK, D_MODEL), F32)],
        scratch_shapes=[pltpu.VMEM((hist + tm, FF_CHUNK), F32), pltpu.VMEM((hist + tm, FF_CHUNK), F32),
                        pltpu.VMEM((tm + fut, FF_CHUNK), F32), pltpu.VMEM((tm + fut, FF_CHUNK), F32),
                        pltpu.VMEM((2, fut, FF_CHUNK), F32), pltpu.VMEM((2, 8 * FFN_TAPS, FF_CHUNK), F32),
                        pltpu.VMEM((tm, FF_CHUNK), BF16), pltpu.VMEM((2, tm, FF_CHUNK), BF16),
                        pltpu.VMEM((2, D_MODEL, FF_CHUNK), F32), pltpu.VMEM((FF_CHUNK, D_MODEL), F32)],
        compiler_params=_cparams(2, V7X_VMEM_LIMIT),
    )(dh2, hn2, ug0, ug0, ug0, ug0, w_down, w_up, w_up, kf)


def _mixer_bwd(dh2, dhn2, h1, g2, h0, z, u1, hn1, m, w_out, w_in, conv_k, conv_v, avg, pool_w, g1, tm,
               x_rows=None):
    rows = dh2.shape[0]
    nt = rows // tm
    rb = CONV_ROW_BLOCK
    in_shard = IN_COLS // 4

    def body(dh2_ref, dhn2_ref, h1_ref, g2_ref, h0_ref, z_ref, zh_ref, u1_ref, hn1_ref, m_ref, wo_ref, wi_ref, ck_ref,
             cv_ref, avg_ref, pw_ref, g1_ref,
             dh0_ref, dtop_ref, dk_ref, ds_ref, dpw_ref, dg1_ref, dg2_ref, dwo_ref, dwi_ref,
             ubuf, dbuf, pbuf, ebuf, dcarry, ecarry, dkacc, dzs, dz_s, dh1_s, acc_out, acc_in, xcarry):
        i = pl.program_id(0)
        ti = nt - 1 - i
        has_past = jnp.where(ti > 0, 1.0, 0.0)

        @pl.when(i == 0)
        def _():
            dcarry[...] = jnp.zeros(dcarry.shape, F32)
            ecarry[...] = jnp.zeros(ecarry.shape, F32)
            dkacc[...] = jnp.zeros(dkacc.shape, F32)
            ds_ref[...] = jnp.zeros(ds_ref.shape, F32)
            dpw_ref[...] = jnp.zeros(dpw_ref.shape, F32)
            dg1_ref[...] = jnp.zeros(dg1_ref.shape, F32)
            dg2_ref[...] = jnp.zeros(dg2_ref.shape, F32)
            acc_out[...] = jnp.zeros(acc_out.shape, F32)
            acc_in[...] = jnp.zeros(acc_in.shape, F32)
            xcarry[...] = jnp.zeros(xcarry.shape, F32)

        xh1, r1 = _rms(h1_ref[...])
        dx1, dg2_rows = _rms_bwd(dhn2_ref[0] + dhn2_ref[1], xh1, r1, g2_ref[...])
        dh1_s[...] = dh2_ref[...] + dx1
        dg2_ref[...] += _colsum(dg2_rows)
        dh1b = dh1_s[...].astype(BF16)
        acc_out[...] += _dot_tn(m_ref[...], dh1b)
        dm = _dot_nt(dh1b, wo_ref[...])
        z = z_ref[...]
        a = z[:, :CONV_W]
        sg = _sigmoid(z[:, CONV_W:2 * CONV_W])
        p = z[:, 2 * CONV_W:]
        zh = zh_ref[...] * has_past
        ubuf[pl.ds(0, CONV_HIST), :] = zh[:, :CONV_W] * _sigmoid(zh[:, CONV_W:2 * CONV_W])
        ubuf[pl.ds(CONV_HIST, tm), :] = a * sg
        pbuf[pl.ds(0, POOL_HIST), :] = zh[CONV_HIST - POOL_HIST:, 2 * CONV_W:]
        pbuf[pl.ds(POOL_HIST, tm), :] = p

        u1 = u1_ref[...]
        cen = u1 - _split_dot(u1, avg_ref)
        rstd = lax.rsqrt(_split_dot(cen * cen, avg_ref) + EPS)
        xhat = cen * rstd
        u2 = xhat * cv_ref[1:2, :] + cv_ref[2:3, :]
        s2 = _sigmoid(u2)
        du2 = dm[:, :CONV_W] * (s2 * (1.0 + u2 * (1.0 - s2)))
        ds_ref[1:2, :] += _colsum(du2 * xhat)
        ds_ref[2:3, :] += _colsum(du2)
        dxh = du2 * cv_ref[1:2, :]
        du1 = rstd * (dxh - _split_dot(dxh, avg_ref) - xhat * _split_dot(dxh * xhat, avg_ref))
        ds_ref[0:1, :] += _colsum(du1)
        dbuf[pl.ds(0, tm), :] = du1
        dbuf[pl.ds(tm, CONV_HIST), :] = dcarry[...]
        dcarry[...] = dbuf[pl.ds(0, CONV_HIST), :]

        def conv_block(r0):
            for l in range(CONV_W // 128):
                ls = pl.ds(128 * l, 128)
                dwin = dbuf[pl.ds(r0, rb + CONV_HIST), ls]
                uwin = ubuf[pl.ds(r0, rb + CONV_HIST), ls]
                dblk = dwin[0:rb]
                du0 = jnp.zeros((rb, 128), F32)
                for s in range(8):
                    ds_ = _shifted(dwin, s)
                    us_ = _shifted(uwin, s)
                    for q in range(CONV_HIST // 8 + 1):
                        o = 8 * q + s
                        if 0 <= CONV_TAPS - 1 - o < CONV_TAPS:
                            j = CONV_TAPS - 1 - o
                            du0 = du0 + ck_ref[j:j + 1, ls] * ds_[8 * q:8 * q + rb]
                        j = o - 2
                        if 0 <= j < CONV_TAPS:
                            prod = dblk * us_[8 * q:8 * q + rb]
                            part = prod[0:8]
                            for v in range(1, rb // 8):
                                part = part + prod[8 * v:8 * v + 8]
                            dkacc[pl.ds(8 * j, 8), ls] += part
                dzs[pl.ds(r0, rb), ls] = du0

        _for_row_blocks(tm // rb, rb, conv_block, unroll=True)
        du0 = dzs[:, 0:CONV_W]
        dz_s[:, 0:CONV_W] = (du0 * sg).astype(BF16)
        dz_s[:, CONV_W:2 * CONV_W] = (du0 * a * sg * (1.0 - sg)).astype(BF16)

        for gi, w in enumerate(POOL_WINDOWS):
            ls = pl.ds(POOL_G * gi, POOL_G)
            cols = slice(CONV_W + POOL_G * gi, CONV_W + POOL_G * (gi + 1))
            inv = _inv_count(ti, tm, w)
            s = pbuf[pl.ds(POOL_HIST, tm), ls]
            for j in range(1, w):
                s = s + pbuf[pl.ds(POOL_HIST - j, tm), ls]
            d = (s * inv - p[:, POOL_G * gi:POOL_G * (gi + 1)]).astype(BF16)
            dyp = dm[:, cols]
            ds_ref[3:4, ls] += _colsum(dyp * _dot(d, pw_ref[gi]))
            dyb = (dyp * cv_ref[3:4, ls]).astype(BF16)
            dpw_ref[gi] += _dot_tn(d, dyb)
            dd = _dot_nt(dyb, pw_ref[gi])
            ebuf[pl.ds(0, tm), ls] = dd * inv
            ebuf[pl.ds(tm, POOL_HIST), ls] = ecarry[:, ls]
            dp = ebuf[pl.ds(0, tm), ls] - dd
            for j in range(1, w):
                dp = dp + ebuf[pl.ds(j, tm), ls]
            dz_s[:, pl.ds(2 * CONV_W + POOL_G * gi, POOL_G)] = dp.astype(BF16)
        ecarry[...] = ebuf[pl.ds(0, POOL_HIST), :]

        acc_in[...] += _dot_tn(hn1_ref[...], dz_s[...])
        dhn = _dot_nt(dz_s[...], wi_ref[...])
        xh, r = _rms(h0_ref[...])
        dx, dg_rows = _rms_bwd(dhn, xh, r, g1_ref[...])
        dh0 = dh1_s[...] + dx
        if x_rows is None:
            dh0_ref[...] = dh0
        else:
            dh0_ref[0:tm - N_META, :] = dh0[N_META:]
            dh0_ref[tm - N_META:tm, :] = xcarry[...]
            xcarry[...] = dh0[0:N_META]
        dg1_ref[...] += _colsum(dg_rows)

        @pl.when(i == nt - 1)
        def _():
            dtop_ref[...] = dh0[0:N_META]
            for j in range(CONV_TAPS):
                dk_ref[j:j + 1, :] = _colsum(dkacc[pl.ds(8 * j, 8), :])
            dk_ref[CONV_TAPS:CONV_HIST, :] = jnp.zeros((CONV_HIST - CONV_TAPS, CONV_W), F32)
            pltpu.sync_copy(acc_out, dwo_ref)
            for k in range(4):
                pltpu.sync_copy(acc_in.at[:, pl.ds(in_shard * k, in_shard)], dwi_ref.at[k])

    def tile(c):
        return pl.BlockSpec((tm, c), lambda i: (nt - 1 - i, 0))

    halo = pl.BlockSpec((CONV_HIST, IN_COLS), lambda i: (jnp.maximum((nt - 1 - i) * (tm // CONV_HIST) - 1, 0), 0))
    vec = jax.ShapeDtypeStruct((1, D_MODEL), F32)
    if x_rows is None:
        first_spec, first_shape = tile(D_MODEL), jax.ShapeDtypeStruct((rows, D_MODEL), F32)
    else:
        x_tiles = -(-x_rows // tm)
        first_spec = pl.BlockSpec((tm, D_MODEL), lambda i: (jnp.minimum(nt - 1 - i, x_tiles - 1), 0))
        first_shape = jax.ShapeDtypeStruct((x_rows, D_MODEL), F32)
    return pl.pallas_call(
        body, name="mixer_bwd", grid=(nt,),
        in_specs=[tile(D_MODEL), pl.BlockSpec((2, tm, D_MODEL), lambda i: (0, nt - 1 - i, 0)), tile(D_MODEL),
                  _whole((1, D_MODEL)), tile(D_MODEL), tile(IN_COLS), halo, tile(CONV_W), tile(D_MODEL), tile(D_MODEL),
                  _whole((D_MODEL, D_MODEL)), _whole((D_MODEL, IN_COLS)), _whole((CONV_HIST, CONV_W)), _whole((8, CONV_W)),
                  _whole((CONV_W, CONV_W)), _whole((4, POOL_G, POOL_G)), _whole((1, D_MODEL))],
        out_specs=[first_spec, _whole((N_META, D_MODEL)), _whole((CONV_HIST, CONV_W)), _whole((8, CONV_W)),
                   _whole((4, POOL_G, POOL_G)), _whole((1, D_MODEL)), _whole((1, D_MODEL)), _ANY, _ANY],
        out_shape=[first_shape, jax.ShapeDtypeStruct((N_META, D_MODEL), F32), jax.ShapeDtypeStruct((CONV_HIST, CONV_W), F32),
                   jax.ShapeDtypeStruct((8, CONV_W), F32), jax.ShapeDtypeStruct((4, POOL_G, POOL_G), F32), vec, vec,
                   jax.ShapeDtypeStruct((D_MODEL, D_MODEL), F32), jax.ShapeDtypeStruct((4, D_MODEL, in_shard), F32)],
        scratch_shapes=[pltpu.VMEM((CONV_HIST + tm, CONV_W), F32), pltpu.VMEM((tm + CONV_HIST, CONV_W), F32),
                        pltpu.VMEM((POOL_HIST + tm, POOL_W), F32), pltpu.VMEM((tm + POOL_HIST, POOL_W), F32),
                        pltpu.VMEM((CONV_HIST, CONV_W), F32), pltpu.VMEM((POOL_HIST, POOL_W), F32),
                        pltpu.VMEM((8 * CONV_HIST, CONV_W), F32), pltpu.VMEM((tm, CONV_W), F32),
                        pltpu.VMEM((tm, IN_COLS), BF16), pltpu.VMEM((tm, D_MODEL), F32),
                        pltpu.VMEM((D_MODEL, D_MODEL), F32), pltpu.VMEM((D_MODEL, IN_COLS), F32),
                        pltpu.VMEM((N_META, D_MODEL), F32)],
        compiler_params=_cparams(1, V7X_VMEM_LIMIT),
    )(dh2, dhn2, h1, g2, h0, z, z, u1, hn1, m, w_out, w_in, conv_k, conv_v, avg, pool_w, g1)


def _head_average():
    head = lax.broadcasted_iota(jnp.int32, (CONV_W, CONV_W), 0) // 64
    return jnp.where(head == head.T, 1.0 / 64, 0.0).astype(BF16)


def _local_step(x, meta, tgt, layers, ffn_weights, ffn_grads_done, final_g, tm, tm_ffn):
    avg = _head_average()
    depth = len(layers)
    saved = []
    seq = x.shape[0]
    step = math.lcm(tm, tm_ffn)
    rows = -(-(N_META + seq) // step) * step
    h, hn = _embed(x, meta, layers[0]["g1"], rows, tm)
    for l, w in enumerate(layers):
        g_next = layers[l + 1]["g1"] if l + 1 < depth else final_g
        z, u1, m, h1, hn2 = _mixer_fwd(h, hn, w["w_in"], w["conv_k"], w["conv_v"], avg[:128, :128], w["pool_w"], w["w_out"],
                                       w["g2"], tm)
        if l == 0:
            w_up_all, ffn = ffn_weights(hn2)
        ug0, h2, hn_next = _ffn_fwd(h1, hn2, w_up_all, l, w["kf"], ffn[l]["w_down"], g_next, tm_ffn)
        saved.append((h, hn, z, u1, m, h1, hn2, ug0))
        h, hn = h2, hn_next
    dh, loss_cols, dfinal_g = _loss_head(h, tgt, final_g, seq, tm)

    grads = [None] * depth
    ffn_grads = [None] * depth
    for l in reversed(range(depth)):
        w = layers[l]
        h0, hn1, z, u1, m, h1, hn2, ug0 = saved[l]
        dhn2, dkf, dw_up, dw_down = _ffn_bwd(dh, hn2, ug0, ffn[l]["w_down"], w_up_all, l, w["kf"], tm)
        ffn_grads[l] = (dw_up.transpose(1, 0, 2, 3).reshape(4, D_MODEL, FF_CHUNK), dw_down.reshape(4, D_FF // 4, D_MODEL))
        g2 = w["g2"] + ffn_grads_done(ffn_grads) if l == 0 else w["g2"]
        dh0, dtop, dk, dsmall, dpw, dg1, dg2, dw_out, dw_in = _mixer_bwd(
            dh, dhn2, h1, g2, h0, z, u1, hn1, m, w["w_out"], w["w_in"], w["conv_k"], w["conv_v"], avg,
            w["pool_w"], w["g1"], tm, x_rows=seq if l == 0 else None)
        grads[l] = dict(dw_in=dw_in, dw_out=dw_out.reshape(4, D_MODEL // 4, D_MODEL), dk=dk, dsmall=dsmall, dpw=dpw,
                        dg1=dg1, dg2=dg2, dkf=dkf)
        dh = dh0
    return loss_cols, dh, dtop, grads, dfinal_g


_ANY = pl.BlockSpec(memory_space=pl.ANY)


def _place():
    x, y, c = lax.axis_index("x"), lax.axis_index("y"), lax.axis_index("c")
    chips = [(1 - x, y), (x, 1 - y), (1 - x, 1 - y)]
    return x, y, c, chips


def _remote(src, dst, send_sems, recv_sems, idx, to):
    return pltpu.make_async_remote_copy(src_ref=src, dst_ref=dst, send_sem=send_sems.at[idx], recv_sem=recv_sems.at[idx],
                                        device_id=to, device_id_type=MESH)


def _gather_chips(xs):
    n = len(xs)

    def body(*refs):
        x_refs, o_refs = refs[:n], refs[n:2 * n]
        send_sems, recv_sems = refs[2 * n:]
        x, y, c, chips = _place()
        k = 2 * x + y
        sibling = (x, y, 1 - c)
        sends = []
        for j, chip in enumerate(chips):
            for a in range(n):
                sends.append(_remote(x_refs[a].at[k, c], o_refs[a].at[k, c], send_sems, recv_sems, 3 * a + j, (*chip, c)))
                sends[-1].start()
        for j, chip in enumerate(chips):
            kj = 2 * chip[0] + chip[1]
            for a in range(n):
                landed = o_refs[a].at[kj, c]
                _remote(landed, landed, send_sems, recv_sems, 3 * a + j, sibling).wait_recv()
                sends.append(_remote(landed, landed, send_sems, recv_sems, 3 * n + 3 * a + j, sibling))
                sends[-1].start()
        for j, chip in enumerate(chips):
            kj = 2 * chip[0] + chip[1]
            for a in range(n):
                passed = o_refs[a].at[kj, 1 - c]
                _remote(passed, passed, send_sems, recv_sems, 3 * n + 3 * a + j, sibling).wait_recv()
        for cp in sends:
            cp.wait_send()

    return pl.pallas_call(
        body, name="gather_chips", in_specs=[_ANY] * n, out_specs=[_ANY] * n,
        out_shape=[jax.ShapeDtypeStruct(v.shape, v.dtype) for v in xs],
        input_output_aliases={a: a for a in range(n)},
        scratch_shapes=[pltpu.SemaphoreType.DMA((6 * n,)), pltpu.SemaphoreType.DMA((6 * n,))],
    )(*xs)


_HBM = pl.BlockSpec(memory_space=pltpu.HBM)
_SEM = pl.BlockSpec(memory_space=pltpu.SEMAPHORE)
_SPLIT_COPY = pltpu.CompilerParams(has_side_effects=pltpu.SideEffectType.DATAFLOW_SIDE_EFFECTING)


def _gather_start(xs, after):
    n = len(xs)

    def body(*refs):
        x_refs = refs[:n]
        send_sems, recv_sems = refs[n + 1], refs[n + 2]
        token = refs[2 * n + 3]
        x, y, c, chips = _place()
        k = 2 * x + y
        for j, chip in enumerate(chips):
            for a in range(n):
                mine = x_refs[a].at[k, c]
                _remote(mine, mine, send_sems, recv_sems, 3 * a + j, (*chip, c)).start()
        token[...] = jnp.zeros(token.shape, F32)

    return pl.pallas_call(
        body, name="gather_start", in_specs=[_HBM] * n + [_ANY],
        out_specs=(_SEM, _SEM, *[_HBM] * n, pl.BlockSpec(memory_space=pltpu.VMEM)),
        out_shape=(pltpu.SemaphoreType.DMA((3 * n,)), pltpu.SemaphoreType.DMA((3 * n,)),
                   *[pltpu.HBM(v.shape, v.dtype) for v in xs], jax.ShapeDtypeStruct((8, 128), F32)),
        input_output_aliases={a: 2 + a for a in range(n)}, compiler_params=_SPLIT_COPY,
    )(*[pltpu.with_memory_space_constraint(v, pltpu.HBM) for v in xs], after)


def _gather_wait(send_sems, recv_sems, xs, after):
    n = len(xs)

    def body(*refs):
        x_refs = refs[:n]
        send_sems, recv_sems = refs[n], refs[n + 1]
        x, y, c, chips = _place()
        k = 2 * x + y
        for j, chip in enumerate(chips):
            kj = 2 * chip[0] + chip[1]
            for a in range(n):
                cp = _remote(x_refs[a].at[k, c], x_refs[a].at[kj, c], send_sems, recv_sems, 3 * a + j, (*chip, c))
                cp.wait_send()
                cp.wait_recv()

    return pl.pallas_call(
        body, name="gather_wait", in_specs=[_HBM] * n + [_SEM, _SEM, _ANY], out_specs=[_HBM] * n,
        out_shape=[pltpu.HBM(v.shape, v.dtype) for v in xs],
        input_output_aliases={a: a for a in range(n)}, compiler_params=_SPLIT_COPY,
    )(*xs, send_sems, recv_sems, after)


def _gather_forward(xs):
    n = len(xs)

    def body(*refs):
        x_refs, o_refs = refs[:n], refs[n:2 * n]
        send_sems, recv_sems = refs[2 * n:]
        x, y, c, chips = _place()
        sibling = (x, y, 1 - c)
        sends = []
        for j, chip in enumerate(chips):
            kj = 2 * chip[0] + chip[1]
            for a in range(n):
                landed = x_refs[a].at[kj, c]
                sends.append(_remote(landed, o_refs[a].at[kj, c], send_sems, recv_sems, 3 * a + j, sibling))
                sends[-1].start()
        for j, chip in enumerate(chips):
            kj = 2 * chip[0] + chip[1]
            for a in range(n):
                passed = o_refs[a].at[kj, 1 - c]
                _remote(passed, passed, send_sems, recv_sems, 3 * a + j, sibling).wait_recv()
        for cp in sends:
            cp.wait_send()

    return pl.pallas_call(
        body, name="gather_forward", in_specs=[_ANY] * n, out_specs=[_ANY] * n,
        out_shape=[jax.ShapeDtypeStruct(v.shape, v.dtype) for v in xs],
        input_output_aliases={a: a for a in range(n)},
        scratch_shapes=[pltpu.SemaphoreType.DMA((3 * n,)), pltpu.SemaphoreType.DMA((3 * n,))],
    )(*xs)


def _pair_exchange(g0s, g1s, name):
    n = len(g0s)

    def body(*refs):
        g0, g1, out = refs[:n], refs[n:2 * n], refs[2 * n:3 * n]
        send_sems, recv_sems = refs[3 * n:]
        x, y, c, _ = _place()
        sibling = (x, y, 1 - c)
        for a in range(n):
            @pl.when(c == 0)
            def _():
                _remote(g1[a], out[a], send_sems, recv_sems, a, sibling).start()

            @pl.when(c == 1)
            def _():
                _remote(g0[a], out[a], send_sems, recv_sems, a, sibling).start()
        for a in range(n):
            cp = _remote(g0[a], out[a], send_sems, recv_sems, a, sibling)
            cp.wait_recv()
            cp.wait_send()

    return pl.pallas_call(
        body, name=name, in_specs=[_ANY] * (2 * n), out_specs=[_ANY] * n,
        out_shape=[jax.ShapeDtypeStruct(v.shape, v.dtype) for v in g0s],
        scratch_shapes=[pltpu.SemaphoreType.DMA((n,)), pltpu.SemaphoreType.DMA((n,))],
    )(*g0s, *g1s)


def _chip_exchange_start(parts, name):
    n = len(parts)
    lands = [lax.empty((3,) + v.shape[1:], v.dtype) for v in parts]

    def body(*refs):
        p_refs, l_refs = refs[:n], refs[n:2 * n]
        send_sems, recv_sems = refs[2 * n], refs[2 * n + 1]
        token = refs[4 * n + 2]
        x, y, c, chips = _place()
        for j, chip in enumerate(chips):
            kj = 2 * chip[0] + chip[1]
            for a in range(n):
                _remote(p_refs[a].at[kj], l_refs[a].at[j], send_sems, recv_sems, 3 * a + j, (*chip, c)).start()
        token[...] = jnp.zeros(token.shape, F32)

    hbm = [pltpu.with_memory_space_constraint(v, pltpu.HBM) for v in parts + lands]
    return pl.pallas_call(
        body, name=name, in_specs=[_HBM] * (2 * n),
        out_specs=(_SEM, _SEM, *[_HBM] * (2 * n), pl.BlockSpec(memory_space=pltpu.VMEM)),
        out_shape=(pltpu.SemaphoreType.DMA((3 * n,)), pltpu.SemaphoreType.DMA((3 * n,)),
                   *[pltpu.HBM(v.shape, v.dtype) for v in parts + lands], jax.ShapeDtypeStruct((8, 128), F32)),
        input_output_aliases={a: 2 + a for a in range(2 * n)}, compiler_params=_SPLIT_COPY,
    )(*hbm)


def _chip_exchange_wait(send_sems, recv_sems, parts, lands, after, name):
    n = len(parts)

    def body(*refs):
        p_refs, l_refs = refs[:n], refs[n:2 * n]
        send_sems, recv_sems = refs[2 * n], refs[2 * n + 1]
        x, y, c, chips = _place()
        for j, chip in enumerate(chips):
            kj = 2 * chip[0] + chip[1]
            for a in range(n):
                cp = _remote(p_refs[a].at[kj], l_refs[a].at[j], send_sems, recv_sems, 3 * a + j, (*chip, c))
                cp.wait_send()
                cp.wait_recv()

    return pl.pallas_call(
        body, name=name, in_specs=[_HBM] * (2 * n) + [_SEM, _SEM, _ANY], out_specs=[_HBM] * (2 * n),
        out_shape=[pltpu.HBM(v.shape, v.dtype) for v in parts + lands],
        input_output_aliases={a: a for a in range(2 * n)}, compiler_params=_SPLIT_COPY,
    )(*parts, *lands, send_sems, recv_sems, after)[n:]


def _pair_share(reds, name):
    n = len(reds)

    def body(*refs):
        r_refs, o_refs = refs[:n], refs[n:2 * n]
        send_sems, recv_sems = refs[2 * n:]
        x, y, c, _ = _place()
        sends = [_remote(r_refs[a], o_refs[a], send_sems, recv_sems, a, (x, y, 1 - c)) for a in range(n)]
        for cp in sends:
            cp.start()
        for cp in sends:
            cp.wait_recv()
        for cp in sends:
            cp.wait_send()

    return pl.pallas_call(
        body, name=name, in_specs=[_ANY] * n, out_specs=[_ANY] * n,
        out_shape=[jax.ShapeDtypeStruct(v.shape, v.dtype) for v in reds],
        scratch_shapes=[pltpu.SemaphoreType.DMA((n,)), pltpu.SemaphoreType.DMA((n,))],
    )(*reds)


def _all_reduce_small(pack):
    p, cols = pack.shape
    half = p // 2

    def body(x_ref, o_ref, sib, chipbuf, send_sems, recv_sems):
        x, y, c, chips = _place()
        k = 2 * x + y
        sibling = (x, y, 1 - c)
        mine = pl.ds(pl.multiple_of(c * half, 8), half)
        other = pl.ds(pl.multiple_of((1 - c) * half, 8), half)
        pair = _remote(x_ref.at[other], sib, send_sems, recv_sems, 0, sibling)
        pair.start()
        pair.wait_recv()
        chipbuf[k] = x_ref[mine, :] + sib[...]
        sends = [_remote(chipbuf.at[k], chipbuf.at[k], send_sems, recv_sems, 1 + j, (*chip, c)) for j, chip in enumerate(chips)]
        for cp in sends:
            cp.start()
        for j, chip in enumerate(chips):
            landed = chipbuf.at[2 * chip[0] + chip[1]]
            _remote(landed, landed, send_sems, recv_sems, 1 + j, (*chip, c)).wait_recv()
        o_ref[mine, :] = (chipbuf[0] + chipbuf[1]) + (chipbuf[2] + chipbuf[3])
        share = _remote(o_ref.at[mine], o_ref.at[mine], send_sems, recv_sems, 4, sibling)
        share.start()
        _remote(o_ref.at[other], o_ref.at[other], send_sems, recv_sems, 4, sibling).wait_recv()
        for cp in [pair, share] + sends:
            cp.wait_send()

    vm = pl.BlockSpec(memory_space=pltpu.VMEM)
    return pl.pallas_call(
        body, name="all_reduce_small", in_specs=[vm], out_specs=vm,
        out_shape=jax.ShapeDtypeStruct(pack.shape, F32),
        scratch_shapes=[pltpu.VMEM((half, cols), F32), pltpu.VMEM((4, half, cols), F32),
                        pltpu.SemaphoreType.DMA((5,)), pltpu.SemaphoreType.DMA((5,))],
    )(pack)


_BIG = ("w_in", "w_out", "w_up", "w_down")
_SMALL = ("norm1_g", "conv_dw_b", "conv_ln_g", "conv_ln_b", "pool_w", "pool_scale", "norm2_g", "final_g",
          "meta_tokens", "conv_dw_k", "ffn_dw_k")


def _rows8(v):
    return jnp.pad(v, ((0, -v.shape[0] % 8), (0, 0)))


def _pack_flat(arrs, rows):
    flat = jnp.concatenate([a.reshape(-1) for a in arrs])
    return jnp.pad(flat, (0, rows * D_MODEL - flat.shape[0])).reshape(1, rows, D_MODEL)


def _unpack_flat(packed, like):
    flat = packed.reshape(-1)
    out, off = [], 0
    for a in like:
        out.append(flat[off:off + a.size].reshape(a.shape))
        off += a.size
    return out


def kernel(x, meta_tokens, norm1_g, w_in, conv_dw_k, conv_dw_b, conv_ln_g, conv_ln_b, pool_w, pool_scale, w_out, norm2_g, w_up, ffn_dw_k, w_down, final_g, loss_target, m_meta_tokens, m_norm1_g, m_w_in, m_conv_dw_k, m_conv_dw_b, m_conv_ln_g, m_conv_ln_b, m_pool_w, m_pool_scale, m_w_out, m_norm2_g, m_w_up, m_ffn_dw_k, m_w_down, m_final_g, v_meta_tokens, v_norm1_g, v_w_in, v_conv_dw_k, v_conv_dw_b, v_conv_ln_g, v_conv_ln_b, v_pool_w, v_pool_scale, v_w_out, v_norm2_g, v_w_up, v_ffn_dw_k, v_w_down, v_final_g):
    weights = dict(meta_tokens=meta_tokens, norm1_g=norm1_g, w_in=w_in, conv_dw_k=conv_dw_k, conv_dw_b=conv_dw_b,
                   conv_ln_g=conv_ln_g, conv_ln_b=conv_ln_b, pool_w=pool_w, pool_scale=pool_scale, w_out=w_out,
                   norm2_g=norm2_g, w_up=w_up, ffn_dw_k=ffn_dw_k, w_down=w_down, final_g=final_g)
    mom1 = dict(meta_tokens=m_meta_tokens, norm1_g=m_norm1_g, w_in=m_w_in, conv_dw_k=m_conv_dw_k, conv_dw_b=m_conv_dw_b,
                conv_ln_g=m_conv_ln_g, conv_ln_b=m_conv_ln_b, pool_w=m_pool_w, pool_scale=m_pool_scale, w_out=m_w_out,
                norm2_g=m_norm2_g, w_up=m_w_up, ffn_dw_k=m_ffn_dw_k, w_down=m_w_down, final_g=m_final_g)
    mom2 = dict(meta_tokens=v_meta_tokens, norm1_g=v_norm1_g, w_in=v_w_in, conv_dw_k=v_conv_dw_k, conv_dw_b=v_conv_dw_b,
                conv_ln_g=v_conv_ln_g, conv_ln_b=v_conv_ln_b, pool_w=v_pool_w, pool_scale=v_pool_scale, w_out=v_w_out,
                norm2_g=v_norm2_g, w_up=v_w_up, ffn_dw_k=v_ffn_dw_k, w_down=v_w_down, final_g=v_final_g)
    order = list(weights)
    depth = w_in.shape[0]
    seq = x.shape[1]
    chip = 2 * lax.axis_index("x") + lax.axis_index("y")
    core = lax.axis_index("c")
    chip_arr = chip.astype(jnp.int32).reshape(1)
    core_arr = core.astype(jnp.int32).reshape(1)

    small_sharded = dict(conv_dw_k=jnp.pad(conv_dw_k, ((0, 0), (0, CONV_HIST - CONV_TAPS), (0, 0))),
                         ffn_dw_k=jnp.pad(ffn_dw_k, ((0, 0), (0, 8 - FFN_TAPS), (0, 0))),
                         meta_tokens=meta_tokens.reshape(2, N_META // 2, D_MODEL // 4))
    placed = {nm: _place_own(chip_arr, weights[nm], BF16, "place_" + nm) for nm in _BIG}
    g_in, g_out, g_cdk, g_fdk, g_meta = _gather_chips(
        [placed["w_in"], placed["w_out"]] + [_place_own(chip_arr, v, F32, "place_" + nm) for nm, v in small_sharded.items()])
    send_sems, recv_sems, up_buf, down_buf, token = _gather_start([placed["w_up"], placed["w_down"]], g_meta)
    meta_full = g_meta.transpose(1, 2, 0, 3).reshape(N_META, D_MODEL)
    layers = []
    for l in range(depth):
        w_in_l = g_in[:, l].transpose(1, 0, 2).reshape(D_MODEL, IN_COLS)
        w_out_l = g_out[:, l].reshape(D_MODEL, D_MODEL)
        pw = pool_w[l].astype(BF16)
        conv_v = jnp.pad(jnp.stack([conv_dw_b[l], conv_ln_g[l], conv_ln_b[l], pool_scale[l]]), ((0, 4), (0, 0)))
        layers.append(dict(
            w_in=w_in_l, conv_k=g_cdk[:, l].transpose(1, 0, 2).reshape(CONV_HIST, CONV_W), conv_v=conv_v,
            pool_w=pw, w_out=w_out_l, kf=g_fdk[:, l],
            g1=norm1_g[l][None], g2=norm2_g[l][None]))
    layers[0]["g1"] = layers[0]["g1"] + token[0, 0]

    def ffn_weights(after):
        g_up, g_down = _gather_forward(_gather_wait(send_sems, recv_sems, [up_buf, down_buf], after))
        per_layer = []
        for l in range(depth):
            per_layer.append(dict(w_down=g_down[:, l].reshape(2, FF_CHUNK, D_MODEL)))
        return g_up, per_layer

    def pair_reduced(names, g0s, g1s, tag):
        theirs = _pair_exchange(g0s, g1s, "pair_exchange_" + tag)
        return [_pair_sum_bf16(core_arr, g0s[a], g1s[a], theirs[a], "pair_sum_" + nm) for a, nm in enumerate(names)]

    in_flight = {}

    def ffn_grads_done(ffn_grads):
        parts = pair_reduced(("w_up", "w_down"), list(ffn_grads[0]), list(ffn_grads[1]), "ffn")
        send, recv, up_parts, down_parts, up_land, down_land, zero = _chip_exchange_start(parts, "chip_exchange_ffn_start")
        in_flight.update(sems=(send, recv), parts=[up_parts, down_parts], lands=[up_land, down_land])
        return zero[0, 0]

    loss_cols, dx, dmeta, grads, dfinal_g = _local_step(x[0], meta_full, loss_target[0], layers, ffn_weights, ffn_grads_done,
                                                         final_g[None], ROW_TILE, FFN_ROW_TILE)
    grad_x = dx[None]
    grad, delta, new_m, new_v = {}, {}, {}, {}

    def finish(names, parts, received):
        reds = [_chip_sum(chip_arr, parts[a], received[a], "chip_sum_" + nm) for a, nm in enumerate(names)]
        for nm, red, other in zip(names, reds, _pair_share(reds, "pair_share_" + names[0])):
            grad[nm], delta[nm], new_m[nm], new_v[nm] = _adamw_pair(core_arr, weights[nm], red, other, mom1[nm], mom2[nm],
                                                                    "adamw_" + nm)
        return delta[names[-1]]

    mixer_parts = pair_reduced(("w_in", "w_out"), [grads[0]["dw_in"], grads[0]["dw_out"]],
                               [grads[1]["dw_in"], grads[1]["dw_out"]], "mixer")
    m_send, m_recv, in_parts, out_parts, in_land, out_land, zero = _chip_exchange_start(mixer_parts, "chip_exchange_mixer_start")
    ffn_received = _chip_exchange_wait(*in_flight["sems"], in_flight["parts"], in_flight["lands"], zero, "chip_exchange_ffn_wait")
    ffn_done = finish(("w_up", "w_down"), in_flight["parts"], ffn_received)
    mixer_received = _chip_exchange_wait(m_send, m_recv, [in_parts, out_parts], [in_land, out_land], ffn_done,
                                         "chip_exchange_mixer_wait")
    finish(("w_in", "w_out"), [in_parts, out_parts], mixer_received)

    pack = jnp.concatenate([
        _rows8(jnp.concatenate([grads[l]["dg1"] for l in range(depth)])),
        _rows8(jnp.concatenate([grads[l]["dg2"] for l in range(depth)])),
        _rows8(jnp.concatenate([dfinal_g, loss_cols])),
        jnp.concatenate([grads[l]["dsmall"] for l in range(depth)], axis=1),
        jnp.stack([grads[l]["dpw"] for l in range(depth)]).reshape(-1, D_MODEL),
        dmeta,
        jnp.concatenate([grads[l]["dk"] for l in range(depth)], axis=1),
        jnp.stack([grads[l]["dkf"] for l in range(depth)]).reshape(-1, D_MODEL),
    ])
    red = _all_reduce_small(jnp.pad(pack, ((0, -pack.shape[0] % 16), (0, 0))))
    o = 0
    grad["norm1_g"] = red[o:o + depth]
    o += 8
    grad["norm2_g"] = red[o:o + depth]
    o += 8
    grad["final_g"] = red[o]
    loss = red[o + 1, 0]
    o += 8
    sm = red[o:o + 8].reshape(8, depth, CONV_W)
    grad["conv_dw_b"], grad["conv_ln_g"], grad["conv_ln_b"], grad["pool_scale"] = sm[0], sm[1], sm[2], sm[3]
    o += 8
    n_pw = depth * 4 * POOL_G * POOL_G // D_MODEL
    grad["pool_w"] = red[o:o + n_pw].reshape(pool_w.shape)
    o += n_pw
    grad["meta_tokens"] = lax.dynamic_slice_in_dim(red[o:o + N_META], chip * (D_MODEL // 4), D_MODEL // 4, axis=1)
    o += N_META
    dk_all = red[o:o + CONV_HIST].reshape(CONV_HIST, depth, 4, CONV_W // 4)
    grad["conv_dw_k"] = lax.dynamic_index_in_dim(dk_all, chip, axis=2, keepdims=False)[:CONV_TAPS].transpose(1, 0, 2)
    o += CONV_HIST
    dkf_all = red[o:pack.shape[0]].reshape(depth, 4, 8, FF_CHUNK)
    grad["ffn_dw_k"] = lax.dynamic_index_in_dim(dkf_all, chip, axis=1, keepdims=False)[:, :FFN_TAPS]

    small_rows = -(-sum(weights[nm].size for nm in _SMALL) // (8 * D_MODEL)) * 8
    packed = [_pack_flat([d[nm] for nm in _SMALL], small_rows) for d in (weights, grad, mom1, mom2)]
    for res, packed_out in zip((delta, new_m, new_v), _adamw(*packed, "adamw_small")):
        for nm, val in zip(_SMALL, _unpack_flat(packed_out, [weights[nm] for nm in _SMALL])):
            res[nm] = val

    return (loss, grad_x, *[grad[nm] for nm in order], *[delta[nm] for nm in order],
            *[new_m[nm] for nm in order], *[new_v[nm] for nm in order])
```
